```python
import jax, jax.numpy as jnp
from jax import lax
import numpy as np

D_MODEL = 1024
BATCH = 4
SEQ = 4096
DEPTH = 2
DEC_BATCH = 2
DEC_SEQ = 8192
PAST_LEN = 128

GRID_W = 64
A_HEADS = 4
A_DK = 32
A_DV = 64
A_W = A_HEADS * A_DV
A_KW = A_HEADS * A_DK
GLA_RANK = 16
GLA_GATE_NORM = 16.0
GLA_CHUNK = 64
B_GROUPS = 4
B_GC = 64
B_W = B_GROUPS * B_GC
C_HEADS = 8
C_KV = 2
C_HD = 64
C_W = C_HEADS * C_HD
C_KVW = C_KV * C_HD
Q_BLOCK = 128
ROPE_AXIS_DIM = C_HD // 2
ROPE_THETA = 10000.0
D_GROUPS = 4
D_GC = 64
D_W = D_GROUPS * D_GC
SGU_CHUNK = 128
D_MIX = A_W + B_W + C_W + D_W
IN_SPLITS = (A_KW, A_KW, A_W, 2 * GLA_RANK, B_W, C_W, C_KVW, C_KVW, D_W, D_W, D_MIX)
D_IN = 2 * A_KW + A_W + 2 * GLA_RANK + B_W + C_W + 2 * C_KVW + 2 * D_W + D_MIX
EPS = 1e-6

kernel_name = "hymba_style_bidir_hybrid_encoder"


def rms_norm(x, g):
    xf = x.astype(jnp.float32)
    y = xf * lax.rsqrt(jnp.mean(xf * xf, axis=-1, keepdims=True) + EPS)
    return (y * g.astype(jnp.float32)).astype(x.dtype)


def gla_chunked(q, k, v, g, strict):
    bsz, n, h, dk = q.shape
    dv = v.shape[-1]
    nc = n // GLA_CHUNK

    def chunks(t):
        return t.astype(jnp.float32).reshape(bsz, nc, GLA_CHUNK, h, t.shape[-1]).transpose(1, 0, 3, 2, 4)

    qc, kc, vc, gc = chunks(q), chunks(k), chunks(v), chunks(g)
    mask = jnp.tril(jnp.ones((GLA_CHUNK, GLA_CHUNK), dtype=bool), k=-1 if strict else 0)

    def step(state, inp):
        qi, ki, vi, gi = inp
        b = jnp.cumsum(gi, axis=-2)
        inter = jnp.einsum('bhtd,bhdv->bhtv', qi * jnp.exp(b), state)
        diff = b[:, :, :, None, :] - b[:, :, None, :, :]
        decay = jnp.exp(jnp.where(mask[:, :, None], diff, -jnp.inf))
        scores = jnp.einsum('bhtd,bhsd,bhtsd->bhts', qi, ki, decay)
        intra = jnp.einsum('bhts,bhsv->bhtv', scores, vi)
        b_last = b[:, :, -1:, :]
        state = jnp.exp(b_last[:, :, 0, :])[..., None] * state + jnp.einsum(
            'bhsd,bhsv->bhdv', ki * jnp.exp(b_last - b), vi)
        return state, inter + intra

    state0 = jnp.zeros((bsz, h, dk, dv), jnp.float32)
    _, out = lax.scan(step, state0, (qc, kc, vc, gc))
    return out.transpose(1, 0, 3, 2, 4).reshape(bsz, n, h, dv)


def gla_branch(q, k, v, lr, wg2_f, bg_f, wg2_b, bg_b, onorm_g):
    bsz, n, _ = q.shape
    q = q.reshape(bsz, n, A_HEADS, A_DK) * (A_DK ** -0.5)
    k = k.reshape(bsz, n, A_HEADS, A_DK)
    v = v.reshape(bsz, n, A_HEADS, A_DV)
    lr_f, lr_b = jnp.split(lr, 2, axis=-1)

    def log_gate(lr_d, w2, b2):
        logits = (lr_d @ w2 + b2).astype(jnp.float32)
        return (jax.nn.log_sigmoid(logits) / GLA_GATE_NORM).reshape(bsz, n, A_HEADS, A_DK)

    o_f = gla_chunked(q, k, v, log_gate(lr_f, wg2_f, bg_f), strict=False)
    flip = lambda t: jnp.flip(t, axis=1)
    o_b = flip(gla_chunked(flip(q), flip(k), flip(v), flip(log_gate(lr_b, wg2_b, bg_b)), strict=True))
    o = rms_norm(o_f + o_b, onorm_g)
    return o.reshape(bsz, n, A_W).astype(v.dtype)


def fnet_branch(u, fnet_w):
    bsz, n, _ = u.shape
    uf = u.astype(jnp.float32).reshape(bsz, n, B_GROUPS, B_GC)
    mixed = jnp.real(jnp.fft.fft2(uf, axes=(1, 3), norm="ortho"))
    return mixed.reshape(bsz, n, B_W).astype(u.dtype) @ fnet_w


def axial_rope_angles(n_tokens):
    rows = n_tokens // GRID_W
    row = jnp.repeat(jnp.arange(rows, dtype=jnp.float32), GRID_W)
    col = jnp.tile(jnp.arange(GRID_W, dtype=jnp.float32), rows)
    freqs = ROPE_THETA ** (-jnp.arange(0, ROPE_AXIS_DIM, 2, dtype=jnp.float32) / ROPE_AXIS_DIM)
    ang = jnp.concatenate([row[:, None] * freqs, col[:, None] * freqs], axis=-1)
    return jnp.cos(ang), jnp.sin(ang)


def apply_rope(x, cos, sin):
    xf = x.astype(jnp.float32).reshape(*x.shape[:-1], C_HD // 2, 2)
    x0, x1 = xf[..., 0], xf[..., 1]
    c = cos[None, :, None, :]
    s = sin[None, :, None, :]
    out = jnp.stack([x0 * c - x1 * s, x0 * s + x1 * c], axis=-1).reshape(x.shape)
    return out.astype(x.dtype)


def attention_branch(q, k, v, qn_g, kn_g):
    bsz, n, _ = q.shape
    nb = n // Q_BLOCK
    grp = C_HEADS // C_KV
    q = rms_norm(q.reshape(bsz, n, C_HEADS, C_HD), qn_g)
    k = rms_norm(k.reshape(bsz, n, C_KV, C_HD), kn_g)
    v = v.reshape(bsz, n, C_KV, C_HD)
    cos, sin = axial_rope_angles(n)
    q = apply_rope(q, cos, sin)
    k = apply_rope(k, cos, sin)
    qb = q.reshape(bsz, nb, Q_BLOCK, C_KV, grp, C_HD).transpose(1, 0, 3, 4, 2, 5)
    kt = k.transpose(0, 2, 1, 3)
    vt = v.transpose(0, 2, 1, 3)
    scale = C_HD ** -0.5

    def attend_block(qblk):
        s = jnp.einsum('bkgqd,bksd->bkgqs', qblk, kt).astype(jnp.float32) * scale
        p = jax.nn.softmax(s, axis=-1)
        return jnp.einsum('bkgqs,bksd->bkgqd', p.astype(vt.dtype), vt)

    o = lax.map(attend_block, qb)
    return o.transpose(1, 0, 4, 2, 3, 5).reshape(bsz, n, C_W)


def sgu_branch(u, v, norm_g, w_s, b_s):
    bsz, n, _ = u.shape
    nch = n // SGU_CHUNK
    vn = rms_norm(v, norm_g).reshape(bsz, nch, SGU_CHUNK, D_GROUPS, D_GC)
    mixed = jnp.einsum('gts,bnsgc->bntgc', w_s, vn) + b_s.T[None, None, :, :, None]
    return u * mixed.reshape(bsz, n, D_W)


def hybrid_layer(x, c, ada_w, ada_b, pre_g, post_g, w_in, gla_wg2_f, gla_bg_f, gla_wg2_b, gla_bg_b,
                 gla_onorm_g, fnet_w, q_norm_g, k_norm_g, sgu_norm_g, sgu_w, sgu_b, w_out):
    shift, scale, gate = jnp.split(jax.nn.silu(c) @ ada_w + ada_b, 3, axis=-1)
    h = rms_norm(x, pre_g) * (1 + scale[:, None, :]) + shift[:, None, :]
    proj = h @ w_in
    splits = [int(i) for i in np.cumsum(IN_SPLITS)[:-1]]
    a_q, a_k, a_v, a_lr, b_u, c_q, c_k, c_v, d_u, d_v, z = jnp.split(proj, splits, axis=-1)
    out_a = gla_branch(a_q, a_k, a_v, a_lr, gla_wg2_f, gla_bg_f, gla_wg2_b, gla_bg_b, gla_onorm_g)
    out_b = fnet_branch(b_u, fnet_w)
    out_c = attention_branch(c_q, c_k, c_v, q_norm_g, k_norm_g)
    out_d = sgu_branch(d_u, d_v, sgu_norm_g, sgu_w, sgu_b)
    mixed = jnp.concatenate([out_a, out_b, out_c, out_d], axis=-1) * jax.nn.silu(z)
    y = rms_norm(mixed @ w_out, post_g)
    return x + gate[:, None, :] * y


def setup_inputs(seed: int = 0) -> dict:
    key = jax.random.key(seed)
    ks = jax.random.split(key, 21)
    f32 = jnp.float32
    nrm = lambda k, shape, s: jax.random.normal(k, shape, f32) * s
    return {
        "x_prompt": nrm(ks[0], (BATCH, SEQ, D_MODEL), 1.0),
        "x_sample": nrm(ks[1], (DEC_BATCH, DEC_SEQ, D_MODEL), 1.0),
        "c_prompt": nrm(ks[2], (BATCH, D_MODEL), 1.0),
        "c_sample": nrm(ks[3], (DEC_BATCH, D_MODEL), 1.0),
        "ada_w": nrm(ks[4], (DEPTH, D_MODEL, 3 * D_MODEL), 0.5 * D_MODEL ** -0.5),
        "ada_b": nrm(ks[5], (DEPTH, 3 * D_MODEL), 0.02),
        "norm_pre_g": 1.0 + nrm(ks[6], (DEPTH, D_MODEL), 0.02),
        "norm_post_g": 1.0 + nrm(ks[7], (DEPTH, D_MODEL), 0.02),
        "w_in": nrm(ks[8], (DEPTH, D_MODEL, D_IN), D_MODEL ** -0.5),
        "gla_wg2_f": nrm(ks[9], (DEPTH, GLA_RANK, A_KW), GLA_RANK ** -0.5),
        "gla_bg_f": nrm(ks[10], (DEPTH, A_KW), 0.1),
        "gla_wg2_b": nrm(ks[11], (DEPTH, GLA_RANK, A_KW), GLA_RANK ** -0.5),
        "gla_bg_b": nrm(ks[12], (DEPTH, A_KW), 0.1),
        "gla_onorm_g": 1.0 + nrm(ks[13], (DEPTH, A_DV), 0.02),
        "fnet_w": nrm(ks[14], (DEPTH, B_W, B_W), B_W ** -0.5),
        "q_norm_g": 1.0 + nrm(ks[15], (DEPTH, C_HD), 0.02),
        "k_norm_g": 1.0 + nrm(ks[16], (DEPTH, C_HD), 0.02),
        "sgu_norm_g": 1.0 + nrm(ks[17], (DEPTH, D_W), 0.02),
        "sgu_w": nrm(ks[18], (DEPTH, D_GROUPS, SGU_CHUNK, SGU_CHUNK), SGU_CHUNK ** -0.5),
        "sgu_b": 1.0 + nrm(ks[19], (DEPTH, D_GROUPS, SGU_CHUNK), 0.02),
        "w_out": nrm(ks[20], (DEPTH, D_MIX, D_MODEL), D_MIX ** -0.5),
    }


def reference(x_prompt, x_sample, c_prompt, c_sample, ada_w, ada_b, norm_pre_g, norm_post_g, w_in,
              gla_wg2_f, gla_bg_f, gla_wg2_b, gla_bg_b, gla_onorm_g, fnet_w, q_norm_g, k_norm_g,
              sgu_norm_g, sgu_w, sgu_b, w_out):
    y_prompt = x_prompt
    y_sample = x_sample
    for l in range(DEPTH):
        layer_params = (ada_w[l], ada_b[l], norm_pre_g[l], norm_post_g[l], w_in[l],
                        gla_wg2_f[l], gla_bg_f[l], gla_wg2_b[l], gla_bg_b[l], gla_onorm_g[l],
                        fnet_w[l], q_norm_g[l], k_norm_g[l], sgu_norm_g[l], sgu_w[l], sgu_b[l], w_out[l])
        y_prompt = hybrid_layer(y_prompt, c_prompt, *layer_params)
        y_sample = hybrid_layer(y_sample, c_sample, *layer_params)
    return (y_prompt, y_sample)
```

```python
import math

import numpy as np
import jax
import jax.numpy as jnp
from jax import lax
from jax.experimental import pallas as pl
from jax.experimental.pallas import tpu as pltpu

F32 = jnp.float32
BF16 = jnp.bfloat16
HIGHEST = lax.Precision.HIGHEST

D_MODEL = 1024
DEPTH = 2
GRID_W = 64
A_HEADS, A_DK, A_DV = 4, 32, 64
A_W, A_KW = A_HEADS * A_DV, A_HEADS * A_DK
GLA_RANK = 16
GLA_GATE_NORM = 16.0
B_GROUPS, B_GC = 4, 64
B_W = B_GROUPS * B_GC
C_HEADS, C_KV, C_HD = 8, 2, 64
C_GRP = C_HEADS // C_KV
C_W, C_KVW = C_HEADS * C_HD, C_KV * C_HD
ROPE_THETA = 10000.0
D_GROUPS, D_GC = 4, 64
D_W = D_GROUPS * D_GC
SGU_CHUNK = 128
D_MIX = A_W + B_W + C_W + D_W
EPS = 1e-6
LOG2E = 1.4426950408889634

LANES = 128
VMEM_LIMIT_BYTES = 56 * 1024 * 1024

GQ, GK, GV, FU, AQ, AK, AV, DU, DV, ZZ, LR = 0, 128, 256, 512, 768, 1280, 1408, 1664, 1920, 2176, 3456
W_PACKED = 3584
AV_W = 2 * LANES
ONES_LANE = C_HD

TM_IN = 512
TM_OUT = 512
GLA_SUB = 16
GLA_GRP = 128
GLA_SEG = 512
FFT_N2 = 64
FFT_NB = 8
FFT_K1 = 8
ATT_TQ = 256
ATT_TK = 1024


def _params(*sem):
    return pltpu.CompilerParams(dimension_semantics=sem, vmem_limit_bytes=VMEM_LIMIT_BYTES)


def _full(shape):
    n = len(shape)
    return pl.BlockSpec(shape, lambda *_: (0,) * n)


def _adaln_kernel(c_ref, w_ref, b_ref, o_ref):
    c = c_ref[...]
    sc = c / (1.0 + jnp.exp(-c))
    o_ref[0] = jnp.dot(sc, w_ref[0], precision=HIGHEST, preferred_element_type=F32) + b_ref[0]


def _adaln(c_all, ada_w, ada_b):
    rows = c_all.shape[0]
    ncol = 3 * D_MODEL // D_MODEL
    return pl.pallas_call(
        _adaln_kernel,
        out_shape=jax.ShapeDtypeStruct((DEPTH, rows, 3 * D_MODEL), F32),
        grid=(DEPTH, ncol),
        in_specs=[
            pl.BlockSpec((rows, D_MODEL), lambda l, j: (0, 0)),
            pl.BlockSpec((1, D_MODEL, D_MODEL), lambda l, j: (l, 0, j)),
            pl.BlockSpec((1, 1, D_MODEL), lambda l, j: (l, 0, j)),
        ],
        out_specs=pl.BlockSpec((1, rows, D_MODEL), lambda l, j: (l, 0, j)),
        compiler_params=_params("arbitrary", "arbitrary"),
        name="adaln",
    )(c_all, ada_w, ada_b.reshape(DEPTH, 1, 3 * D_MODEL))


def _rope(x, cos, sin_signed):
    width = x.shape[-1]
    lane = lax.broadcasted_iota(jnp.int32, x.shape, 1)
    first_half = (lane % C_HD) < (C_HD // 2)
    swapped = jnp.where(first_half, pltpu.roll(x, width - C_HD // 2, 1), pltpu.roll(x, C_HD // 2, 1))
    reps = width // LANES
    cos_w = jnp.concatenate([cos] * reps, axis=-1) if reps > 1 else cos
    sin_w = jnp.concatenate([sin_signed] * reps, axis=-1) if reps > 1 else sin_signed
    return x * cos_w + swapped * sin_w


def _inproj_kernel(x_ref, mod_ref, preg_ref, w_ref, wg2_ref, bg_ref, qg_ref, kg_ref, sg_ref,
                   cos_ref, sin_ref, bd_ref, vone_ref,
                   gq_ref, gk_ref, gv_ref, gf_ref, gb_ref, fu_ref, aq_ref, kt_ref, va_ref,
                   du_ref, dvn_ref, sz_ref):
    x = x_ref[0]
    ms = jnp.mean(x * x, axis=-1, keepdims=True)
    shift = mod_ref[0, :, 0:D_MODEL]
    scale = mod_ref[0, :, D_MODEL:2 * D_MODEL]
    h = (x * lax.rsqrt(ms + EPS)) * preg_ref[...] * (1.0 + scale) + shift
    hb = h.astype(BF16)

    def seg(off, width):
        return jnp.dot(hb, w_ref[:, off:off + width], preferred_element_type=F32)

    gq_ref[0] = (seg(GQ, A_KW) * (A_DK ** -0.5)).astype(BF16)
    gk_ref[0] = seg(GK, A_KW).astype(BF16)
    gv_ref[0] = seg(GV, A_W).astype(BF16)
    lr = seg(LR, LANES).astype(BF16)
    logits = jnp.dot(lr, wg2_ref[...], preferred_element_type=F32) + bg_ref[...]
    logg = (jnp.minimum(logits, 0.0) - jnp.log1p(jnp.exp(-jnp.abs(logits)))) * (1.0 / GLA_GATE_NORM)
    gf_ref[0] = logg[:, :A_KW]
    gb_ref[0] = logg[:, A_KW:]

    fu_ref[0] = seg(FU, B_W).astype(BF16)

    cos = cos_ref[...]
    sin = sin_ref[...]
    q = seg(AQ, C_W)
    qms = jnp.dot((q * q).astype(BF16), bd_ref[...], preferred_element_type=F32)
    qn = q * lax.rsqrt(qms + EPS) * qg_ref[...]
    aq_ref[0] = (_rope(qn, cos, sin) * (C_HD ** -0.5 * LOG2E)).astype(BF16)

    k = seg(AK, C_KVW)
    kms = jnp.dot((k * k).astype(BF16), bd_ref[0:C_KVW, 0:C_KVW], preferred_element_type=F32)
    kn = k * lax.rsqrt(kms + EPS) * kg_ref[...]
    kt = _rope(kn, cos, sin).T.astype(BF16)
    for g in range(C_KV):
        for r in range(C_GRP):
            kt_ref[0, g, r * C_HD:(r + 1) * C_HD, :] = kt[g * C_HD:(g + 1) * C_HD, :]

    va = (seg(AV, AV_W) + vone_ref[...]).astype(BF16)
    va_ref[0, 0] = va[:, :LANES]
    va_ref[0, 1] = va[:, LANES:]

    du_ref[0] = seg(DU, D_W).astype(BF16)
    dv = seg(DV, D_W)
    dms = jnp.mean(dv * dv, axis=-1, keepdims=True)
    dvn_ref[0] = (dv * lax.rsqrt(dms + EPS) * sg_ref[...]).astype(BF16)

    z = seg(ZZ, D_MIX)
    sz_ref[0] = (z / (1.0 + jnp.exp(-z))).astype(BF16)


def _inproj(x, mod, lp, tabs):
    bsz, n, _ = x.shape
    tm = min(TM_IN, n)
    grid = (bsz, n // tm)
    tok = lambda w: pl.BlockSpec((1, tm, w), lambda b, i: (b, i, 0))
    out_shapes = (
        jax.ShapeDtypeStruct((bsz, n, A_KW), BF16),
        jax.ShapeDtypeStruct((bsz, n, A_KW), BF16),
        jax.ShapeDtypeStruct((bsz, n, A_W), BF16),
        jax.ShapeDtypeStruct((bsz, n, A_KW), F32),
        jax.ShapeDtypeStruct((bsz, n, A_KW), F32),
        jax.ShapeDtypeStruct((bsz, n, B_W), BF16),
        jax.ShapeDtypeStruct((bsz, n, C_W), BF16),
        jax.ShapeDtypeStruct((bsz, C_KV, C_GRP * C_HD, n), BF16),
        jax.ShapeDtypeStruct((bsz, C_KV, n, LANES), BF16),
        jax.ShapeDtypeStruct((bsz, n, D_W), BF16),
        jax.ShapeDtypeStruct((bsz, n, D_W), BF16),
        jax.ShapeDtypeStruct((bsz, n, D_MIX), BF16),
    )
    out_specs = (
        tok(A_KW), tok(A_KW), tok(A_W), tok(A_KW), tok(A_KW), tok(B_W), tok(C_W),
        pl.BlockSpec((1, C_KV, C_GRP * C_HD, tm), lambda b, i: (b, 0, 0, i)),
        pl.BlockSpec((1, C_KV, tm, LANES), lambda b, i: (b, 0, i, 0)),
        tok(D_W), tok(D_W), tok(D_MIX),
    )
    in_specs = [
        tok(D_MODEL),
        pl.BlockSpec((1, 1, 3 * D_MODEL), lambda b, i: (b, 0, 0)),
        _full((1, D_MODEL)),
        _full((D_MODEL, W_PACKED)),
        _full((LANES, 2 * A_KW)),
        _full((1, 2 * A_KW)),
        _full((1, C_W)),
        _full((1, C_KVW)),
        _full((1, D_W)),
        pl.BlockSpec((tm, LANES), lambda b, i: (i, 0)),
        pl.BlockSpec((tm, LANES), lambda b, i: (i, 0)),
        _full((C_W, C_W)),
        _full((1, 2 * LANES)),
    ]
    return pl.pallas_call(
        _inproj_kernel, out_shape=out_shapes, grid=grid, in_specs=in_specs, out_specs=out_specs,
        compiler_params=_params("parallel", "parallel"), name="inproj",
    )(x, mod, lp["pre_g"], lp["w_in"], lp["wg2"], lp["bg"], lp["qg"], lp["kg"], lp["sg"],
      tabs["cos"], tabs["sin"], lp["bd_q"], lp["vone"])


def _gla_direction(q_ref, k_ref, v_ref, g_ref, out_ref, state_ref,
                   qt_ref, kh_ref, dec_ref, ap_ref, kp_ref, vp_ref, o_ref, reverse):
    seg = q_ref.shape[1]
    pad = GLA_SUB
    n_grp = seg // GLA_GRP
    n_sub = seg // GLA_SUB

    r_i = lax.broadcasted_iota(jnp.int32, (GLA_GRP, GLA_GRP), 0)
    c_i = lax.broadcasted_iota(jnp.int32, (GLA_GRP, GLA_GRP), 1)
    same = (r_i // GLA_SUB) == (c_i // GLA_SUB)
    tri = (c_i >= r_i) if reverse else (c_i <= r_i)
    cum_mat = jnp.where(same & tri, 1.0, 0.0).astype(F32)
    tot_mat = jnp.where(same, 1.0, 0.0).astype(F32)
    row_in_sub = lax.broadcasted_iota(jnp.int32, (GLA_GRP, A_KW), 0) % GLA_SUB
    edge_row = GLA_SUB - 1 if reverse else 0
    hsum = jnp.where(lax.broadcasted_iota(jnp.int32, (A_KW, A_W), 0) // A_DK
                     == lax.broadcasted_iota(jnp.int32, (A_KW, A_W), 1) // A_DV, 1.0, 0.0).astype(BF16)
    smask = (lax.broadcasted_iota(jnp.int32, (A_W, A_KW), 0) // A_DV
             == lax.broadcasted_iota(jnp.int32, (A_W, A_KW), 1) // A_DK)

    zpad_k = jnp.zeros((pad, A_KW), F32)
    zpad_v = jnp.zeros((pad, A_W), F32)
    ap_ref[0:pad, :] = zpad_k
    ap_ref[pad + seg:pad + seg + pad, :] = zpad_k
    kp_ref[0:pad, :] = zpad_k
    kp_ref[pad + seg:pad + seg + pad, :] = zpad_k
    vp_ref[0:pad, :] = zpad_v
    vp_ref[pad + seg:pad + seg + pad, :] = zpad_v

    for r in range(n_grp):
        rows = slice(r * GLA_GRP, (r + 1) * GLA_GRP)
        prow = slice(pad + r * GLA_GRP, pad + (r + 1) * GLA_GRP)
        g = g_ref[0, rows, :]
        cum = jnp.dot(cum_mat, g, precision=HIGHEST, preferred_element_type=F32)
        tot = jnp.dot(tot_mat, g, precision=HIGHEST, preferred_element_type=F32)
        q = q_ref[0, rows, :].astype(F32)
        k = k_ref[0, rows, :].astype(F32)
        qt_ref[rows, :] = (q * jnp.exp(cum)).astype(BF16)
        kh_ref[rows, :] = (k * jnp.exp(tot - cum)).astype(BF16)
        dec_ref[rows, :] = jnp.exp(tot)
        ap_ref[prow, :] = jnp.where(row_in_sub == edge_row, 0.0, jnp.exp(g))
        kp_ref[prow, :] = k
        vp_ref[prow, :] = v_ref[0, rows, :].astype(F32)

    sgn = 1 if reverse else -1
    for r in range(n_grp):
        rows = slice(r * GLA_GRP, (r + 1) * GLA_GRP)
        base = pad + r * GLA_GRP
        q = q_ref[0, rows, :].astype(F32)
        if reverse:
            acc = jnp.zeros((GLA_GRP, A_W), F32)
        else:
            p0 = (q * kp_ref[base:base + GLA_GRP, :]).astype(BF16)
            acc = jnp.dot(p0, hsum, preferred_element_type=F32) * vp_ref[base:base + GLA_GRP, :]
        run = q
        for d in range(1, GLA_SUB):
            o_a = base + sgn * (d - 1)
            o_k = base + sgn * d
            run = run * ap_ref[o_a:o_a + GLA_GRP, :]
            p = (run * kp_ref[o_k:o_k + GLA_GRP, :]).astype(BF16)
            acc = acc + jnp.dot(p, hsum, preferred_element_type=F32) * vp_ref[o_k:o_k + GLA_GRP, :]
        o_ref[rows, :] = acc

    def step(j, carry):
        i = (n_sub - 1 - j) if reverse else j
        r0 = pl.multiple_of(i * GLA_SUB, GLA_SUB)
        rows = pl.ds(r0, GLA_SUB)
        s = state_ref[...]
        inter = lax.dot_general(qt_ref[rows, :], s.astype(BF16), (((1,), (1,)), ((), ())),
                                preferred_element_type=F32)
        o_ref[rows, :] = o_ref[rows, :] + inter
        upd = lax.dot_general(v_ref[0, rows, :], kh_ref[rows, :], (((0,), (0,)), ((), ())),
                              preferred_element_type=F32)
        state_ref[...] = s * dec_ref[pl.ds(r0, 1), :] + jnp.where(smask, upd, 0.0)
        return carry

    lax.fori_loop(0, n_sub, step, 0)
    out_ref[0] = o_ref[...].astype(BF16)


def _gla_kernel(qf_ref, kf_ref, vf_ref, gf_ref, qb_ref, kb_ref, vb_ref, gb_ref,
                of_ref, ob_ref,
                sf_ref, sb_ref, qt_ref, kh_ref, dec_ref, ap_ref, kp_ref, vp_ref, o_ref):
    @pl.when(pl.program_id(1) == 0)
    def _():
        sf_ref[...] = jnp.zeros_like(sf_ref)
        sb_ref[...] = jnp.zeros_like(sb_ref)

    scratch = (qt_ref, kh_ref, dec_ref, ap_ref, kp_ref, vp_ref, o_ref)
    _gla_direction(qf_ref, kf_ref, vf_ref, gf_ref, of_ref, sf_ref, *scratch, reverse=False)
    _gla_direction(qb_ref, kb_ref, vb_ref, gb_ref, ob_ref, sb_ref, *scratch, reverse=True)


def _gla(gq, gk, gv, gf, gb):
    bsz, n, _ = gq.shape
    seg = min(GLA_SEG, n)
    nseg = n // seg
    fwd = lambda w: pl.BlockSpec((1, seg, w), lambda b, s: (b, s, 0))
    bwd = lambda w: pl.BlockSpec((1, seg, w), lambda b, s: (b, nseg - 1 - s, 0))
    pad = GLA_SUB
    return pl.pallas_call(
        _gla_kernel,
        out_shape=(jax.ShapeDtypeStruct((bsz, n, A_W), BF16), jax.ShapeDtypeStruct((bsz, n, A_W), BF16)),
        grid=(bsz, nseg),
        in_specs=[fwd(A_KW), fwd(A_KW), fwd(A_W), fwd(A_KW), bwd(A_KW), bwd(A_KW), bwd(A_W), bwd(A_KW)],
        out_specs=(fwd(A_W), bwd(A_W)),
        scratch_shapes=[
            pltpu.VMEM((A_W, A_KW), F32), pltpu.VMEM((A_W, A_KW), F32),
            pltpu.VMEM((seg, A_KW), BF16), pltpu.VMEM((seg, A_KW), BF16), pltpu.VMEM((seg, A_KW), F32),
            pltpu.VMEM((seg + 2 * pad, A_KW), F32), pltpu.VMEM((seg + 2 * pad, A_KW), F32),
            pltpu.VMEM((seg + 2 * pad, A_W), F32), pltpu.VMEM((seg, A_W), F32),
        ],
        compiler_params=_params("parallel", "arbitrary"), name="gla",
    )(gq, gk, gv, gf, gq, gk, gv, gb)


def _fft1_kernel(u_ref, a1_ref, cos_ref, sin_ref, t_ref):
    n1 = u_ref.shape[1]
    nb = cos_ref.shape[0]
    t = jnp.dot(a1_ref[...], u_ref[0], preferred_element_type=F32)
    for j in range(nb):
        cols = slice(j * B_W, (j + 1) * B_W)
        tr = t[:n1, cols]
        ti = t[n1:, cols]
        c = jnp.concatenate([cos_ref[j]] * (B_W // LANES), axis=-1)
        s = jnp.concatenate([sin_ref[j]] * (B_W // LANES), axis=-1)
        t_ref[0, 0, :, cols] = (tr * c + ti * s).astype(BF16)
        t_ref[0, 1, :, cols] = (ti * c - tr * s).astype(BF16)


def _fft2_kernel(t_ref, l_ref, cc_ref, cs_ref, w_ref, o_ref):
    k1b, n2 = t_ref.shape[2], t_ref.shape[3]
    half = k1b * n2
    x = t_ref[0].reshape(2 * half, B_W)
    g = jnp.dot(l_ref[...], x, preferred_element_type=F32)
    gr = g[:half].astype(BF16)
    gi = g[half:].astype(BF16)
    mixed = (jnp.dot(gr, cc_ref[...], preferred_element_type=F32)
             + jnp.dot(gi, cs_ref[...], preferred_element_type=F32))
    y = jnp.dot(mixed.astype(BF16), w_ref[...], preferred_element_type=F32)
    o_ref[0] = y.reshape(n2, k1b, B_W)


def _fnet(fu, lp, tabs):
    bsz, n, _ = fu.shape
    n2 = FFT_N2
    n1 = n // n2
    nb = FFT_NB
    k1b = FFT_K1
    u2 = fu.reshape(bsz, n1, n2 * B_W)
    tt = pl.pallas_call(
        _fft1_kernel,
        out_shape=jax.ShapeDtypeStruct((bsz, 2, n1, n2 * B_W), BF16),
        grid=(n2 // nb, bsz),
        in_specs=[
            pl.BlockSpec((1, n1, nb * B_W), lambda j, b: (b, 0, j)),
            _full((2 * n1, n1)),
            pl.BlockSpec((nb, n1, LANES), lambda j, b: (j, 0, 0)),
            pl.BlockSpec((nb, n1, LANES), lambda j, b: (j, 0, 0)),
        ],
        out_specs=pl.BlockSpec((1, 2, n1, nb * B_W), lambda j, b: (b, 0, 0, j)),
        compiler_params=_params("parallel", "parallel"), name="fft_pass1",
    )(u2, tabs["a1"], tabs["tw_cos"], tabs["tw_sin"])
    t5 = tt.reshape(bsz, 2, n1, n2, B_W)
    y = pl.pallas_call(
        _fft2_kernel,
        out_shape=jax.ShapeDtypeStruct((bsz, n2, n1, B_W), F32),
        grid=(bsz, n1 // k1b),
        in_specs=[
            pl.BlockSpec((1, 2, k1b, n2, B_W), lambda b, i: (b, 0, i, 0, 0)),
            _full((2 * k1b * n2, 2 * k1b * n2)),
            _full((B_W, B_W)), _full((B_W, B_W)), _full((B_W, B_W)),
        ],
        out_specs=pl.BlockSpec((1, n2, k1b, B_W), lambda b, i: (b, 0, i, 0)),
        compiler_params=_params("parallel", "parallel"), name="fft_pass2",
    )(t5, tabs["l2"], tabs["chan_cos"], tabs["chan_sin"], lp["fnet_w"])
    return y.reshape(bsz, n, B_W)


def _attn_kernel(q_ref, kt_ref, va_ref, o_ref, qm_ref, m_ref, acc_ref):
    tq = q_ref.shape[1]
    n = kt_ref.shape[3]
    tk = min(ATT_TK, n)
    q = q_ref[0]
    lane = lax.broadcasted_iota(jnp.int32, q.shape, 1)
    for h in range(C_GRP):
        qm_ref[h * tq:(h + 1) * tq, :] = jnp.where(lane // C_HD == h, q, jnp.zeros_like(q))
    m_ref[...] = jnp.full(m_ref.shape, -jnp.inf, F32)
    acc_ref[...] = jnp.zeros(acc_ref.shape, F32)

    def step(c, carry):
        c0 = pl.multiple_of(c * tk, tk)
        kt = kt_ref[0, 0, :, pl.ds(c0, tk)]
        v = va_ref[0, 0, pl.ds(c0, tk), :]
        s = jnp.dot(qm_ref[...], kt, preferred_element_type=F32)
        s_max = s[:, 0:LANES]
        for t in range(1, tk // LANES):
            s_max = jnp.maximum(s_max, s[:, t * LANES:(t + 1) * LANES])
        row_max = jnp.max(s_max, axis=-1, keepdims=True)
        m_old = m_ref[...]
        m_new = jnp.maximum(m_old, row_max)
        p = jnp.exp2(s - jnp.concatenate([m_new] * (tk // LANES), axis=-1))
        alpha = jnp.exp2(m_old - m_new)
        acc_ref[...] = alpha * acc_ref[...] + jnp.dot(p.astype(BF16), v, preferred_element_type=F32)
        m_ref[...] = m_new
        return carry

    lax.fori_loop(0, n // tk, step, 0)
    acc = acc_ref[...]
    inv = 1.0 / acc[:, ONES_LANE:ONES_LANE + 1]
    outs = [(acc[h * tq:(h + 1) * tq, 0:C_HD] * inv[h * tq:(h + 1) * tq]) for h in range(C_GRP)]
    o_ref[0] = jnp.concatenate(outs, axis=-1).astype(BF16)


def _attention(aq, kt4, va):
    bsz, n, _ = aq.shape
    tq = min(ATT_TQ, n)
    return pl.pallas_call(
        _attn_kernel,
        out_shape=jax.ShapeDtypeStruct((bsz, n, C_W), BF16),
        grid=(bsz, C_KV, n // tq),
        in_specs=[
            pl.BlockSpec((1, tq, C_GRP * C_HD), lambda b, g, i: (b, i, g)),
            pl.BlockSpec((1, 1, C_GRP * C_HD, n), lambda b, g, i: (b, g, 0, 0)),
            pl.BlockSpec((1, 1, n, LANES), lambda b, g, i: (b, g, 0, 0)),
        ],
        out_specs=pl.BlockSpec((1, tq, C_GRP * C_HD), lambda b, g, i: (b, i, g)),
        scratch_shapes=[
            pltpu.VMEM((C_GRP * tq, C_GRP * C_HD), BF16),
            pltpu.VMEM((C_GRP * tq, LANES), F32),
            pltpu.VMEM((C_GRP * tq, LANES), F32),
        ],
        compiler_params=_params("parallel", "parallel", "arbitrary"), name="attention",
    )(aq, kt4, va)


def _outproj_kernel(x_ref, mod_ref, of_ref, ob_ref, yb_ref, ao_ref, du_ref, dvn_ref, sz_ref,
                    w_ref, postg_ref, og_ref, bd_ref, sw_ref, sb_ref, o_ref):
    tm = x_ref.shape[1]
    sz = sz_ref[0].astype(F32)

    o = of_ref[0].astype(F32) + ob_ref[0].astype(F32)
    oms = jnp.dot((o * o).astype(BF16), bd_ref[...], preferred_element_type=F32)
    out_a = o * lax.rsqrt(oms + EPS) * og_ref[...]
    y = jnp.dot((out_a * sz[:, 0:A_W]).astype(BF16), w_ref[0:A_W, :], preferred_element_type=F32)

    m_b = (yb_ref[0] * sz[:, A_W:A_W + B_W]).astype(BF16)
    y = y + jnp.dot(m_b, w_ref[A_W:A_W + B_W, :], preferred_element_type=F32)

    m_c = (ao_ref[0].astype(F32) * sz[:, A_W + B_W:A_W + B_W + C_W]).astype(BF16)
    y = y + jnp.dot(m_c, w_ref[A_W + B_W:A_W + B_W + C_W, :], preferred_element_type=F32)

    lane = lax.broadcasted_iota(jnp.int32, (SGU_CHUNK, D_W), 1)
    parts = []
    for c in range(tm // SGU_CHUNK):
        rows = slice(c * SGU_CHUNK, (c + 1) * SGU_CHUNK)
        r = jnp.dot(sw_ref[...], dvn_ref[0, rows, :], preferred_element_type=F32)
        mix = sb_ref[...]
        for g in range(D_GROUPS):
            mix = mix + jnp.where(lane // D_GC == g, r[g * SGU_CHUNK:(g + 1) * SGU_CHUNK, :], 0.0)
        parts.append(du_ref[0, rows, :].astype(F32) * mix)
    out_d = jnp.concatenate(parts, axis=0)
    m_d = (out_d * sz[:, A_W + B_W + C_W:]).astype(BF16)
    y = y + jnp.dot(m_d, w_ref[A_W + B_W + C_W:, :], preferred_element_type=F32)

    yms = jnp.mean(y * y, axis=-1, keepdims=True)
    yn = y * lax.rsqrt(yms + EPS) * postg_ref[...]
    gate = mod_ref[0, :, 2 * D_MODEL:3 * D_MODEL]
    o_ref[0] = x_ref[0] + gate * yn


def _outproj(x, mod, of, ob, yb, ao, du, dvn, sz, lp):
    bsz, n, _ = x.shape
    tm = min(TM_OUT, n)
    tok = lambda w: pl.BlockSpec((1, tm, w), lambda b, i: (b, i, 0))
    return pl.pallas_call(
        _outproj_kernel,
        out_shape=jax.ShapeDtypeStruct((bsz, n, D_MODEL), F32),
        grid=(bsz, n // tm),
        in_specs=[
            tok(D_MODEL),
            pl.BlockSpec((1, 1, 3 * D_MODEL), lambda b, i: (b, 0, 0)),
            tok(A_W), tok(A_W), tok(B_W), tok(C_W), tok(D_W), tok(D_W), tok(D_MIX),
            _full((D_MIX, D_MODEL)), _full((1, D_MODEL)), _full((1, A_W)), _full((A_W, A_W)),
            _full((D_GROUPS * SGU_CHUNK, SGU_CHUNK)), _full((SGU_CHUNK, D_W)),
        ],
        out_specs=tok(D_MODEL),
        compiler_params=_params("parallel", "parallel"), name="outproj",
    )(x, mod, of, ob, yb, ao, du, dvn, sz,
      lp["w_out"], lp["post_g"], lp["og"], lp["bd_a"], lp["sgu_w"], lp["sgu_b"])


def _win_column_map():
    sizes = [A_KW, A_KW, A_W, 2 * GLA_RANK, B_W, C_W, C_KVW, C_KVW, D_W, D_W, D_MIX]
    off = np.concatenate([[0], np.cumsum(sizes)])
    aq0, ak0, av0, lr0, bu0, cq0, ck0, cv0, du0, dv0, z0 = [int(o) for o in off[:-1]]
    deint = np.concatenate([np.arange(0, C_HD, 2), np.arange(1, C_HD, 2)])
    cols = []
    cols += list(range(aq0, aq0 + A_KW)) + list(range(ak0, ak0 + A_KW)) + list(range(av0, av0 + A_W))
    cols += list(range(bu0, bu0 + B_W))
    for h in range(C_HEADS):
        cols += list(cq0 + C_HD * h + deint)
    for h in range(C_KV):
        cols += list(ck0 + C_HD * h + deint)
    for h in range(C_KV):
        cols += list(range(cv0 + C_HD * h, cv0 + C_HD * (h + 1))) + [-1] * (LANES - C_HD)
    cols += list(range(du0, du0 + D_W)) + list(range(dv0, dv0 + D_W)) + list(range(z0, z0 + D_MIX))
    cols += list(range(lr0, lr0 + 2 * GLA_RANK)) + [-1] * (LANES - 2 * GLA_RANK)
    cols = np.asarray(cols, np.int32)
    assert cols.shape[0] == W_PACKED
    return cols, deint


def _block_diag_mean(width, group):
    idx = np.arange(width) // group
    return jnp.asarray((idx[:, None] == idx[None, :]).astype(np.float32) / group, BF16)


def _layer_params(l, ada_unused, norm_pre_g, norm_post_g, w_in, gla_wg2_f, gla_bg_f, gla_wg2_b, gla_bg_b,
                  gla_onorm_g, fnet_w, q_norm_g, k_norm_g, sgu_norm_g, sgu_w, sgu_b, w_out):
    cols, deint = _win_column_map()
    w = jnp.take(w_in[l], jnp.asarray(np.maximum(cols, 0)), axis=1)
    w = jnp.where(jnp.asarray(cols >= 0)[None, :], w, 0.0).astype(BF16)
    wg2 = jnp.zeros((LANES, 2 * A_KW), F32)
    wg2 = wg2.at[0:GLA_RANK, 0:A_KW].set(gla_wg2_f[l])
    wg2 = wg2.at[GLA_RANK:2 * GLA_RANK, A_KW:].set(gla_wg2_b[l])
    vone = np.zeros((1, 2 * LANES), np.float32)
    vone[0, ONES_LANE] = 1.0
    vone[0, LANES + ONES_LANE] = 1.0
    return {
        "pre_g": norm_pre_g[l].reshape(1, D_MODEL),
        "post_g": norm_post_g[l].reshape(1, D_MODEL),
        "w_in": w,
        "wg2": wg2.astype(BF16),
        "bg": jnp.concatenate([gla_bg_f[l], gla_bg_b[l]]).reshape(1, 2 * A_KW),
        "qg": jnp.tile(q_norm_g[l][deint], C_HEADS).reshape(1, C_W),
        "kg": jnp.tile(k_norm_g[l][deint], C_KV).reshape(1, C_KVW),
        "sg": sgu_norm_g[l].reshape(1, D_W),
        "bd_q": _block_diag_mean(C_W, C_HD),
        "bd_a": _block_diag_mean(A_W, A_DV),
        "vone": jnp.asarray(vone),
        "og": jnp.tile(gla_onorm_g[l], A_HEADS).reshape(1, A_W),
        "fnet_w": fnet_w[l].astype(BF16),
        "sgu_w": sgu_w[l].reshape(D_GROUPS * SGU_CHUNK, SGU_CHUNK).astype(BF16),
        "sgu_b": jnp.repeat(sgu_b[l].T, D_GC, axis=1),
        "w_out": w_out[l].astype(BF16),
    }


def _seq_tables(n):
    rows = n // GRID_W
    row = jnp.repeat(jnp.arange(rows, dtype=F32), GRID_W)
    col = jnp.tile(jnp.arange(GRID_W, dtype=F32), rows)
    rope_axis = C_HD // 2
    freqs = ROPE_THETA ** (-jnp.arange(0, rope_axis, 2, dtype=F32) / rope_axis)
    ang = jnp.concatenate([row[:, None] * freqs, col[:, None] * freqs], axis=-1)
    cos, sin = jnp.cos(ang), jnp.sin(ang)
    cos_t = jnp.tile(jnp.concatenate([cos, cos], axis=-1), (1, LANES // C_HD))
    sin_t = jnp.tile(jnp.concatenate([-sin, sin], axis=-1), (1, LANES // C_HD))

    n2 = FFT_N2
    n1 = n // n2

    def dft_angles(rows, cols, period):
        prod = (jnp.arange(rows, dtype=jnp.int32)[:, None] * jnp.arange(cols, dtype=jnp.int32)[None, :]) % period
        return prod.astype(F32) * (2.0 * math.pi / period)

    a1_ang = dft_angles(n1, n1, n1)
    a1 = jnp.concatenate([jnp.cos(a1_ang), -jnp.sin(a1_ang)], axis=0)
    tw_ang = dft_angles(n2, n1, n)
    scale = 1.0 / math.sqrt(n * B_GC)
    tw_cos = jnp.broadcast_to((jnp.cos(tw_ang) * scale)[:, :, None], (n2, n1, LANES))
    tw_sin = jnp.broadcast_to((jnp.sin(tw_ang) * scale)[:, :, None], (n2, n1, LANES))
    ang2 = dft_angles(n2, n2, n2)
    c2, s2 = jnp.cos(ang2), jnp.sin(ang2)
    eye = jnp.eye(FFT_K1, dtype=F32)
    blk = lambda m: (m[:, None, None, :] * eye[None, :, :, None]).reshape(n2 * FFT_K1, FFT_K1 * n2)
    l2 = jnp.block([[blk(c2), blk(s2)], [blk(-s2), blk(c2)]])
    angc = dft_angles(B_GC, B_GC, B_GC)
    grp = jnp.eye(B_GROUPS, dtype=F32)
    kron = lambda m: (grp[:, None, :, None] * m[None, :, None, :]).reshape(B_W, B_W)
    chan_cos = kron(jnp.cos(angc))
    chan_sin = kron(jnp.sin(angc))
    return {
        "cos": cos_t, "sin": sin_t,
        "a1": a1.astype(BF16),
        "tw_cos": tw_cos, "tw_sin": tw_sin,
        "l2": l2.astype(BF16),
        "chan_cos": chan_cos.astype(BF16), "chan_sin": chan_sin.astype(BF16),
    }


def _layer(x, mod, lp, tabs):
    gq, gk, gv, gf, gb, fu, aq, kt4, va, du, dvn, sz = _inproj(x, mod, lp, tabs)
    of, ob = _gla(gq, gk, gv, gf, gb)
    yb = _fnet(fu, lp, tabs)
    ao = _attention(aq, kt4, va)
    return _outproj(x, mod, of, ob, yb, ao, du, dvn, sz, lp)


def kernel(x_prompt, x_sample, c_prompt, c_sample, ada_w, ada_b, norm_pre_g, norm_post_g, w_in,
           gla_wg2_f, gla_bg_f, gla_wg2_b, gla_bg_b, gla_onorm_g, fnet_w, q_norm_g, k_norm_g,
           sgu_norm_g, sgu_w, sgu_b, w_out):
    bp, bs = x_prompt.shape[0], x_sample.shape[0]
    pad_rows = (-(bp + bs)) % 8
    c_all = jnp.concatenate([c_prompt, c_sample, jnp.zeros((pad_rows, D_MODEL), F32)], axis=0)
    mod = _adaln(c_all, ada_w, ada_b)
    tabs_p = _seq_tables(x_prompt.shape[1])
    tabs_s = _seq_tables(x_sample.shape[1])
    y_prompt, y_sample = x_prompt, x_sample
    for l in range(DEPTH):
        lp = _layer_params(l, None, norm_pre_g, norm_post_g, w_in, gla_wg2_f, gla_bg_f, gla_wg2_b,
                           gla_bg_b, gla_onorm_g, fnet_w, q_norm_g, k_norm_g, sgu_norm_g, sgu_w,
                           sgu_b, w_out)
        mod_p = mod[l, 0:bp].reshape(bp, 1, 3 * D_MODEL)
        mod_s = mod[l, bp:bp + bs].reshape(bs, 1, 3 * D_MODEL)
        y_prompt = _layer(y_prompt, mod_p, lp, tabs_p)
        y_sample = _layer(y_sample, mod_s, lp, tabs_s)
    return (y_prompt, y_sample)
```

```python
import math

import numpy as np
import jax
import jax.numpy as jnp
from jax import lax
from jax.experimental import pallas as pl
from jax.experimental.pallas import tpu as pltpu

F32 = jnp.float32
BF16 = jnp.bfloat16
HIGHEST = lax.Precision.HIGHEST

D_MODEL = 1024
DEPTH = 2
GRID_W = 64
A_HEADS, A_DK, A_DV = 4, 32, 64
A_W, A_KW = A_HEADS * A_DV, A_HEADS * A_DK
GLA_RANK = 16
GLA_GATE_NORM = 16.0
B_GROUPS, B_GC = 4, 64
B_W = B_GROUPS * B_GC
C_HEADS, C_KV, C_HD = 8, 2, 64
C_GRP = C_HEADS // C_KV
C_W, C_KVW = C_HEADS * C_HD, C_KV * C_HD
ROPE_THETA = 10000.0
D_GROUPS, D_GC = 4, 64
D_W = D_GROUPS * D_GC
SGU_CHUNK = 128
D_MIX = A_W + B_W + C_W + D_W
EPS = 1e-6
LOG2E = 1.4426950408889634

LANES = 128
VMEM_LIMIT_BYTES = 56 * 1024 * 1024

GQ, GK, GV, FU, AQ, AK, AV, DU, DV, ZZ, LR = 0, 128, 256, 512, 768, 1280, 1408, 1664, 1920, 2176, 3456
W_PACKED = 3584
AV_W = 2 * LANES
ONES_LANE = C_HD

TM_IN = 512
TM_OUT = 512
GLA_SUB = 16
GLA_BAND_TILE = 64
GLA_GRP = 128
GLA_SEG = 512
FFT_N2 = 64
FFT_NB = 8
FFT_K1 = 8
ATT_TQ = 256
ATT_TK = 1024


def _params(*sem):
    return pltpu.CompilerParams(dimension_semantics=sem, vmem_limit_bytes=VMEM_LIMIT_BYTES)


def _full(shape):
    n = len(shape)
    return pl.BlockSpec(shape, lambda *_: (0,) * n)


def _adaln_kernel(c_ref, w_ref, b_ref, o_ref):
    c = c_ref[...]
    sc = c / (1.0 + jnp.exp(-c))
    o_ref[0] = jnp.dot(sc, w_ref[0], precision=HIGHEST, preferred_element_type=F32) + b_ref[0]


def _adaln(c_all, ada_w, ada_b):
    rows = c_all.shape[0]
    ncol = 3 * D_MODEL // D_MODEL
    return pl.pallas_call(
        _adaln_kernel,
        out_shape=jax.ShapeDtypeStruct((DEPTH, rows, 3 * D_MODEL), F32),
        grid=(DEPTH, ncol),
        in_specs=[
            pl.BlockSpec((rows, D_MODEL), lambda l, j: (0, 0)),
            pl.BlockSpec((1, D_MODEL, D_MODEL), lambda l, j: (l, 0, j)),
            pl.BlockSpec((1, 1, D_MODEL), lambda l, j: (l, 0, j)),
        ],
        out_specs=pl.BlockSpec((1, rows, D_MODEL), lambda l, j: (l, 0, j)),
        compiler_params=_params("arbitrary", "arbitrary"),
        name="adaln",
    )(c_all, ada_w, ada_b.reshape(DEPTH, 1, 3 * D_MODEL))


def _rope(x, cos, sin_signed):
    width = x.shape[-1]
    lane = lax.broadcasted_iota(jnp.int32, x.shape, 1)
    first_half = (lane % C_HD) < (C_HD // 2)
    swapped = jnp.where(first_half, pltpu.roll(x, width - C_HD // 2, 1), pltpu.roll(x, C_HD // 2, 1))
    reps = width // LANES
    cos_w = jnp.concatenate([cos] * reps, axis=-1) if reps > 1 else cos
    sin_w = jnp.concatenate([sin_signed] * reps, axis=-1) if reps > 1 else sin_signed
    return x * cos_w + swapped * sin_w


def _inproj_kernel(x_ref, mod_ref, preg_ref, w_ref, wg2_ref, bg_ref, qg_ref, kg_ref, sg_ref,
                   cos_ref, sin_ref, bd_ref, vone_ref,
                   gq_ref, gk_ref, gv_ref, gf_ref, gb_ref, fu_ref, aq_ref, kt_ref, va_ref,
                   du_ref, dvn_ref, sz_ref):
    x = x_ref[0]
    ms = jnp.mean(x * x, axis=-1, keepdims=True)
    shift = mod_ref[0, :, 0:D_MODEL]
    scale = mod_ref[0, :, D_MODEL:2 * D_MODEL]
    h = (x * lax.rsqrt(ms + EPS)) * preg_ref[...] * (1.0 + scale) + shift
    hb = h.astype(BF16)

    def seg(off, width):
        return jnp.dot(hb, w_ref[:, off:off + width], preferred_element_type=F32)

    gq_ref[0] = (seg(GQ, A_KW) * (A_DK ** -0.5)).astype(BF16)
    gk_ref[0] = seg(GK, A_KW).astype(BF16)
    gv_ref[0] = seg(GV, A_W).astype(BF16)
    lr = seg(LR, LANES).astype(BF16)
    logits = jnp.dot(lr, wg2_ref[...], preferred_element_type=F32) + bg_ref[...]
    logg = (jnp.minimum(logits, 0.0) - jnp.log1p(jnp.exp(-jnp.abs(logits)))) * (1.0 / GLA_GATE_NORM)
    gf_ref[0] = logg[:, :A_KW]
    gb_ref[0] = logg[:, A_KW:]

    fu_ref[0] = seg(FU, B_W).astype(BF16)

    cos = cos_ref[...]
    sin = sin_ref[...]
    q = seg(AQ, C_W)
    qms = jnp.dot((q * q).astype(BF16), bd_ref[...], preferred_element_type=F32)
    qn = q * lax.rsqrt(qms + EPS) * qg_ref[...]
    aq_ref[0] = (_rope(qn, cos, sin) * (C_HD ** -0.5 * LOG2E)).astype(BF16)

    k = seg(AK, C_KVW)
    kms = jnp.dot((k * k).astype(BF16), bd_ref[0:C_KVW, 0:C_KVW], preferred_element_type=F32)
    kn = k * lax.rsqrt(kms + EPS) * kg_ref[...]
    kt = _rope(kn, cos, sin).T.astype(BF16)
    for g in range(C_KV):
        for r in range(C_GRP):
            kt_ref[0, g, r * C_HD:(r + 1) * C_HD, :] = kt[g * C_HD:(g + 1) * C_HD, :]

    va = (seg(AV, AV_W) + vone_ref[...]).astype(BF16)
    va_ref[0, 0] = va[:, :LANES]
    va_ref[0, 1] = va[:, LANES:]

    du_ref[0] = seg(DU, D_W).astype(BF16)
    dv = seg(DV, D_W)
    dms = jnp.mean(dv * dv, axis=-1, keepdims=True)
    dvn_ref[0] = (dv * lax.rsqrt(dms + EPS) * sg_ref[...]).astype(BF16)

    z = seg(ZZ, D_MIX)
    sz_ref[0] = (z / (1.0 + jnp.exp(-z))).astype(BF16)


def _inproj(x, mod, lp, tabs):
    bsz, n, _ = x.shape
    tm = min(TM_IN, n)
    grid = (bsz, n // tm)
    tok = lambda w: pl.BlockSpec((1, tm, w), lambda b, i: (b, i, 0))
    out_shapes = (
        jax.ShapeDtypeStruct((bsz, n, A_KW), BF16),
        jax.ShapeDtypeStruct((bsz, n, A_KW), BF16),
        jax.ShapeDtypeStruct((bsz, n, A_W), BF16),
        jax.ShapeDtypeStruct((bsz, n, A_KW), F32),
        jax.ShapeDtypeStruct((bsz, n, A_KW), F32),
        jax.ShapeDtypeStruct((bsz, n, B_W), BF16),
        jax.ShapeDtypeStruct((bsz, n, C_W), BF16),
        jax.ShapeDtypeStruct((bsz, C_KV, C_GRP * C_HD, n), BF16),
        jax.ShapeDtypeStruct((bsz, C_KV, n, LANES), BF16),
        jax.ShapeDtypeStruct((bsz, n, D_W), BF16),
        jax.ShapeDtypeStruct((bsz, n, D_W), BF16),
        jax.ShapeDtypeStruct((bsz, n, D_MIX), BF16),
    )
    out_specs = (
        tok(A_KW), tok(A_KW), tok(A_W), tok(A_KW), tok(A_KW), tok(B_W), tok(C_W),
        pl.BlockSpec((1, C_KV, C_GRP * C_HD, tm), lambda b, i: (b, 0, 0, i)),
        pl.BlockSpec((1, C_KV, tm, LANES), lambda b, i: (b, 0, i, 0)),
        tok(D_W), tok(D_W), tok(D_MIX),
    )
    in_specs = [
        tok(D_MODEL),
        pl.BlockSpec((1, 1, 3 * D_MODEL), lambda b, i: (b, 0, 0)),
        _full((1, D_MODEL)),
        _full((D_MODEL, W_PACKED)),
        _full((LANES, 2 * A_KW)),
        _full((1, 2 * A_KW)),
        _full((1, C_W)),
        _full((1, C_KVW)),
        _full((1, D_W)),
        pl.BlockSpec((tm, LANES), lambda b, i: (i, 0)),
        pl.BlockSpec((tm, LANES), lambda b, i: (i, 0)),
        _full((C_W, C_W)),
        _full((1, 2 * LANES)),
    ]
    return pl.pallas_call(
        _inproj_kernel, out_shape=out_shapes, grid=grid, in_specs=in_specs, out_specs=out_specs,
        compiler_params=_params("parallel", "parallel"), name="inproj",
    )(x, mod, lp["pre_g"], lp["w_in"], lp["wg2"], lp["bg"], lp["qg"], lp["kg"], lp["sg"],
      tabs["cos"], tabs["sin"], lp["bd_q"], lp["vone"])


def _head_match(shape, rows_per_head, cols_per_head):
    return (lax.broadcasted_iota(jnp.int32, shape, 0) // rows_per_head
            == lax.broadcasted_iota(jnp.int32, shape, 1) // cols_per_head)


def _gla_band_kernel(q_ref, k_ref, v_ref, gf_ref, gb_ref, o_ref, kf_ref, vf_ref, bf_ref, cb_ref):
    ts = q_ref.shape[2]
    hsum = jnp.where(_head_match((A_KW, A_W), A_DK, A_DV), 1.0, 0.0).astype(BF16)
    run = jnp.zeros((ts, A_KW), F32)
    for p in range(GLA_SUB):
        run = run + gf_ref[0, p]
        bf_ref[p] = run
        kf_ref[p] = k_ref[0, p].astype(F32)
        vf_ref[p] = v_ref[0, p].astype(F32)
    run = jnp.zeros((ts, A_KW), F32)
    for p in reversed(range(GLA_SUB)):
        run = run + gb_ref[0, p]
        cb_ref[p] = run
    for p in range(GLA_SUB):
        q = q_ref[0, p].astype(F32)
        prods = []
        for s in range(GLA_SUB):
            qk = q * kf_ref[s]
            if s < p:
                qk = qk * jnp.exp(bf_ref[p] - bf_ref[s])
            elif s > p:
                qk = qk * jnp.exp(cb_ref[p] - cb_ref[s])
            prods.append(qk.astype(BF16))
        a = jnp.dot(jnp.concatenate(prods, axis=0), hsum, preferred_element_type=F32)
        acc = a[0:ts] * vf_ref[0]
        for s in range(1, GLA_SUB):
            acc = acc + a[s * ts:(s + 1) * ts] * vf_ref[s]
        o_ref[0, p] = acc.astype(BF16)


def _gla_band(gq, gk, gv, gf, gb):
    bsz, n, _ = gq.shape
    ns = n // GLA_SUB
    ts = min(GLA_BAND_TILE, ns)
    perm = lambda a: a.reshape(bsz, ns, GLA_SUB, a.shape[-1]).transpose(0, 2, 1, 3)
    blk = lambda w: pl.BlockSpec((1, GLA_SUB, ts, w), lambda b, i: (b, 0, i, 0))
    o = pl.pallas_call(
        _gla_band_kernel,
        out_shape=jax.ShapeDtypeStruct((bsz, GLA_SUB, ns, A_W), BF16),
        grid=(bsz, ns // ts),
        in_specs=[blk(A_KW), blk(A_KW), blk(A_W), blk(A_KW), blk(A_KW)],
        out_specs=blk(A_W),
        scratch_shapes=[
            pltpu.VMEM((GLA_SUB, ts, A_KW), F32), pltpu.VMEM((GLA_SUB, ts, A_W), F32),
            pltpu.VMEM((GLA_SUB, ts, A_KW), F32), pltpu.VMEM((GLA_SUB, ts, A_KW), F32),
        ],
        compiler_params=_params("parallel", "parallel"), name="gla_band",
    )(perm(gq), perm(gk), perm(gv), perm(gf), perm(gb))
    return o.transpose(0, 2, 1, 3).reshape(bsz, n, A_W)


def _gla_prep(q_ref, k_ref, g_ref, qt_ref, kh_ref, dect_ref, reverse):
    seg = q_ref.shape[1]
    r_i = lax.broadcasted_iota(jnp.int32, (GLA_GRP, GLA_GRP), 0)
    c_i = lax.broadcasted_iota(jnp.int32, (GLA_GRP, GLA_GRP), 1)
    same = (r_i // GLA_SUB) == (c_i // GLA_SUB)
    tri = (c_i >= r_i) if reverse else (c_i <= r_i)
    cum_mat = jnp.where(same & tri, 1.0, 0.0).astype(F32)
    tot_mat = jnp.where(same, 1.0, 0.0).astype(F32)
    for r in range(seg // GLA_GRP):
        rows = slice(r * GLA_GRP, (r + 1) * GLA_GRP)
        g = g_ref[0, rows, :]
        cum = jnp.dot(cum_mat, g, precision=HIGHEST, preferred_element_type=F32)
        tot = jnp.dot(tot_mat, g, precision=HIGHEST, preferred_element_type=F32)
        qt_ref[rows, :] = (q_ref[0, rows, :].astype(F32) * jnp.exp(cum)).astype(BF16)
        kh_ref[rows, :] = (k_ref[0, rows, :].astype(F32) * jnp.exp(tot - cum)).astype(BF16)
    sel = jnp.where(lax.broadcasted_iota(jnp.int32, (seg, LANES), 0) // GLA_SUB
                    == lax.broadcasted_iota(jnp.int32, (seg, LANES), 1), 1.0, 0.0).astype(F32)
    tot_t = lax.dot_general(g_ref[0], sel, (((0,), (0,)), ((), ())), precision=HIGHEST,
                            preferred_element_type=F32)
    dect_ref[...] = jnp.exp(tot_t)


def _gla_rec_kernel(qf_ref, kf_ref, vf_ref, gf_ref, qb_ref, kb_ref, vb_ref, gb_ref, of_ref, ob_ref,
                    sf_ref, sb_ref, qtf_ref, khf_ref, dtf_ref, qtb_ref, khb_ref, dtb_ref):
    @pl.when(pl.program_id(1) == 0)
    def _():
        sf_ref[...] = jnp.zeros_like(sf_ref)
        sb_ref[...] = jnp.zeros_like(sb_ref)

    seg = qf_ref.shape[1]
    n_sub = seg // GLA_SUB
    _gla_prep(qf_ref, kf_ref, gf_ref, qtf_ref, khf_ref, dtf_ref, reverse=False)
    _gla_prep(qb_ref, kb_ref, gb_ref, qtb_ref, khb_ref, dtb_ref, reverse=True)
    smask = _head_match((A_KW, A_W), A_DK, A_DV)

    def step(i, qt_ref, kh_ref, v_ref, dect_ref, s_ref, o_ref):
        rows = slice(i * GLA_SUB, (i + 1) * GLA_SUB)
        s = s_ref[...]
        o_ref[0, rows, :] = jnp.dot(qt_ref[rows, :], s.astype(BF16), preferred_element_type=F32).astype(BF16)
        upd = lax.dot_general(kh_ref[rows, :], v_ref[0, rows, :], (((0,), (0,)), ((), ())),
                              preferred_element_type=F32)
        s_ref[...] = s * dect_ref[:, i:i + 1] + jnp.where(smask, upd, 0.0)

    for j in range(n_sub):
        step(j, qtf_ref, khf_ref, vf_ref, dtf_ref, sf_ref, of_ref)
        step(n_sub - 1 - j, qtb_ref, khb_ref, vb_ref, dtb_ref, sb_ref, ob_ref)


def _gla_rec(gq, gk, gv, gf, gb):
    bsz, n, _ = gq.shape
    seg = min(GLA_SEG, n)
    nseg = n // seg
    assert seg // GLA_SUB <= LANES
    fwd = lambda w: pl.BlockSpec((1, seg, w), lambda b, s: (b, s, 0))
    bwd = lambda w: pl.BlockSpec((1, seg, w), lambda b, s: (b, nseg - 1 - s, 0))
    return pl.pallas_call(
        _gla_rec_kernel,
        out_shape=(jax.ShapeDtypeStruct((bsz, n, A_W), BF16), jax.ShapeDtypeStruct((bsz, n, A_W), BF16)),
        grid=(bsz, nseg),
        in_specs=[fwd(A_KW), fwd(A_KW), fwd(A_W), fwd(A_KW), bwd(A_KW), bwd(A_KW), bwd(A_W), bwd(A_KW)],
        out_specs=(fwd(A_W), bwd(A_W)),
        scratch_shapes=[
            pltpu.VMEM((A_KW, A_W), F32), pltpu.VMEM((A_KW, A_W), F32),
            pltpu.VMEM((seg, A_KW), BF16), pltpu.VMEM((seg, A_KW), BF16), pltpu.VMEM((A_KW, LANES), F32),
            pltpu.VMEM((seg, A_KW), BF16), pltpu.VMEM((seg, A_KW), BF16), pltpu.VMEM((A_KW, LANES), F32),
        ],
        compiler_params=_params("parallel", "arbitrary"), name="gla_rec",
    )(gq, gk, gv, gf, gq, gk, gv, gb)


def _fft1_kernel(u_ref, a1_ref, cos_ref, sin_ref, t_ref):
    n1 = u_ref.shape[1]
    nb = cos_ref.shape[0]
    t = jnp.dot(a1_ref[...], u_ref[0], preferred_element_type=F32)
    for j in range(nb):
        cols = slice(j * B_W, (j + 1) * B_W)
        tr = t[:n1, cols]
        ti = t[n1:, cols]
        c = jnp.concatenate([cos_ref[j]] * (B_W // LANES), axis=-1)
        s = jnp.concatenate([sin_ref[j]] * (B_W // LANES), axis=-1)
        t_ref[0, 0, :, cols] = (tr * c + ti * s).astype(BF16)
        t_ref[0, 1, :, cols] = (ti * c - tr * s).astype(BF16)


def _fft2_kernel(t_ref, l_ref, cc_ref, cs_ref, w_ref, o_ref):
    k1b, n2 = t_ref.shape[2], t_ref.shape[3]
    half = k1b * n2
    x = t_ref[0].reshape(2 * half, B_W)
    g = jnp.dot(l_ref[...], x, preferred_element_type=F32)
    gr = g[:half].astype(BF16)
    gi = g[half:].astype(BF16)
    mixed = (jnp.dot(gr, cc_ref[...], preferred_element_type=F32)
             + jnp.dot(gi, cs_ref[...], preferred_element_type=F32))
    y = jnp.dot(mixed.astype(BF16), w_ref[...], preferred_element_type=F32)
    o_ref[0] = y.reshape(n2, k1b, B_W)


def _fnet(fu, lp, tabs):
    bsz, n, _ = fu.shape
    n2 = FFT_N2
    n1 = n // n2
    nb = FFT_NB
    k1b = FFT_K1
    u2 = fu.reshape(bsz, n1, n2 * B_W)
    tt = pl.pallas_call(
        _fft1_kernel,
        out_shape=jax.ShapeDtypeStruct((bsz, 2, n1, n2 * B_W), BF16),
        grid=(n2 // nb, bsz),
        in_specs=[
            pl.BlockSpec((1, n1, nb * B_W), lambda j, b: (b, 0, j)),
            _full((2 * n1, n1)),
            pl.BlockSpec((nb, n1, LANES), lambda j, b: (j, 0, 0)),
            pl.BlockSpec((nb, n1, LANES), lambda j, b: (j, 0, 0)),
        ],
        out_specs=pl.BlockSpec((1, 2, n1, nb * B_W), lambda j, b: (b, 0, 0, j)),
        compiler_params=_params("parallel", "parallel"), name="fft_pass1",
    )(u2, tabs["a1"], tabs["tw_cos"], tabs["tw_sin"])
    t5 = tt.reshape(bsz, 2, n1, n2, B_W)
    y = pl.pallas_call(
        _fft2_kernel,
        out_shape=jax.ShapeDtypeStruct((bsz, n2, n1, B_W), F32),
        grid=(bsz, n1 // k1b),
        in_specs=[
            pl.BlockSpec((1, 2, k1b, n2, B_W), lambda b, i: (b, 0, i, 0, 0)),
            _full((2 * k1b * n2, 2 * k1b * n2)),
            _full((B_W, B_W)), _full((B_W, B_W)), _full((B_W, B_W)),
        ],
        out_specs=pl.BlockSpec((1, n2, k1b, B_W), lambda b, i: (b, 0, i, 0)),
        compiler_params=_params("parallel", "parallel"), name="fft_pass2",
    )(t5, tabs["l2"], tabs["chan_cos"], tabs["chan_sin"], lp["fnet_w"])
    return y.reshape(bsz, n, B_W)


def _attn_kernel(q_ref, kt_ref, va_ref, o_ref, qm_ref, m_ref, acc_ref):
    tq = q_ref.shape[1]
    n = kt_ref.shape[3]
    tk = min(ATT_TK, n)
    q = q_ref[0]
    lane = lax.broadcasted_iota(jnp.int32, q.shape, 1)
    for h in range(C_GRP):
        qm_ref[h * tq:(h + 1) * tq, :] = jnp.where(lane // C_HD == h, q, jnp.zeros_like(q))
    m_ref[...] = jnp.full(m_ref.shape, -jnp.inf, F32)
    acc_ref[...] = jnp.zeros(acc_ref.shape, F32)

    def step(c, carry):
        c0 = pl.multiple_of(c * tk, tk)
        kt = kt_ref[0, 0, :, pl.ds(c0, tk)]
        v = va_ref[0, 0, pl.ds(c0, tk), :]
        s = jnp.dot(qm_ref[...], kt, preferred_element_type=F32)
        s_max = s[:, 0:LANES]
        for t in range(1, tk // LANES):
            s_max = jnp.maximum(s_max, s[:, t * LANES:(t + 1) * LANES])
        row_max = jnp.max(s_max, axis=-1, keepdims=True)
        m_old = m_ref[...]
        m_new = jnp.maximum(m_old, row_max)
        p = jnp.exp2(s - jnp.concatenate([m_new] * (tk // LANES), axis=-1))
        alpha = jnp.exp2(m_old - m_new)
        acc_ref[...] = alpha * acc_ref[...] + jnp.dot(p.astype(BF16), v, preferred_element_type=F32)
        m_ref[...] = m_new
        return carry

    lax.fori_loop(0, n // tk, step, 0)
    acc = acc_ref[...]
    inv = 1.0 / acc[:, ONES_LANE:ONES_LANE + 1]
    outs = [(acc[h * tq:(h + 1) * tq, 0:C_HD] * inv[h * tq:(h + 1) * tq]) for h in range(C_GRP)]
    o_ref[0] = jnp.concatenate(outs, axis=-1).astype(BF16)


def _attention(aq, kt4, va):
    bsz, n, _ = aq.shape
    tq = min(ATT_TQ, n)
    return pl.pallas_call(
        _attn_kernel,
        out_shape=jax.ShapeDtypeStruct((bsz, n, C_W), BF16),
        grid=(bsz, C_KV, n // tq),
        in_specs=[
            pl.BlockSpec((1, tq, C_GRP * C_HD), lambda b, g, i: (b, i, g)),
            pl.BlockSpec((1, 1, C_GRP * C_HD, n), lambda b, g, i: (b, g, 0, 0)),
            pl.BlockSpec((1, 1, n, LANES), lambda b, g, i: (b, g, 0, 0)),
        ],
        out_specs=pl.BlockSpec((1, tq, C_GRP * C_HD), lambda b, g, i: (b, i, g)),
        scratch_shapes=[
            pltpu.VMEM((C_GRP * tq, C_GRP * C_HD), BF16),
            pltpu.VMEM((C_GRP * tq, LANES), F32),
            pltpu.VMEM((C_GRP * tq, LANES), F32),
        ],
        compiler_params=_params("parallel", "parallel", "arbitrary"), name="attention",
    )(aq, kt4, va)


def _outproj_kernel(x_ref, mod_ref, oa_ref, of_ref, ob_ref, yb_ref, ao_ref, du_ref, dvn_ref, sz_ref,
                    w_ref, postg_ref, og_ref, bd_ref, sw_ref, sb_ref, o_ref):
    tm = x_ref.shape[1]
    sz = sz_ref[0].astype(F32)

    o = oa_ref[0].astype(F32) + of_ref[0].astype(F32) + ob_ref[0].astype(F32)
    oms = jnp.dot((o * o).astype(BF16), bd_ref[...], preferred_element_type=F32)
    out_a = o * lax.rsqrt(oms + EPS) * og_ref[...]
    y = jnp.dot((out_a * sz[:, 0:A_W]).astype(BF16), w_ref[0:A_W, :], preferred_element_type=F32)

    m_b = (yb_ref[0] * sz[:, A_W:A_W + B_W]).astype(BF16)
    y = y + jnp.dot(m_b, w_ref[A_W:A_W + B_W, :], preferred_element_type=F32)

    m_c = (ao_ref[0].astype(F32) * sz[:, A_W + B_W:A_W + B_W + C_W]).astype(BF16)
    y = y + jnp.dot(m_c, w_ref[A_W + B_W:A_W + B_W + C_W, :], preferred_element_type=F32)

    lane = lax.broadcasted_iota(jnp.int32, (SGU_CHUNK, D_W), 1)
    parts = []
    for c in range(tm // SGU_CHUNK):
        rows = slice(c * SGU_CHUNK, (c + 1) * SGU_CHUNK)
        r = jnp.dot(sw_ref[...], dvn_ref[0, rows, :], preferred_element_type=F32)
        mix = sb_ref[...]
        for g in range(D_GROUPS):
            mix = mix + jnp.where(lane // D_GC == g, r[g * SGU_CHUNK:(g + 1) * SGU_CHUNK, :], 0.0)
        parts.append(du_ref[0, rows, :].astype(F32) * mix)
    out_d = jnp.concatenate(parts, axis=0)
    m_d = (out_d * sz[:, A_W + B_W + C_W:]).astype(BF16)
    y = y + jnp.dot(m_d, w_ref[A_W + B_W + C_W:, :], preferred_element_type=F32)

    yms = jnp.mean(y * y, axis=-1, keepdims=True)
    yn = y * lax.rsqrt(yms + EPS) * postg_ref[...]
    gate = mod_ref[0, :, 2 * D_MODEL:3 * D_MODEL]
    o_ref[0] = x_ref[0] + gate * yn


def _outproj(x, mod, oa, of, ob, yb, ao, du, dvn, sz, lp):
    bsz, n, _ = x.shape
    tm = min(TM_OUT, n)
    tok = lambda w: pl.BlockSpec((1, tm, w), lambda b, i: (b, i, 0))
    return pl.pallas_call(
        _outproj_kernel,
        out_shape=jax.ShapeDtypeStruct((bsz, n, D_MODEL), F32),
        grid=(bsz, n // tm),
        in_specs=[
            tok(D_MODEL),
            pl.BlockSpec((1, 1, 3 * D_MODEL), lambda b, i: (b, 0, 0)),
            tok(A_W), tok(A_W), tok(A_W), tok(B_W), tok(C_W), tok(D_W), tok(D_W), tok(D_MIX),
            _full((D_MIX, D_MODEL)), _full((1, D_MODEL)), _full((1, A_W)), _full((A_W, A_W)),
            _full((D_GROUPS * SGU_CHUNK, SGU_CHUNK)), _full((SGU_CHUNK, D_W)),
        ],
        out_specs=tok(D_MODEL),
        compiler_params=_params("parallel", "parallel"), name="outproj",
    )(x, mod, oa, of, ob, yb, ao, du, dvn, sz,
      lp["w_out"], lp["post_g"], lp["og"], lp["bd_a"], lp["sgu_w"], lp["sgu_b"])


def _win_column_map():
    sizes = [A_KW, A_KW, A_W, 2 * GLA_RANK, B_W, C_W, C_KVW, C_KVW, D_W, D_W, D_MIX]
    off = np.concatenate([[0], np.cumsum(sizes)])
    aq0, ak0, av0, lr0, bu0, cq0, ck0, cv0, du0, dv0, z0 = [int(o) for o in off[:-1]]
    deint = np.concatenate([np.arange(0, C_HD, 2), np.arange(1, C_HD, 2)])
    cols = []
    cols += list(range(aq0, aq0 + A_KW)) + list(range(ak0, ak0 + A_KW)) + list(range(av0, av0 + A_W))
    cols += list(range(bu0, bu0 + B_W))
    for h in range(C_HEADS):
        cols += list(cq0 + C_HD * h + deint)
    for h in range(C_KV):
        cols += list(ck0 + C_HD * h + deint)
    for h in range(C_KV):
        cols += list(range(cv0 + C_HD * h, cv0 + C_HD * (h + 1))) + [-1] * (LANES - C_HD)
    cols += list(range(du0, du0 + D_W)) + list(range(dv0, dv0 + D_W)) + list(range(z0, z0 + D_MIX))
    cols += list(range(lr0, lr0 + 2 * GLA_RANK)) + [-1] * (LANES - 2 * GLA_RANK)
    cols = np.asarray(cols, np.int32)
    assert cols.shape[0] == W_PACKED
    return cols, deint


def _block_diag_mean(width, group):
    idx = np.arange(width) // group
    return jnp.asarray((idx[:, None] == idx[None, :]).astype(np.float32) / group, BF16)


def _layer_params(l, ada_unused, norm_pre_g, norm_post_g, w_in, gla_wg2_f, gla_bg_f, gla_wg2_b, gla_bg_b,
                  gla_onorm_g, fnet_w, q_norm_g, k_norm_g, sgu_norm_g, sgu_w, sgu_b, w_out):
    cols, deint = _win_column_map()
    w = jnp.take(w_in[l], jnp.asarray(np.maximum(cols, 0)), axis=1)
    w = jnp.where(jnp.asarray(cols >= 0)[None, :], w, 0.0).astype(BF16)
    wg2 = jnp.zeros((LANES, 2 * A_KW), F32)
    wg2 = wg2.at[0:GLA_RANK, 0:A_KW].set(gla_wg2_f[l])
    wg2 = wg2.at[GLA_RANK:2 * GLA_RANK, A_KW:].set(gla_wg2_b[l])
    vone = np.zeros((1, 2 * LANES), np.float32)
    vone[0, ONES_LANE] = 1.0
    vone[0, LANES + ONES_LANE] = 1.0
    return {
        "pre_g": norm_pre_g[l].reshape(1, D_MODEL),
        "post_g": norm_post_g[l].reshape(1, D_MODEL),
        "w_in": w,
        "wg2": wg2.astype(BF16),
        "bg": jnp.concatenate([gla_bg_f[l], gla_bg_b[l]]).reshape(1, 2 * A_KW),
        "qg": jnp.tile(q_norm_g[l][deint], C_HEADS).reshape(1, C_W),
        "kg": jnp.tile(k_norm_g[l][deint], C_KV).reshape(1, C_KVW),
        "sg": sgu_norm_g[l].reshape(1, D_W),
        "bd_q": _block_diag_mean(C_W, C_HD),
        "bd_a": _block_diag_mean(A_W, A_DV),
        "vone": jnp.asarray(vone),
        "og": jnp.tile(gla_onorm_g[l], A_HEADS).reshape(1, A_W),
        "fnet_w": fnet_w[l].astype(BF16),
        "sgu_w": sgu_w[l].reshape(D_GROUPS * SGU_CHUNK, SGU_CHUNK).astype(BF16),
        "sgu_b": jnp.repeat(sgu_b[l].T, D_GC, axis=1),
        "w_out": w_out[l].astype(BF16),
    }


def _seq_tables(n):
    rows = n // GRID_W
    row = jnp.repeat(jnp.arange(rows, dtype=F32), GRID_W)
    col = jnp.tile(jnp.arange(GRID_W, dtype=F32), rows)
    rope_axis = C_HD // 2
    freqs = ROPE_THETA ** (-jnp.arange(0, rope_axis, 2, dtype=F32) / rope_axis)
    ang = jnp.concatenate([row[:, None] * freqs, col[:, None] * freqs], axis=-1)
    cos, sin = jnp.cos(ang), jnp.sin(ang)
    cos_t = jnp.tile(jnp.concatenate([cos, cos], axis=-1), (1, LANES // C_HD))
    sin_t = jnp.tile(jnp.concatenate([-sin, sin], axis=-1), (1, LANES // C_HD))

    n2 = FFT_N2
    n1 = n // n2

    def dft_angles(rows, cols, period):
        prod = (jnp.arange(rows, dtype=jnp.int32)[:, None] * jnp.arange(cols, dtype=jnp.int32)[None, :]) % period
        return prod.astype(F32) * (2.0 * math.pi / period)

    a1_ang = dft_angles(n1, n1, n1)
    a1 = jnp.concatenate([jnp.cos(a1_ang), -jnp.sin(a1_ang)], axis=0)
    tw_ang = dft_angles(n2, n1, n)
    scale = 1.0 / math.sqrt(n * B_GC)
    tw_cos = jnp.broadcast_to((jnp.cos(tw_ang) * scale)[:, :, None], (n2, n1, LANES))
    tw_sin = jnp.broadcast_to((jnp.sin(tw_ang) * scale)[:, :, None], (n2, n1, LANES))
    ang2 = dft_angles(n2, n2, n2)
    c2, s2 = jnp.cos(ang2), jnp.sin(ang2)
    eye = jnp.eye(FFT_K1, dtype=F32)
    blk = lambda m: (m[:, None, None, :] * eye[None, :, :, None]).reshape(n2 * FFT_K1, FFT_K1 * n2)
    l2 = jnp.block([[blk(c2), blk(s2)], [blk(-s2), blk(c2)]])
    angc = dft_angles(B_GC, B_GC, B_GC)
    grp = jnp.eye(B_GROUPS, dtype=F32)
    kron = lambda m: (grp[:, None, :, None] * m[None, :, None, :]).reshape(B_W, B_W)
    chan_cos = kron(jnp.cos(angc))
    chan_sin = kron(jnp.sin(angc))
    return {
        "cos": cos_t, "sin": sin_t,
        "a1": a1.astype(BF16),
        "tw_cos": tw_cos, "tw_sin": tw_sin,
        "l2": l2.astype(BF16),
        "chan_cos": chan_cos.astype(BF16), "chan_sin": chan_sin.astype(BF16),
    }


def _layer(x, mod, lp, tabs):
    gq, gk, gv, gf, gb, fu, aq, kt4, va, du, dvn, sz = _inproj(x, mod, lp, tabs)
    oa = _gla_band(gq, gk, gv, gf, gb)
    of, ob = _gla_rec(gq, gk, gv, gf, gb)
    yb = _fnet(fu, lp, tabs)
    ao = _attention(aq, kt4, va)
    return _outproj(x, mod, oa, of, ob, yb, ao, du, dvn, sz, lp)


def kernel(x_prompt, x_sample, c_prompt, c_sample, ada_w, ada_b, norm_pre_g, norm_post_g, w_in,
           gla_wg2_f, gla_bg_f, gla_wg2_b, gla_bg_b, gla_onorm_g, fnet_w, q_norm_g, k_norm_g,
           sgu_norm_g, sgu_w, sgu_b, w_out):
    bp, bs = x_prompt.shape[0], x_sample.shape[0]
    pad_rows = (-(bp + bs)) % 8
    c_all = jnp.concatenate([c_prompt, c_sample, jnp.zeros((pad_rows, D_MODEL), F32)], axis=0)
    mod = _adaln(c_all, ada_w, ada_b)
    tabs_p = _seq_tables(x_prompt.shape[1])
    tabs_s = _seq_tables(x_sample.shape[1])
    y_prompt, y_sample = x_prompt, x_sample
    for l in range(DEPTH):
        lp = _layer_params(l, None, norm_pre_g, norm_post_g, w_in, gla_wg2_f, gla_bg_f, gla_wg2_b,
                           gla_bg_b, gla_onorm_g, fnet_w, q_norm_g, k_norm_g, sgu_norm_g, sgu_w,
                           sgu_b, w_out)
        mod_p = mod[l, 0:bp].reshape(bp, 1, 3 * D_MODEL)
        mod_s = mod[l, bp:bp + bs].reshape(bs, 1, 3 * D_MODEL)
        y_prompt = _layer(y_prompt, mod_p, lp, tabs_p)
        y_sample = _layer(y_sample, mod_s, lp, tabs_s)
    return (y_prompt, y_sample)
```

```python
import math

import numpy as np
import jax
import jax.numpy as jnp
from jax import lax
from jax.experimental import pallas as pl
from jax.experimental.pallas import tpu as pltpu

F32 = jnp.float32
BF16 = jnp.bfloat16
HIGHEST = lax.Precision.HIGHEST

D_MODEL = 1024
DEPTH = 2
GRID_W = 64
A_HEADS, A_DK, A_DV = 4, 32, 64
A_W, A_KW = A_HEADS * A_DV, A_HEADS * A_DK
GLA_RANK = 16
GLA_GATE_NORM = 16.0
B_GROUPS, B_GC = 4, 64
B_W = B_GROUPS * B_GC
C_HEADS, C_KV, C_HD = 8, 2, 64
C_GRP = C_HEADS // C_KV
C_W, C_KVW = C_HEADS * C_HD, C_KV * C_HD
ROPE_THETA = 10000.0
D_GROUPS, D_GC = 4, 64
D_W = D_GROUPS * D_GC
SGU_CHUNK = 128
D_MIX = A_W + B_W + C_W + D_W
EPS = 1e-6
LOG2E = 1.4426950408889634

LANES = 128
VMEM_LIMIT_BYTES = 56 * 1024 * 1024

GQ, GK, GV, FU, AQ, AK, AV, DU, DV, ZZ, LR = 0, 128, 256, 512, 768, 1280, 1408, 1664, 1920, 2176, 3456
W_PACKED = 3584
AV_W = 2 * LANES
ONES_LANE = C_HD

TM_IN = 512
TM_OUT = 512
GLA_SUB = 16
GLA_BAND_TILE = 64
GLA_GRP = 128
GLA_SEG = 512
FFT_N2 = 64
FFT_NB = 8
FFT_K1 = 8
ATT_TQ = 256
ATT_TK = 1024
ATT_HP = 2


def _params(*sem):
    return pltpu.CompilerParams(dimension_semantics=sem, vmem_limit_bytes=VMEM_LIMIT_BYTES)


def _full(shape):
    n = len(shape)
    return pl.BlockSpec(shape, lambda *_: (0,) * n)


def _adaln_kernel(c_ref, w_ref, b_ref, o_ref):
    c = c_ref[...]
    sc = c / (1.0 + jnp.exp(-c))
    o_ref[0] = jnp.dot(sc, w_ref[0], precision=HIGHEST, preferred_element_type=F32) + b_ref[0]


def _adaln(c_all, ada_w, ada_b):
    rows = c_all.shape[0]
    ncol = 3 * D_MODEL // D_MODEL
    return pl.pallas_call(
        _adaln_kernel,
        out_shape=jax.ShapeDtypeStruct((DEPTH, rows, 3 * D_MODEL), F32),
        grid=(DEPTH, ncol),
        in_specs=[
            pl.BlockSpec((rows, D_MODEL), lambda l, j: (0, 0)),
            pl.BlockSpec((1, D_MODEL, D_MODEL), lambda l, j: (l, 0, j)),
            pl.BlockSpec((1, 1, D_MODEL), lambda l, j: (l, 0, j)),
        ],
        out_specs=pl.BlockSpec((1, rows, D_MODEL), lambda l, j: (l, 0, j)),
        compiler_params=_params("arbitrary", "arbitrary"),
        name="adaln",
    )(c_all, ada_w, ada_b.reshape(DEPTH, 1, 3 * D_MODEL))


def _rope(x, cos, sin_signed):
    width = x.shape[-1]
    lane = lax.broadcasted_iota(jnp.int32, x.shape, 1)
    first_half = (lane % C_HD) < (C_HD // 2)
    swapped = jnp.where(first_half, pltpu.roll(x, width - C_HD // 2, 1), pltpu.roll(x, C_HD // 2, 1))
    reps = width // LANES
    cos_w = jnp.concatenate([cos] * reps, axis=-1) if reps > 1 else cos
    sin_w = jnp.concatenate([sin_signed] * reps, axis=-1) if reps > 1 else sin_signed
    return x * cos_w + swapped * sin_w


def _inproj_kernel(x_ref, mod_ref, preg_ref, w_ref, wg2_ref, bg_ref, qg_ref, kg_ref, sg_ref,
                   cos_ref, sin_ref, bd_ref, vone_ref,
                   gq_ref, gk_ref, gv_ref, gf_ref, gb_ref, fu_ref, aq_ref, ak_ref, vt_ref,
                   du_ref, dvn_ref, sz_ref):
    x = x_ref[0]
    ms = jnp.mean(x * x, axis=-1, keepdims=True)
    shift = mod_ref[0, :, 0:D_MODEL]
    scale = mod_ref[0, :, D_MODEL:2 * D_MODEL]
    h = (x * lax.rsqrt(ms + EPS)) * preg_ref[...] * (1.0 + scale) + shift
    hb = h.astype(BF16)

    def seg(off, width):
        return jnp.dot(hb, w_ref[:, off:off + width], preferred_element_type=F32)

    gq_ref[0] = (seg(GQ, A_KW) * (A_DK ** -0.5)).astype(BF16)
    gk_ref[0] = seg(GK, A_KW).astype(BF16)
    gv_ref[0] = seg(GV, A_W).astype(BF16)
    lr = seg(LR, LANES).astype(BF16)
    logits = jnp.dot(lr, wg2_ref[...], preferred_element_type=F32) + bg_ref[...]
    logg = (jnp.minimum(logits, 0.0) - jnp.log1p(jnp.exp(-jnp.abs(logits)))) * (1.0 / GLA_GATE_NORM)
    gf_ref[0] = logg[:, :A_KW]
    gb_ref[0] = logg[:, A_KW:]

    fu_ref[0] = seg(FU, B_W).astype(BF16)

    cos = cos_ref[...]
    sin = sin_ref[...]
    q = seg(AQ, C_W)
    qms = jnp.dot((q * q).astype(BF16), bd_ref[...], preferred_element_type=F32)
    qn = q * lax.rsqrt(qms + EPS) * qg_ref[...]
    aq_ref[0] = (_rope(qn, cos, sin) * (C_HD ** -0.5 * LOG2E)).T.astype(BF16)

    k = seg(AK, C_KVW)
    kms = jnp.dot((k * k).astype(BF16), bd_ref[0:C_KVW, 0:C_KVW], preferred_element_type=F32)
    kn = k * lax.rsqrt(kms + EPS) * kg_ref[...]
    ak_ref[0] = _rope(kn, cos, sin).astype(BF16)

    vat = (seg(AV, AV_W) + vone_ref[...]).T.astype(BF16)
    vt_ref[0, 0] = vat[:LANES, :]
    vt_ref[0, 1] = vat[LANES:, :]

    du_ref[0] = seg(DU, D_W).astype(BF16)
    dv = seg(DV, D_W)
    dms = jnp.mean(dv * dv, axis=-1, keepdims=True)
    dvn_ref[0] = (dv * lax.rsqrt(dms + EPS) * sg_ref[...]).astype(BF16)

    z = seg(ZZ, D_MIX)
    sz_ref[0] = (z / (1.0 + jnp.exp(-z))).astype(BF16)


def _inproj(x, mod, lp, tabs):
    bsz, n, _ = x.shape
    tm = min(TM_IN, n)
    grid = (bsz, n // tm)
    tok = lambda w: pl.BlockSpec((1, tm, w), lambda b, i: (b, i, 0))
    out_shapes = (
        jax.ShapeDtypeStruct((bsz, n, A_KW), BF16),
        jax.ShapeDtypeStruct((bsz, n, A_KW), BF16),
        jax.ShapeDtypeStruct((bsz, n, A_W), BF16),
        jax.ShapeDtypeStruct((bsz, n, A_KW), F32),
        jax.ShapeDtypeStruct((bsz, n, A_KW), F32),
        jax.ShapeDtypeStruct((bsz, n, B_W), BF16),
        jax.ShapeDtypeStruct((bsz, C_W, n), BF16),
        jax.ShapeDtypeStruct((bsz, n, C_KVW), BF16),
        jax.ShapeDtypeStruct((bsz, C_KV, LANES, n), BF16),
        jax.ShapeDtypeStruct((bsz, n, D_W), BF16),
        jax.ShapeDtypeStruct((bsz, n, D_W), BF16),
        jax.ShapeDtypeStruct((bsz, n, D_MIX), BF16),
    )
    out_specs = (
        tok(A_KW), tok(A_KW), tok(A_W), tok(A_KW), tok(A_KW), tok(B_W),
        pl.BlockSpec((1, C_W, tm), lambda b, i: (b, 0, i)),
        tok(C_KVW),
        pl.BlockSpec((1, C_KV, LANES, tm), lambda b, i: (b, 0, 0, i)),
        tok(D_W), tok(D_W), tok(D_MIX),
    )
    in_specs = [
        tok(D_MODEL),
        pl.BlockSpec((1, 1, 3 * D_MODEL), lambda b, i: (b, 0, 0)),
        _full((1, D_MODEL)),
        _full((D_MODEL, W_PACKED)),
        _full((LANES, 2 * A_KW)),
        _full((1, 2 * A_KW)),
        _full((1, C_W)),
        _full((1, C_KVW)),
        _full((1, D_W)),
        pl.BlockSpec((tm, LANES), lambda b, i: (i, 0)),
        pl.BlockSpec((tm, LANES), lambda b, i: (i, 0)),
        _full((C_W, C_W)),
        _full((1, 2 * LANES)),
    ]
    return pl.pallas_call(
        _inproj_kernel, out_shape=out_shapes, grid=grid, in_specs=in_specs, out_specs=out_specs,
        compiler_params=_params("parallel", "parallel"), name="inproj",
    )(x, mod, lp["pre_g"], lp["w_in"], lp["wg2"], lp["bg"], lp["qg"], lp["kg"], lp["sg"],
      tabs["cos"], tabs["sin"], lp["bd_q"], lp["vone"])


def _head_match(shape, rows_per_head, cols_per_head):
    return (lax.broadcasted_iota(jnp.int32, shape, 0) // rows_per_head
            == lax.broadcasted_iota(jnp.int32, shape, 1) // cols_per_head)


def _gla_band_kernel(q_ref, k_ref, v_ref, gf_ref, gb_ref, o_ref, kf_ref, vf_ref, bf_ref, cb_ref):
    ts = q_ref.shape[2]
    hsum = jnp.where(_head_match((A_KW, A_W), A_DK, A_DV), 1.0, 0.0).astype(BF16)
    run = jnp.zeros((ts, A_KW), F32)
    for p in range(GLA_SUB):
        run = run + gf_ref[0, p]
        bf_ref[p] = run
        kf_ref[p] = k_ref[0, p].astype(F32)
        vf_ref[p] = v_ref[0, p].astype(F32)
    run = jnp.zeros((ts, A_KW), F32)
    for p in reversed(range(GLA_SUB)):
        run = run + gb_ref[0, p]
        cb_ref[p] = run
    for p in range(GLA_SUB):
        q = q_ref[0, p].astype(F32)
        prods = []
        for s in range(GLA_SUB):
            qk = q * kf_ref[s]
            if s < p:
                qk = qk * jnp.exp(bf_ref[p] - bf_ref[s])
            elif s > p:
                qk = qk * jnp.exp(cb_ref[p] - cb_ref[s])
            prods.append(qk.astype(BF16))
        a = jnp.dot(jnp.concatenate(prods, axis=0), hsum, preferred_element_type=F32)
        acc = a[0:ts] * vf_ref[0]
        for s in range(1, GLA_SUB):
            acc = acc + a[s * ts:(s + 1) * ts] * vf_ref[s]
        o_ref[0, p] = acc.astype(BF16)


def _gla_band(gq, gk, gv, gf, gb):
    bsz, n, _ = gq.shape
    ns = n // GLA_SUB
    ts = min(GLA_BAND_TILE, ns)
    perm = lambda a: a.reshape(bsz, ns, GLA_SUB, a.shape[-1]).transpose(0, 2, 1, 3)
    blk = lambda w: pl.BlockSpec((1, GLA_SUB, ts, w), lambda b, i: (b, 0, i, 0))
    o = pl.pallas_call(
        _gla_band_kernel,
        out_shape=jax.ShapeDtypeStruct((bsz, GLA_SUB, ns, A_W), BF16),
        grid=(bsz, ns // ts),
        in_specs=[blk(A_KW), blk(A_KW), blk(A_W), blk(A_KW), blk(A_KW)],
        out_specs=blk(A_W),
        scratch_shapes=[
            pltpu.VMEM((GLA_SUB, ts, A_KW), F32), pltpu.VMEM((GLA_SUB, ts, A_W), F32),
            pltpu.VMEM((GLA_SUB, ts, A_KW), F32), pltpu.VMEM((GLA_SUB, ts, A_KW), F32),
        ],
        compiler_params=_params("parallel", "parallel"), name="gla_band",
    )(perm(gq), perm(gk), perm(gv), perm(gf), perm(gb))
    return o.transpose(0, 2, 1, 3).reshape(bsz, n, A_W)


def _gla_prep(q_ref, k_ref, g_ref, qt_ref, kh_ref, dect_ref, reverse):
    seg = q_ref.shape[1]
    r_i = lax.broadcasted_iota(jnp.int32, (GLA_GRP, GLA_GRP), 0)
    c_i = lax.broadcasted_iota(jnp.int32, (GLA_GRP, GLA_GRP), 1)
    same = (r_i // GLA_SUB) == (c_i // GLA_SUB)
    tri = (c_i >= r_i) if reverse else (c_i <= r_i)
    cum_mat = jnp.where(same & tri, 1.0, 0.0).astype(F32)
    tot_mat = jnp.where(same, 1.0, 0.0).astype(F32)
    for r in range(seg // GLA_GRP):
        rows = slice(r * GLA_GRP, (r + 1) * GLA_GRP)
        g = g_ref[0, rows, :]
        cum = jnp.dot(cum_mat, g, precision=HIGHEST, preferred_element_type=F32)
        tot = jnp.dot(tot_mat, g, precision=HIGHEST, preferred_element_type=F32)
        qt_ref[rows, :] = (q_ref[0, rows, :].astype(F32) * jnp.exp(cum)).astype(BF16)
        kh_ref[rows, :] = (k_ref[0, rows, :].astype(F32) * jnp.exp(tot - cum)).astype(BF16)
    sel = jnp.where(lax.broadcasted_iota(jnp.int32, (seg, LANES), 0) // GLA_SUB
                    == lax.broadcasted_iota(jnp.int32, (seg, LANES), 1), 1.0, 0.0).astype(F32)
    tot_t = lax.dot_general(g_ref[0], sel, (((0,), (0,)), ((), ())), precision=HIGHEST,
                            preferred_element_type=F32)
    dect_ref[...] = jnp.exp(tot_t)


def _gla_rec_kernel(qf_ref, kf_ref, vf_ref, gf_ref, qb_ref, kb_ref, vb_ref, gb_ref, of_ref, ob_ref,
                    sf_ref, sb_ref, qtf_ref, khf_ref, dtf_ref, qtb_ref, khb_ref, dtb_ref):
    @pl.when(pl.program_id(1) == 0)
    def _():
        sf_ref[...] = jnp.zeros_like(sf_ref)
        sb_ref[...] = jnp.zeros_like(sb_ref)

    seg = qf_ref.shape[1]
    n_sub = seg // GLA_SUB
    _gla_prep(qf_ref, kf_ref, gf_ref, qtf_ref, khf_ref, dtf_ref, reverse=False)
    _gla_prep(qb_ref, kb_ref, gb_ref, qtb_ref, khb_ref, dtb_ref, reverse=True)
    smask = _head_match((A_KW, A_W), A_DK, A_DV)

    def step(i, qt_ref, kh_ref, v_ref, dect_ref, s_ref, o_ref):
        rows = slice(i * GLA_SUB, (i + 1) * GLA_SUB)
        s = s_ref[...]
        o_ref[0, rows, :] = jnp.dot(qt_ref[rows, :], s.astype(BF16), preferred_element_type=F32).astype(BF16)
        upd = lax.dot_general(kh_ref[rows, :], v_ref[0, rows, :], (((0,), (0,)), ((), ())),
                              preferred_element_type=F32)
        s_ref[...] = s * dect_ref[:, i:i + 1] + jnp.where(smask, upd, 0.0)

    for j in range(n_sub):
        step(j, qtf_ref, khf_ref, vf_ref, dtf_ref, sf_ref, of_ref)
        step(n_sub - 1 - j, qtb_ref, khb_ref, vb_ref, dtb_ref, sb_ref, ob_ref)


def _gla_rec(gq, gk, gv, gf, gb):
    bsz, n, _ = gq.shape
    seg = min(GLA_SEG, n)
    nseg = n // seg
    assert seg // GLA_SUB <= LANES
    fwd = lambda w: pl.BlockSpec((1, seg, w), lambda b, s: (b, s, 0))
    bwd = lambda w: pl.BlockSpec((1, seg, w), lambda b, s: (b, nseg - 1 - s, 0))
    return pl.pallas_call(
        _gla_rec_kernel,
        out_shape=(jax.ShapeDtypeStruct((bsz, n, A_W), BF16), jax.ShapeDtypeStruct((bsz, n, A_W), BF16)),
        grid=(bsz, nseg),
        in_specs=[fwd(A_KW), fwd(A_KW), fwd(A_W), fwd(A_KW), bwd(A_KW), bwd(A_KW), bwd(A_W), bwd(A_KW)],
        out_specs=(fwd(A_W), bwd(A_W)),
        scratch_shapes=[
            pltpu.VMEM((A_KW, A_W), F32), pltpu.VMEM((A_KW, A_W), F32),
            pltpu.VMEM((seg, A_KW), BF16), pltpu.VMEM((seg, A_KW), BF16), pltpu.VMEM((A_KW, LANES), F32),
            pltpu.VMEM((seg, A_KW), BF16), pltpu.VMEM((seg, A_KW), BF16), pltpu.VMEM((A_KW, LANES), F32),
        ],
        compiler_params=_params("parallel", "arbitrary"), name="gla_rec",
    )(gq, gk, gv, gf, gq, gk, gv, gb)


def _fft1_kernel(u_ref, a1_ref, cos_ref, sin_ref, t_ref):
    n1 = u_ref.shape[1]
    nb = cos_ref.shape[0]
    t = jnp.dot(a1_ref[...], u_ref[0], preferred_element_type=F32)
    for j in range(nb):
        cols = slice(j * B_W, (j + 1) * B_W)
        tr = t[:n1, cols]
        ti = t[n1:, cols]
        c = jnp.concatenate([cos_ref[j]] * (B_W // LANES), axis=-1)
        s = jnp.concatenate([sin_ref[j]] * (B_W // LANES), axis=-1)
        t_ref[0, 0, :, cols] = (tr * c + ti * s).astype(BF16)
        t_ref[0, 1, :, cols] = (ti * c - tr * s).astype(BF16)


def _fft2_kernel(t_ref, l_ref, cc_ref, cs_ref, w_ref, o_ref):
    k1b, n2 = t_ref.shape[2], t_ref.shape[3]
    half = k1b * n2
    x = t_ref[0].reshape(2 * half, B_W)
    g = jnp.dot(l_ref[...], x, preferred_element_type=F32)
    gr = g[:half].astype(BF16)
    gi = g[half:].astype(BF16)
    mixed = (jnp.dot(gr, cc_ref[...], preferred_element_type=F32)
             + jnp.dot(gi, cs_ref[...], preferred_element_type=F32))
    y = jnp.dot(mixed.astype(BF16), w_ref[...], preferred_element_type=F32)
    o_ref[0] = y.reshape(n2, k1b, B_W)


def _fnet(fu, lp, tabs):
    bsz, n, _ = fu.shape
    n2 = FFT_N2
    n1 = n // n2
    nb = FFT_NB
    k1b = FFT_K1
    u2 = fu.reshape(bsz, n1, n2 * B_W)
    tt = pl.pallas_call(
        _fft1_kernel,
        out_shape=jax.ShapeDtypeStruct((bsz, 2, n1, n2 * B_W), BF16),
        grid=(n2 // nb, bsz),
        in_specs=[
            pl.BlockSpec((1, n1, nb * B_W), lambda j, b: (b, 0, j)),
            _full((2 * n1, n1)),
            pl.BlockSpec((nb, n1, LANES), lambda j, b: (j, 0, 0)),
            pl.BlockSpec((nb, n1, LANES), lambda j, b: (j, 0, 0)),
        ],
        out_specs=pl.BlockSpec((1, 2, n1, nb * B_W), lambda j, b: (b, 0, 0, j)),
        compiler_params=_params("parallel", "parallel"), name="fft_pass1",
    )(u2, tabs["a1"], tabs["tw_cos"], tabs["tw_sin"])
    t5 = tt.reshape(bsz, 2, n1, n2, B_W)
    y = pl.pallas_call(
        _fft2_kernel,
        out_shape=jax.ShapeDtypeStruct((bsz, n2, n1, B_W), F32),
        grid=(bsz, n1 // k1b),
        in_specs=[
            pl.BlockSpec((1, 2, k1b, n2, B_W), lambda b, i: (b, 0, i, 0, 0)),
            _full((2 * k1b * n2, 2 * k1b * n2)),
            _full((B_W, B_W)), _full((B_W, B_W)), _full((B_W, B_W)),
        ],
        out_specs=pl.BlockSpec((1, n2, k1b, B_W), lambda b, i: (b, 0, i, 0)),
        compiler_params=_params("parallel", "parallel"), name="fft_pass2",
    )(t5, tabs["l2"], tabs["chan_cos"], tabs["chan_sin"], lp["fnet_w"])
    return y.reshape(bsz, n, B_W)


def _attn_kernel(qt_ref, k_ref, vt_ref, o_ref, w_ref, m_ref, acc_ref, s_ref, sm_ref, p_ref, al_ref):
    tq = qt_ref.shape[2]
    n = k_ref.shape[1]
    tk = min(ATT_TK, n)
    n_chunks = n // tk
    g = pl.program_id(1)
    own_rows = lax.broadcasted_iota(jnp.int32, (C_KVW, tq), 0) // C_HD == g
    for h in range(C_GRP):
        qh = qt_ref[0, h * C_HD:(h + 1) * C_HD, :]
        wh = jnp.where(own_rows, jnp.concatenate([qh] * C_KV, axis=0), jnp.zeros((C_KVW, tq), BF16))
        w_ref[h // ATT_HP, :, (h % ATT_HP) * tq:(h % ATT_HP + 1) * tq] = wh
    m_ref[...] = jnp.full(m_ref.shape, -jnp.inf, F32)
    acc_ref[...] = jnp.zeros(acc_ref.shape, F32)
    pair_a, pair_b = 0, 1

    def scores(c, j):
        c0 = pl.multiple_of(c * tk, tk)
        return jnp.dot(k_ref[0, pl.ds(c0, tk), :], w_ref[j], preferred_element_type=F32)

    def values_t(c):
        return vt_ref[0, 0, :, pl.ds(pl.multiple_of(c * tk, tk), tk)]

    def scores_and_max(c, j):
        s = scores(c, j)
        return s, jnp.max(s, axis=0, keepdims=True)

    def softmax(j, s, s_max):
        m_old = m_ref[j]
        m_new = jnp.maximum(m_old, s_max)
        m_ref[j] = m_new
        return jnp.exp2(s - m_new).astype(BF16), jnp.exp2(m_old - m_new)

    def accumulate(j, alpha, vt, p):
        acc_ref[j] = alpha * acc_ref[j] + jnp.dot(vt, p, preferred_element_type=F32)

    s_ref[0], sm_ref[0] = scores_and_max(0, pair_a)
    p_ref[1] = jnp.zeros(p_ref.shape[1:], BF16)
    al_ref[1] = jnp.ones(al_ref.shape[1:], F32)

    def chunk(c, cur):
        nxt = 1 - cur
        s_b, smax_b = scores_and_max(c, pair_b)
        accumulate(pair_b, al_ref[nxt], values_t(jnp.maximum(c - 1, 0)), p_ref[nxt])
        p_a, al_a = softmax(pair_a, s_ref[cur], sm_ref[cur])
        s_ref[nxt], sm_ref[nxt] = scores_and_max(jnp.minimum(c + 1, n_chunks - 1), pair_a)
        accumulate(pair_a, al_a, values_t(c), p_a)
        p_b, al_b = softmax(pair_b, s_b, smax_b)
        p_ref[cur] = p_b
        al_ref[cur] = al_b

    def step(i, carry):
        chunk(2 * i, 0)
        chunk(2 * i + 1, 1)
        return carry

    assert n_chunks % 2 == 0 and C_GRP // ATT_HP == 2
    lax.fori_loop(0, n_chunks // 2, step, 0)
    accumulate(pair_b, al_ref[1], values_t(n_chunks - 1), p_ref[1])

    outs = []
    for h in range(C_GRP):
        acc = acc_ref[h // ATT_HP, :, (h % ATT_HP) * tq:(h % ATT_HP + 1) * tq]
        outs.append(acc[0:C_HD, :] * (1.0 / acc[ONES_LANE:ONES_LANE + 1, :]))
    o_ref[0] = jnp.concatenate(outs, axis=0).T.astype(BF16)


def _attention(aqt, ak, vt):
    bsz, _, n = aqt.shape
    tq = min(ATT_TQ, n)
    n_pairs = C_GRP // ATT_HP
    return pl.pallas_call(
        _attn_kernel,
        out_shape=jax.ShapeDtypeStruct((bsz, n, C_W), BF16),
        grid=(bsz, C_KV, n // tq),
        in_specs=[
            pl.BlockSpec((1, C_GRP * C_HD, tq), lambda b, g, i: (b, g, i)),
            pl.BlockSpec((1, n, C_KVW), lambda b, g, i: (b, 0, 0)),
            pl.BlockSpec((1, 1, LANES, n), lambda b, g, i: (b, g, 0, 0)),
        ],
        out_specs=pl.BlockSpec((1, tq, C_GRP * C_HD), lambda b, g, i: (b, i, g)),
        scratch_shapes=[
            pltpu.VMEM((n_pairs, C_KVW, ATT_HP * tq), BF16),
            pltpu.VMEM((n_pairs, 1, ATT_HP * tq), F32),
            pltpu.VMEM((n_pairs, LANES, ATT_HP * tq), F32),
            pltpu.VMEM((2, min(ATT_TK, n), ATT_HP * tq), F32),
            pltpu.VMEM((2, 1, ATT_HP * tq), F32),
            pltpu.VMEM((2, min(ATT_TK, n), ATT_HP * tq), BF16),
            pltpu.VMEM((2, 1, ATT_HP * tq), F32),
        ],
        compiler_params=_params("parallel", "parallel", "arbitrary"), name="attention",
    )(aqt, ak, vt)


def _outproj_kernel(x_ref, mod_ref, oa_ref, of_ref, ob_ref, yb_ref, ao_ref, du_ref, dvn_ref, sz_ref,
                    w_ref, postg_ref, og_ref, bd_ref, sw_ref, sb_ref, o_ref):
    tm = x_ref.shape[1]
    sz = sz_ref[0].astype(F32)

    o = oa_ref[0].astype(F32) + of_ref[0].astype(F32) + ob_ref[0].astype(F32)
    oms = jnp.dot((o * o).astype(BF16), bd_ref[...], preferred_element_type=F32)
    out_a = o * lax.rsqrt(oms + EPS) * og_ref[...]
    y = jnp.dot((out_a * sz[:, 0:A_W]).astype(BF16), w_ref[0:A_W, :], preferred_element_type=F32)

    m_b = (yb_ref[0] * sz[:, A_W:A_W + B_W]).astype(BF16)
    y = y + jnp.dot(m_b, w_ref[A_W:A_W + B_W, :], preferred_element_type=F32)

    m_c = (ao_ref[0].astype(F32) * sz[:, A_W + B_W:A_W + B_W + C_W]).astype(BF16)
    y = y + jnp.dot(m_c, w_ref[A_W + B_W:A_W + B_W + C_W, :], preferred_element_type=F32)

    lane = lax.broadcasted_iota(jnp.int32, (SGU_CHUNK, D_W), 1)
    parts = []
    for c in range(tm // SGU_CHUNK):
        rows = slice(c * SGU_CHUNK, (c + 1) * SGU_CHUNK)
        r = jnp.dot(sw_ref[...], dvn_ref[0, rows, :], preferred_element_type=F32)
        mix = sb_ref[...]
        for g in range(D_GROUPS):
            mix = mix + jnp.where(lane // D_GC == g, r[g * SGU_CHUNK:(g + 1) * SGU_CHUNK, :], 0.0)
        parts.append(du_ref[0, rows, :].astype(F32) * mix)
    out_d = jnp.concatenate(parts, axis=0)
    m_d = (out_d * sz[:, A_W + B_W + C_W:]).astype(BF16)
    y = y + jnp.dot(m_d, w_ref[A_W + B_W + C_W:, :], preferred_element_type=F32)

    yms = jnp.mean(y * y, axis=-1, keepdims=True)
    yn = y * lax.rsqrt(yms + EPS) * postg_ref[...]
    gate = mod_ref[0, :, 2 * D_MODEL:3 * D_MODEL]
    o_ref[0] = x_ref[0] + gate * yn


def _outproj(x, mod, oa, of, ob, yb, ao, du, dvn, sz, lp):
    bsz, n, _ = x.shape
    tm = min(TM_OUT, n)
    tok = lambda w: pl.BlockSpec((1, tm, w), lambda b, i: (b, i, 0))
    return pl.pallas_call(
        _outproj_kernel,
        out_shape=jax.ShapeDtypeStruct((bsz, n, D_MODEL), F32),
        grid=(bsz, n // tm),
        in_specs=[
            tok(D_MODEL),
            pl.BlockSpec((1, 1, 3 * D_MODEL), lambda b, i: (b, 0, 0)),
            tok(A_W), tok(A_W), tok(A_W), tok(B_W), tok(C_W), tok(D_W), tok(D_W), tok(D_MIX),
            _full((D_MIX, D_MODEL)), _full((1, D_MODEL)), _full((1, A_W)), _full((A_W, A_W)),
            _full((D_GROUPS * SGU_CHUNK, SGU_CHUNK)), _full((SGU_CHUNK, D_W)),
        ],
        out_specs=tok(D_MODEL),
        compiler_params=_params("parallel", "parallel"), name="outproj",
    )(x, mod, oa, of, ob, yb, ao, du, dvn, sz,
      lp["w_out"], lp["post_g"], lp["og"], lp["bd_a"], lp["sgu_w"], lp["sgu_b"])


def _win_column_map():
    sizes = [A_KW, A_KW, A_W, 2 * GLA_RANK, B_W, C_W, C_KVW, C_KVW, D_W, D_W, D_MIX]
    off = np.concatenate([[0], np.cumsum(sizes)])
    aq0, ak0, av0, lr0, bu0, cq0, ck0, cv0, du0, dv0, z0 = [int(o) for o in off[:-1]]
    deint = np.concatenate([np.arange(0, C_HD, 2), np.arange(1, C_HD, 2)])
    cols = []
    cols += list(range(aq0, aq0 + A_KW)) + list(range(ak0, ak0 + A_KW)) + list(range(av0, av0 + A_W))
    cols += list(range(bu0, bu0 + B_W))
    for h in range(C_HEADS):
        cols += list(cq0 + C_HD * h + deint)
    for h in range(C_KV):
        cols += list(ck0 + C_HD * h + deint)
    for h in range(C_KV):
        cols += list(range(cv0 + C_HD * h, cv0 + C_HD * (h + 1))) + [-1] * (LANES - C_HD)
    cols += list(range(du0, du0 + D_W)) + list(range(dv0, dv0 + D_W)) + list(range(z0, z0 + D_MIX))
    cols += list(range(lr0, lr0 + 2 * GLA_RANK)) + [-1] * (LANES - 2 * GLA_RANK)
    cols = np.asarray(cols, np.int32)
    assert cols.shape[0] == W_PACKED
    return cols, deint


def _block_diag_mean(width, group):
    idx = np.arange(width) // group
    return jnp.asarray((idx[:, None] == idx[None, :]).astype(np.float32) / group, BF16)


def _layer_params(l, ada_unused, norm_pre_g, norm_post_g, w_in, gla_wg2_f, gla_bg_f, gla_wg2_b, gla_bg_b,
                  gla_onorm_g, fnet_w, q_norm_g, k_norm_g, sgu_norm_g, sgu_w, sgu_b, w_out):
    cols, deint = _win_column_map()
    w = jnp.take(w_in[l], jnp.asarray(np.maximum(cols, 0)), axis=1)
    w = jnp.where(jnp.asarray(cols >= 0)[None, :], w, 0.0).astype(BF16)
    wg2 = jnp.zeros((LANES, 2 * A_KW), F32)
    wg2 = wg2.at[0:GLA_RANK, 0:A_KW].set(gla_wg2_f[l])
    wg2 = wg2.at[GLA_RANK:2 * GLA_RANK, A_KW:].set(gla_wg2_b[l])
    vone = np.zeros((1, 2 * LANES), np.float32)
    vone[0, ONES_LANE] = 1.0
    vone[0, LANES + ONES_LANE] = 1.0
    return {
        "pre_g": norm_pre_g[l].reshape(1, D_MODEL),
        "post_g": norm_post_g[l].reshape(1, D_MODEL),
        "w_in": w,
        "wg2": wg2.astype(BF16),
        "bg": jnp.concatenate([gla_bg_f[l], gla_bg_b[l]]).reshape(1, 2 * A_KW),
        "qg": jnp.tile(q_norm_g[l][deint], C_HEADS).reshape(1, C_W),
        "kg": jnp.tile(k_norm_g[l][deint], C_KV).reshape(1, C_KVW),
        "sg": sgu_norm_g[l].reshape(1, D_W),
        "bd_q": _block_diag_mean(C_W, C_HD),
        "bd_a": _block_diag_mean(A_W, A_DV),
        "vone": jnp.asarray(vone),
        "og": jnp.tile(gla_onorm_g[l], A_HEADS).reshape(1, A_W),
        "fnet_w": fnet_w[l].astype(BF16),
        "sgu_w": sgu_w[l].reshape(D_GROUPS * SGU_CHUNK, SGU_CHUNK).astype(BF16),
        "sgu_b": jnp.repeat(sgu_b[l].T, D_GC, axis=1),
        "w_out": w_out[l].astype(BF16),
    }


def _seq_tables(n):
    rows = n // GRID_W
    row = jnp.repeat(jnp.arange(rows, dtype=F32), GRID_W)
    col = jnp.tile(jnp.arange(GRID_W, dtype=F32), rows)
    rope_axis = C_HD // 2
    freqs = ROPE_THETA ** (-jnp.arange(0, rope_axis, 2, dtype=F32) / rope_axis)
    ang = jnp.concatenate([row[:, None] * freqs, col[:, None] * freqs], axis=-1)
    cos, sin = jnp.cos(ang), jnp.sin(ang)
    cos_t = jnp.tile(jnp.concatenate([cos, cos], axis=-1), (1, LANES // C_HD))
    sin_t = jnp.tile(jnp.concatenate([-sin, sin], axis=-1), (1, LANES // C_HD))

    n2 = FFT_N2
    n1 = n // n2

    def dft_angles(rows, cols, period):
        prod = (jnp.arange(rows, dtype=jnp.int32)[:, None] * jnp.arange(cols, dtype=jnp.int32)[None, :]) % period
        return prod.astype(F32) * (2.0 * math.pi / period)

    a1_ang = dft_angles(n1, n1, n1)
    a1 = jnp.concatenate([jnp.cos(a1_ang), -jnp.sin(a1_ang)], axis=0)
    tw_ang = dft_angles(n2, n1, n)
    scale = 1.0 / math.sqrt(n * B_GC)
    tw_cos = jnp.broadcast_to((jnp.cos(tw_ang) * scale)[:, :, None], (n2, n1, LANES))
    tw_sin = jnp.broadcast_to((jnp.sin(tw_ang) * scale)[:, :, None], (n2, n1, LANES))
    ang2 = dft_angles(n2, n2, n2)
    c2, s2 = jnp.cos(ang2), jnp.sin(ang2)
    eye = jnp.eye(FFT_K1, dtype=F32)
    blk = lambda m: (m[:, None, None, :] * eye[None, :, :, None]).reshape(n2 * FFT_K1, FFT_K1 * n2)
    l2 = jnp.block([[blk(c2), blk(s2)], [blk(-s2), blk(c2)]])
    angc = dft_angles(B_GC, B_GC, B_GC)
    grp = jnp.eye(B_GROUPS, dtype=F32)
    kron = lambda m: (grp[:, None, :, None] * m[None, :, None, :]).reshape(B_W, B_W)
    chan_cos = kron(jnp.cos(angc))
    chan_sin = kron(jnp.sin(angc))
    return {
        "cos": cos_t, "sin": sin_t,
        "a1": a1.astype(BF16),
        "tw_cos": tw_cos, "tw_sin": tw_sin,
        "l2": l2.astype(BF16),
        "chan_cos": chan_cos.astype(BF16), "chan_sin": chan_sin.astype(BF16),
    }


def _layer(x, mod, lp, tabs):
    gq, gk, gv, gf, gb, fu, aq, kt4, va, du, dvn, sz = _inproj(x, mod, lp, tabs)
    oa = _gla_band(gq, gk, gv, gf, gb)
    of, ob = _gla_rec(gq, gk, gv, gf, gb)
    yb = _fnet(fu, lp, tabs)
    ao = _attention(aq, kt4, va)
    return _outproj(x, mod, oa, of, ob, yb, ao, du, dvn, sz, lp)


def kernel(x_prompt, x_sample, c_prompt, c_sample, ada_w, ada_b, norm_pre_g, norm_post_g, w_in,
           gla_wg2_f, gla_bg_f, gla_wg2_b, gla_bg_b, gla_onorm_g, fnet_w, q_norm_g, k_norm_g,
           sgu_norm_g, sgu_w, sgu_b, w_out):
    bp, bs = x_prompt.shape[0], x_sample.shape[0]
    pad_rows = (-(bp + bs)) % 8
    c_all = jnp.concatenate([c_prompt, c_sample, jnp.zeros((pad_rows, D_MODEL), F32)], axis=0)
    mod = _adaln(c_all, ada_w, ada_b)
    tabs_p = _seq_tables(x_prompt.shape[1])
    tabs_s = _seq_tables(x_sample.shape[1])
    y_prompt, y_sample = x_prompt, x_sample
    for l in range(DEPTH):
        lp = _layer_params(l, None, norm_pre_g, norm_post_g, w_in, gla_wg2_f, gla_bg_f, gla_wg2_b,
                           gla_bg_b, gla_onorm_g, fnet_w, q_norm_g, k_norm_g, sgu_norm_g, sgu_w,
                           sgu_b, w_out)
        mod_p = mod[l, 0:bp].reshape(bp, 1, 3 * D_MODEL)
        mod_s = mod[l, bp:bp + bs].reshape(bs, 1, 3 * D_MODEL)
        y_prompt = _layer(y_prompt, mod_p, lp, tabs_p)
        y_sample = _layer(y_sample, mod_s, lp, tabs_s)
    return (y_prompt, y_sample)
```

```python
import math

import numpy as np
import jax
import jax.numpy as jnp
from jax import lax
from jax.experimental import pallas as pl
from jax.experimental.pallas import tpu as pltpu

F32 = jnp.float32
BF16 = jnp.bfloat16
HIGHEST = lax.Precision.HIGHEST

D_MODEL = 1024
DEPTH = 2
GRID_W = 64
A_HEADS, A_DK, A_DV = 4, 32, 64
A_W, A_KW = A_HEADS * A_DV, A_HEADS * A_DK
GLA_RANK = 16
GLA_GATE_NORM = 16.0
B_GROUPS, B_GC = 4, 64
B_W = B_GROUPS * B_GC
C_HEADS, C_KV, C_HD = 8, 2, 64
C_GRP = C_HEADS // C_KV
C_W, C_KVW = C_HEADS * C_HD, C_KV * C_HD
ROPE_THETA = 10000.0
D_GROUPS, D_GC = 4, 64
D_W = D_GROUPS * D_GC
SGU_CHUNK = 128
D_MIX = A_W + B_W + C_W + D_W
EPS = 1e-6
LOG2E = 1.4426950408889634

LANES = 128
VMEM_LIMIT_BYTES = 56 * 1024 * 1024

GQ, GK, GV, FU, AQ, AK, AV, DU, DV, ZZ, LR = 0, 128, 256, 512, 768, 1280, 1408, 1664, 1920, 2176, 3456
W_PACKED = 3584
AV_W = 2 * LANES
ONES_LANE = C_HD

TM_IN = 512
TM_OUT = 512
GLA_SUB = 16
GLA_BAND_TILE = 64
GLA_GRP = 128
GLA_SEG = 512
GLA_LOOKAHEAD = 3
FFT_N2 = 64
FFT_NB = 8
FFT_K1 = 8
ATT_TQ = 256
ATT_TK = 1024
ATT_UNROLL = 4


def _params(*sem):
    return pltpu.CompilerParams(dimension_semantics=sem, vmem_limit_bytes=VMEM_LIMIT_BYTES)


def _full(shape):
    n = len(shape)
    return pl.BlockSpec(shape, lambda *_: (0,) * n)


def _adaln_kernel(c_ref, w_ref, b_ref, o_ref):
    c = c_ref[...]
    sc = c / (1.0 + jnp.exp(-c))
    o_ref[0] = jnp.dot(sc, w_ref[0], precision=HIGHEST, preferred_element_type=F32) + b_ref[0]


def _adaln(c_all, ada_w, ada_b):
    rows = c_all.shape[0]
    ncol = 3 * D_MODEL // D_MODEL
    return pl.pallas_call(
        _adaln_kernel,
        out_shape=jax.ShapeDtypeStruct((DEPTH, rows, 3 * D_MODEL), F32),
        grid=(DEPTH, ncol),
        in_specs=[
            pl.BlockSpec((rows, D_MODEL), lambda l, j: (0, 0)),
            pl.BlockSpec((1, D_MODEL, D_MODEL), lambda l, j: (l, 0, j)),
            pl.BlockSpec((1, 1, D_MODEL), lambda l, j: (l, 0, j)),
        ],
        out_specs=pl.BlockSpec((1, rows, D_MODEL), lambda l, j: (l, 0, j)),
        compiler_params=_params("arbitrary", "arbitrary"),
        name="adaln",
    )(c_all, ada_w, ada_b.reshape(DEPTH, 1, 3 * D_MODEL))


def _rope(x, cos, sin_signed):
    width = x.shape[-1]
    lane = lax.broadcasted_iota(jnp.int32, x.shape, 1)
    first_half = (lane % C_HD) < (C_HD // 2)
    swapped = jnp.where(first_half, pltpu.roll(x, width - C_HD // 2, 1), pltpu.roll(x, C_HD // 2, 1))
    reps = width // LANES
    cos_w = jnp.concatenate([cos] * reps, axis=-1) if reps > 1 else cos
    sin_w = jnp.concatenate([sin_signed] * reps, axis=-1) if reps > 1 else sin_signed
    return x * cos_w + swapped * sin_w


def _inproj_kernel(x_ref, mod_ref, preg_ref, w_ref, wg2_ref, bg_ref, qg_ref, kg_ref, sg_ref,
                   cos_ref, sin_ref, bd_ref, vone_ref,
                   gq_ref, gk_ref, gv_ref, gf_ref, gb_ref, fu_ref, aq_ref, kt_ref, va_ref,
                   du_ref, dvn_ref, sz_ref):
    x = x_ref[0]
    ms = jnp.mean(x * x, axis=-1, keepdims=True)
    shift = mod_ref[0, :, 0:D_MODEL]
    scale = mod_ref[0, :, D_MODEL:2 * D_MODEL]
    h = (x * lax.rsqrt(ms + EPS)) * preg_ref[...] * (1.0 + scale) + shift
    hb = h.astype(BF16)

    def seg(off, width):
        return jnp.dot(hb, w_ref[:, off:off + width], preferred_element_type=F32)

    gq_ref[0] = (seg(GQ, A_KW) * (A_DK ** -0.5)).astype(BF16)
    gk_ref[0] = seg(GK, A_KW).astype(BF16)
    gv_ref[0] = seg(GV, A_W).astype(BF16)
    lr = seg(LR, LANES).astype(BF16)
    logits = jnp.dot(lr, wg2_ref[...], preferred_element_type=F32) + bg_ref[...]
    logg = (jnp.minimum(logits, 0.0) - jnp.log1p(jnp.exp(-jnp.abs(logits)))) * (1.0 / GLA_GATE_NORM)
    gf_ref[0] = logg[:, :A_KW]
    gb_ref[0] = logg[:, A_KW:]

    fu_ref[0] = seg(FU, B_W).astype(BF16)

    cos = cos_ref[...]
    sin = sin_ref[...]
    q = seg(AQ, C_W)
    qms = jnp.dot((q * q).astype(BF16), bd_ref[...], preferred_element_type=F32)
    qn = q * lax.rsqrt(qms + EPS) * qg_ref[...]
    aq_ref[0] = (_rope(qn, cos, sin) * (C_HD ** -0.5 * LOG2E)).astype(BF16)

    k = seg(AK, C_KVW)
    kms = jnp.dot((k * k).astype(BF16), bd_ref[0:C_KVW, 0:C_KVW], preferred_element_type=F32)
    kn = k * lax.rsqrt(kms + EPS) * kg_ref[...]
    kt = _rope(kn, cos, sin).T.astype(BF16)
    for g in range(C_KV):
        for r in range(C_GRP):
            kt_ref[0, g, r * C_HD:(r + 1) * C_HD, :] = kt[g * C_HD:(g + 1) * C_HD, :]

    va = (seg(AV, AV_W) + vone_ref[...]).astype(BF16)
    va_ref[0, 0] = va[:, :LANES]
    va_ref[0, 1] = va[:, LANES:]

    du_ref[0] = seg(DU, D_W).astype(BF16)
    dv = seg(DV, D_W)
    dms = jnp.mean(dv * dv, axis=-1, keepdims=True)
    dvn_ref[0] = (dv * lax.rsqrt(dms + EPS) * sg_ref[...]).astype(BF16)

    z = seg(ZZ, D_MIX)
    sz_ref[0] = (z / (1.0 + jnp.exp(-z))).astype(BF16)


def _inproj(x, mod, lp, tabs):
    bsz, n, _ = x.shape
    tm = min(TM_IN, n)
    grid = (bsz, n // tm)
    tok = lambda w: pl.BlockSpec((1, tm, w), lambda b, i: (b, i, 0))
    out_shapes = (
        jax.ShapeDtypeStruct((bsz, n, A_KW), BF16),
        jax.ShapeDtypeStruct((bsz, n, A_KW), BF16),
        jax.ShapeDtypeStruct((bsz, n, A_W), BF16),
        jax.ShapeDtypeStruct((bsz, n, A_KW), F32),
        jax.ShapeDtypeStruct((bsz, n, A_KW), F32),
        jax.ShapeDtypeStruct((bsz, n, B_W), BF16),
        jax.ShapeDtypeStruct((bsz, n, C_W), BF16),
        jax.ShapeDtypeStruct((bsz, C_KV, C_GRP * C_HD, n), BF16),
        jax.ShapeDtypeStruct((bsz, C_KV, n, LANES), BF16),
        jax.ShapeDtypeStruct((bsz, n, D_W), BF16),
        jax.ShapeDtypeStruct((bsz, n, D_W), BF16),
        jax.ShapeDtypeStruct((bsz, n, D_MIX), BF16),
    )
    out_specs = (
        tok(A_KW), tok(A_KW), tok(A_W), tok(A_KW), tok(A_KW), tok(B_W), tok(C_W),
        pl.BlockSpec((1, C_KV, C_GRP * C_HD, tm), lambda b, i: (b, 0, 0, i)),
        pl.BlockSpec((1, C_KV, tm, LANES), lambda b, i: (b, 0, i, 0)),
        tok(D_W), tok(D_W), tok(D_MIX),
    )
    in_specs = [
        tok(D_MODEL),
        pl.BlockSpec((1, 1, 3 * D_MODEL), lambda b, i: (b, 0, 0)),
        _full((1, D_MODEL)),
        _full((D_MODEL, W_PACKED)),
        _full((LANES, 2 * A_KW)),
        _full((1, 2 * A_KW)),
        _full((1, C_W)),
        _full((1, C_KVW)),
        _full((1, D_W)),
        pl.BlockSpec((tm, LANES), lambda b, i: (i, 0)),
        pl.BlockSpec((tm, LANES), lambda b, i: (i, 0)),
        _full((C_W, C_W)),
        _full((1, 2 * LANES)),
    ]
    return pl.pallas_call(
        _inproj_kernel, out_shape=out_shapes, grid=grid, in_specs=in_specs, out_specs=out_specs,
        compiler_params=_params("parallel", "parallel"), name="inproj",
    )(x, mod, lp["pre_g"], lp["w_in"], lp["wg2"], lp["bg"], lp["qg"], lp["kg"], lp["sg"],
      tabs["cos"], tabs["sin"], lp["bd_q"], lp["vone"])


def _head_match(shape, rows_per_head, cols_per_head):
    return (lax.broadcasted_iota(jnp.int32, shape, 0) // rows_per_head
            == lax.broadcasted_iota(jnp.int32, shape, 1) // cols_per_head)


def _gla_band_kernel(q_ref, k_ref, v_ref, gf_ref, gb_ref, o_ref, kf_ref, vf_ref, bf_ref, cb_ref):
    ts = q_ref.shape[2]
    hsum = jnp.where(_head_match((A_KW, A_W), A_DK, A_DV), 1.0, 0.0).astype(BF16)
    run = jnp.zeros((ts, A_KW), F32)
    for p in range(GLA_SUB):
        run = run + gf_ref[0, p]
        bf_ref[p] = run
        kf_ref[p] = k_ref[0, p].astype(F32)
        vf_ref[p] = v_ref[0, p].astype(F32)
    run = jnp.zeros((ts, A_KW), F32)
    for p in reversed(range(GLA_SUB)):
        run = run + gb_ref[0, p]
        cb_ref[p] = run
    for p in range(GLA_SUB):
        q = q_ref[0, p].astype(F32)
        prods = []
        for s in range(GLA_SUB):
            qk = q * kf_ref[s]
            if s < p:
                qk = qk * jnp.exp(bf_ref[p] - bf_ref[s])
            elif s > p:
                qk = qk * jnp.exp(cb_ref[p] - cb_ref[s])
            prods.append(qk.astype(BF16))
        a = jnp.dot(jnp.concatenate(prods, axis=0), hsum, preferred_element_type=F32)
        acc = a[0:ts] * vf_ref[0]
        for s in range(1, GLA_SUB):
            acc = acc + a[s * ts:(s + 1) * ts] * vf_ref[s]
        o_ref[0, p] = acc.astype(BF16)


def _gla_band(gq, gk, gv, gf, gb):
    bsz, n, _ = gq.shape
    ns = n // GLA_SUB
    ts = min(GLA_BAND_TILE, ns)
    perm = lambda a: a.reshape(bsz, ns, GLA_SUB, a.shape[-1]).transpose(0, 2, 1, 3)
    blk = lambda w: pl.BlockSpec((1, GLA_SUB, ts, w), lambda b, i: (b, 0, i, 0))
    o = pl.pallas_call(
        _gla_band_kernel,
        out_shape=jax.ShapeDtypeStruct((bsz, GLA_SUB, ns, A_W), BF16),
        grid=(bsz, ns // ts),
        in_specs=[blk(A_KW), blk(A_KW), blk(A_W), blk(A_KW), blk(A_KW)],
        out_specs=blk(A_W),
        scratch_shapes=[
            pltpu.VMEM((GLA_SUB, ts, A_KW), F32), pltpu.VMEM((GLA_SUB, ts, A_W), F32),
            pltpu.VMEM((GLA_SUB, ts, A_KW), F32), pltpu.VMEM((GLA_SUB, ts, A_KW), F32),
        ],
        compiler_params=_params("parallel", "parallel"), name="gla_band",
    )(perm(gq), perm(gk), perm(gv), perm(gf), perm(gb))
    return o.transpose(0, 2, 1, 3).reshape(bsz, n, A_W)


def _gla_prep(q_ref, k_ref, g_ref, qt_ref, kh_ref, dect_ref, reverse):
    seg = q_ref.shape[1]
    r_i = lax.broadcasted_iota(jnp.int32, (GLA_GRP, GLA_GRP), 0)
    c_i = lax.broadcasted_iota(jnp.int32, (GLA_GRP, GLA_GRP), 1)
    same = (r_i // GLA_SUB) == (c_i // GLA_SUB)
    tri = (c_i >= r_i) if reverse else (c_i <= r_i)
    cum_mat = jnp.where(same & tri, 1.0, 0.0).astype(F32)
    tot_mat = jnp.where(same, 1.0, 0.0).astype(F32)
    for r in range(seg // GLA_GRP):
        rows = slice(r * GLA_GRP, (r + 1) * GLA_GRP)
        g = g_ref[0, rows, :]
        cum = jnp.dot(cum_mat, g, precision=HIGHEST, preferred_element_type=F32)
        tot = jnp.dot(tot_mat, g, precision=HIGHEST, preferred_element_type=F32)
        qt_ref[rows, :] = (q_ref[0, rows, :].astype(F32) * jnp.exp(cum)).astype(BF16)
        kh_ref[rows, :] = (k_ref[0, rows, :].astype(F32) * jnp.exp(tot - cum)).astype(BF16)
    sel = jnp.where(lax.broadcasted_iota(jnp.int32, (seg, LANES), 0) // GLA_SUB
                    == lax.broadcasted_iota(jnp.int32, (seg, LANES), 1), 1.0, 0.0).astype(F32)
    tot_t = lax.dot_general(g_ref[0], sel, (((0,), (0,)), ((), ())), precision=HIGHEST,
                            preferred_element_type=F32)
    dect_ref[...] = jnp.exp(tot_t)


def _gla_rec_kernel(qf_ref, kf_ref, vf_ref, gf_ref, qb_ref, kb_ref, vb_ref, gb_ref, of_ref, ob_ref,
                    sf_ref, sb_ref, qtf_ref, khf_ref, dtf_ref, qtb_ref, khb_ref, dtb_ref):
    @pl.when(pl.program_id(1) == 0)
    def _():
        sf_ref[...] = jnp.zeros_like(sf_ref)
        sb_ref[...] = jnp.zeros_like(sb_ref)

    seg = qf_ref.shape[1]
    n_sub = seg // GLA_SUB
    _gla_prep(qf_ref, kf_ref, gf_ref, qtf_ref, khf_ref, dtf_ref, reverse=False)
    _gla_prep(qb_ref, kb_ref, gb_ref, qtb_ref, khb_ref, dtb_ref, reverse=True)
    smask = _head_match((A_KW, A_W), A_DK, A_DV)

    def update(i, kh_ref, v_ref):
        rows = slice(i * GLA_SUB, (i + 1) * GLA_SUB)
        upd = lax.dot_general(kh_ref[rows, :], v_ref[0, rows, :], (((0,), (0,)), ((), ())),
                              preferred_element_type=F32)
        return jnp.where(smask, upd, 0.0)

    def step(i, s, upd, qt_ref, dect_ref, o_ref):
        rows = slice(i * GLA_SUB, (i + 1) * GLA_SUB)
        o_ref[0, rows, :] = jnp.dot(qt_ref[rows, :], s.astype(BF16), preferred_element_type=F32).astype(BF16)
        return s * dect_ref[:, i:i + 1] + upd

    order_f = list(range(n_sub))
    order_b = list(reversed(range(n_sub)))
    upd_f = [update(i, khf_ref, vf_ref) for i in order_f[:GLA_LOOKAHEAD]]
    upd_b = [update(i, khb_ref, vb_ref) for i in order_b[:GLA_LOOKAHEAD]]
    s_f = sf_ref[...]
    s_b = sb_ref[...]
    for j in range(n_sub):
        if j + GLA_LOOKAHEAD < n_sub:
            upd_f.append(update(order_f[j + GLA_LOOKAHEAD], khf_ref, vf_ref))
            upd_b.append(update(order_b[j + GLA_LOOKAHEAD], khb_ref, vb_ref))
        s_f = step(order_f[j], s_f, upd_f[j], qtf_ref, dtf_ref, of_ref)
        s_b = step(order_b[j], s_b, upd_b[j], qtb_ref, dtb_ref, ob_ref)
    sf_ref[...] = s_f
    sb_ref[...] = s_b


def _gla_rec(gq, gk, gv, gf, gb):
    bsz, n, _ = gq.shape
    seg = min(GLA_SEG, n)
    nseg = n // seg
    assert seg // GLA_SUB <= LANES
    fwd = lambda w: pl.BlockSpec((1, seg, w), lambda b, s: (b, s, 0))
    bwd = lambda w: pl.BlockSpec((1, seg, w), lambda b, s: (b, nseg - 1 - s, 0))
    return pl.pallas_call(
        _gla_rec_kernel,
        out_shape=(jax.ShapeDtypeStruct((bsz, n, A_W), BF16), jax.ShapeDtypeStruct((bsz, n, A_W), BF16)),
        grid=(bsz, nseg),
        in_specs=[fwd(A_KW), fwd(A_KW), fwd(A_W), fwd(A_KW), bwd(A_KW), bwd(A_KW), bwd(A_W), bwd(A_KW)],
        out_specs=(fwd(A_W), bwd(A_W)),
        scratch_shapes=[
            pltpu.VMEM((A_KW, A_W), F32), pltpu.VMEM((A_KW, A_W), F32),
            pltpu.VMEM((seg, A_KW), BF16), pltpu.VMEM((seg, A_KW), BF16), pltpu.VMEM((A_KW, LANES), F32),
            pltpu.VMEM((seg, A_KW), BF16), pltpu.VMEM((seg, A_KW), BF16), pltpu.VMEM((A_KW, LANES), F32),
        ],
        compiler_params=_params("parallel", "arbitrary"), name="gla_rec",
    )(gq, gk, gv, gf, gq, gk, gv, gb)


def _fft1_kernel(u_ref, a1_ref, cos_ref, sin_ref, t_ref):
    n1 = u_ref.shape[1]
    nb = cos_ref.shape[0]
    t = jnp.dot(a1_ref[...], u_ref[0], preferred_element_type=F32)
    for j in range(nb):
        cols = slice(j * B_W, (j + 1) * B_W)
        tr = t[:n1, cols]
        ti = t[n1:, cols]
        c = jnp.concatenate([cos_ref[j]] * (B_W // LANES), axis=-1)
        s = jnp.concatenate([sin_ref[j]] * (B_W // LANES), axis=-1)
        t_ref[0, 0, :, cols] = (tr * c + ti * s).astype(BF16)
        t_ref[0, 1, :, cols] = (ti * c - tr * s).astype(BF16)


def _fft2_kernel(t_ref, l_ref, cc_ref, cs_ref, w_ref, o_ref):
    k1b, n2 = t_ref.shape[2], t_ref.shape[3]
    half = k1b * n2
    x = t_ref[0].reshape(2 * half, B_W)
    g = jnp.dot(l_ref[...], x, preferred_element_type=F32)
    gr = g[:half].astype(BF16)
    gi = g[half:].astype(BF16)
    mixed = (jnp.dot(gr, cc_ref[...], preferred_element_type=F32)
             + jnp.dot(gi, cs_ref[...], preferred_element_type=F32))
    y = jnp.dot(mixed.astype(BF16), w_ref[...], preferred_element_type=F32)
    o_ref[0] = y.reshape(n2, k1b, B_W)


def _fnet(fu, lp, tabs):
    bsz, n, _ = fu.shape
    n2 = FFT_N2
    n1 = n // n2
    nb = FFT_NB
    k1b = FFT_K1
    u2 = fu.reshape(bsz, n1, n2 * B_W)
    tt = pl.pallas_call(
        _fft1_kernel,
        out_shape=jax.ShapeDtypeStruct((bsz, 2, n1, n2 * B_W), BF16),
        grid=(n2 // nb, bsz),
        in_specs=[
            pl.BlockSpec((1, n1, nb * B_W), lambda j, b: (b, 0, j)),
            _full((2 * n1, n1)),
            pl.BlockSpec((nb, n1, LANES), lambda j, b: (j, 0, 0)),
            pl.BlockSpec((nb, n1, LANES), lambda j, b: (j, 0, 0)),
        ],
        out_specs=pl.BlockSpec((1, 2, n1, nb * B_W), lambda j, b: (b, 0, 0, j)),
        compiler_params=_params("parallel", "parallel"), name="fft_pass1",
    )(u2, tabs["a1"], tabs["tw_cos"], tabs["tw_sin"])
    t5 = tt.reshape(bsz, 2, n1, n2, B_W)
    y = pl.pallas_call(
        _fft2_kernel,
        out_shape=jax.ShapeDtypeStruct((bsz, n2, n1, B_W), F32),
        grid=(bsz, n1 // k1b),
        in_specs=[
            pl.BlockSpec((1, 2, k1b, n2, B_W), lambda b, i: (b, 0, i, 0, 0)),
            _full((2 * k1b * n2, 2 * k1b * n2)),
            _full((B_W, B_W)), _full((B_W, B_W)), _full((B_W, B_W)),
        ],
        out_specs=pl.BlockSpec((1, n2, k1b, B_W), lambda b, i: (b, 0, i, 0)),
        compiler_params=_params("parallel", "parallel"), name="fft_pass2",
    )(t5, tabs["l2"], tabs["chan_cos"], tabs["chan_sin"], lp["fnet_w"])
    return y.reshape(bsz, n, B_W)


def _attn_kernel(q_ref, kt_ref, va_ref, o_ref, qm_ref, m_ref, acc_ref):
    tq = q_ref.shape[1]
    n = kt_ref.shape[3]
    tk = min(ATT_TK, n)
    q = q_ref[0]
    lane = lax.broadcasted_iota(jnp.int32, q.shape, 1)
    for h in range(C_GRP):
        qm_ref[h * tq:(h + 1) * tq, :] = jnp.where(lane // C_HD == h, q, jnp.zeros_like(q))
    m_ref[...] = jnp.full(m_ref.shape, -jnp.inf, F32)
    acc_ref[...] = jnp.zeros(acc_ref.shape, F32)

    def scores(c):
        c0 = pl.multiple_of(c * tk, tk)
        return jnp.dot(qm_ref[...], kt_ref[0, 0, :, pl.ds(c0, tk)], preferred_element_type=F32)

    def softmax_accumulate(c, s):
        c0 = pl.multiple_of(c * tk, tk)
        v = va_ref[0, 0, pl.ds(c0, tk), :]
        s_max = s[:, 0:LANES]
        for t in range(1, tk // LANES):
            s_max = jnp.maximum(s_max, s[:, t * LANES:(t + 1) * LANES])
        row_max = jnp.max(s_max, axis=-1, keepdims=True)
        m_old = m_ref[...]
        m_new = jnp.maximum(m_old, row_max)
        p = jnp.exp2(s - jnp.concatenate([m_new] * (tk // LANES), axis=-1))
        alpha = jnp.exp2(m_old - m_new)
        acc_ref[...] = alpha * acc_ref[...] + jnp.dot(p.astype(BF16), v, preferred_element_type=F32)
        m_ref[...] = m_new

    def step(c, carry):
        softmax_accumulate(c, scores(c))
        return carry

    lax.fori_loop(0, n // tk, step, 0, unroll=ATT_UNROLL)
    acc = acc_ref[...]
    inv = 1.0 / acc[:, ONES_LANE:ONES_LANE + 1]
    outs = [(acc[h * tq:(h + 1) * tq, 0:C_HD] * inv[h * tq:(h + 1) * tq]) for h in range(C_GRP)]
    o_ref[0] = jnp.concatenate(outs, axis=-1).astype(BF16)


def _attention(aq, kt4, va):
    bsz, n, _ = aq.shape
    tq = min(ATT_TQ, n)
    return pl.pallas_call(
        _attn_kernel,
        out_shape=jax.ShapeDtypeStruct((bsz, n, C_W), BF16),
        grid=(bsz, C_KV, n // tq),
        in_specs=[
            pl.BlockSpec((1, tq, C_GRP * C_HD), lambda b, g, i: (b, i, g)),
            pl.BlockSpec((1, 1, C_GRP * C_HD, n), lambda b, g, i: (b, g, 0, 0)),
            pl.BlockSpec((1, 1, n, LANES), lambda b, g, i: (b, g, 0, 0)),
        ],
        out_specs=pl.BlockSpec((1, tq, C_GRP * C_HD), lambda b, g, i: (b, i, g)),
        scratch_shapes=[
            pltpu.VMEM((C_GRP * tq, C_GRP * C_HD), BF16),
            pltpu.VMEM((C_GRP * tq, LANES), F32),
            pltpu.VMEM((C_GRP * tq, LANES), F32),
        ],
        compiler_params=_params("parallel", "parallel", "arbitrary"), name="attention",
    )(aq, kt4, va)


def _outproj_kernel(x_ref, mod_ref, oa_ref, of_ref, ob_ref, yb_ref, ao_ref, du_ref, dvn_ref, sz_ref,
                    w_ref, postg_ref, og_ref, bd_ref, sw_ref, sb_ref, o_ref):
    tm = x_ref.shape[1]
    sz = sz_ref[0].astype(F32)

    o = oa_ref[0].astype(F32) + of_ref[0].astype(F32) + ob_ref[0].astype(F32)
    oms = jnp.dot((o * o).astype(BF16), bd_ref[...], preferred_element_type=F32)
    out_a = o * lax.rsqrt(oms + EPS) * og_ref[...]
    y = jnp.dot((out_a * sz[:, 0:A_W]).astype(BF16), w_ref[0:A_W, :], preferred_element_type=F32)

    m_b = (yb_ref[0] * sz[:, A_W:A_W + B_W]).astype(BF16)
    y = y + jnp.dot(m_b, w_ref[A_W:A_W + B_W, :], preferred_element_type=F32)

    m_c = (ao_ref[0].astype(F32) * sz[:, A_W + B_W:A_W + B_W + C_W]).astype(BF16)
    y = y + jnp.dot(m_c, w_ref[A_W + B_W:A_W + B_W + C_W, :], preferred_element_type=F32)

    lane = lax.broadcasted_iota(jnp.int32, (SGU_CHUNK, D_W), 1)
    parts = []
    for c in range(tm // SGU_CHUNK):
        rows = slice(c * SGU_CHUNK, (c + 1) * SGU_CHUNK)
        r = jnp.dot(sw_ref[...], dvn_ref[0, rows, :], preferred_element_type=F32)
        mix = sb_ref[...]
        for g in range(D_GROUPS):
            mix = mix + jnp.where(lane // D_GC == g, r[g * SGU_CHUNK:(g + 1) * SGU_CHUNK, :], 0.0)
        parts.append(du_ref[0, rows, :].astype(F32) * mix)
    out_d = jnp.concatenate(parts, axis=0)
    m_d = (out_d * sz[:, A_W + B_W + C_W:]).astype(BF16)
    y = y + jnp.dot(m_d, w_ref[A_W + B_W + C_W:, :], preferred_element_type=F32)

    yms = jnp.mean(y * y, axis=-1, keepdims=True)
    yn = y * lax.rsqrt(yms + EPS) * postg_ref[...]
    gate = mod_ref[0, :, 2 * D_MODEL:3 * D_MODEL]
    o_ref[0] = x_ref[0] + gate * yn


def _outproj(x, mod, oa, of, ob, yb, ao, du, dvn, sz, lp):
    bsz, n, _ = x.shape
    tm = min(TM_OUT, n)
    tok = lambda w: pl.BlockSpec((1, tm, w), lambda b, i: (b, i, 0))
    return pl.pallas_call(
        _outproj_kernel,
        out_shape=jax.ShapeDtypeStruct((bsz, n, D_MODEL), F32),
        grid=(bsz, n // tm),
        in_specs=[
            tok(D_MODEL),
            pl.BlockSpec((1, 1, 3 * D_MODEL), lambda b, i: (b, 0, 0)),
            tok(A_W), tok(A_W), tok(A_W), tok(B_W), tok(C_W), tok(D_W), tok(D_W), tok(D_MIX),
            _full((D_MIX, D_MODEL)), _full((1, D_MODEL)), _full((1, A_W)), _full((A_W, A_W)),
            _full((D_GROUPS * SGU_CHUNK, SGU_CHUNK)), _full((SGU_CHUNK, D_W)),
        ],
        out_specs=tok(D_MODEL),
        compiler_params=_params("parallel", "parallel"), name="outproj",
    )(x, mod, oa, of, ob, yb, ao, du, dvn, sz,
      lp["w_out"], lp["post_g"], lp["og"], lp["bd_a"], lp["sgu_w"], lp["sgu_b"])


def _pack_w_in(w):
    sizes = [A_KW, A_KW, A_W, 2 * GLA_RANK, B_W, C_W, C_KVW, C_KVW, D_W, D_W, D_MIX]
    off = [0] + [int(o) for o in np.cumsum(sizes)]
    a_q, a_k, a_v, a_lr, b_u, c_q, c_k, c_v, d_u, d_v, z = [w[:, off[i]:off[i + 1]] for i in range(len(sizes))]

    def deinterleave(x, heads):
        r = x.reshape(D_MODEL, heads, C_HD // 2, 2)
        return jnp.concatenate([r[..., 0], r[..., 1]], axis=-1).reshape(D_MODEL, heads * C_HD)

    zeros = lambda width: jnp.zeros((D_MODEL, width), w.dtype)
    pieces = [a_q, a_k, a_v, b_u, deinterleave(c_q, C_HEADS), deinterleave(c_k, C_KV)]
    for h in range(C_KV):
        pieces += [c_v[:, h * C_HD:(h + 1) * C_HD], zeros(LANES - C_HD)]
    pieces += [d_u, d_v, z, a_lr, zeros(LANES - 2 * GLA_RANK)]
    packed = jnp.concatenate(pieces, axis=1)
    assert packed.shape[1] == W_PACKED
    return packed.astype(BF16)


def _block_diag_mean(width, group):
    idx = np.arange(width) // group
    return jnp.asarray((idx[:, None] == idx[None, :]).astype(np.float32) / group, BF16)


def _layer_params(l, ada_unused, norm_pre_g, norm_post_g, w_in, gla_wg2_f, gla_bg_f, gla_wg2_b, gla_bg_b,
                  gla_onorm_g, fnet_w, q_norm_g, k_norm_g, sgu_norm_g, sgu_w, sgu_b, w_out):
    deint = np.concatenate([np.arange(0, C_HD, 2), np.arange(1, C_HD, 2)])
    w = _pack_w_in(w_in[l])
    wg2 = jnp.zeros((LANES, 2 * A_KW), F32)
    wg2 = wg2.at[0:GLA_RANK, 0:A_KW].set(gla_wg2_f[l])
    wg2 = wg2.at[GLA_RANK:2 * GLA_RANK, A_KW:].set(gla_wg2_b[l])
    vone = np.zeros((1, 2 * LANES), np.float32)
    vone[0, ONES_LANE] = 1.0
    vone[0, LANES + ONES_LANE] = 1.0
    return {
        "pre_g": norm_pre_g[l].reshape(1, D_MODEL),
        "post_g": norm_post_g[l].reshape(1, D_MODEL),
        "w_in": w,
        "wg2": wg2.astype(BF16),
        "bg": jnp.concatenate([gla_bg_f[l], gla_bg_b[l]]).reshape(1, 2 * A_KW),
        "qg": jnp.tile(q_norm_g[l][deint], C_HEADS).reshape(1, C_W),
        "kg": jnp.tile(k_norm_g[l][deint], C_KV).reshape(1, C_KVW),
        "sg": sgu_norm_g[l].reshape(1, D_W),
        "bd_q": _block_diag_mean(C_W, C_HD),
        "bd_a": _block_diag_mean(A_W, A_DV),
        "vone": jnp.asarray(vone),
        "og": jnp.tile(gla_onorm_g[l], A_HEADS).reshape(1, A_W),
        "fnet_w": fnet_w[l].astype(BF16),
        "sgu_w": sgu_w[l].reshape(D_GROUPS * SGU_CHUNK, SGU_CHUNK).astype(BF16),
        "sgu_b": jnp.repeat(sgu_b[l].T, D_GC, axis=1),
        "w_out": w_out[l].astype(BF16),
    }


def _seq_tables(n):
    rows = n // GRID_W
    row = jnp.repeat(jnp.arange(rows, dtype=F32), GRID_W)
    col = jnp.tile(jnp.arange(GRID_W, dtype=F32), rows)
    rope_axis = C_HD // 2
    freqs = ROPE_THETA ** (-jnp.arange(0, rope_axis, 2, dtype=F32) / rope_axis)
    ang = jnp.concatenate([row[:, None] * freqs, col[:, None] * freqs], axis=-1)
    cos, sin = jnp.cos(ang), jnp.sin(ang)
    cos_t = jnp.tile(jnp.concatenate([cos, cos], axis=-1), (1, LANES // C_HD))
    sin_t = jnp.tile(jnp.concatenate([-sin, sin], axis=-1), (1, LANES // C_HD))

    n2 = FFT_N2
    n1 = n // n2

    def dft_angles(rows, cols, period):
        prod = (jnp.arange(rows, dtype=jnp.int32)[:, None] * jnp.arange(cols, dtype=jnp.int32)[None, :]) % period
        return prod.astype(F32) * (2.0 * math.pi / period)

    a1_ang = dft_angles(n1, n1, n1)
    a1 = jnp.concatenate([jnp.cos(a1_ang), -jnp.sin(a1_ang)], axis=0)
    tw_ang = dft_angles(n2, n1, n)
    scale = 1.0 / math.sqrt(n * B_GC)
    tw_cos = jnp.broadcast_to((jnp.cos(tw_ang) * scale)[:, :, None], (n2, n1, LANES))
    tw_sin = jnp.broadcast_to((jnp.sin(tw_ang) * scale)[:, :, None], (n2, n1, LANES))
    ang2 = dft_angles(n2, n2, n2)
    c2, s2 = jnp.cos(ang2), jnp.sin(ang2)
    eye = jnp.eye(FFT_K1, dtype=F32)
    blk = lambda m: (m[:, None, None, :] * eye[None, :, :, None]).reshape(n2 * FFT_K1, FFT_K1 * n2)
    l2 = jnp.block([[blk(c2), blk(s2)], [blk(-s2), blk(c2)]])
    angc = dft_angles(B_GC, B_GC, B_GC)
    grp = jnp.eye(B_GROUPS, dtype=F32)
    kron = lambda m: (grp[:, None, :, None] * m[None, :, None, :]).reshape(B_W, B_W)
    chan_cos = kron(jnp.cos(angc))
    chan_sin = kron(jnp.sin(angc))
    return {
        "cos": cos_t, "sin": sin_t,
        "a1": a1.astype(BF16),
        "tw_cos": tw_cos, "tw_sin": tw_sin,
        "l2": l2.astype(BF16),
        "chan_cos": chan_cos.astype(BF16), "chan_sin": chan_sin.astype(BF16),
    }


def _layer(x, mod, lp, tabs):
    gq, gk, gv, gf, gb, fu, aq, kt4, va, du, dvn, sz = _inproj(x, mod, lp, tabs)
    oa = _gla_band(gq, gk, gv, gf, gb)
    of, ob = _gla_rec(gq, gk, gv, gf, gb)
    yb = _fnet(fu, lp, tabs)
    ao = _attention(aq, kt4, va)
    return _outproj(x, mod, oa, of, ob, yb, ao, du, dvn, sz, lp)


def kernel(x_prompt, x_sample, c_prompt, c_sample, ada_w, ada_b, norm_pre_g, norm_post_g, w_in,
           gla_wg2_f, gla_bg_f, gla_wg2_b, gla_bg_b, gla_onorm_g, fnet_w, q_norm_g, k_norm_g,
           sgu_norm_g, sgu_w, sgu_b, w_out):
    bp, bs = x_prompt.shape[0], x_sample.shape[0]
    pad_rows = (-(bp + bs)) % 8
    c_all = jnp.concatenate([c_prompt, c_sample, jnp.zeros((pad_rows, D_MODEL), F32)], axis=0)
    mod = _adaln(c_all, ada_w, ada_b)
    tabs_p = _seq_tables(x_prompt.shape[1])
    tabs_s = _seq_tables(x_sample.shape[1])
    y_prompt, y_sample = x_prompt, x_sample
    for l in range(DEPTH):
        lp = _layer_params(l, None, norm_pre_g, norm_post_g, w_in, gla_wg2_f, gla_bg_f, gla_wg2_b,
                           gla_bg_b, gla_onorm_g, fnet_w, q_norm_g, k_norm_g, sgu_norm_g, sgu_w,
                           sgu_b, w_out)
        mod_p = mod[l, 0:bp].reshape(bp, 1, 3 * D_MODEL)
        mod_s = mod[l, bp:bp + bs].reshape(bs, 1, 3 * D_MODEL)
        y_prompt = _layer(y_prompt, mod_p, lp, tabs_p)
        y_sample = _layer(y_sample, mod_s, lp, tabs_s)
    return (y_prompt, y_sample)
```

```python
import math

import numpy as np
import jax
import jax.numpy as jnp
from jax import lax
from jax.experimental import pallas as pl
from jax.experimental.pallas import tpu as pltpu

F32 = jnp.float32
BF16 = jnp.bfloat16
HIGHEST = lax.Precision.HIGHEST

D_MODEL = 1024
DEPTH = 2
GRID_W = 64
A_HEADS, A_DK, A_DV = 4, 32, 64
A_W, A_KW = A_HEADS * A_DV, A_HEADS * A_DK
GLA_RANK = 16
GLA_GATE_NORM = 16.0
B_GROUPS, B_GC = 4, 64
B_W = B_GROUPS * B_GC
C_HEADS, C_KV, C_HD = 8, 2, 64
C_GRP = C_HEADS // C_KV
C_W, C_KVW = C_HEADS * C_HD, C_KV * C_HD
ROPE_THETA = 10000.0
D_GROUPS, D_GC = 4, 64
D_W = D_GROUPS * D_GC
SGU_CHUNK = 128
D_MIX = A_W + B_W + C_W + D_W
EPS = 1e-6
LOG2E = 1.4426950408889634

LANES = 128
VMEM_LIMIT_BYTES = 56 * 1024 * 1024

GQ, GK, GV, FU, AQ, AK, LR, AV, DU, DV, ZZ = 0, 128, 256, 512, 768, 1280, 1408, 1536, 1792, 2048, 2304
W_PACKED = 3584
AV_W = 2 * LANES
ONES_LANE = C_HD

TM_IN = 512
TM_OUT = 512
GLA_SUB = 16
GLA_BAND_TILE = 64
GLA_GRP = 128
GLA_SEG = 512
GLA_LOOKAHEAD = 3
FFT_N2 = 64
FFT_NB = 8
FFT_K1 = 8
ATT_TQ = 512
ATT_TK = 1024
ATT_UNROLL = 4


def _params(*sem):
    return pltpu.CompilerParams(dimension_semantics=sem, vmem_limit_bytes=VMEM_LIMIT_BYTES)


def _full(shape):
    n = len(shape)
    return pl.BlockSpec(shape, lambda *_: (0,) * n)


def _adaln_kernel(c_ref, w_ref, b_ref, o_ref):
    c = c_ref[...]
    sc = c / (1.0 + jnp.exp(-c))
    o_ref[0] = jnp.dot(sc, w_ref[0], precision=HIGHEST, preferred_element_type=F32) + b_ref[0]


def _adaln(c_all, ada_w, ada_b):
    rows = c_all.shape[0]
    ncol = 3 * D_MODEL // D_MODEL
    return pl.pallas_call(
        _adaln_kernel,
        out_shape=jax.ShapeDtypeStruct((DEPTH, rows, 3 * D_MODEL), F32),
        grid=(DEPTH, ncol),
        in_specs=[
            pl.BlockSpec((rows, D_MODEL), lambda l, j: (0, 0)),
            pl.BlockSpec((1, D_MODEL, D_MODEL), lambda l, j: (l, 0, j)),
            pl.BlockSpec((1, 1, D_MODEL), lambda l, j: (l, 0, j)),
        ],
        out_specs=pl.BlockSpec((1, rows, D_MODEL), lambda l, j: (l, 0, j)),
        compiler_params=_params("arbitrary", "arbitrary"),
        name="adaln",
    )(c_all, ada_w, ada_b.reshape(DEPTH, 1, 3 * D_MODEL))


def _rope(x, cos, sin_signed):
    width = x.shape[-1]
    lane = lax.broadcasted_iota(jnp.int32, x.shape, 1)
    first_half = (lane % C_HD) < (C_HD // 2)
    swapped = jnp.where(first_half, pltpu.roll(x, width - C_HD // 2, 1), pltpu.roll(x, C_HD // 2, 1))
    reps = width // LANES
    cos_w = jnp.concatenate([cos] * reps, axis=-1) if reps > 1 else cos
    sin_w = jnp.concatenate([sin_signed] * reps, axis=-1) if reps > 1 else sin_signed
    return x * cos_w + swapped * sin_w


def _inproj_kernel(x_ref, mod_ref, preg_ref, w_ref, wg2_ref, bg_ref, qg_ref, kg_ref, sg_ref,
                   cos_ref, sin_ref, bd_ref, vone_ref,
                   gq_ref, gk_ref, gv_ref, gf_ref, gb_ref, fu_ref, aq_ref, kt_ref, va_ref,
                   du_ref, dvn_ref, sz_ref):
    x = x_ref[0]
    ms = jnp.mean(x * x, axis=-1, keepdims=True)
    shift = mod_ref[0, :, 0:D_MODEL]
    scale = mod_ref[0, :, D_MODEL:2 * D_MODEL]
    h = (x * lax.rsqrt(ms + EPS)) * preg_ref[...] * (1.0 + scale) + shift
    hb = h.astype(BF16)

    def seg(off, width):
        return jnp.dot(hb, w_ref[:, off:off + width], preferred_element_type=F32)

    gqk = seg(GQ, 2 * A_KW)
    gq_ref[0] = (gqk[:, :A_KW] * (A_DK ** -0.5)).astype(BF16)
    gk_ref[0] = gqk[:, A_KW:].astype(BF16)
    gv_ref[0] = seg(GV, A_W).astype(BF16)
    klr = seg(AK, C_KVW + LANES)
    lr = klr[:, C_KVW:].astype(BF16)
    logits = jnp.dot(lr, wg2_ref[...], preferred_element_type=F32) + bg_ref[...]
    logg = (jnp.minimum(logits, 0.0) - jnp.log1p(jnp.exp(-jnp.abs(logits)))) * (1.0 / GLA_GATE_NORM)
    gf_ref[0] = logg[:, :A_KW]
    gb_ref[0] = logg[:, A_KW:]

    fu_ref[0] = seg(FU, B_W).astype(BF16)

    cos = cos_ref[...]
    sin = sin_ref[...]
    q = seg(AQ, C_W)
    qms = jnp.dot((q * q).astype(BF16), bd_ref[...], preferred_element_type=F32)
    qn = q * lax.rsqrt(qms + EPS) * qg_ref[...]
    aq_ref[0] = (_rope(qn, cos, sin) * (C_HD ** -0.5 * LOG2E)).astype(BF16)

    k = klr[:, :C_KVW]
    kms = jnp.dot((k * k).astype(BF16), bd_ref[0:C_KVW, 0:C_KVW], preferred_element_type=F32)
    kn = k * lax.rsqrt(kms + EPS) * kg_ref[...]
    kt = _rope(kn, cos, sin).T.astype(BF16)
    for g in range(C_KV):
        for r in range(C_GRP):
            kt_ref[0, g, r * C_HD:(r + 1) * C_HD, :] = kt[g * C_HD:(g + 1) * C_HD, :]

    va = (seg(AV, AV_W) + vone_ref[...]).astype(BF16)
    va_ref[0, 0] = va[:, :LANES]
    va_ref[0, 1] = va[:, LANES:]

    du_ref[0] = seg(DU, D_W).astype(BF16)
    dv = seg(DV, D_W)
    dms = jnp.mean(dv * dv, axis=-1, keepdims=True)
    dvn_ref[0] = (dv * lax.rsqrt(dms + EPS) * sg_ref[...]).astype(BF16)

    z = seg(ZZ, D_MIX)
    sz_ref[0] = (z / (1.0 + jnp.exp(-z))).astype(BF16)


def _inproj(x, mod, lp, tabs):
    bsz, n, _ = x.shape
    tm = min(TM_IN, n)
    grid = (bsz, n // tm)
    tok = lambda w: pl.BlockSpec((1, tm, w), lambda b, i: (b, i, 0))
    out_shapes = (
        jax.ShapeDtypeStruct((bsz, n, A_KW), BF16),
        jax.ShapeDtypeStruct((bsz, n, A_KW), BF16),
        jax.ShapeDtypeStruct((bsz, n, A_W), BF16),
        jax.ShapeDtypeStruct((bsz, n, A_KW), F32),
        jax.ShapeDtypeStruct((bsz, n, A_KW), F32),
        jax.ShapeDtypeStruct((bsz, n, B_W), BF16),
        jax.ShapeDtypeStruct((bsz, n, C_W), BF16),
        jax.ShapeDtypeStruct((bsz, C_KV, C_GRP * C_HD, n), BF16),
        jax.ShapeDtypeStruct((bsz, C_KV, n, LANES), BF16),
        jax.ShapeDtypeStruct((bsz, n, D_W), BF16),
        jax.ShapeDtypeStruct((bsz, n, D_W), BF16),
        jax.ShapeDtypeStruct((bsz, n, D_MIX), BF16),
    )
    out_specs = (
        tok(A_KW), tok(A_KW), tok(A_W), tok(A_KW), tok(A_KW), tok(B_W), tok(C_W),
        pl.BlockSpec((1, C_KV, C_GRP * C_HD, tm), lambda b, i: (b, 0, 0, i)),
        pl.BlockSpec((1, C_KV, tm, LANES), lambda b, i: (b, 0, i, 0)),
        tok(D_W), tok(D_W), tok(D_MIX),
    )
    in_specs = [
        tok(D_MODEL),
        pl.BlockSpec((1, 1, 3 * D_MODEL), lambda b, i: (b, 0, 0)),
        _full((1, D_MODEL)),
        _full((D_MODEL, W_PACKED)),
        _full((LANES, 2 * A_KW)),
        _full((1, 2 * A_KW)),
        _full((1, C_W)),
        _full((1, C_KVW)),
        _full((1, D_W)),
        pl.BlockSpec((tm, LANES), lambda b, i: (i, 0)),
        pl.BlockSpec((tm, LANES), lambda b, i: (i, 0)),
        _full((C_W, C_W)),
        _full((1, 2 * LANES)),
    ]
    return pl.pallas_call(
        _inproj_kernel, out_shape=out_shapes, grid=grid, in_specs=in_specs, out_specs=out_specs,
        compiler_params=_params("parallel", "parallel"), name="inproj",
    )(x, mod, lp["pre_g"], lp["w_in"], lp["wg2"], lp["bg"], lp["qg"], lp["kg"], lp["sg"],
      tabs["cos"], tabs["sin"], lp["bd_q"], lp["vone"])


def _head_match(shape, rows_per_head, cols_per_head):
    return (lax.broadcasted_iota(jnp.int32, shape, 0) // rows_per_head
            == lax.broadcasted_iota(jnp.int32, shape, 1) // cols_per_head)


def _gla_band_kernel(q_ref, k_ref, v_ref, gf_ref, gb_ref, o_ref, kf_ref, vf_ref, bf_ref, cb_ref):
    ts = q_ref.shape[2]
    hsum = jnp.where(_head_match((A_KW, A_W), A_DK, A_DV), 1.0, 0.0).astype(BF16)
    run = jnp.zeros((ts, A_KW), F32)
    for p in range(GLA_SUB):
        run = run + gf_ref[0, p]
        bf_ref[p] = run
        kf_ref[p] = k_ref[0, p].astype(F32)
        vf_ref[p] = v_ref[0, p].astype(F32)
    run = jnp.zeros((ts, A_KW), F32)
    for p in reversed(range(GLA_SUB)):
        run = run + gb_ref[0, p]
        cb_ref[p] = run
    for p in range(GLA_SUB):
        q = q_ref[0, p].astype(F32)
        prods = []
        for s in range(GLA_SUB):
            qk = q * kf_ref[s]
            if s < p:
                qk = qk * jnp.exp(bf_ref[p] - bf_ref[s])
            elif s > p:
                qk = qk * jnp.exp(cb_ref[p] - cb_ref[s])
            prods.append(qk.astype(BF16))
        a = jnp.dot(jnp.concatenate(prods, axis=0), hsum, preferred_element_type=F32)
        acc = a[0:ts] * vf_ref[0]
        for s in range(1, GLA_SUB):
            acc = acc + a[s * ts:(s + 1) * ts] * vf_ref[s]
        o_ref[0, p] = acc.astype(BF16)


def _gla_band(gq, gk, gv, gf, gb):
    bsz, n, _ = gq.shape
    ns = n // GLA_SUB
    ts = min(GLA_BAND_TILE, ns)
    perm = lambda a: a.reshape(bsz, ns, GLA_SUB, a.shape[-1]).transpose(0, 2, 1, 3)
    blk = lambda w: pl.BlockSpec((1, GLA_SUB, ts, w), lambda b, i: (b, 0, i, 0))
    o = pl.pallas_call(
        _gla_band_kernel,
        out_shape=jax.ShapeDtypeStruct((bsz, GLA_SUB, ns, A_W), BF16),
        grid=(bsz, ns // ts),
        in_specs=[blk(A_KW), blk(A_KW), blk(A_W), blk(A_KW), blk(A_KW)],
        out_specs=blk(A_W),
        scratch_shapes=[
            pltpu.VMEM((GLA_SUB, ts, A_KW), F32), pltpu.VMEM((GLA_SUB, ts, A_W), F32),
            pltpu.VMEM((GLA_SUB, ts, A_KW), F32), pltpu.VMEM((GLA_SUB, ts, A_KW), F32),
        ],
        compiler_params=_params("parallel", "parallel"), name="gla_band",
    )(perm(gq), perm(gk), perm(gv), perm(gf), perm(gb))
    return o.transpose(0, 2, 1, 3).reshape(bsz, n, A_W)


def _gla_prep(q_ref, k_ref, g_ref, qt_ref, kh_ref, dect_ref, reverse):
    seg = q_ref.shape[1]
    r_i = lax.broadcasted_iota(jnp.int32, (GLA_GRP, GLA_GRP), 0)
    c_i = lax.broadcasted_iota(jnp.int32, (GLA_GRP, GLA_GRP), 1)
    same = (r_i // GLA_SUB) == (c_i // GLA_SUB)
    tri = (c_i >= r_i) if reverse else (c_i <= r_i)
    cum_mat = jnp.where(same & tri, 1.0, 0.0).astype(F32)
    tot_mat = jnp.where(same, 1.0, 0.0).astype(F32)
    for r in range(seg // GLA_GRP):
        rows = slice(r * GLA_GRP, (r + 1) * GLA_GRP)
        g = g_ref[0, rows, :]
        cum = jnp.dot(cum_mat, g, precision=HIGHEST, preferred_element_type=F32)
        tot = jnp.dot(tot_mat, g, precision=HIGHEST, preferred_element_type=F32)
        qt_ref[rows, :] = (q_ref[0, rows, :].astype(F32) * jnp.exp(cum)).astype(BF16)
        kh_ref[rows, :] = (k_ref[0, rows, :].astype(F32) * jnp.exp(tot - cum)).astype(BF16)
    sel = jnp.where(lax.broadcasted_iota(jnp.int32, (seg, LANES), 0) // GLA_SUB
                    == lax.broadcasted_iota(jnp.int32, (seg, LANES), 1), 1.0, 0.0).astype(F32)
    tot_t = lax.dot_general(g_ref[0], sel, (((0,), (0,)), ((), ())), precision=HIGHEST,
                            preferred_element_type=F32)
    dect_ref[...] = jnp.exp(tot_t)


def _gla_rec_kernel(qf_ref, kf_ref, vf_ref, gf_ref, qb_ref, kb_ref, vb_ref, gb_ref, of_ref, ob_ref,
                    sf_ref, sb_ref, qtf_ref, khf_ref, dtf_ref, qtb_ref, khb_ref, dtb_ref):
    @pl.when(pl.program_id(1) == 0)
    def _():
        sf_ref[...] = jnp.zeros_like(sf_ref)
        sb_ref[...] = jnp.zeros_like(sb_ref)

    seg = qf_ref.shape[1]
    n_sub = seg // GLA_SUB
    _gla_prep(qf_ref, kf_ref, gf_ref, qtf_ref, khf_ref, dtf_ref, reverse=False)
    _gla_prep(qb_ref, kb_ref, gb_ref, qtb_ref, khb_ref, dtb_ref, reverse=True)
    smask = _head_match((A_KW, A_W), A_DK, A_DV)

    def update(i, kh_ref, v_ref):
        rows = slice(i * GLA_SUB, (i + 1) * GLA_SUB)
        upd = lax.dot_general(kh_ref[rows, :], v_ref[0, rows, :], (((0,), (0,)), ((), ())),
                              preferred_element_type=F32)
        return jnp.where(smask, upd, 0.0)

    def step(i, s, upd, qt_ref, dect_ref, o_ref):
        rows = slice(i * GLA_SUB, (i + 1) * GLA_SUB)
        o_ref[0, rows, :] = jnp.dot(qt_ref[rows, :], s.astype(BF16), preferred_element_type=F32).astype(BF16)
        return s * dect_ref[:, i:i + 1] + upd

    order_f = list(range(n_sub))
    order_b = list(reversed(range(n_sub)))
    upd_f = [update(i, khf_ref, vf_ref) for i in order_f[:GLA_LOOKAHEAD]]
    upd_b = [update(i, khb_ref, vb_ref) for i in order_b[:GLA_LOOKAHEAD]]
    s_f = sf_ref[...]
    s_b = sb_ref[...]
    for j in range(n_sub):
        if j + GLA_LOOKAHEAD < n_sub:
            upd_f.append(update(order_f[j + GLA_LOOKAHEAD], khf_ref, vf_ref))
            upd_b.append(update(order_b[j + GLA_LOOKAHEAD], khb_ref, vb_ref))
        s_f = step(order_f[j], s_f, upd_f[j], qtf_ref, dtf_ref, of_ref)
        s_b = step(order_b[j], s_b, upd_b[j], qtb_ref, dtb_ref, ob_ref)
    sf_ref[...] = s_f
    sb_ref[...] = s_b


def _gla_rec(gq, gk, gv, gf, gb):
    bsz, n, _ = gq.shape
    seg = min(GLA_SEG, n)
    nseg = n // seg
    assert seg // GLA_SUB <= LANES
    fwd = lambda w: pl.BlockSpec((1, seg, w), lambda b, s: (b, s, 0))
    bwd = lambda w: pl.BlockSpec((1, seg, w), lambda b, s: (b, nseg - 1 - s, 0))
    return pl.pallas_call(
        _gla_rec_kernel,
        out_shape=(jax.ShapeDtypeStruct((bsz, n, A_W), BF16), jax.ShapeDtypeStruct((bsz, n, A_W), BF16)),
        grid=(bsz, nseg),
        in_specs=[fwd(A_KW), fwd(A_KW), fwd(A_W), fwd(A_KW), bwd(A_KW), bwd(A_KW), bwd(A_W), bwd(A_KW)],
        out_specs=(fwd(A_W), bwd(A_W)),
        scratch_shapes=[
            pltpu.VMEM((A_KW, A_W), F32), pltpu.VMEM((A_KW, A_W), F32),
            pltpu.VMEM((seg, A_KW), BF16), pltpu.VMEM((seg, A_KW), BF16), pltpu.VMEM((A_KW, LANES), F32),
            pltpu.VMEM((seg, A_KW), BF16), pltpu.VMEM((seg, A_KW), BF16), pltpu.VMEM((A_KW, LANES), F32),
        ],
        compiler_params=_params("parallel", "arbitrary"), name="gla_rec",
    )(gq, gk, gv, gf, gq, gk, gv, gb)


def _fft1_kernel(u_ref, a1_ref, cos_ref, sin_ref, t_ref):
    n1 = u_ref.shape[1]
    nb = cos_ref.shape[0]
    t = jnp.dot(a1_ref[...], u_ref[0], preferred_element_type=F32)
    for j in range(nb):
        cols = slice(j * B_W, (j + 1) * B_W)
        tr = t[:n1, cols]
        ti = t[n1:, cols]
        c = jnp.concatenate([cos_ref[j]] * (B_W // LANES), axis=-1)
        s = jnp.concatenate([sin_ref[j]] * (B_W // LANES), axis=-1)
        t_ref[0, 0, :, cols] = (tr * c + ti * s).astype(BF16)
        t_ref[0, 1, :, cols] = (ti * c - tr * s).astype(BF16)


def _fft2_kernel(t_ref, l_ref, cc_ref, cs_ref, w_ref, o_ref):
    k1b, n2 = t_ref.shape[2], t_ref.shape[3]
    half = k1b * n2
    x = t_ref[0].reshape(2 * half, B_W)
    g = jnp.dot(l_ref[...], x, preferred_element_type=F32)
    gr = g[:half].astype(BF16)
    gi = g[half:].astype(BF16)
    mixed = (jnp.dot(gr, cc_ref[...], preferred_element_type=F32)
             + jnp.dot(gi, cs_ref[...], preferred_element_type=F32))
    y = jnp.dot(mixed.astype(BF16), w_ref[...], preferred_element_type=F32)
    o_ref[0] = y.reshape(n2, k1b, B_W)


def _fnet(fu, lp, tabs):
    bsz, n, _ = fu.shape
    n2 = FFT_N2
    n1 = n // n2
    nb = FFT_NB
    k1b = FFT_K1
    u2 = fu.reshape(bsz, n1, n2 * B_W)
    tt = pl.pallas_call(
        _fft1_kernel,
        out_shape=jax.ShapeDtypeStruct((bsz, 2, n1, n2 * B_W), BF16),
        grid=(n2 // nb, bsz),
        in_specs=[
            pl.BlockSpec((1, n1, nb * B_W), lambda j, b: (b, 0, j)),
            _full((2 * n1, n1)),
            pl.BlockSpec((nb, n1, LANES), lambda j, b: (j, 0, 0)),
            pl.BlockSpec((nb, n1, LANES), lambda j, b: (j, 0, 0)),
        ],
        out_specs=pl.BlockSpec((1, 2, n1, nb * B_W), lambda j, b: (b, 0, 0, j)),
        compiler_params=_params("parallel", "parallel"), name="fft_pass1",
    )(u2, tabs["a1"], tabs["tw_cos"], tabs["tw_sin"])
    t5 = tt.reshape(bsz, 2, n1, n2, B_W)
    y = pl.pallas_call(
        _fft2_kernel,
        out_shape=jax.ShapeDtypeStruct((bsz, n2, n1, B_W), F32),
        grid=(bsz, n1 // k1b),
        in_specs=[
            pl.BlockSpec((1, 2, k1b, n2, B_W), lambda b, i: (b, 0, i, 0, 0)),
            _full((2 * k1b * n2, 2 * k1b * n2)),
            _full((B_W, B_W)), _full((B_W, B_W)), _full((B_W, B_W)),
        ],
        out_specs=pl.BlockSpec((1, n2, k1b, B_W), lambda b, i: (b, 0, i, 0)),
        compiler_params=_params("parallel", "parallel"), name="fft_pass2",
    )(t5, tabs["l2"], tabs["chan_cos"], tabs["chan_sin"], lp["fnet_w"])
    return y.reshape(bsz, n, B_W)


def _attn_kernel(q_ref, kt_ref, va_ref, o_ref, qm_ref, m_ref, acc_ref):
    tq = q_ref.shape[1]
    n = kt_ref.shape[3]
    tk = min(ATT_TK, n)
    q = q_ref[0]
    lane = lax.broadcasted_iota(jnp.int32, q.shape, 1)
    for h in range(C_GRP):
        qm_ref[h * tq:(h + 1) * tq, :] = jnp.where(lane // C_HD == h, q, jnp.zeros_like(q))
    m_ref[...] = jnp.full(m_ref.shape, -jnp.inf, F32)
    acc_ref[...] = jnp.zeros(acc_ref.shape, F32)

    def scores(c):
        c0 = pl.multiple_of(c * tk, tk)
        return jnp.dot(qm_ref[...], kt_ref[0, 0, :, pl.ds(c0, tk)], preferred_element_type=F32)

    def softmax_accumulate(c, s):
        c0 = pl.multiple_of(c * tk, tk)
        v = va_ref[0, 0, pl.ds(c0, tk), :]
        s_max = s[:, 0:LANES]
        for t in range(1, tk // LANES):
            s_max = jnp.maximum(s_max, s[:, t * LANES:(t + 1) * LANES])
        row_max = jnp.max(s_max, axis=-1, keepdims=True)
        m_old = m_ref[...]
        m_new = jnp.maximum(m_old, row_max)
        p = jnp.exp2(s - jnp.concatenate([m_new] * (tk // LANES), axis=-1))
        alpha = jnp.exp2(m_old - m_new)
        acc_ref[...] = alpha * acc_ref[...] + jnp.dot(p.astype(BF16), v, preferred_element_type=F32)
        m_ref[...] = m_new

    def step(c, carry):
        softmax_accumulate(c, scores(c))
        return carry

    lax.fori_loop(0, n // tk, step, 0, unroll=ATT_UNROLL)
    acc = acc_ref[...]
    inv = 1.0 / acc[:, ONES_LANE:ONES_LANE + 1]
    outs = [(acc[h * tq:(h + 1) * tq, 0:C_HD] * inv[h * tq:(h + 1) * tq]) for h in range(C_GRP)]
    o_ref[0] = jnp.concatenate(outs, axis=-1).astype(BF16)


def _attention(aq, kt4, va):
    bsz, n, _ = aq.shape
    tq = min(ATT_TQ, n)
    return pl.pallas_call(
        _attn_kernel,
        out_shape=jax.ShapeDtypeStruct((bsz, n, C_W), BF16),
        grid=(bsz, C_KV, n // tq),
        in_specs=[
            pl.BlockSpec((1, tq, C_GRP * C_HD), lambda b, g, i: (b, i, g)),
            pl.BlockSpec((1, 1, C_GRP * C_HD, n), lambda b, g, i: (b, g, 0, 0)),
            pl.BlockSpec((1, 1, n, LANES), lambda b, g, i: (b, g, 0, 0)),
        ],
        out_specs=pl.BlockSpec((1, tq, C_GRP * C_HD), lambda b, g, i: (b, i, g)),
        scratch_shapes=[
            pltpu.VMEM((C_GRP * tq, C_GRP * C_HD), BF16),
            pltpu.VMEM((C_GRP * tq, LANES), F32),
            pltpu.VMEM((C_GRP * tq, LANES), F32),
        ],
        compiler_params=_params("parallel", "parallel", "arbitrary"), name="attention",
    )(aq, kt4, va)


def _outproj_kernel(x_ref, mod_ref, oa_ref, of_ref, ob_ref, yb_ref, ao_ref, du_ref, dvn_ref, sz_ref,
                    w_ref, postg_ref, og_ref, bd_ref, sw_ref, sb_ref, o_ref):
    tm = x_ref.shape[1]
    sz = sz_ref[0].astype(F32)

    o = oa_ref[0].astype(F32) + of_ref[0].astype(F32) + ob_ref[0].astype(F32)
    oms = jnp.dot((o * o).astype(BF16), bd_ref[...], preferred_element_type=F32)
    out_a = o * lax.rsqrt(oms + EPS) * og_ref[...]
    y = jnp.dot((out_a * sz[:, 0:A_W]).astype(BF16), w_ref[0:A_W, :], preferred_element_type=F32)

    m_b = (yb_ref[0] * sz[:, A_W:A_W + B_W]).astype(BF16)
    y = y + jnp.dot(m_b, w_ref[A_W:A_W + B_W, :], preferred_element_type=F32)

    m_c = (ao_ref[0].astype(F32) * sz[:, A_W + B_W:A_W + B_W + C_W]).astype(BF16)
    y = y + jnp.dot(m_c, w_ref[A_W + B_W:A_W + B_W + C_W, :], preferred_element_type=F32)

    lane = lax.broadcasted_iota(jnp.int32, (SGU_CHUNK, D_W), 1)
    parts = []
    for c in range(tm // SGU_CHUNK):
        rows = slice(c * SGU_CHUNK, (c + 1) * SGU_CHUNK)
        r = jnp.dot(sw_ref[...], dvn_ref[0, rows, :], preferred_element_type=F32)
        mix = sb_ref[...]
        for g in range(D_GROUPS):
            mix = mix + jnp.where(lane // D_GC == g, r[g * SGU_CHUNK:(g + 1) * SGU_CHUNK, :], 0.0)
        parts.append(du_ref[0, rows, :].astype(F32) * mix)
    out_d = jnp.concatenate(parts, axis=0)
    m_d = (out_d * sz[:, A_W + B_W + C_W:]).astype(BF16)
    y = y + jnp.dot(m_d, w_ref[A_W + B_W + C_W:, :], preferred_element_type=F32)

    yms = jnp.mean(y * y, axis=-1, keepdims=True)
    yn = y * lax.rsqrt(yms + EPS) * postg_ref[...]
    gate = mod_ref[0, :, 2 * D_MODEL:3 * D_MODEL]
    o_ref[0] = x_ref[0] + gate * yn


def _outproj(x, mod, oa, of, ob, yb, ao, du, dvn, sz, lp):
    bsz, n, _ = x.shape
    tm = min(TM_OUT, n)
    tok = lambda w: pl.BlockSpec((1, tm, w), lambda b, i: (b, i, 0))
    return pl.pallas_call(
        _outproj_kernel,
        out_shape=jax.ShapeDtypeStruct((bsz, n, D_MODEL), F32),
        grid=(bsz, n // tm),
        in_specs=[
            tok(D_MODEL),
            pl.BlockSpec((1, 1, 3 * D_MODEL), lambda b, i: (b, 0, 0)),
            tok(A_W), tok(A_W), tok(A_W), tok(B_W), tok(C_W), tok(D_W), tok(D_W), tok(D_MIX),
            _full((D_MIX, D_MODEL)), _full((1, D_MODEL)), _full((1, A_W)), _full((A_W, A_W)),
            _full((D_GROUPS * SGU_CHUNK, SGU_CHUNK)), _full((SGU_CHUNK, D_W)),
        ],
        out_specs=tok(D_MODEL),
        compiler_params=_params("parallel", "parallel"), name="outproj",
    )(x, mod, oa, of, ob, yb, ao, du, dvn, sz,
      lp["w_out"], lp["post_g"], lp["og"], lp["bd_a"], lp["sgu_w"], lp["sgu_b"])


def _pack_w_in(w):
    sizes = [A_KW, A_KW, A_W, 2 * GLA_RANK, B_W, C_W, C_KVW, C_KVW, D_W, D_W, D_MIX]
    off = [0] + [int(o) for o in np.cumsum(sizes)]
    a_q, a_k, a_v, a_lr, b_u, c_q, c_k, c_v, d_u, d_v, z = [w[:, off[i]:off[i + 1]] for i in range(len(sizes))]

    def deinterleave(x, heads):
        r = x.reshape(D_MODEL, heads, C_HD // 2, 2)
        return jnp.concatenate([r[..., 0], r[..., 1]], axis=-1).reshape(D_MODEL, heads * C_HD)

    zeros = lambda width: jnp.zeros((D_MODEL, width), w.dtype)
    pieces = [a_q, a_k, a_v, b_u, deinterleave(c_q, C_HEADS), deinterleave(c_k, C_KV),
              a_lr, zeros(LANES - 2 * GLA_RANK)]
    for h in range(C_KV):
        pieces += [c_v[:, h * C_HD:(h + 1) * C_HD], zeros(LANES - C_HD)]
    pieces += [d_u, d_v, z]
    packed = jnp.concatenate(pieces, axis=1)
    assert packed.shape[1] == W_PACKED
    return packed.astype(BF16)


def _block_diag_mean(width, group):
    idx = np.arange(width) // group
    return jnp.asarray((idx[:, None] == idx[None, :]).astype(np.float32) / group, BF16)


def _layer_params(l, ada_unused, norm_pre_g, norm_post_g, w_in, gla_wg2_f, gla_bg_f, gla_wg2_b, gla_bg_b,
                  gla_onorm_g, fnet_w, q_norm_g, k_norm_g, sgu_norm_g, sgu_w, sgu_b, w_out):
    deint = np.concatenate([np.arange(0, C_HD, 2), np.arange(1, C_HD, 2)])
    w = _pack_w_in(w_in[l])
    wg2 = jnp.zeros((LANES, 2 * A_KW), F32)
    wg2 = wg2.at[0:GLA_RANK, 0:A_KW].set(gla_wg2_f[l])
    wg2 = wg2.at[GLA_RANK:2 * GLA_RANK, A_KW:].set(gla_wg2_b[l])
    vone = np.zeros((1, 2 * LANES), np.float32)
    vone[0, ONES_LANE] = 1.0
    vone[0, LANES + ONES_LANE] = 1.0
    return {
        "pre_g": norm_pre_g[l].reshape(1, D_MODEL),
        "post_g": norm_post_g[l].reshape(1, D_MODEL),
        "w_in": w,
        "wg2": wg2.astype(BF16),
        "bg": jnp.concatenate([gla_bg_f[l], gla_bg_b[l]]).reshape(1, 2 * A_KW),
        "qg": jnp.tile(q_norm_g[l][deint], C_HEADS).reshape(1, C_W),
        "kg": jnp.tile(k_norm_g[l][deint], C_KV).reshape(1, C_KVW),
        "sg": sgu_norm_g[l].reshape(1, D_W),
        "bd_q": _block_diag_mean(C_W, C_HD),
        "bd_a": _block_diag_mean(A_W, A_DV),
        "vone": jnp.asarray(vone),
        "og": jnp.tile(gla_onorm_g[l], A_HEADS).reshape(1, A_W),
        "fnet_w": fnet_w[l].astype(BF16),
        "sgu_w": sgu_w[l].reshape(D_GROUPS * SGU_CHUNK, SGU_CHUNK).astype(BF16),
        "sgu_b": jnp.repeat(sgu_b[l].T, D_GC, axis=1),
        "w_out": w_out[l].astype(BF16),
    }


def _seq_tables(n):
    rows = n // GRID_W
    row = jnp.repeat(jnp.arange(rows, dtype=F32), GRID_W)
    col = jnp.tile(jnp.arange(GRID_W, dtype=F32), rows)
    rope_axis = C_HD // 2
    freqs = ROPE_THETA ** (-jnp.arange(0, rope_axis, 2, dtype=F32) / rope_axis)
    ang = jnp.concatenate([row[:, None] * freqs, col[:, None] * freqs], axis=-1)
    cos, sin = jnp.cos(ang), jnp.sin(ang)
    cos_t = jnp.tile(jnp.concatenate([cos, cos], axis=-1), (1, LANES // C_HD))
    sin_t = jnp.tile(jnp.concatenate([-sin, sin], axis=-1), (1, LANES // C_HD))

    n2 = FFT_N2
    n1 = n // n2

    def dft_angles(rows, cols, period):
        prod = (jnp.arange(rows, dtype=jnp.int32)[:, None] * jnp.arange(cols, dtype=jnp.int32)[None, :]) % period
        return prod.astype(F32) * (2.0 * math.pi / period)

    a1_ang = dft_angles(n1, n1, n1)
    a1 = jnp.concatenate([jnp.cos(a1_ang), -jnp.sin(a1_ang)], axis=0)
    tw_ang = dft_angles(n2, n1, n)
    scale = 1.0 / math.sqrt(n * B_GC)
    tw_cos = jnp.broadcast_to((jnp.cos(tw_ang) * scale)[:, :, None], (n2, n1, LANES))
    tw_sin = jnp.broadcast_to((jnp.sin(tw_ang) * scale)[:, :, None], (n2, n1, LANES))
    ang2 = dft_angles(n2, n2, n2)
    c2, s2 = jnp.cos(ang2), jnp.sin(ang2)
    eye = jnp.eye(FFT_K1, dtype=F32)
    blk = lambda m: (m[:, None, None, :] * eye[None, :, :, None]).reshape(n2 * FFT_K1, FFT_K1 * n2)
    l2 = jnp.block([[blk(c2), blk(s2)], [blk(-s2), blk(c2)]])
    angc = dft_angles(B_GC, B_GC, B_GC)
    grp = jnp.eye(B_GROUPS, dtype=F32)
    kron = lambda m: (grp[:, None, :, None] * m[None, :, None, :]).reshape(B_W, B_W)
    chan_cos = kron(jnp.cos(angc))
    chan_sin = kron(jnp.sin(angc))
    return {
        "cos": cos_t, "sin": sin_t,
        "a1": a1.astype(BF16),
        "tw_cos": tw_cos, "tw_sin": tw_sin,
        "l2": l2.astype(BF16),
        "chan_cos": chan_cos.astype(BF16), "chan_sin": chan_sin.astype(BF16),
    }


def _layer(x, mod, lp, tabs):
    gq, gk, gv, gf, gb, fu, aq, kt4, va, du, dvn, sz = _inproj(x, mod, lp, tabs)
    oa = _gla_band(gq, gk, gv, gf, gb)
    of, ob = _gla_rec(gq, gk, gv, gf, gb)
    yb = _fnet(fu, lp, tabs)
    ao = _attention(aq, kt4, va)
    return _outproj(x, mod, oa, of, ob, yb, ao, du, dvn, sz, lp)


def kernel(x_prompt, x_sample, c_prompt, c_sample, ada_w, ada_b, norm_pre_g, norm_post_g, w_in,
           gla_wg2_f, gla_bg_f, gla_wg2_b, gla_bg_b, gla_onorm_g, fnet_w, q_norm_g, k_norm_g,
           sgu_norm_g, sgu_w, sgu_b, w_out):
    bp, bs = x_prompt.shape[0], x_sample.shape[0]
    pad_rows = (-(bp + bs)) % 8
    c_all = jnp.concatenate([c_prompt, c_sample, jnp.zeros((pad_rows, D_MODEL), F32)], axis=0)
    mod = _adaln(c_all, ada_w, ada_b)
    tabs_p = _seq_tables(x_prompt.shape[1])
    tabs_s = _seq_tables(x_sample.shape[1])
    y_prompt, y_sample = x_prompt, x_sample
    for l in range(DEPTH):
        lp = _layer_params(l, None, norm_pre_g, norm_post_g, w_in, gla_wg2_f, gla_bg_f, gla_wg2_b,
                           gla_bg_b, gla_onorm_g, fnet_w, q_norm_g, k_norm_g, sgu_norm_g, sgu_w,
                           sgu_b, w_out)
        mod_p = mod[l, 0:bp].reshape(bp, 1, 3 * D_MODEL)
        mod_s = mod[l, bp:bp + bs].reshape(bs, 1, 3 * D_MODEL)
        y_prompt = _layer(y_prompt, mod_p, lp, tabs_p)
        y_sample = _layer(y_sample, mod_s, lp, tabs_s)
    return (y_prompt, y_sample)
```

```python
import math

import numpy as np
import jax
import jax.numpy as jnp
from jax import lax
from jax.experimental import pallas as pl
from jax.experimental.pallas import tpu as pltpu

F32 = jnp.float32
BF16 = jnp.bfloat16
HIGHEST = lax.Precision.HIGHEST

D_MODEL = 1024
DEPTH = 2
GRID_W = 64
A_HEADS, A_DK, A_DV = 4, 32, 64
A_W, A_KW = A_HEADS * A_DV, A_HEADS * A_DK
GLA_RANK = 16
GLA_GATE_NORM = 16.0
B_GROUPS, B_GC = 4, 64
B_W = B_GROUPS * B_GC
C_HEADS, C_KV, C_HD = 8, 2, 64
C_GRP = C_HEADS // C_KV
C_W, C_KVW = C_HEADS * C_HD, C_KV * C_HD
ROPE_THETA = 10000.0
D_GROUPS, D_GC = 4, 64
D_W = D_GROUPS * D_GC
SGU_CHUNK = 128
D_MIX = A_W + B_W + C_W + D_W
EPS = 1e-6
LOG2E = 1.4426950408889634

LANES = 128
VMEM_LIMIT_BYTES = 56 * 1024 * 1024

GQ, GK, GV, FU, AQ, AK, LR, AV, DU, DV, ZZ = 0, 128, 256, 512, 768, 1280, 1408, 1536, 1792, 2048, 2304
W_PACKED = 3584
AV_W = 2 * LANES
ONES_LANE = C_HD

TM_IN = 512
TM_OUT = 512
GLA_SUB = 16
GLA_BAND_TILE = 64
GLA_GRP = 128
GLA_SEG = 512
GLA_LOOKAHEAD = 3
FFT_N2 = 64
FFT_NB = 8
FFT_K1 = 8
ATT_TQ = 512
ATT_TK = 1024
ATT_UNROLL = 4


def _params(*sem):
    return pltpu.CompilerParams(dimension_semantics=sem, vmem_limit_bytes=VMEM_LIMIT_BYTES)


def _full(shape):
    n = len(shape)
    return pl.BlockSpec(shape, lambda *_: (0,) * n)


def _adaln_kernel(c_ref, w_ref, b_ref, o_ref):
    c = c_ref[...]
    sc = c / (1.0 + jnp.exp(-c))
    o_ref[0] = jnp.dot(sc, w_ref[0], precision=HIGHEST, preferred_element_type=F32) + b_ref[0]


def _adaln(c_all, ada_w, ada_b):
    rows = c_all.shape[0]
    ncol = 3 * D_MODEL // D_MODEL
    return pl.pallas_call(
        _adaln_kernel,
        out_shape=jax.ShapeDtypeStruct((DEPTH, rows, 3 * D_MODEL), F32),
        grid=(DEPTH, ncol),
        in_specs=[
            pl.BlockSpec((rows, D_MODEL), lambda l, j: (0, 0)),
            pl.BlockSpec((1, D_MODEL, D_MODEL), lambda l, j: (l, 0, j)),
            pl.BlockSpec((1, 1, D_MODEL), lambda l, j: (l, 0, j)),
        ],
        out_specs=pl.BlockSpec((1, rows, D_MODEL), lambda l, j: (l, 0, j)),
        compiler_params=_params("arbitrary", "arbitrary"),
        name="adaln",
    )(c_all, ada_w, ada_b.reshape(DEPTH, 1, 3 * D_MODEL))


def _rope(x, cos, sin_signed):
    width = x.shape[-1]
    lane = lax.broadcasted_iota(jnp.int32, x.shape, 1)
    first_half = (lane % C_HD) < (C_HD // 2)
    swapped = jnp.where(first_half, pltpu.roll(x, width - C_HD // 2, 1), pltpu.roll(x, C_HD // 2, 1))
    reps = width // LANES
    cos_w = jnp.concatenate([cos] * reps, axis=-1) if reps > 1 else cos
    sin_w = jnp.concatenate([sin_signed] * reps, axis=-1) if reps > 1 else sin_signed
    return x * cos_w + swapped * sin_w


def _inproj_kernel(x_ref, mod_ref, preg_ref, w_ref, wg2_ref, bg_ref, qg_ref, kg_ref, sg_ref,
                   cos_ref, sin_ref, bd_ref, vone_ref,
                   gq_ref, gk_ref, gv_ref, gf_ref, gb_ref, fu_ref, aq_ref, kt_ref, va_ref,
                   du_ref, dvn_ref, sz_ref):
    x = x_ref[0]
    ms = jnp.mean(x * x, axis=-1, keepdims=True)
    shift = mod_ref[0, :, 0:D_MODEL]
    scale = mod_ref[0, :, D_MODEL:2 * D_MODEL]
    h = (x * lax.rsqrt(ms + EPS)) * preg_ref[...] * (1.0 + scale) + shift
    hb = h.astype(BF16)

    def seg(off, width):
        return jnp.dot(hb, w_ref[:, off:off + width], preferred_element_type=F32)

    klr = seg(AK, C_KVW + LANES)
    q = seg(AQ, C_W)
    z = seg(ZZ, D_MIX)
    gqk = seg(GQ, 2 * A_KW)
    gv = seg(GV, A_W)
    fu = seg(FU, B_W)
    va = seg(AV, AV_W)
    du = seg(DU, D_W)
    dv = seg(DV, D_W)
    k = klr[:, :C_KVW]
    lr = klr[:, C_KVW:].astype(BF16)
    logits = jnp.dot(lr, wg2_ref[...], preferred_element_type=F32) + bg_ref[...]
    qms = jnp.dot((q * q).astype(BF16), bd_ref[...], preferred_element_type=F32)
    kms = jnp.dot((k * k).astype(BF16), bd_ref[0:C_KVW, 0:C_KVW], preferred_element_type=F32)

    gq_ref[0] = (gqk[:, :A_KW] * (A_DK ** -0.5)).astype(BF16)
    gk_ref[0] = gqk[:, A_KW:].astype(BF16)
    gv_ref[0] = gv.astype(BF16)
    logg = (jnp.minimum(logits, 0.0) - jnp.log1p(jnp.exp(-jnp.abs(logits)))) * (1.0 / GLA_GATE_NORM)
    gf_ref[0] = logg[:, :A_KW]
    gb_ref[0] = logg[:, A_KW:]

    fu_ref[0] = fu.astype(BF16)

    cos = cos_ref[...]
    sin = sin_ref[...]
    qn = q * lax.rsqrt(qms + EPS) * qg_ref[...]
    aq_ref[0] = (_rope(qn, cos, sin) * (C_HD ** -0.5 * LOG2E)).astype(BF16)
    kn = k * lax.rsqrt(kms + EPS) * kg_ref[...]
    kt = _rope(kn, cos, sin).T.astype(BF16)
    for g in range(C_KV):
        for r in range(C_GRP):
            kt_ref[0, g, r * C_HD:(r + 1) * C_HD, :] = kt[g * C_HD:(g + 1) * C_HD, :]
    vab = (va + vone_ref[...]).astype(BF16)
    va_ref[0, 0] = vab[:, :LANES]
    va_ref[0, 1] = vab[:, LANES:]

    du_ref[0] = du.astype(BF16)
    dms = jnp.mean(dv * dv, axis=-1, keepdims=True)
    dvn_ref[0] = (dv * lax.rsqrt(dms + EPS) * sg_ref[...]).astype(BF16)

    sz_ref[0] = (z / (1.0 + jnp.exp(-z))).astype(BF16)


def _inproj(x, mod, lp, tabs):
    bsz, n, _ = x.shape
    tm = min(TM_IN, n)
    grid = (bsz, n // tm)
    tok = lambda w: pl.BlockSpec((1, tm, w), lambda b, i: (b, i, 0))
    out_shapes = (
        jax.ShapeDtypeStruct((bsz, n, A_KW), BF16),
        jax.ShapeDtypeStruct((bsz, n, A_KW), BF16),
        jax.ShapeDtypeStruct((bsz, n, A_W), BF16),
        jax.ShapeDtypeStruct((bsz, n, A_KW), F32),
        jax.ShapeDtypeStruct((bsz, n, A_KW), F32),
        jax.ShapeDtypeStruct((bsz, n, B_W), BF16),
        jax.ShapeDtypeStruct((bsz, n, C_W), BF16),
        jax.ShapeDtypeStruct((bsz, C_KV, C_GRP * C_HD, n), BF16),
        jax.ShapeDtypeStruct((bsz, C_KV, n, LANES), BF16),
        jax.ShapeDtypeStruct((bsz, n, D_W), BF16),
        jax.ShapeDtypeStruct((bsz, n, D_W), BF16),
        jax.ShapeDtypeStruct((bsz, n, D_MIX), BF16),
    )
    out_specs = (
        tok(A_KW), tok(A_KW), tok(A_W), tok(A_KW), tok(A_KW), tok(B_W), tok(C_W),
        pl.BlockSpec((1, C_KV, C_GRP * C_HD, tm), lambda b, i: (b, 0, 0, i)),
        pl.BlockSpec((1, C_KV, tm, LANES), lambda b, i: (b, 0, i, 0)),
        tok(D_W), tok(D_W), tok(D_MIX),
    )
    in_specs = [
        tok(D_MODEL),
        pl.BlockSpec((1, 1, 3 * D_MODEL), lambda b, i: (b, 0, 0)),
        _full((1, D_MODEL)),
        _full((D_MODEL, W_PACKED)),
        _full((LANES, 2 * A_KW)),
        _full((1, 2 * A_KW)),
        _full((1, C_W)),
        _full((1, C_KVW)),
        _full((1, D_W)),
        pl.BlockSpec((tm, LANES), lambda b, i: (i, 0)),
        pl.BlockSpec((tm, LANES), lambda b, i: (i, 0)),
        _full((C_W, C_W)),
        _full((1, 2 * LANES)),
    ]
    return pl.pallas_call(
        _inproj_kernel, out_shape=out_shapes, grid=grid, in_specs=in_specs, out_specs=out_specs,
        compiler_params=_params("parallel", "parallel"), name="inproj",
    )(x, mod, lp["pre_g"], lp["w_in"], lp["wg2"], lp["bg"], lp["qg"], lp["kg"], lp["sg"],
      tabs["cos"], tabs["sin"], lp["bd_q"], lp["vone"])


def _head_match(shape, rows_per_head, cols_per_head):
    return (lax.broadcasted_iota(jnp.int32, shape, 0) // rows_per_head
            == lax.broadcasted_iota(jnp.int32, shape, 1) // cols_per_head)


def _gla_band_kernel(q_ref, k_ref, v_ref, gf_ref, gb_ref, o_ref, kf_ref, vf_ref, bf_ref, cb_ref):
    ts = q_ref.shape[2]
    hsum = jnp.where(_head_match((A_KW, A_W), A_DK, A_DV), 1.0, 0.0).astype(BF16)
    run = jnp.zeros((ts, A_KW), F32)
    for p in range(GLA_SUB):
        run = run + gf_ref[0, p]
        bf_ref[p] = run
        kf_ref[p] = k_ref[0, p].astype(F32)
        vf_ref[p] = v_ref[0, p].astype(F32)
    run = jnp.zeros((ts, A_KW), F32)
    for p in reversed(range(GLA_SUB)):
        run = run + gb_ref[0, p]
        cb_ref[p] = run
    for p in range(GLA_SUB):
        q = q_ref[0, p].astype(F32)
        prods = []
        for s in range(GLA_SUB):
            qk = q * kf_ref[s]
            if s < p:
                qk = qk * jnp.exp(bf_ref[p] - bf_ref[s])
            elif s > p:
                qk = qk * jnp.exp(cb_ref[p] - cb_ref[s])
            prods.append(qk.astype(BF16))
        a = jnp.dot(jnp.concatenate(prods, axis=0), hsum, preferred_element_type=F32)
        acc = a[0:ts] * vf_ref[0]
        for s in range(1, GLA_SUB):
            acc = acc + a[s * ts:(s + 1) * ts] * vf_ref[s]
        o_ref[0, p] = acc.astype(BF16)


def _gla_band(gq, gk, gv, gf, gb):
    bsz, n, _ = gq.shape
    ns = n // GLA_SUB
    ts = min(GLA_BAND_TILE, ns)
    perm = lambda a: a.reshape(bsz, ns, GLA_SUB, a.shape[-1]).transpose(0, 2, 1, 3)
    blk = lambda w: pl.BlockSpec((1, GLA_SUB, ts, w), lambda b, i: (b, 0, i, 0))
    o = pl.pallas_call(
        _gla_band_kernel,
        out_shape=jax.ShapeDtypeStruct((bsz, GLA_SUB, ns, A_W), BF16),
        grid=(bsz, ns // ts),
        in_specs=[blk(A_KW), blk(A_KW), blk(A_W), blk(A_KW), blk(A_KW)],
        out_specs=blk(A_W),
        scratch_shapes=[
            pltpu.VMEM((GLA_SUB, ts, A_KW), F32), pltpu.VMEM((GLA_SUB, ts, A_W), F32),
            pltpu.VMEM((GLA_SUB, ts, A_KW), F32), pltpu.VMEM((GLA_SUB, ts, A_KW), F32),
        ],
        compiler_params=_params("parallel", "parallel"), name="gla_band",
    )(perm(gq), perm(gk), perm(gv), perm(gf), perm(gb))
    return o.transpose(0, 2, 1, 3).reshape(bsz, n, A_W)


def _gla_prep(q_ref, k_ref, g_ref, qt_ref, kh_ref, dect_ref, reverse):
    seg = q_ref.shape[1]
    r_i = lax.broadcasted_iota(jnp.int32, (GLA_GRP, GLA_GRP), 0)
    c_i = lax.broadcasted_iota(jnp.int32, (GLA_GRP, GLA_GRP), 1)
    same = (r_i // GLA_SUB) == (c_i // GLA_SUB)
    tri = (c_i >= r_i) if reverse else (c_i <= r_i)
    cum_mat = jnp.where(same & tri, 1.0, 0.0).astype(F32)
    tot_mat = jnp.where(same, 1.0, 0.0).astype(F32)
    for r in range(seg // GLA_GRP):
        rows = slice(r * GLA_GRP, (r + 1) * GLA_GRP)
        g = g_ref[0, rows, :]
        cum = jnp.dot(cum_mat, g, precision=HIGHEST, preferred_element_type=F32)
        tot = jnp.dot(tot_mat, g, precision=HIGHEST, preferred_element_type=F32)
        qt_ref[rows, :] = (q_ref[0, rows, :].astype(F32) * jnp.exp(cum)).astype(BF16)
        kh_ref[rows, :] = (k_ref[0, rows, :].astype(F32) * jnp.exp(tot - cum)).astype(BF16)
    sel = jnp.where(lax.broadcasted_iota(jnp.int32, (seg, LANES), 0) // GLA_SUB
                    == lax.broadcasted_iota(jnp.int32, (seg, LANES), 1), 1.0, 0.0).astype(F32)
    tot_t = lax.dot_general(g_ref[0], sel, (((0,), (0,)), ((), ())), precision=HIGHEST,
                            preferred_element_type=F32)
    dect_ref[...] = jnp.exp(tot_t)


def _gla_rec_kernel(qf_ref, kf_ref, vf_ref, gf_ref, qb_ref, kb_ref, vb_ref, gb_ref, of_ref, ob_ref,
                    sf_ref, sb_ref, qtf_ref, khf_ref, dtf_ref, qtb_ref, khb_ref, dtb_ref):
    @pl.when(pl.program_id(1) == 0)
    def _():
        sf_ref[...] = jnp.zeros_like(sf_ref)
        sb_ref[...] = jnp.zeros_like(sb_ref)

    seg = qf_ref.shape[1]
    n_sub = seg // GLA_SUB
    _gla_prep(qf_ref, kf_ref, gf_ref, qtf_ref, khf_ref, dtf_ref, reverse=False)
    _gla_prep(qb_ref, kb_ref, gb_ref, qtb_ref, khb_ref, dtb_ref, reverse=True)
    smask = _head_match((A_KW, A_W), A_DK, A_DV)

    def update(i, kh_ref, v_ref):
        rows = slice(i * GLA_SUB, (i + 1) * GLA_SUB)
        upd = lax.dot_general(kh_ref[rows, :], v_ref[0, rows, :], (((0,), (0,)), ((), ())),
                              preferred_element_type=F32)
        return jnp.where(smask, upd, 0.0)

    def step(i, s, upd, qt_ref, dect_ref, o_ref):
        rows = slice(i * GLA_SUB, (i + 1) * GLA_SUB)
        o_ref[0, rows, :] = jnp.dot(qt_ref[rows, :], s.astype(BF16), preferred_element_type=F32).astype(BF16)
        return s * dect_ref[:, i:i + 1] + upd

    order_f = list(range(n_sub))
    order_b = list(reversed(range(n_sub)))
    upd_f = [update(i, khf_ref, vf_ref) for i in order_f[:GLA_LOOKAHEAD]]
    upd_b = [update(i, khb_ref, vb_ref) for i in order_b[:GLA_LOOKAHEAD]]
    s_f = sf_ref[...]
    s_b = sb_ref[...]
    for j in range(n_sub):
        if j + GLA_LOOKAHEAD < n_sub:
            upd_f.append(update(order_f[j + GLA_LOOKAHEAD], khf_ref, vf_ref))
            upd_b.append(update(order_b[j + GLA_LOOKAHEAD], khb_ref, vb_ref))
        s_f = step(order_f[j], s_f, upd_f[j], qtf_ref, dtf_ref, of_ref)
        s_b = step(order_b[j], s_b, upd_b[j], qtb_ref, dtb_ref, ob_ref)
    sf_ref[...] = s_f
    sb_ref[...] = s_b


def _gla_rec(gq, gk, gv, gf, gb):
    bsz, n, _ = gq.shape
    seg = min(GLA_SEG, n)
    nseg = n // seg
    assert seg // GLA_SUB <= LANES
    fwd = lambda w: pl.BlockSpec((1, seg, w), lambda b, s: (b, s, 0))
    bwd = lambda w: pl.BlockSpec((1, seg, w), lambda b, s: (b, nseg - 1 - s, 0))
    return pl.pallas_call(
        _gla_rec_kernel,
        out_shape=(jax.ShapeDtypeStruct((bsz, n, A_W), BF16), jax.ShapeDtypeStruct((bsz, n, A_W), BF16)),
        grid=(bsz, nseg),
        in_specs=[fwd(A_KW), fwd(A_KW), fwd(A_W), fwd(A_KW), bwd(A_KW), bwd(A_KW), bwd(A_W), bwd(A_KW)],
        out_specs=(fwd(A_W), bwd(A_W)),
        scratch_shapes=[
            pltpu.VMEM((A_KW, A_W), F32), pltpu.VMEM((A_KW, A_W), F32),
            pltpu.VMEM((seg, A_KW), BF16), pltpu.VMEM((seg, A_KW), BF16), pltpu.VMEM((A_KW, LANES), F32),
            pltpu.VMEM((seg, A_KW), BF16), pltpu.VMEM((seg, A_KW), BF16), pltpu.VMEM((A_KW, LANES), F32),
        ],
        compiler_params=_params("parallel", "arbitrary"), name="gla_rec",
    )(gq, gk, gv, gf, gq, gk, gv, gb)


def _fft1_kernel(u_ref, a1_ref, cos_ref, sin_ref, t_ref):
    n1 = u_ref.shape[1]
    nb = cos_ref.shape[0]
    t = jnp.dot(a1_ref[...], u_ref[0], preferred_element_type=F32)
    for j in range(nb):
        cols = slice(j * B_W, (j + 1) * B_W)
        tr = t[:n1, cols]
        ti = t[n1:, cols]
        c = jnp.concatenate([cos_ref[j]] * (B_W // LANES), axis=-1)
        s = jnp.concatenate([sin_ref[j]] * (B_W // LANES), axis=-1)
        t_ref[0, 0, :, cols] = (tr * c + ti * s).astype(BF16)
        t_ref[0, 1, :, cols] = (ti * c - tr * s).astype(BF16)


def _fft2_kernel(t_ref, l_ref, cc_ref, cs_ref, w_ref, o_ref):
    k1b, n2 = t_ref.shape[2], t_ref.shape[3]
    half = k1b * n2
    x = t_ref[0].reshape(2 * half, B_W)
    g = jnp.dot(l_ref[...], x, preferred_element_type=F32)
    gr = g[:half].astype(BF16)
    gi = g[half:].astype(BF16)
    mixed = (jnp.dot(gr, cc_ref[...], preferred_element_type=F32)
             + jnp.dot(gi, cs_ref[...], preferred_element_type=F32))
    y = jnp.dot(mixed.astype(BF16), w_ref[...], preferred_element_type=F32)
    o_ref[0] = y.reshape(n2, k1b, B_W)


def _fnet(fu, lp, tabs):
    bsz, n, _ = fu.shape
    n2 = FFT_N2
    n1 = n // n2
    nb = FFT_NB
    k1b = FFT_K1
    u2 = fu.reshape(bsz, n1, n2 * B_W)
    tt = pl.pallas_call(
        _fft1_kernel,
        out_shape=jax.ShapeDtypeStruct((bsz, 2, n1, n2 * B_W), BF16),
        grid=(n2 // nb, bsz),
        in_specs=[
            pl.BlockSpec((1, n1, nb * B_W), lambda j, b: (b, 0, j)),
            _full((2 * n1, n1)),
            pl.BlockSpec((nb, n1, LANES), lambda j, b: (j, 0, 0)),
            pl.BlockSpec((nb, n1, LANES), lambda j, b: (j, 0, 0)),
        ],
        out_specs=pl.BlockSpec((1, 2, n1, nb * B_W), lambda j, b: (b, 0, 0, j)),
        compiler_params=_params("parallel", "parallel"), name="fft_pass1",
    )(u2, tabs["a1"], tabs["tw_cos"], tabs["tw_sin"])
    t5 = tt.reshape(bsz, 2, n1, n2, B_W)
    y = pl.pallas_call(
        _fft2_kernel,
        out_shape=jax.ShapeDtypeStruct((bsz, n2, n1, B_W), F32),
        grid=(bsz, n1 // k1b),
        in_specs=[
            pl.BlockSpec((1, 2, k1b, n2, B_W), lambda b, i: (b, 0, i, 0, 0)),
            _full((2 * k1b * n2, 2 * k1b * n2)),
            _full((B_W, B_W)), _full((B_W, B_W)), _full((B_W, B_W)),
        ],
        out_specs=pl.BlockSpec((1, n2, k1b, B_W), lambda b, i: (b, 0, i, 0)),
        compiler_params=_params("parallel", "parallel"), name="fft_pass2",
    )(t5, tabs["l2"], tabs["chan_cos"], tabs["chan_sin"], lp["fnet_w"])
    return y.reshape(bsz, n, B_W)


def _attn_kernel(q_ref, kt_ref, va_ref, o_ref, qm_ref, m_ref, acc_ref):
    tq = q_ref.shape[1]
    n = kt_ref.shape[3]
    tk = min(ATT_TK, n)
    q = q_ref[0]
    lane = lax.broadcasted_iota(jnp.int32, q.shape, 1)
    for h in range(C_GRP):
        qm_ref[h * tq:(h + 1) * tq, :] = jnp.where(lane // C_HD == h, q, jnp.zeros_like(q))
    m_ref[...] = jnp.full(m_ref.shape, -jnp.inf, F32)
    acc_ref[...] = jnp.zeros(acc_ref.shape, F32)

    def scores(c):
        c0 = pl.multiple_of(c * tk, tk)
        return jnp.dot(qm_ref[...], kt_ref[0, 0, :, pl.ds(c0, tk)], preferred_element_type=F32)

    def softmax_accumulate(c, s):
        c0 = pl.multiple_of(c * tk, tk)
        v = va_ref[0, 0, pl.ds(c0, tk), :]
        s_max = s[:, 0:LANES]
        for t in range(1, tk // LANES):
            s_max = jnp.maximum(s_max, s[:, t * LANES:(t + 1) * LANES])
        row_max = jnp.max(s_max, axis=-1, keepdims=True)
        m_old = m_ref[...]
        m_new = jnp.maximum(m_old, row_max)
        p = jnp.exp2(s - jnp.concatenate([m_new] * (tk // LANES), axis=-1))
        alpha = jnp.exp2(m_old - m_new)
        acc_ref[...] = alpha * acc_ref[...] + jnp.dot(p.astype(BF16), v, preferred_element_type=F32)
        m_ref[...] = m_new

    def step(c, carry):
        softmax_accumulate(c, scores(c))
        return carry

    lax.fori_loop(0, n // tk, step, 0, unroll=ATT_UNROLL)
    acc = acc_ref[...]
    inv = 1.0 / acc[:, ONES_LANE:ONES_LANE + 1]
    outs = [(acc[h * tq:(h + 1) * tq, 0:C_HD] * inv[h * tq:(h + 1) * tq]) for h in range(C_GRP)]
    o_ref[0] = jnp.concatenate(outs, axis=-1).astype(BF16)


def _attention(aq, kt4, va):
    bsz, n, _ = aq.shape
    tq = min(ATT_TQ, n)
    return pl.pallas_call(
        _attn_kernel,
        out_shape=jax.ShapeDtypeStruct((bsz, n, C_W), BF16),
        grid=(bsz, C_KV, n // tq),
        in_specs=[
            pl.BlockSpec((1, tq, C_GRP * C_HD), lambda b, g, i: (b, i, g)),
            pl.BlockSpec((1, 1, C_GRP * C_HD, n), lambda b, g, i: (b, g, 0, 0)),
            pl.BlockSpec((1, 1, n, LANES), lambda b, g, i: (b, g, 0, 0)),
        ],
        out_specs=pl.BlockSpec((1, tq, C_GRP * C_HD), lambda b, g, i: (b, i, g)),
        scratch_shapes=[
            pltpu.VMEM((C_GRP * tq, C_GRP * C_HD), BF16),
            pltpu.VMEM((C_GRP * tq, LANES), F32),
            pltpu.VMEM((C_GRP * tq, LANES), F32),
        ],
        compiler_params=_params("parallel", "parallel", "arbitrary"), name="attention",
    )(aq, kt4, va)


def _outproj_kernel(x_ref, mod_ref, oa_ref, of_ref, ob_ref, yb_ref, ao_ref, du_ref, dvn_ref, sz_ref,
                    w_ref, postg_ref, og_ref, bd_ref, sw_ref, sb_ref, o_ref):
    tm = x_ref.shape[1]
    sz = sz_ref[0].astype(F32)

    o = oa_ref[0].astype(F32) + of_ref[0].astype(F32) + ob_ref[0].astype(F32)
    oms = jnp.dot((o * o).astype(BF16), bd_ref[...], preferred_element_type=F32)
    out_a = o * lax.rsqrt(oms + EPS) * og_ref[...]
    y = jnp.dot((out_a * sz[:, 0:A_W]).astype(BF16), w_ref[0:A_W, :], preferred_element_type=F32)

    m_b = (yb_ref[0] * sz[:, A_W:A_W + B_W]).astype(BF16)
    y = y + jnp.dot(m_b, w_ref[A_W:A_W + B_W, :], preferred_element_type=F32)

    m_c = (ao_ref[0].astype(F32) * sz[:, A_W + B_W:A_W + B_W + C_W]).astype(BF16)
    y = y + jnp.dot(m_c, w_ref[A_W + B_W:A_W + B_W + C_W, :], preferred_element_type=F32)

    lane = lax.broadcasted_iota(jnp.int32, (SGU_CHUNK, D_W), 1)
    parts = []
    for c in range(tm // SGU_CHUNK):
        rows = slice(c * SGU_CHUNK, (c + 1) * SGU_CHUNK)
        r = jnp.dot(sw_ref[...], dvn_ref[0, rows, :], preferred_element_type=F32)
        mix = sb_ref[...]
        for g in range(D_GROUPS):
            mix = mix + jnp.where(lane // D_GC == g, r[g * SGU_CHUNK:(g + 1) * SGU_CHUNK, :], 0.0)
        parts.append(du_ref[0, rows, :].astype(F32) * mix)
    out_d = jnp.concatenate(parts, axis=0)
    m_d = (out_d * sz[:, A_W + B_W + C_W:]).astype(BF16)
    y = y + jnp.dot(m_d, w_ref[A_W + B_W + C_W:, :], preferred_element_type=F32)

    yms = jnp.mean(y * y, axis=-1, keepdims=True)
    yn = y * lax.rsqrt(yms + EPS) * postg_ref[...]
    gate = mod_ref[0, :, 2 * D_MODEL:3 * D_MODEL]
    o_ref[0] = x_ref[0] + gate * yn


def _outproj(x, mod, oa, of, ob, yb, ao, du, dvn, sz, lp):
    bsz, n, _ = x.shape
    tm = min(TM_OUT, n)
    tok = lambda w: pl.BlockSpec((1, tm, w), lambda b, i: (b, i, 0))
    return pl.pallas_call(
        _outproj_kernel,
        out_shape=jax.ShapeDtypeStruct((bsz, n, D_MODEL), F32),
        grid=(bsz, n // tm),
        in_specs=[
            tok(D_MODEL),
            pl.BlockSpec((1, 1, 3 * D_MODEL), lambda b, i: (b, 0, 0)),
            tok(A_W), tok(A_W), tok(A_W), tok(B_W), tok(C_W), tok(D_W), tok(D_W), tok(D_MIX),
            _full((D_MIX, D_MODEL)), _full((1, D_MODEL)), _full((1, A_W)), _full((A_W, A_W)),
            _full((D_GROUPS * SGU_CHUNK, SGU_CHUNK)), _full((SGU_CHUNK, D_W)),
        ],
        out_specs=tok(D_MODEL),
        compiler_params=_params("parallel", "parallel"), name="outproj",
    )(x, mod, oa, of, ob, yb, ao, du, dvn, sz,
      lp["w_out"], lp["post_g"], lp["og"], lp["bd_a"], lp["sgu_w"], lp["sgu_b"])


def _pack_w_in(w):
    sizes = [A_KW, A_KW, A_W, 2 * GLA_RANK, B_W, C_W, C_KVW, C_KVW, D_W, D_W, D_MIX]
    off = [0] + [int(o) for o in np.cumsum(sizes)]
    a_q, a_k, a_v, a_lr, b_u, c_q, c_k, c_v, d_u, d_v, z = [w[:, off[i]:off[i + 1]] for i in range(len(sizes))]

    def deinterleave(x, heads):
        r = x.reshape(D_MODEL, heads, C_HD // 2, 2)
        return jnp.concatenate([r[..., 0], r[..., 1]], axis=-1).reshape(D_MODEL, heads * C_HD)

    zeros = lambda width: jnp.zeros((D_MODEL, width), w.dtype)
    pieces = [a_q, a_k, a_v, b_u, deinterleave(c_q, C_HEADS), deinterleave(c_k, C_KV),
              a_lr, zeros(LANES - 2 * GLA_RANK)]
    for h in range(C_KV):
        pieces += [c_v[:, h * C_HD:(h + 1) * C_HD], zeros(LANES - C_HD)]
    pieces += [d_u, d_v, z]
    packed = jnp.concatenate(pieces, axis=1)
    assert packed.shape[1] == W_PACKED
    return packed.astype(BF16)


def _block_diag_mean(width, group):
    idx = np.arange(width) // group
    return jnp.asarray((idx[:, None] == idx[None, :]).astype(np.float32) / group, BF16)


def _layer_params(l, ada_unused, norm_pre_g, norm_post_g, w_in, gla_wg2_f, gla_bg_f, gla_wg2_b, gla_bg_b,
                  gla_onorm_g, fnet_w, q_norm_g, k_norm_g, sgu_norm_g, sgu_w, sgu_b, w_out):
    deint = np.concatenate([np.arange(0, C_HD, 2), np.arange(1, C_HD, 2)])
    w = _pack_w_in(w_in[l])
    wg2 = jnp.zeros((LANES, 2 * A_KW), F32)
    wg2 = wg2.at[0:GLA_RANK, 0:A_KW].set(gla_wg2_f[l])
    wg2 = wg2.at[GLA_RANK:2 * GLA_RANK, A_KW:].set(gla_wg2_b[l])
    vone = np.zeros((1, 2 * LANES), np.float32)
    vone[0, ONES_LANE] = 1.0
    vone[0, LANES + ONES_LANE] = 1.0
    return {
        "pre_g": norm_pre_g[l].reshape(1, D_MODEL),
        "post_g": norm_post_g[l].reshape(1, D_MODEL),
        "w_in": w,
        "wg2": wg2.astype(BF16),
        "bg": jnp.concatenate([gla_bg_f[l], gla_bg_b[l]]).reshape(1, 2 * A_KW),
        "qg": jnp.tile(q_norm_g[l][deint], C_HEADS).reshape(1, C_W),
        "kg": jnp.tile(k_norm_g[l][deint], C_KV).reshape(1, C_KVW),
        "sg": sgu_norm_g[l].reshape(1, D_W),
        "bd_q": _block_diag_mean(C_W, C_HD),
        "bd_a": _block_diag_mean(A_W, A_DV),
        "vone": jnp.asarray(vone),
        "og": jnp.tile(gla_onorm_g[l], A_HEADS).reshape(1, A_W),
        "fnet_w": fnet_w[l].astype(BF16),
        "sgu_w": sgu_w[l].reshape(D_GROUPS * SGU_CHUNK, SGU_CHUNK).astype(BF16),
        "sgu_b": jnp.repeat(sgu_b[l].T, D_GC, axis=1),
        "w_out": w_out[l].astype(BF16),
    }


def _seq_tables(n):
    rows = n // GRID_W
    row = jnp.repeat(jnp.arange(rows, dtype=F32), GRID_W)
    col = jnp.tile(jnp.arange(GRID_W, dtype=F32), rows)
    rope_axis = C_HD // 2
    freqs = ROPE_THETA ** (-jnp.arange(0, rope_axis, 2, dtype=F32) / rope_axis)
    ang = jnp.concatenate([row[:, None] * freqs, col[:, None] * freqs], axis=-1)
    cos, sin = jnp.cos(ang), jnp.sin(ang)
    cos_t = jnp.tile(jnp.concatenate([cos, cos], axis=-1), (1, LANES // C_HD))
    sin_t = jnp.tile(jnp.concatenate([-sin, sin], axis=-1), (1, LANES // C_HD))

    n2 = FFT_N2
    n1 = n // n2

    def dft_angles(rows, cols, period):
        prod = (jnp.arange(rows, dtype=jnp.int32)[:, None] * jnp.arange(cols, dtype=jnp.int32)[None, :]) % period
        return prod.astype(F32) * (2.0 * math.pi / period)

    a1_ang = dft_angles(n1, n1, n1)
    a1 = jnp.concatenate([jnp.cos(a1_ang), -jnp.sin(a1_ang)], axis=0)
    tw_ang = dft_angles(n2, n1, n)
    scale = 1.0 / math.sqrt(n * B_GC)
    tw_cos = jnp.broadcast_to((jnp.cos(tw_ang) * scale)[:, :, None], (n2, n1, LANES))
    tw_sin = jnp.broadcast_to((jnp.sin(tw_ang) * scale)[:, :, None], (n2, n1, LANES))
    ang2 = dft_angles(n2, n2, n2)
    c2, s2 = jnp.cos(ang2), jnp.sin(ang2)
    eye = jnp.eye(FFT_K1, dtype=F32)
    blk = lambda m: (m[:, None, None, :] * eye[None, :, :, None]).reshape(n2 * FFT_K1, FFT_K1 * n2)
    l2 = jnp.block([[blk(c2), blk(s2)], [blk(-s2), blk(c2)]])
    angc = dft_angles(B_GC, B_GC, B_GC)
    grp = jnp.eye(B_GROUPS, dtype=F32)
    kron = lambda m: (grp[:, None, :, None] * m[None, :, None, :]).reshape(B_W, B_W)
    chan_cos = kron(jnp.cos(angc))
    chan_sin = kron(jnp.sin(angc))
    return {
        "cos": cos_t, "sin": sin_t,
        "a1": a1.astype(BF16),
        "tw_cos": tw_cos, "tw_sin": tw_sin,
        "l2": l2.astype(BF16),
        "chan_cos": chan_cos.astype(BF16), "chan_sin": chan_sin.astype(BF16),
    }


def _layer(x, mod, lp, tabs):
    gq, gk, gv, gf, gb, fu, aq, kt4, va, du, dvn, sz = _inproj(x, mod, lp, tabs)
    oa = _gla_band(gq, gk, gv, gf, gb)
    of, ob = _gla_rec(gq, gk, gv, gf, gb)
    yb = _fnet(fu, lp, tabs)
    ao = _attention(aq, kt4, va)
    return _outproj(x, mod, oa, of, ob, yb, ao, du, dvn, sz, lp)


def kernel(x_prompt, x_sample, c_prompt, c_sample, ada_w, ada_b, norm_pre_g, norm_post_g, w_in,
           gla_wg2_f, gla_bg_f, gla_wg2_b, gla_bg_b, gla_onorm_g, fnet_w, q_norm_g, k_norm_g,
           sgu_norm_g, sgu_w, sgu_b, w_out):
    bp, bs = x_prompt.shape[0], x_sample.shape[0]
    pad_rows = (-(bp + bs)) % 8
    c_all = jnp.concatenate([c_prompt, c_sample, jnp.zeros((pad_rows, D_MODEL), F32)], axis=0)
    mod = _adaln(c_all, ada_w, ada_b)
    tabs_p = _seq_tables(x_prompt.shape[1])
    tabs_s = _seq_tables(x_sample.shape[1])
    y_prompt, y_sample = x_prompt, x_sample
    for l in range(DEPTH):
        lp = _layer_params(l, None, norm_pre_g, norm_post_g, w_in, gla_wg2_f, gla_bg_f, gla_wg2_b,
                           gla_bg_b, gla_onorm_g, fnet_w, q_norm_g, k_norm_g, sgu_norm_g, sgu_w,
                           sgu_b, w_out)
        mod_p = mod[l, 0:bp].reshape(bp, 1, 3 * D_MODEL)
        mod_s = mod[l, bp:bp + bs].reshape(bs, 1, 3 * D_MODEL)
        y_prompt = _layer(y_prompt, mod_p, lp, tabs_p)
        y_sample = _layer(y_sample, mod_s, lp, tabs_s)
    return (y_prompt, y_sample)
```

```python
import math

import numpy as np
import jax
import jax.numpy as jnp
from jax import lax
from jax.experimental import pallas as pl
from jax.experimental.pallas import tpu as pltpu

F32 = jnp.float32
BF16 = jnp.bfloat16
HIGHEST = lax.Precision.HIGHEST

D_MODEL = 1024
DEPTH = 2
GRID_W = 64
A_HEADS, A_DK, A_DV = 4, 32, 64
A_W, A_KW = A_HEADS * A_DV, A_HEADS * A_DK
GLA_RANK = 16
GLA_GATE_NORM = 16.0
B_GROUPS, B_GC = 4, 64
B_W = B_GROUPS * B_GC
C_HEADS, C_KV, C_HD = 8, 2, 64
C_GRP = C_HEADS // C_KV
C_W, C_KVW = C_HEADS * C_HD, C_KV * C_HD
ROPE_THETA = 10000.0
D_GROUPS, D_GC = 4, 64
D_W = D_GROUPS * D_GC
SGU_CHUNK = 128
D_MIX = A_W + B_W + C_W + D_W
EPS = 1e-6
LOG2E = 1.4426950408889634

LANES = 128
VMEM_LIMIT_BYTES = 56 * 1024 * 1024

GQ, GK, GV, FU, AQ, AK, LR, AV, DU, DV, ZZ = 0, 128, 256, 512, 768, 1280, 1408, 1536, 1792, 2048, 2304
W_PACKED = 3584
AV_W = 2 * LANES
ONES_LANE = C_HD

TM_IN = 512
TM_OUT = 512
GLA_SUB = 16
GLA_BAND_TILE = 64
GLA_GRP = 128
GLA_SEG = 512
GLA_LOOKAHEAD = 3
FFT_N2 = 64
FFT_NB = 8
FFT_K1 = 8
ATT_TQ = 512
ATT_TK = 1024
ATT_UNROLL = 4


def _params(*sem):
    return pltpu.CompilerParams(dimension_semantics=sem, vmem_limit_bytes=VMEM_LIMIT_BYTES)


def _full(shape):
    n = len(shape)
    return pl.BlockSpec(shape, lambda *_: (0,) * n)


def _adaln_kernel(c_ref, w_ref, b_ref, o_ref):
    c = c_ref[...]
    sc = c / (1.0 + jnp.exp(-c))
    o_ref[0] = jnp.dot(sc, w_ref[0], precision=HIGHEST, preferred_element_type=F32) + b_ref[0]


def _adaln(c_all, ada_w, ada_b):
    rows = c_all.shape[0]
    ncol = 3 * D_MODEL // D_MODEL
    return pl.pallas_call(
        _adaln_kernel,
        out_shape=jax.ShapeDtypeStruct((DEPTH, rows, 3 * D_MODEL), F32),
        grid=(DEPTH, ncol),
        in_specs=[
            pl.BlockSpec((rows, D_MODEL), lambda l, j: (0, 0)),
            pl.BlockSpec((1, D_MODEL, D_MODEL), lambda l, j: (l, 0, j)),
            pl.BlockSpec((1, 1, D_MODEL), lambda l, j: (l, 0, j)),
        ],
        out_specs=pl.BlockSpec((1, rows, D_MODEL), lambda l, j: (l, 0, j)),
        compiler_params=_params("arbitrary", "arbitrary"),
        name="adaln",
    )(c_all, ada_w, ada_b.reshape(DEPTH, 1, 3 * D_MODEL))


def _rope(x, cos, sin_signed):
    width = x.shape[-1]
    lane = lax.broadcasted_iota(jnp.int32, x.shape, 1)
    first_half = (lane % C_HD) < (C_HD // 2)
    swapped = jnp.where(first_half, pltpu.roll(x, width - C_HD // 2, 1), pltpu.roll(x, C_HD // 2, 1))
    reps = width // LANES
    cos_w = jnp.concatenate([cos] * reps, axis=-1) if reps > 1 else cos
    sin_w = jnp.concatenate([sin_signed] * reps, axis=-1) if reps > 1 else sin_signed
    return x * cos_w + swapped * sin_w


def _inproj_kernel(x_ref, mod_ref, preg_ref, w_ref, wg2_ref, bg_ref, qg_ref, kg_ref, sg_ref,
                   cos_ref, sin_ref, bd_ref, vone_ref,
                   gq_ref, gk_ref, gv_ref, gf_ref, gb_ref, fu_ref, aq_ref, kt_ref, va_ref,
                   du_ref, dvn_ref, sz_ref):
    x = x_ref[0]
    ms = jnp.mean(x * x, axis=-1, keepdims=True)
    shift = mod_ref[0, :, 0:D_MODEL]
    scale = mod_ref[0, :, D_MODEL:2 * D_MODEL]
    h = (x * lax.rsqrt(ms + EPS)) * preg_ref[...] * (1.0 + scale) + shift
    hb = h.astype(BF16)

    def seg(off, width):
        return jnp.dot(hb, w_ref[:, off:off + width], preferred_element_type=F32)

    klr = seg(AK, C_KVW + LANES)
    q = seg(AQ, C_W)
    z = seg(ZZ, D_MIX)
    gqk = seg(GQ, 2 * A_KW)
    gv = seg(GV, A_W)
    fu = seg(FU, B_W)
    va = seg(AV, AV_W)
    du = seg(DU, D_W)
    dv = seg(DV, D_W)
    k = klr[:, :C_KVW]
    lr = klr[:, C_KVW:].astype(BF16)
    logits = jnp.dot(lr, wg2_ref[...], preferred_element_type=F32) + bg_ref[...]
    qms = jnp.dot((q * q).astype(BF16), bd_ref[...], preferred_element_type=F32)
    kms = jnp.dot((k * k).astype(BF16), bd_ref[0:C_KVW, 0:C_KVW], preferred_element_type=F32)

    gq_ref[0] = (gqk[:, :A_KW] * (A_DK ** -0.5)).astype(BF16)
    gk_ref[0] = gqk[:, A_KW:].astype(BF16)
    gv_ref[0] = gv.astype(BF16)
    logg = (jnp.minimum(logits, 0.0) - jnp.log1p(jnp.exp(-jnp.abs(logits)))) * (1.0 / GLA_GATE_NORM)
    gf_ref[0] = logg[:, :A_KW]
    gb_ref[0] = logg[:, A_KW:]

    fu_ref[0] = fu.astype(BF16)

    cos = cos_ref[...]
    sin = sin_ref[...]
    qn = q * lax.rsqrt(qms + EPS) * qg_ref[...]
    aq_ref[0] = (_rope(qn, cos, sin) * (C_HD ** -0.5 * LOG2E)).astype(BF16)
    kn = k * lax.rsqrt(kms + EPS) * kg_ref[...]
    kt = _rope(kn, cos, sin).T.astype(BF16)
    for g in range(C_KV):
        for r in range(C_GRP):
            kt_ref[0, g, r * C_HD:(r + 1) * C_HD, :] = kt[g * C_HD:(g + 1) * C_HD, :]
    vab = (va + vone_ref[...]).astype(BF16)
    va_ref[0, 0] = vab[:, :LANES]
    va_ref[0, 1] = vab[:, LANES:]

    du_ref[0] = du.astype(BF16)
    dms = jnp.mean(dv * dv, axis=-1, keepdims=True)
    dvn_ref[0] = (dv * lax.rsqrt(dms + EPS) * sg_ref[...]).astype(BF16)

    sz_ref[0] = (z / (1.0 + jnp.exp(-z))).astype(BF16)


def _inproj(x, mod, lp, tabs):
    bsz, n, _ = x.shape
    tm = min(TM_IN, n)
    grid = (bsz, n // tm)
    tok = lambda w: pl.BlockSpec((1, tm, w), lambda b, i: (b, i, 0))
    out_shapes = (
        jax.ShapeDtypeStruct((bsz, n, A_KW), BF16),
        jax.ShapeDtypeStruct((bsz, n, A_KW), BF16),
        jax.ShapeDtypeStruct((bsz, n, A_W), BF16),
        jax.ShapeDtypeStruct((bsz, n, A_KW), F32),
        jax.ShapeDtypeStruct((bsz, n, A_KW), F32),
        jax.ShapeDtypeStruct((bsz, n, B_W), BF16),
        jax.ShapeDtypeStruct((bsz, n, C_W), BF16),
        jax.ShapeDtypeStruct((bsz, C_KV, C_GRP * C_HD, n), BF16),
        jax.ShapeDtypeStruct((bsz, C_KV, n, LANES), BF16),
        jax.ShapeDtypeStruct((bsz, n, D_W), BF16),
        jax.ShapeDtypeStruct((bsz, n, D_W), BF16),
        jax.ShapeDtypeStruct((bsz, n, D_MIX), BF16),
    )
    out_specs = (
        tok(A_KW), tok(A_KW), tok(A_W), tok(A_KW), tok(A_KW), tok(B_W), tok(C_W),
        pl.BlockSpec((1, C_KV, C_GRP * C_HD, tm), lambda b, i: (b, 0, 0, i)),
        pl.BlockSpec((1, C_KV, tm, LANES), lambda b, i: (b, 0, i, 0)),
        tok(D_W), tok(D_W), tok(D_MIX),
    )
    in_specs = [
        tok(D_MODEL),
        pl.BlockSpec((1, 1, 3 * D_MODEL), lambda b, i: (b, 0, 0)),
        _full((1, D_MODEL)),
        _full((D_MODEL, W_PACKED)),
        _full((LANES, 2 * A_KW)),
        _full((1, 2 * A_KW)),
        _full((1, C_W)),
        _full((1, C_KVW)),
        _full((1, D_W)),
        pl.BlockSpec((tm, LANES), lambda b, i: (i, 0)),
        pl.BlockSpec((tm, LANES), lambda b, i: (i, 0)),
        _full((C_W, C_W)),
        _full((1, 2 * LANES)),
    ]
    return pl.pallas_call(
        _inproj_kernel, out_shape=out_shapes, grid=grid, in_specs=in_specs, out_specs=out_specs,
        compiler_params=_params("parallel", "parallel"), name="inproj",
    )(x, mod, lp["pre_g"], lp["w_in"], lp["wg2"], lp["bg"], lp["qg"], lp["kg"], lp["sg"],
      tabs["cos"], tabs["sin"], lp["bd_q"], lp["vone"])


def _head_match(shape, rows_per_head, cols_per_head):
    return (lax.broadcasted_iota(jnp.int32, shape, 0) // rows_per_head
            == lax.broadcasted_iota(jnp.int32, shape, 1) // cols_per_head)


def _gla_band_kernel(q_ref, k_ref, v_ref, gf_ref, gb_ref, o_ref, kf_ref, vf_ref, bf_ref, cb_ref):
    ts = q_ref.shape[2]
    hsum = jnp.where(_head_match((A_KW, A_W), A_DK, A_DV), 1.0, 0.0).astype(BF16)
    run = jnp.zeros((ts, A_KW), F32)
    for p in range(GLA_SUB):
        run = run + gf_ref[0, p] * LOG2E
        bf_ref[p] = run
        kf_ref[p] = k_ref[0, p].astype(F32)
        vf_ref[p] = v_ref[0, p].astype(F32)
    run = jnp.zeros((ts, A_KW), F32)
    for p in reversed(range(GLA_SUB)):
        run = run + gb_ref[0, p] * LOG2E
        cb_ref[p] = run
    for p in range(GLA_SUB):
        q = q_ref[0, p].astype(F32)
        prods = []
        for s in range(GLA_SUB):
            qk = q * kf_ref[s]
            if s < p:
                qk = qk * jnp.exp2(bf_ref[p] - bf_ref[s])
            elif s > p:
                qk = qk * jnp.exp2(cb_ref[p] - cb_ref[s])
            prods.append(qk.astype(BF16))
        a = jnp.dot(jnp.concatenate(prods, axis=0), hsum, preferred_element_type=F32)
        acc = a[0:ts] * vf_ref[0]
        for s in range(1, GLA_SUB):
            acc = acc + a[s * ts:(s + 1) * ts] * vf_ref[s]
        o_ref[0, p] = acc.astype(BF16)


def _gla_band(gq, gk, gv, gf, gb):
    bsz, n, _ = gq.shape
    ns = n // GLA_SUB
    ts = min(GLA_BAND_TILE, ns)
    perm = lambda a: a.reshape(bsz, ns, GLA_SUB, a.shape[-1]).transpose(0, 2, 1, 3)
    blk = lambda w: pl.BlockSpec((1, GLA_SUB, ts, w), lambda b, i: (b, 0, i, 0))
    o = pl.pallas_call(
        _gla_band_kernel,
        out_shape=jax.ShapeDtypeStruct((bsz, GLA_SUB, ns, A_W), BF16),
        grid=(bsz, ns // ts),
        in_specs=[blk(A_KW), blk(A_KW), blk(A_W), blk(A_KW), blk(A_KW)],
        out_specs=blk(A_W),
        scratch_shapes=[
            pltpu.VMEM((GLA_SUB, ts, A_KW), F32), pltpu.VMEM((GLA_SUB, ts, A_W), F32),
            pltpu.VMEM((GLA_SUB, ts, A_KW), F32), pltpu.VMEM((GLA_SUB, ts, A_KW), F32),
        ],
        compiler_params=_params("parallel", "parallel"), name="gla_band",
    )(perm(gq), perm(gk), perm(gv), perm(gf), perm(gb))
    return o.transpose(0, 2, 1, 3).reshape(bsz, n, A_W)


def _split3(x):
    hi = x.astype(BF16)
    rest = x - hi.astype(F32)
    mid = rest.astype(BF16)
    lo = (rest - mid.astype(F32)).astype(BF16)
    return hi, mid, lo


def _gla_prep(q_ref, k_ref, g_ref, qt_ref, kh_ref, dect_ref, tot_ref, pad_ref, reverse):
    seg = q_ref.shape[1]
    n_sub = seg // GLA_SUB
    r_i = lax.broadcasted_iota(jnp.int32, (GLA_GRP, GLA_GRP), 0)
    c_i = lax.broadcasted_iota(jnp.int32, (GLA_GRP, GLA_GRP), 1)
    same = (r_i // GLA_SUB) == (c_i // GLA_SUB)
    tri = (c_i >= r_i) if reverse else (c_i <= r_i)
    sums_mat = jnp.concatenate([jnp.where(same & tri, 1.0, 0.0), jnp.where(same, 1.0, 0.0)], axis=0).astype(BF16)
    for r in range(seg // GLA_GRP):
        rows = slice(r * GLA_GRP, (r + 1) * GLA_GRP)
        g3 = jnp.concatenate(_split3(g_ref[0, rows, :]), axis=-1)
        sums = jnp.dot(sums_mat, g3, preferred_element_type=F32)
        sums = sums[:, 0:A_KW] + sums[:, A_KW:2 * A_KW] + sums[:, 2 * A_KW:]
        cum, tot = sums[:GLA_GRP], sums[GLA_GRP:]
        qt_ref[rows, :] = (q_ref[0, rows, :].astype(F32) * jnp.exp(cum)).astype(BF16)
        kh_ref[rows, :] = (k_ref[0, rows, :].astype(F32) * jnp.exp(tot - cum)).astype(BF16)
        tot_ref[rows, :] = tot
    pad_ref[...] = jnp.zeros_like(pad_ref)
    pad_ref[0:n_sub, :] = tot_ref[pl.ds(0, n_sub, stride=GLA_SUB), :]
    dect_ref[...] = jnp.exp(pad_ref[...].T)


def _gla_rec_kernel(qf_ref, kf_ref, vf_ref, gf_ref, qb_ref, kb_ref, vb_ref, gb_ref, of_ref, ob_ref,
                    sf_ref, sb_ref, qtf_ref, khf_ref, dtf_ref, qtb_ref, khb_ref, dtb_ref, tot_ref, pad_ref):
    @pl.when(pl.program_id(1) == 0)
    def _():
        sf_ref[...] = jnp.zeros_like(sf_ref)
        sb_ref[...] = jnp.zeros_like(sb_ref)

    seg = qf_ref.shape[1]
    n_sub = seg // GLA_SUB
    _gla_prep(qf_ref, kf_ref, gf_ref, qtf_ref, khf_ref, dtf_ref, tot_ref, pad_ref, reverse=False)
    _gla_prep(qb_ref, kb_ref, gb_ref, qtb_ref, khb_ref, dtb_ref, tot_ref, pad_ref, reverse=True)
    smask = _head_match((A_KW, A_W), A_DK, A_DV)

    def update(i, kh_ref, v_ref):
        rows = slice(i * GLA_SUB, (i + 1) * GLA_SUB)
        upd = lax.dot_general(kh_ref[rows, :], v_ref[0, rows, :], (((0,), (0,)), ((), ())),
                              preferred_element_type=F32)
        return jnp.where(smask, upd, 0.0)

    def step(i, s, upd, qt_ref, dect_ref, o_ref):
        rows = slice(i * GLA_SUB, (i + 1) * GLA_SUB)
        o_ref[0, rows, :] = jnp.dot(qt_ref[rows, :], s.astype(BF16), preferred_element_type=F32).astype(BF16)
        return s * dect_ref[:, i:i + 1] + upd

    order_f = list(range(n_sub))
    order_b = list(reversed(range(n_sub)))
    upd_f = [update(i, khf_ref, vf_ref) for i in order_f[:GLA_LOOKAHEAD]]
    upd_b = [update(i, khb_ref, vb_ref) for i in order_b[:GLA_LOOKAHEAD]]
    s_f = sf_ref[...]
    s_b = sb_ref[...]
    for j in range(n_sub):
        if j + GLA_LOOKAHEAD < n_sub:
            upd_f.append(update(order_f[j + GLA_LOOKAHEAD], khf_ref, vf_ref))
            upd_b.append(update(order_b[j + GLA_LOOKAHEAD], khb_ref, vb_ref))
        s_f = step(order_f[j], s_f, upd_f[j], qtf_ref, dtf_ref, of_ref)
        s_b = step(order_b[j], s_b, upd_b[j], qtb_ref, dtb_ref, ob_ref)
    sf_ref[...] = s_f
    sb_ref[...] = s_b


def _gla_rec(gq, gk, gv, gf, gb):
    bsz, n, _ = gq.shape
    seg = min(GLA_SEG, n)
    nseg = n // seg
    assert seg // GLA_SUB <= LANES
    fwd = lambda w: pl.BlockSpec((1, seg, w), lambda b, s: (b, s, 0))
    bwd = lambda w: pl.BlockSpec((1, seg, w), lambda b, s: (b, nseg - 1 - s, 0))
    return pl.pallas_call(
        _gla_rec_kernel,
        out_shape=(jax.ShapeDtypeStruct((bsz, n, A_W), BF16), jax.ShapeDtypeStruct((bsz, n, A_W), BF16)),
        grid=(bsz, nseg),
        in_specs=[fwd(A_KW), fwd(A_KW), fwd(A_W), fwd(A_KW), bwd(A_KW), bwd(A_KW), bwd(A_W), bwd(A_KW)],
        out_specs=(fwd(A_W), bwd(A_W)),
        scratch_shapes=[
            pltpu.VMEM((A_KW, A_W), F32), pltpu.VMEM((A_KW, A_W), F32),
            pltpu.VMEM((seg, A_KW), BF16), pltpu.VMEM((seg, A_KW), BF16), pltpu.VMEM((A_KW, LANES), F32),
            pltpu.VMEM((seg, A_KW), BF16), pltpu.VMEM((seg, A_KW), BF16), pltpu.VMEM((A_KW, LANES), F32),
            pltpu.VMEM((seg, A_KW), F32), pltpu.VMEM((LANES, A_KW), F32),
        ],
        compiler_params=_params("parallel", "arbitrary"), name="gla_rec",
    )(gq, gk, gv, gf, gq, gk, gv, gb)


def _fft1_kernel(u_ref, a1_ref, cos_ref, sin_ref, t_ref):
    n1 = u_ref.shape[1]
    nb = cos_ref.shape[0]
    t = jnp.dot(a1_ref[...], u_ref[0], preferred_element_type=F32)
    for j in range(nb):
        cols = slice(j * B_W, (j + 1) * B_W)
        tr = t[:n1, cols]
        ti = t[n1:, cols]
        c = jnp.concatenate([cos_ref[j]] * (B_W // LANES), axis=-1)
        s = jnp.concatenate([sin_ref[j]] * (B_W // LANES), axis=-1)
        t_ref[0, 0, :, cols] = (tr * c + ti * s).astype(BF16)
        t_ref[0, 1, :, cols] = (ti * c - tr * s).astype(BF16)


def _fft2_kernel(t_ref, l_ref, cc_ref, cs_ref, w_ref, o_ref):
    k1b, n2 = t_ref.shape[2], t_ref.shape[3]
    half = k1b * n2
    x = t_ref[0].reshape(2 * half, B_W)
    g = jnp.dot(l_ref[...], x, preferred_element_type=F32)
    gr = g[:half].astype(BF16)
    gi = g[half:].astype(BF16)
    mixed = (jnp.dot(gr, cc_ref[...], preferred_element_type=F32)
             + jnp.dot(gi, cs_ref[...], preferred_element_type=F32))
    y = jnp.dot(mixed.astype(BF16), w_ref[...], preferred_element_type=F32)
    o_ref[0] = y.reshape(n2, k1b, B_W)


def _fnet(fu, lp, tabs):
    bsz, n, _ = fu.shape
    n2 = FFT_N2
    n1 = n // n2
    nb = FFT_NB
    k1b = FFT_K1
    u2 = fu.reshape(bsz, n1, n2 * B_W)
    tt = pl.pallas_call(
        _fft1_kernel,
        out_shape=jax.ShapeDtypeStruct((bsz, 2, n1, n2 * B_W), BF16),
        grid=(n2 // nb, bsz),
        in_specs=[
            pl.BlockSpec((1, n1, nb * B_W), lambda j, b: (b, 0, j)),
            _full((2 * n1, n1)),
            pl.BlockSpec((nb, n1, LANES), lambda j, b: (j, 0, 0)),
            pl.BlockSpec((nb, n1, LANES), lambda j, b: (j, 0, 0)),
        ],
        out_specs=pl.BlockSpec((1, 2, n1, nb * B_W), lambda j, b: (b, 0, 0, j)),
        compiler_params=_params("parallel", "parallel"), name="fft_pass1",
    )(u2, tabs["a1"], tabs["tw_cos"], tabs["tw_sin"])
    t5 = tt.reshape(bsz, 2, n1, n2, B_W)
    y = pl.pallas_call(
        _fft2_kernel,
        out_shape=jax.ShapeDtypeStruct((bsz, n2, n1, B_W), F32),
        grid=(bsz, n1 // k1b),
        in_specs=[
            pl.BlockSpec((1, 2, k1b, n2, B_W), lambda b, i: (b, 0, i, 0, 0)),
            _full((2 * k1b * n2, 2 * k1b * n2)),
            _full((B_W, B_W)), _full((B_W, B_W)), _full((B_W, B_W)),
        ],
        out_specs=pl.BlockSpec((1, n2, k1b, B_W), lambda b, i: (b, 0, i, 0)),
        compiler_params=_params("parallel", "parallel"), name="fft_pass2",
    )(t5, tabs["l2"], tabs["chan_cos"], tabs["chan_sin"], lp["fnet_w"])
    return y.reshape(bsz, n, B_W)


def _attn_kernel(q_ref, kt_ref, va_ref, o_ref, qm_ref, m_ref, acc_ref):
    tq = q_ref.shape[1]
    n = kt_ref.shape[3]
    tk = min(ATT_TK, n)
    q = q_ref[0]
    lane = lax.broadcasted_iota(jnp.int32, q.shape, 1)
    for h in range(C_GRP):
        qm_ref[h * tq:(h + 1) * tq, :] = jnp.where(lane // C_HD == h, q, jnp.zeros_like(q))
    m_ref[...] = jnp.full(m_ref.shape, -jnp.inf, F32)
    acc_ref[...] = jnp.zeros(acc_ref.shape, F32)

    def scores(c):
        c0 = pl.multiple_of(c * tk, tk)
        return jnp.dot(qm_ref[...], kt_ref[0, 0, :, pl.ds(c0, tk)], preferred_element_type=F32)

    def softmax_accumulate(c, s):
        c0 = pl.multiple_of(c * tk, tk)
        v = va_ref[0, 0, pl.ds(c0, tk), :]
        s_max = s[:, 0:LANES]
        for t in range(1, tk // LANES):
            s_max = jnp.maximum(s_max, s[:, t * LANES:(t + 1) * LANES])
        row_max = jnp.max(s_max, axis=-1, keepdims=True)
        m_old = m_ref[...]
        m_new = jnp.maximum(m_old, row_max)
        p = jnp.exp2(s - jnp.concatenate([m_new] * (tk // LANES), axis=-1))
        alpha = jnp.exp2(m_old - m_new)
        acc_ref[...] = alpha * acc_ref[...] + jnp.dot(p.astype(BF16), v, preferred_element_type=F32)
        m_ref[...] = m_new

    def step(c, carry):
        softmax_accumulate(c, scores(c))
        return carry

    lax.fori_loop(0, n // tk, step, 0, unroll=ATT_UNROLL)
    acc = acc_ref[...]
    inv = 1.0 / acc[:, ONES_LANE:ONES_LANE + 1]
    outs = [(acc[h * tq:(h + 1) * tq, 0:C_HD] * inv[h * tq:(h + 1) * tq]) for h in range(C_GRP)]
    o_ref[0] = jnp.concatenate(outs, axis=-1).astype(BF16)


def _attention(aq, kt4, va):
    bsz, n, _ = aq.shape
    tq = min(ATT_TQ, n)
    return pl.pallas_call(
        _attn_kernel,
        out_shape=jax.ShapeDtypeStruct((bsz, n, C_W), BF16),
        grid=(bsz, C_KV, n // tq),
        in_specs=[
            pl.BlockSpec((1, tq, C_GRP * C_HD), lambda b, g, i: (b, i, g)),
            pl.BlockSpec((1, 1, C_GRP * C_HD, n), lambda b, g, i: (b, g, 0, 0)),
            pl.BlockSpec((1, 1, n, LANES), lambda b, g, i: (b, g, 0, 0)),
        ],
        out_specs=pl.BlockSpec((1, tq, C_GRP * C_HD), lambda b, g, i: (b, i, g)),
        scratch_shapes=[
            pltpu.VMEM((C_GRP * tq, C_GRP * C_HD), BF16),
            pltpu.VMEM((C_GRP * tq, LANES), F32),
            pltpu.VMEM((C_GRP * tq, LANES), F32),
        ],
        compiler_params=_params("parallel", "parallel", "arbitrary"), name="attention",
    )(aq, kt4, va)


def _outproj_kernel(x_ref, mod_ref, oa_ref, of_ref, ob_ref, yb_ref, ao_ref, du_ref, dvn_ref, sz_ref,
                    w_ref, postg_ref, og_ref, bd_ref, sw_ref, sb_ref, o_ref):
    tm = x_ref.shape[1]
    sz = sz_ref[0].astype(F32)

    o = oa_ref[0].astype(F32) + of_ref[0].astype(F32) + ob_ref[0].astype(F32)
    oms = jnp.dot((o * o).astype(BF16), bd_ref[...], preferred_element_type=F32)
    out_a = o * lax.rsqrt(oms + EPS) * og_ref[...]
    y = jnp.dot((out_a * sz[:, 0:A_W]).astype(BF16), w_ref[0:A_W, :], preferred_element_type=F32)

    m_b = (yb_ref[0] * sz[:, A_W:A_W + B_W]).astype(BF16)
    y = y + jnp.dot(m_b, w_ref[A_W:A_W + B_W, :], preferred_element_type=F32)

    m_c = (ao_ref[0].astype(F32) * sz[:, A_W + B_W:A_W + B_W + C_W]).astype(BF16)
    y = y + jnp.dot(m_c, w_ref[A_W + B_W:A_W + B_W + C_W, :], preferred_element_type=F32)

    lane = lax.broadcasted_iota(jnp.int32, (SGU_CHUNK, D_W), 1)
    parts = []
    for c in range(tm // SGU_CHUNK):
        rows = slice(c * SGU_CHUNK, (c + 1) * SGU_CHUNK)
        r = jnp.dot(sw_ref[...], dvn_ref[0, rows, :], preferred_element_type=F32)
        mix = sb_ref[...]
        for g in range(D_GROUPS):
            mix = mix + jnp.where(lane // D_GC == g, r[g * SGU_CHUNK:(g + 1) * SGU_CHUNK, :], 0.0)
        parts.append(du_ref[0, rows, :].astype(F32) * mix)
    out_d = jnp.concatenate(parts, axis=0)
    m_d = (out_d * sz[:, A_W + B_W + C_W:]).astype(BF16)
    y = y + jnp.dot(m_d, w_ref[A_W + B_W + C_W:, :], preferred_element_type=F32)

    yms = jnp.mean(y * y, axis=-1, keepdims=True)
    yn = y * lax.rsqrt(yms + EPS) * postg_ref[...]
    gate = mod_ref[0, :, 2 * D_MODEL:3 * D_MODEL]
    o_ref[0] = x_ref[0] + gate * yn


def _outproj(x, mod, oa, of, ob, yb, ao, du, dvn, sz, lp):
    bsz, n, _ = x.shape
    tm = min(TM_OUT, n)
    tok = lambda w: pl.BlockSpec((1, tm, w), lambda b, i: (b, i, 0))
    return pl.pallas_call(
        _outproj_kernel,
        out_shape=jax.ShapeDtypeStruct((bsz, n, D_MODEL), F32),
        grid=(bsz, n // tm),
        in_specs=[
            tok(D_MODEL),
            pl.BlockSpec((1, 1, 3 * D_MODEL), lambda b, i: (b, 0, 0)),
            tok(A_W), tok(A_W), tok(A_W), tok(B_W), tok(C_W), tok(D_W), tok(D_W), tok(D_MIX),
            _full((D_MIX, D_MODEL)), _full((1, D_MODEL)), _full((1, A_W)), _full((A_W, A_W)),
            _full((D_GROUPS * SGU_CHUNK, SGU_CHUNK)), _full((SGU_CHUNK, D_W)),
        ],
        out_specs=tok(D_MODEL),
        compiler_params=_params("parallel", "parallel"), name="outproj",
    )(x, mod, oa, of, ob, yb, ao, du, dvn, sz,
      lp["w_out"], lp["post_g"], lp["og"], lp["bd_a"], lp["sgu_w"], lp["sgu_b"])


def _pack_w_in(w):
    sizes = [A_KW, A_KW, A_W, 2 * GLA_RANK, B_W, C_W, C_KVW, C_KVW, D_W, D_W, D_MIX]
    off = [0] + [int(o) for o in np.cumsum(sizes)]
    a_q, a_k, a_v, a_lr, b_u, c_q, c_k, c_v, d_u, d_v, z = [w[:, off[i]:off[i + 1]] for i in range(len(sizes))]

    def deinterleave(x, heads):
        r = x.reshape(D_MODEL, heads, C_HD // 2, 2)
        return jnp.concatenate([r[..., 0], r[..., 1]], axis=-1).reshape(D_MODEL, heads * C_HD)

    zeros = lambda width: jnp.zeros((D_MODEL, width), w.dtype)
    pieces = [a_q, a_k, a_v, b_u, deinterleave(c_q, C_HEADS), deinterleave(c_k, C_KV),
              a_lr, zeros(LANES - 2 * GLA_RANK)]
    for h in range(C_KV):
        pieces += [c_v[:, h * C_HD:(h + 1) * C_HD], zeros(LANES - C_HD)]
    pieces += [d_u, d_v, z]
    packed = jnp.concatenate(pieces, axis=1)
    assert packed.shape[1] == W_PACKED
    return packed.astype(BF16)


def _block_diag_mean(width, group):
    idx = np.arange(width) // group
    return jnp.asarray((idx[:, None] == idx[None, :]).astype(np.float32) / group, BF16)


def _layer_params(l, ada_unused, norm_pre_g, norm_post_g, w_in, gla_wg2_f, gla_bg_f, gla_wg2_b, gla_bg_b,
                  gla_onorm_g, fnet_w, q_norm_g, k_norm_g, sgu_norm_g, sgu_w, sgu_b, w_out):
    deint = np.concatenate([np.arange(0, C_HD, 2), np.arange(1, C_HD, 2)])
    w = _pack_w_in(w_in[l])
    wg2 = jnp.zeros((LANES, 2 * A_KW), F32)
    wg2 = wg2.at[0:GLA_RANK, 0:A_KW].set(gla_wg2_f[l])
    wg2 = wg2.at[GLA_RANK:2 * GLA_RANK, A_KW:].set(gla_wg2_b[l])
    vone = np.zeros((1, 2 * LANES), np.float32)
    vone[0, ONES_LANE] = 1.0
    vone[0, LANES + ONES_LANE] = 1.0
    return {
        "pre_g": norm_pre_g[l].reshape(1, D_MODEL),
        "post_g": norm_post_g[l].reshape(1, D_MODEL),
        "w_in": w,
        "wg2": wg2.astype(BF16),
        "bg": jnp.concatenate([gla_bg_f[l], gla_bg_b[l]]).reshape(1, 2 * A_KW),
        "qg": jnp.tile(q_norm_g[l][deint], C_HEADS).reshape(1, C_W),
        "kg": jnp.tile(k_norm_g[l][deint], C_KV).reshape(1, C_KVW),
        "sg": sgu_norm_g[l].reshape(1, D_W),
        "bd_q": _block_diag_mean(C_W, C_HD),
        "bd_a": _block_diag_mean(A_W, A_DV),
        "vone": jnp.asarray(vone),
        "og": jnp.tile(gla_onorm_g[l], A_HEADS).reshape(1, A_W),
        "fnet_w": fnet_w[l].astype(BF16),
        "sgu_w": sgu_w[l].reshape(D_GROUPS * SGU_CHUNK, SGU_CHUNK).astype(BF16),
        "sgu_b": jnp.repeat(sgu_b[l].T, D_GC, axis=1),
        "w_out": w_out[l].astype(BF16),
    }


def _seq_tables(n):
    rows = n // GRID_W
    row = jnp.repeat(jnp.arange(rows, dtype=F32), GRID_W)
    col = jnp.tile(jnp.arange(GRID_W, dtype=F32), rows)
    rope_axis = C_HD // 2
    freqs = ROPE_THETA ** (-jnp.arange(0, rope_axis, 2, dtype=F32) / rope_axis)
    ang = jnp.concatenate([row[:, None] * freqs, col[:, None] * freqs], axis=-1)
    cos, sin = jnp.cos(ang), jnp.sin(ang)
    cos_t = jnp.tile(jnp.concatenate([cos, cos], axis=-1), (1, LANES // C_HD))
    sin_t = jnp.tile(jnp.concatenate([-sin, sin], axis=-1), (1, LANES // C_HD))

    n2 = FFT_N2
    n1 = n // n2

    def dft_angles(rows, cols, period):
        prod = (jnp.arange(rows, dtype=jnp.int32)[:, None] * jnp.arange(cols, dtype=jnp.int32)[None, :]) % period
        return prod.astype(F32) * (2.0 * math.pi / period)

    a1_ang = dft_angles(n1, n1, n1)
    a1 = jnp.concatenate([jnp.cos(a1_ang), -jnp.sin(a1_ang)], axis=0)
    tw_ang = dft_angles(n2, n1, n)
    scale = 1.0 / math.sqrt(n * B_GC)
    tw_cos = jnp.broadcast_to((jnp.cos(tw_ang) * scale)[:, :, None], (n2, n1, LANES))
    tw_sin = jnp.broadcast_to((jnp.sin(tw_ang) * scale)[:, :, None], (n2, n1, LANES))
    ang2 = dft_angles(n2, n2, n2)
    c2, s2 = jnp.cos(ang2), jnp.sin(ang2)
    eye = jnp.eye(FFT_K1, dtype=F32)
    blk = lambda m: (m[:, None, None, :] * eye[None, :, :, None]).reshape(n2 * FFT_K1, FFT_K1 * n2)
    l2 = jnp.block([[blk(c2), blk(s2)], [blk(-s2), blk(c2)]])
    angc = dft_angles(B_GC, B_GC, B_GC)
    grp = jnp.eye(B_GROUPS, dtype=F32)
    kron = lambda m: (grp[:, None, :, None] * m[None, :, None, :]).reshape(B_W, B_W)
    chan_cos = kron(jnp.cos(angc))
    chan_sin = kron(jnp.sin(angc))
    return {
        "cos": cos_t, "sin": sin_t,
        "a1": a1.astype(BF16),
        "tw_cos": tw_cos, "tw_sin": tw_sin,
        "l2": l2.astype(BF16),
        "chan_cos": chan_cos.astype(BF16), "chan_sin": chan_sin.astype(BF16),
    }


def _layer(x, mod, lp, tabs):
    gq, gk, gv, gf, gb, fu, aq, kt4, va, du, dvn, sz = _inproj(x, mod, lp, tabs)
    oa = _gla_band(gq, gk, gv, gf, gb)
    of, ob = _gla_rec(gq, gk, gv, gf, gb)
    yb = _fnet(fu, lp, tabs)
    ao = _attention(aq, kt4, va)
    return _outproj(x, mod, oa, of, ob, yb, ao, du, dvn, sz, lp)


def kernel(x_prompt, x_sample, c_prompt, c_sample, ada_w, ada_b, norm_pre_g, norm_post_g, w_in,
           gla_wg2_f, gla_bg_f, gla_wg2_b, gla_bg_b, gla_onorm_g, fnet_w, q_norm_g, k_norm_g,
           sgu_norm_g, sgu_w, sgu_b, w_out):
    bp, bs = x_prompt.shape[0], x_sample.shape[0]
    pad_rows = (-(bp + bs)) % 8
    c_all = jnp.concatenate([c_prompt, c_sample, jnp.zeros((pad_rows, D_MODEL), F32)], axis=0)
    mod = _adaln(c_all, ada_w, ada_b)
    tabs_p = _seq_tables(x_prompt.shape[1])
    tabs_s = _seq_tables(x_sample.shape[1])
    y_prompt, y_sample = x_prompt, x_sample
    for l in range(DEPTH):
        lp = _layer_params(l, None, norm_pre_g, norm_post_g, w_in, gla_wg2_f, gla_bg_f, gla_wg2_b,
                           gla_bg_b, gla_onorm_g, fnet_w, q_norm_g, k_norm_g, sgu_norm_g, sgu_w,
                           sgu_b, w_out)
        mod_p = mod[l, 0:bp].reshape(bp, 1, 3 * D_MODEL)
        mod_s = mod[l, bp:bp + bs].reshape(bs, 1, 3 * D_MODEL)
        y_prompt = _layer(y_prompt, mod_p, lp, tabs_p)
        y_sample = _layer(y_sample, mod_s, lp, tabs_s)
    return (y_prompt, y_sample)
```

```python
import math

import numpy as np
import jax
import jax.numpy as jnp
from jax import lax
from jax.experimental import pallas as pl
from jax.experimental.pallas import tpu as pltpu

F32 = jnp.float32
BF16 = jnp.bfloat16
HIGHEST = lax.Precision.HIGHEST

D_MODEL = 1024
DEPTH = 2
GRID_W = 64
A_HEADS, A_DK, A_DV = 4, 32, 64
A_W, A_KW = A_HEADS * A_DV, A_HEADS * A_DK
GLA_RANK = 16
GLA_GATE_NORM = 16.0
B_GROUPS, B_GC = 4, 64
B_W = B_GROUPS * B_GC
C_HEADS, C_KV, C_HD = 8, 2, 64
C_GRP = C_HEADS // C_KV
C_W, C_KVW = C_HEADS * C_HD, C_KV * C_HD
ROPE_THETA = 10000.0
D_GROUPS, D_GC = 4, 64
D_W = D_GROUPS * D_GC
SGU_CHUNK = 128
D_MIX = A_W + B_W + C_W + D_W
EPS = 1e-6
LOG2E = 1.4426950408889634

LANES = 128
VMEM_LIMIT_BYTES = 56 * 1024 * 1024

GQ, GK, GV, FU, AQ, AK, LR, AV, DU, DV, ZZ = 0, 128, 256, 512, 768, 1280, 1408, 1536, 1792, 2048, 2304
W_PACKED = 3584
AV_W = 2 * LANES
ONES_LANE = C_HD

TM_IN = 512
TM_OUT = 512
GLA_SUB = 16
GLA_BAND_TILE = 64
GLA_GRP = 128
GLA_SEG = 512
GLA_LOOKAHEAD = 3
FFT_N2 = 64
FFT_NB = 8
FFT_K1 = 8
ATT_TQ = 512
ATT_TK = 1024
ATT_UNROLL = 4


def _params(*sem):
    return pltpu.CompilerParams(dimension_semantics=sem, vmem_limit_bytes=VMEM_LIMIT_BYTES)


def _full(shape):
    n = len(shape)
    return pl.BlockSpec(shape, lambda *_: (0,) * n)


def _adaln_kernel(c_ref, w_ref, b_ref, o_ref):
    c = c_ref[...]
    sc = c / (1.0 + jnp.exp(-c))
    o_ref[0] = jnp.dot(sc, w_ref[0], precision=HIGHEST, preferred_element_type=F32) + b_ref[0]


def _adaln(c_all, ada_w, ada_b):
    rows = c_all.shape[0]
    ncol = 3 * D_MODEL // D_MODEL
    return pl.pallas_call(
        _adaln_kernel,
        out_shape=jax.ShapeDtypeStruct((DEPTH, rows, 3 * D_MODEL), F32),
        grid=(DEPTH, ncol),
        in_specs=[
            pl.BlockSpec((rows, D_MODEL), lambda l, j: (0, 0)),
            pl.BlockSpec((1, D_MODEL, D_MODEL), lambda l, j: (l, 0, j)),
            pl.BlockSpec((1, 1, D_MODEL), lambda l, j: (l, 0, j)),
        ],
        out_specs=pl.BlockSpec((1, rows, D_MODEL), lambda l, j: (l, 0, j)),
        compiler_params=_params("arbitrary", "arbitrary"),
        name="adaln",
    )(c_all, ada_w, ada_b.reshape(DEPTH, 1, 3 * D_MODEL))


def _rope(x, cos, sin_signed):
    width = x.shape[-1]
    lane = lax.broadcasted_iota(jnp.int32, x.shape, 1)
    first_half = (lane % C_HD) < (C_HD // 2)
    swapped = jnp.where(first_half, pltpu.roll(x, width - C_HD // 2, 1), pltpu.roll(x, C_HD // 2, 1))
    reps = width // LANES
    cos_w = jnp.concatenate([cos] * reps, axis=-1) if reps > 1 else cos
    sin_w = jnp.concatenate([sin_signed] * reps, axis=-1) if reps > 1 else sin_signed
    return x * cos_w + swapped * sin_w


def _inproj_kernel(x_ref, mod_ref, preg_ref, w_ref, wg2_ref, bg_ref, qg_ref, kg_ref, sg_ref,
                   cos_ref, sin_ref, bd_ref, vone_ref,
                   gq_ref, gk_ref, gv_ref, gf_ref, gb_ref, fu_ref, aq_ref, kt_ref, va_ref,
                   du_ref, dvn_ref, sz_ref):
    x = x_ref[0]
    ms = jnp.mean(x * x, axis=-1, keepdims=True)
    shift = mod_ref[0, :, 0:D_MODEL]
    scale = mod_ref[0, :, D_MODEL:2 * D_MODEL]
    h = (x * lax.rsqrt(ms + EPS)) * preg_ref[...] * (1.0 + scale) + shift
    hb = h.astype(BF16)

    def seg(off, width):
        return jnp.dot(hb, w_ref[:, off:off + width], preferred_element_type=F32)

    klr = seg(AK, C_KVW + LANES)
    q = seg(AQ, C_W)
    z = seg(ZZ, D_MIX)
    gqk = seg(GQ, 2 * A_KW)
    gv = seg(GV, A_W)
    fu = seg(FU, B_W)
    va = seg(AV, AV_W)
    du = seg(DU, D_W)
    dv = seg(DV, D_W)
    k = klr[:, :C_KVW]
    lr = klr[:, C_KVW:].astype(BF16)
    logits = jnp.dot(lr, wg2_ref[...], preferred_element_type=F32) + bg_ref[...]
    qms = jnp.dot((q * q).astype(BF16), bd_ref[...], preferred_element_type=F32)
    kms = jnp.dot((k * k).astype(BF16), bd_ref[0:C_KVW, 0:C_KVW], preferred_element_type=F32)

    gq_ref[0] = gqk[:, :A_KW] * (A_DK ** -0.5)
    gk_ref[0] = gqk[:, A_KW:]
    gv_ref[0] = gv
    logg = (jnp.minimum(logits, 0.0) - jnp.log1p(jnp.exp(-jnp.abs(logits)))) * (1.0 / GLA_GATE_NORM)
    gf_ref[0] = logg[:, :A_KW]
    gb_ref[0] = logg[:, A_KW:]

    fu_ref[0] = fu.astype(BF16)

    cos = cos_ref[...]
    sin = sin_ref[...]
    qn = q * lax.rsqrt(qms + EPS) * qg_ref[...]
    aq_ref[0] = (_rope(qn, cos, sin) * (C_HD ** -0.5 * LOG2E)).astype(BF16)
    kn = k * lax.rsqrt(kms + EPS) * kg_ref[...]
    kt = _rope(kn, cos, sin).T.astype(BF16)
    for g in range(C_KV):
        for r in range(C_GRP):
            kt_ref[0, g, r * C_HD:(r + 1) * C_HD, :] = kt[g * C_HD:(g + 1) * C_HD, :]
    vab = (va + vone_ref[...]).astype(BF16)
    va_ref[0, 0] = vab[:, :LANES]
    va_ref[0, 1] = vab[:, LANES:]

    du_ref[0] = du.astype(BF16)
    dms = jnp.mean(dv * dv, axis=-1, keepdims=True)
    dvn_ref[0] = (dv * lax.rsqrt(dms + EPS) * sg_ref[...]).astype(BF16)

    sz_ref[0] = (z / (1.0 + jnp.exp(-z))).astype(BF16)


def _inproj(x, mod, lp, tabs):
    bsz, n, _ = x.shape
    tm = min(TM_IN, n)
    grid = (bsz, n // tm)
    tok = lambda w: pl.BlockSpec((1, tm, w), lambda b, i: (b, i, 0))
    out_shapes = (
        jax.ShapeDtypeStruct((bsz, n, A_KW), F32),
        jax.ShapeDtypeStruct((bsz, n, A_KW), F32),
        jax.ShapeDtypeStruct((bsz, n, A_W), F32),
        jax.ShapeDtypeStruct((bsz, n, A_KW), F32),
        jax.ShapeDtypeStruct((bsz, n, A_KW), F32),
        jax.ShapeDtypeStruct((bsz, n, B_W), BF16),
        jax.ShapeDtypeStruct((bsz, n, C_W), BF16),
        jax.ShapeDtypeStruct((bsz, C_KV, C_GRP * C_HD, n), BF16),
        jax.ShapeDtypeStruct((bsz, C_KV, n, LANES), BF16),
        jax.ShapeDtypeStruct((bsz, n, D_W), BF16),
        jax.ShapeDtypeStruct((bsz, n, D_W), BF16),
        jax.ShapeDtypeStruct((bsz, n, D_MIX), BF16),
    )
    out_specs = (
        tok(A_KW), tok(A_KW), tok(A_W), tok(A_KW), tok(A_KW), tok(B_W), tok(C_W),
        pl.BlockSpec((1, C_KV, C_GRP * C_HD, tm), lambda b, i: (b, 0, 0, i)),
        pl.BlockSpec((1, C_KV, tm, LANES), lambda b, i: (b, 0, i, 0)),
        tok(D_W), tok(D_W), tok(D_MIX),
    )
    in_specs = [
        tok(D_MODEL),
        pl.BlockSpec((1, 1, 3 * D_MODEL), lambda b, i: (b, 0, 0)),
        _full((1, D_MODEL)),
        _full((D_MODEL, W_PACKED)),
        _full((LANES, 2 * A_KW)),
        _full((1, 2 * A_KW)),
        _full((1, C_W)),
        _full((1, C_KVW)),
        _full((1, D_W)),
        pl.BlockSpec((tm, LANES), lambda b, i: (i, 0)),
        pl.BlockSpec((tm, LANES), lambda b, i: (i, 0)),
        _full((C_W, C_W)),
        _full((1, 2 * LANES)),
    ]
    return pl.pallas_call(
        _inproj_kernel, out_shape=out_shapes, grid=grid, in_specs=in_specs, out_specs=out_specs,
        compiler_params=_params("parallel", "parallel"), name="inproj",
    )(x, mod, lp["pre_g"], lp["w_in"], lp["wg2"], lp["bg"], lp["qg"], lp["kg"], lp["sg"],
      tabs["cos"], tabs["sin"], lp["bd_q"], lp["vone"])


def _head_match(shape, rows_per_head, cols_per_head):
    return (lax.broadcasted_iota(jnp.int32, shape, 0) // rows_per_head
            == lax.broadcasted_iota(jnp.int32, shape, 1) // cols_per_head)


def _gla_band_kernel(q_ref, k_ref, v_ref, gf_ref, gb_ref, o_ref,
                     qf_ref, kf_ref, vf_ref, bf_ref, cb_ref, vs_ref, os_ref):
    rows = q_ref.shape[1]
    ts = rows // GLA_SUB
    n_slab = A_W // LANES
    every = lambda p: pl.ds(p, ts, stride=GLA_SUB)
    hsum = jnp.where(_head_match((A_KW, A_W), A_DK, A_DV), 1.0, 0.0).astype(BF16)
    for h in range(n_slab):
        vs_ref[h] = v_ref[0, :, h * LANES:(h + 1) * LANES]
    run = jnp.zeros((ts, A_KW), F32)
    for p in range(GLA_SUB):
        run = run + gf_ref[0, every(p), :] * LOG2E
        bf_ref[p] = run
        qf_ref[p] = q_ref[0, every(p), :]
        kf_ref[p] = k_ref[0, every(p), :]
        vf_ref[p] = jnp.concatenate([vs_ref[h, every(p), :] for h in range(n_slab)], axis=-1)
    run = jnp.zeros((ts, A_KW), F32)
    for p in reversed(range(GLA_SUB)):
        run = run + gb_ref[0, every(p), :] * LOG2E
        cb_ref[p] = run
    for p in range(GLA_SUB):
        q = qf_ref[p]
        prods = []
        for s in range(GLA_SUB):
            qk = q * kf_ref[s]
            if s < p:
                qk = qk * jnp.exp2(bf_ref[p] - bf_ref[s])
            elif s > p:
                qk = qk * jnp.exp2(cb_ref[p] - cb_ref[s])
            prods.append(qk.astype(BF16))
        a = jnp.dot(jnp.concatenate(prods, axis=0), hsum, preferred_element_type=F32)
        acc = a[0:ts] * vf_ref[0]
        for s in range(1, GLA_SUB):
            acc = acc + a[s * ts:(s + 1) * ts] * vf_ref[s]
        for h in range(n_slab):
            os_ref[h, every(p), :] = acc[:, h * LANES:(h + 1) * LANES]
    o_ref[0] = jnp.concatenate([os_ref[h] for h in range(n_slab)], axis=-1).astype(BF16)


def _gla_band(gq, gk, gv, gf, gb):
    bsz, n, _ = gq.shape
    rows = min(GLA_BAND_TILE * GLA_SUB, n)
    ts = rows // GLA_SUB
    blk = lambda w: pl.BlockSpec((1, rows, w), lambda b, i: (b, i, 0))
    return pl.pallas_call(
        _gla_band_kernel,
        out_shape=jax.ShapeDtypeStruct((bsz, n, A_W), BF16),
        grid=(bsz, n // rows),
        in_specs=[blk(A_KW), blk(A_KW), blk(A_W), blk(A_KW), blk(A_KW)],
        out_specs=blk(A_W),
        scratch_shapes=[
            pltpu.VMEM((GLA_SUB, ts, A_KW), F32), pltpu.VMEM((GLA_SUB, ts, A_KW), F32),
            pltpu.VMEM((GLA_SUB, ts, A_W), F32),
            pltpu.VMEM((GLA_SUB, ts, A_KW), F32), pltpu.VMEM((GLA_SUB, ts, A_KW), F32),
            pltpu.VMEM((A_W // LANES, rows, LANES), F32), pltpu.VMEM((A_W // LANES, rows, LANES), F32),
        ],
        compiler_params=_params("parallel", "parallel"), name="gla_band",
    )(gq, gk, gv, gf, gb)


def _split3(x):
    hi = x.astype(BF16)
    rest = x - hi.astype(F32)
    mid = rest.astype(BF16)
    lo = (rest - mid.astype(F32)).astype(BF16)
    return hi, mid, lo


def _gla_prep(q_ref, k_ref, g_ref, qt_ref, kh_ref, dect_ref, tot_ref, pad_ref, reverse):
    seg = q_ref.shape[1]
    n_sub = seg // GLA_SUB
    r_i = lax.broadcasted_iota(jnp.int32, (GLA_GRP, GLA_GRP), 0)
    c_i = lax.broadcasted_iota(jnp.int32, (GLA_GRP, GLA_GRP), 1)
    same = (r_i // GLA_SUB) == (c_i // GLA_SUB)
    tri = (c_i >= r_i) if reverse else (c_i <= r_i)
    sums_mat = jnp.concatenate([jnp.where(same & tri, 1.0, 0.0), jnp.where(same, 1.0, 0.0)], axis=0).astype(BF16)
    for r in range(seg // GLA_GRP):
        rows = slice(r * GLA_GRP, (r + 1) * GLA_GRP)
        g3 = jnp.concatenate(_split3(g_ref[0, rows, :]), axis=-1)
        sums = jnp.dot(sums_mat, g3, preferred_element_type=F32)
        sums = sums[:, 0:A_KW] + sums[:, A_KW:2 * A_KW] + sums[:, 2 * A_KW:]
        cum, tot = sums[:GLA_GRP], sums[GLA_GRP:]
        qt_ref[rows, :] = (q_ref[0, rows, :] * jnp.exp(cum)).astype(BF16)
        kh_ref[rows, :] = (k_ref[0, rows, :] * jnp.exp(tot - cum)).astype(BF16)
        tot_ref[rows, :] = tot
    pad_ref[...] = jnp.zeros_like(pad_ref)
    pad_ref[0:n_sub, :] = tot_ref[pl.ds(0, n_sub, stride=GLA_SUB), :]
    dect_ref[...] = jnp.exp(pad_ref[...].T)


def _gla_rec_kernel(qf_ref, kf_ref, vf_ref, gf_ref, qb_ref, kb_ref, vb_ref, gb_ref, of_ref, ob_ref,
                    sf_ref, sb_ref, qtf_ref, khf_ref, dtf_ref, qtb_ref, khb_ref, dtb_ref, tot_ref, pad_ref):
    @pl.when(pl.program_id(1) == 0)
    def _():
        sf_ref[...] = jnp.zeros_like(sf_ref)
        sb_ref[...] = jnp.zeros_like(sb_ref)

    seg = qf_ref.shape[1]
    n_sub = seg // GLA_SUB
    _gla_prep(qf_ref, kf_ref, gf_ref, qtf_ref, khf_ref, dtf_ref, tot_ref, pad_ref, reverse=False)
    _gla_prep(qb_ref, kb_ref, gb_ref, qtb_ref, khb_ref, dtb_ref, tot_ref, pad_ref, reverse=True)
    smask = _head_match((A_KW, A_W), A_DK, A_DV)

    def update(i, kh_ref, v_ref):
        rows = slice(i * GLA_SUB, (i + 1) * GLA_SUB)
        upd = lax.dot_general(kh_ref[rows, :], v_ref[0, rows, :].astype(BF16), (((0,), (0,)), ((), ())),
                              preferred_element_type=F32)
        return jnp.where(smask, upd, 0.0)

    def step(i, s, upd, qt_ref, dect_ref, o_ref):
        rows = slice(i * GLA_SUB, (i + 1) * GLA_SUB)
        o_ref[0, rows, :] = jnp.dot(qt_ref[rows, :], s.astype(BF16), preferred_element_type=F32).astype(BF16)
        return s * dect_ref[:, i:i + 1] + upd

    order_f = list(range(n_sub))
    order_b = list(reversed(range(n_sub)))
    upd_f = [update(i, khf_ref, vf_ref) for i in order_f[:GLA_LOOKAHEAD]]
    upd_b = [update(i, khb_ref, vb_ref) for i in order_b[:GLA_LOOKAHEAD]]
    s_f = sf_ref[...]
    s_b = sb_ref[...]
    for j in range(n_sub):
        if j + GLA_LOOKAHEAD < n_sub:
            upd_f.append(update(order_f[j + GLA_LOOKAHEAD], khf_ref, vf_ref))
            upd_b.append(update(order_b[j + GLA_LOOKAHEAD], khb_ref, vb_ref))
        s_f = step(order_f[j], s_f, upd_f[j], qtf_ref, dtf_ref, of_ref)
        s_b = step(order_b[j], s_b, upd_b[j], qtb_ref, dtb_ref, ob_ref)
    sf_ref[...] = s_f
    sb_ref[...] = s_b


def _gla_rec(gq, gk, gv, gf, gb):
    bsz, n, _ = gq.shape
    seg = min(GLA_SEG, n)
    nseg = n // seg
    assert seg // GLA_SUB <= LANES
    fwd = lambda w: pl.BlockSpec((1, seg, w), lambda b, s: (b, s, 0))
    bwd = lambda w: pl.BlockSpec((1, seg, w), lambda b, s: (b, nseg - 1 - s, 0))
    return pl.pallas_call(
        _gla_rec_kernel,
        out_shape=(jax.ShapeDtypeStruct((bsz, n, A_W), BF16), jax.ShapeDtypeStruct((bsz, n, A_W), BF16)),
        grid=(bsz, nseg),
        in_specs=[fwd(A_KW), fwd(A_KW), fwd(A_W), fwd(A_KW), bwd(A_KW), bwd(A_KW), bwd(A_W), bwd(A_KW)],
        out_specs=(fwd(A_W), bwd(A_W)),
        scratch_shapes=[
            pltpu.VMEM((A_KW, A_W), F32), pltpu.VMEM((A_KW, A_W), F32),
            pltpu.VMEM((seg, A_KW), BF16), pltpu.VMEM((seg, A_KW), BF16), pltpu.VMEM((A_KW, LANES), F32),
            pltpu.VMEM((seg, A_KW), BF16), pltpu.VMEM((seg, A_KW), BF16), pltpu.VMEM((A_KW, LANES), F32),
            pltpu.VMEM((seg, A_KW), F32), pltpu.VMEM((LANES, A_KW), F32),
        ],
        compiler_params=_params("parallel", "arbitrary"), name="gla_rec",
    )(gq, gk, gv, gf, gq, gk, gv, gb)


def _fft1_kernel(u_ref, a1_ref, cos_ref, sin_ref, t_ref):
    n1 = u_ref.shape[1]
    nb = cos_ref.shape[0]
    t = jnp.dot(a1_ref[...], u_ref[0], preferred_element_type=F32)
    for j in range(nb):
        cols = slice(j * B_W, (j + 1) * B_W)
        tr = t[:n1, cols]
        ti = t[n1:, cols]
        c = jnp.concatenate([cos_ref[j]] * (B_W // LANES), axis=-1)
        s = jnp.concatenate([sin_ref[j]] * (B_W // LANES), axis=-1)
        t_ref[0, 0, :, cols] = (tr * c + ti * s).astype(BF16)
        t_ref[0, 1, :, cols] = (ti * c - tr * s).astype(BF16)


def _fft2_kernel(t_ref, l_ref, cc_ref, cs_ref, w_ref, o_ref):
    k1b, n2 = t_ref.shape[2], t_ref.shape[3]
    half = k1b * n2
    x = t_ref[0].reshape(2 * half, B_W)
    g = jnp.dot(l_ref[...], x, preferred_element_type=F32)
    gr = g[:half].astype(BF16)
    gi = g[half:].astype(BF16)
    mixed = (jnp.dot(gr, cc_ref[...], preferred_element_type=F32)
             + jnp.dot(gi, cs_ref[...], preferred_element_type=F32))
    y = jnp.dot(mixed.astype(BF16), w_ref[...], preferred_element_type=F32)
    o_ref[0] = y.reshape(n2, k1b, B_W)


def _fnet(fu, lp, tabs):
    bsz, n, _ = fu.shape
    n2 = FFT_N2
    n1 = n // n2
    nb = FFT_NB
    k1b = FFT_K1
    u2 = fu.reshape(bsz, n1, n2 * B_W)
    tt = pl.pallas_call(
        _fft1_kernel,
        out_shape=jax.ShapeDtypeStruct((bsz, 2, n1, n2 * B_W), BF16),
        grid=(n2 // nb, bsz),
        in_specs=[
            pl.BlockSpec((1, n1, nb * B_W), lambda j, b: (b, 0, j)),
            _full((2 * n1, n1)),
            pl.BlockSpec((nb, n1, LANES), lambda j, b: (j, 0, 0)),
            pl.BlockSpec((nb, n1, LANES), lambda j, b: (j, 0, 0)),
        ],
        out_specs=pl.BlockSpec((1, 2, n1, nb * B_W), lambda j, b: (b, 0, 0, j)),
        compiler_params=_params("parallel", "parallel"), name="fft_pass1",
    )(u2, tabs["a1"], tabs["tw_cos"], tabs["tw_sin"])
    t5 = tt.reshape(bsz, 2, n1, n2, B_W)
    y = pl.pallas_call(
        _fft2_kernel,
        out_shape=jax.ShapeDtypeStruct((bsz, n2, n1, B_W), F32),
        grid=(bsz, n1 // k1b),
        in_specs=[
            pl.BlockSpec((1, 2, k1b, n2, B_W), lambda b, i: (b, 0, i, 0, 0)),
            _full((2 * k1b * n2, 2 * k1b * n2)),
            _full((B_W, B_W)), _full((B_W, B_W)), _full((B_W, B_W)),
        ],
        out_specs=pl.BlockSpec((1, n2, k1b, B_W), lambda b, i: (b, 0, i, 0)),
        compiler_params=_params("parallel", "parallel"), name="fft_pass2",
    )(t5, tabs["l2"], tabs["chan_cos"], tabs["chan_sin"], lp["fnet_w"])
    return y.reshape(bsz, n, B_W)


def _attn_kernel(q_ref, kt_ref, va_ref, o_ref, qm_ref, m_ref, acc_ref):
    tq = q_ref.shape[1]
    n = kt_ref.shape[3]
    tk = min(ATT_TK, n)
    q = q_ref[0]
    lane = lax.broadcasted_iota(jnp.int32, q.shape, 1)
    for h in range(C_GRP):
        qm_ref[h * tq:(h + 1) * tq, :] = jnp.where(lane // C_HD == h, q, jnp.zeros_like(q))
    m_ref[...] = jnp.full(m_ref.shape, -jnp.inf, F32)
    acc_ref[...] = jnp.zeros(acc_ref.shape, F32)

    def scores(c):
        c0 = pl.multiple_of(c * tk, tk)
        return jnp.dot(qm_ref[...], kt_ref[0, 0, :, pl.ds(c0, tk)], preferred_element_type=F32)

    def softmax_accumulate(c, s):
        c0 = pl.multiple_of(c * tk, tk)
        v = va_ref[0, 0, pl.ds(c0, tk), :]
        s_max = s[:, 0:LANES]
        for t in range(1, tk // LANES):
            s_max = jnp.maximum(s_max, s[:, t * LANES:(t + 1) * LANES])
        row_max = jnp.max(s_max, axis=-1, keepdims=True)
        m_old = m_ref[...]
        m_new = jnp.maximum(m_old, row_max)
        p = jnp.exp2(s - jnp.concatenate([m_new] * (tk // LANES), axis=-1))
        alpha = jnp.exp2(m_old - m_new)
        acc_ref[...] = alpha * acc_ref[...] + jnp.dot(p.astype(BF16), v, preferred_element_type=F32)
        m_ref[...] = m_new

    def step(c, carry):
        softmax_accumulate(c, scores(c))
        return carry

    lax.fori_loop(0, n // tk, step, 0, unroll=ATT_UNROLL)
    acc = acc_ref[...]
    inv = 1.0 / acc[:, ONES_LANE:ONES_LANE + 1]
    outs = [(acc[h * tq:(h + 1) * tq, 0:C_HD] * inv[h * tq:(h + 1) * tq]) for h in range(C_GRP)]
    o_ref[0] = jnp.concatenate(outs, axis=-1).astype(BF16)


def _attention(aq, kt4, va):
    bsz, n, _ = aq.shape
    tq = min(ATT_TQ, n)
    return pl.pallas_call(
        _attn_kernel,
        out_shape=jax.ShapeDtypeStruct((bsz, n, C_W), BF16),
        grid=(bsz, C_KV, n // tq),
        in_specs=[
            pl.BlockSpec((1, tq, C_GRP * C_HD), lambda b, g, i: (b, i, g)),
            pl.BlockSpec((1, 1, C_GRP * C_HD, n), lambda b, g, i: (b, g, 0, 0)),
            pl.BlockSpec((1, 1, n, LANES), lambda b, g, i: (b, g, 0, 0)),
        ],
        out_specs=pl.BlockSpec((1, tq, C_GRP * C_HD), lambda b, g, i: (b, i, g)),
        scratch_shapes=[
            pltpu.VMEM((C_GRP * tq, C_GRP * C_HD), BF16),
            pltpu.VMEM((C_GRP * tq, LANES), F32),
            pltpu.VMEM((C_GRP * tq, LANES), F32),
        ],
        compiler_params=_params("parallel", "parallel", "arbitrary"), name="attention",
    )(aq, kt4, va)


def _outproj_kernel(x_ref, mod_ref, oa_ref, of_ref, ob_ref, yb_ref, ao_ref, du_ref, dvn_ref, sz_ref,
                    w_ref, postg_ref, og_ref, bd_ref, sw_ref, sb_ref, o_ref):
    tm = x_ref.shape[1]
    sz = sz_ref[0].astype(F32)

    o = oa_ref[0].astype(F32) + of_ref[0].astype(F32) + ob_ref[0].astype(F32)
    oms = jnp.dot((o * o).astype(BF16), bd_ref[...], preferred_element_type=F32)
    out_a = o * lax.rsqrt(oms + EPS) * og_ref[...]
    y = jnp.dot((out_a * sz[:, 0:A_W]).astype(BF16), w_ref[0:A_W, :], preferred_element_type=F32)

    m_b = (yb_ref[0] * sz[:, A_W:A_W + B_W]).astype(BF16)
    y = y + jnp.dot(m_b, w_ref[A_W:A_W + B_W, :], preferred_element_type=F32)

    m_c = (ao_ref[0].astype(F32) * sz[:, A_W + B_W:A_W + B_W + C_W]).astype(BF16)
    y = y + jnp.dot(m_c, w_ref[A_W + B_W:A_W + B_W + C_W, :], preferred_element_type=F32)

    lane = lax.broadcasted_iota(jnp.int32, (SGU_CHUNK, D_W), 1)
    parts = []
    for c in range(tm // SGU_CHUNK):
        rows = slice(c * SGU_CHUNK, (c + 1) * SGU_CHUNK)
        r = jnp.dot(sw_ref[...], dvn_ref[0, rows, :], preferred_element_type=F32)
        mix = sb_ref[...]
        for g in range(D_GROUPS):
            mix = mix + jnp.where(lane // D_GC == g, r[g * SGU_CHUNK:(g + 1) * SGU_CHUNK, :], 0.0)
        parts.append(du_ref[0, rows, :].astype(F32) * mix)
    out_d = jnp.concatenate(parts, axis=0)
    m_d = (out_d * sz[:, A_W + B_W + C_W:]).astype(BF16)
    y = y + jnp.dot(m_d, w_ref[A_W + B_W + C_W:, :], preferred_element_type=F32)

    yms = jnp.mean(y * y, axis=-1, keepdims=True)
    yn = y * lax.rsqrt(yms + EPS) * postg_ref[...]
    gate = mod_ref[0, :, 2 * D_MODEL:3 * D_MODEL]
    o_ref[0] = x_ref[0] + gate * yn


def _outproj(x, mod, oa, of, ob, yb, ao, du, dvn, sz, lp):
    bsz, n, _ = x.shape
    tm = min(TM_OUT, n)
    tok = lambda w: pl.BlockSpec((1, tm, w), lambda b, i: (b, i, 0))
    return pl.pallas_call(
        _outproj_kernel,
        out_shape=jax.ShapeDtypeStruct((bsz, n, D_MODEL), F32),
        grid=(bsz, n // tm),
        in_specs=[
            tok(D_MODEL),
            pl.BlockSpec((1, 1, 3 * D_MODEL), lambda b, i: (b, 0, 0)),
            tok(A_W), tok(A_W), tok(A_W), tok(B_W), tok(C_W), tok(D_W), tok(D_W), tok(D_MIX),
            _full((D_MIX, D_MODEL)), _full((1, D_MODEL)), _full((1, A_W)), _full((A_W, A_W)),
            _full((D_GROUPS * SGU_CHUNK, SGU_CHUNK)), _full((SGU_CHUNK, D_W)),
        ],
        out_specs=tok(D_MODEL),
        compiler_params=_params("parallel", "parallel"), name="outproj",
    )(x, mod, oa, of, ob, yb, ao, du, dvn, sz,
      lp["w_out"], lp["post_g"], lp["og"], lp["bd_a"], lp["sgu_w"], lp["sgu_b"])


def _pack_w_in(w):
    sizes = [A_KW, A_KW, A_W, 2 * GLA_RANK, B_W, C_W, C_KVW, C_KVW, D_W, D_W, D_MIX]
    off = [0] + [int(o) for o in np.cumsum(sizes)]
    a_q, a_k, a_v, a_lr, b_u, c_q, c_k, c_v, d_u, d_v, z = [w[:, off[i]:off[i + 1]] for i in range(len(sizes))]

    def deinterleave(x, heads):
        r = x.reshape(D_MODEL, heads, C_HD // 2, 2)
        return jnp.concatenate([r[..., 0], r[..., 1]], axis=-1).reshape(D_MODEL, heads * C_HD)

    zeros = lambda width: jnp.zeros((D_MODEL, width), w.dtype)
    pieces = [a_q, a_k, a_v, b_u, deinterleave(c_q, C_HEADS), deinterleave(c_k, C_KV),
              a_lr, zeros(LANES - 2 * GLA_RANK)]
    for h in range(C_KV):
        pieces += [c_v[:, h * C_HD:(h + 1) * C_HD], zeros(LANES - C_HD)]
    pieces += [d_u, d_v, z]
    packed = jnp.concatenate(pieces, axis=1)
    assert packed.shape[1] == W_PACKED
    return packed.astype(BF16)


def _block_diag_mean(width, group):
    idx = np.arange(width) // group
    return jnp.asarray((idx[:, None] == idx[None, :]).astype(np.float32) / group, BF16)


def _layer_params(l, ada_unused, norm_pre_g, norm_post_g, w_in, gla_wg2_f, gla_bg_f, gla_wg2_b, gla_bg_b,
                  gla_onorm_g, fnet_w, q_norm_g, k_norm_g, sgu_norm_g, sgu_w, sgu_b, w_out):
    deint = np.concatenate([np.arange(0, C_HD, 2), np.arange(1, C_HD, 2)])
    w = _pack_w_in(w_in[l])
    wg2 = jnp.zeros((LANES, 2 * A_KW), F32)
    wg2 = wg2.at[0:GLA_RANK, 0:A_KW].set(gla_wg2_f[l])
    wg2 = wg2.at[GLA_RANK:2 * GLA_RANK, A_KW:].set(gla_wg2_b[l])
    vone = np.zeros((1, 2 * LANES), np.float32)
    vone[0, ONES_LANE] = 1.0
    vone[0, LANES + ONES_LANE] = 1.0
    return {
        "pre_g": norm_pre_g[l].reshape(1, D_MODEL),
        "post_g": norm_post_g[l].reshape(1, D_MODEL),
        "w_in": w,
        "wg2": wg2.astype(BF16),
        "bg": jnp.concatenate([gla_bg_f[l], gla_bg_b[l]]).reshape(1, 2 * A_KW),
        "qg": jnp.tile(q_norm_g[l][deint], C_HEADS).reshape(1, C_W),
        "kg": jnp.tile(k_norm_g[l][deint], C_KV).reshape(1, C_KVW),
        "sg": sgu_norm_g[l].reshape(1, D_W),
        "bd_q": _block_diag_mean(C_W, C_HD),
        "bd_a": _block_diag_mean(A_W, A_DV),
        "vone": jnp.asarray(vone),
        "og": jnp.tile(gla_onorm_g[l], A_HEADS).reshape(1, A_W),
        "fnet_w": fnet_w[l].astype(BF16),
        "sgu_w": sgu_w[l].reshape(D_GROUPS * SGU_CHUNK, SGU_CHUNK).astype(BF16),
        "sgu_b": jnp.repeat(sgu_b[l].T, D_GC, axis=1),
        "w_out": w_out[l].astype(BF16),
    }


def _seq_tables(n):
    rows = n // GRID_W
    row = jnp.repeat(jnp.arange(rows, dtype=F32), GRID_W)
    col = jnp.tile(jnp.arange(GRID_W, dtype=F32), rows)
    rope_axis = C_HD // 2
    freqs = ROPE_THETA ** (-jnp.arange(0, rope_axis, 2, dtype=F32) / rope_axis)
    ang = jnp.concatenate([row[:, None] * freqs, col[:, None] * freqs], axis=-1)
    cos, sin = jnp.cos(ang), jnp.sin(ang)
    cos_t = jnp.tile(jnp.concatenate([cos, cos], axis=-1), (1, LANES // C_HD))
    sin_t = jnp.tile(jnp.concatenate([-sin, sin], axis=-1), (1, LANES // C_HD))

    n2 = FFT_N2
    n1 = n // n2

    def dft_angles(rows, cols, period):
        prod = (jnp.arange(rows, dtype=jnp.int32)[:, None] * jnp.arange(cols, dtype=jnp.int32)[None, :]) % period
        return prod.astype(F32) * (2.0 * math.pi / period)

    a1_ang = dft_angles(n1, n1, n1)
    a1 = jnp.concatenate([jnp.cos(a1_ang), -jnp.sin(a1_ang)], axis=0)
    tw_ang = dft_angles(n2, n1, n)
    scale = 1.0 / math.sqrt(n * B_GC)
    tw_cos = jnp.broadcast_to((jnp.cos(tw_ang) * scale)[:, :, None], (n2, n1, LANES))
    tw_sin = jnp.broadcast_to((jnp.sin(tw_ang) * scale)[:, :, None], (n2, n1, LANES))
    ang2 = dft_angles(n2, n2, n2)
    c2, s2 = jnp.cos(ang2), jnp.sin(ang2)
    eye = jnp.eye(FFT_K1, dtype=F32)
    blk = lambda m: (m[:, None, None, :] * eye[None, :, :, None]).reshape(n2 * FFT_K1, FFT_K1 * n2)
    l2 = jnp.block([[blk(c2), blk(s2)], [blk(-s2), blk(c2)]])
    angc = dft_angles(B_GC, B_GC, B_GC)
    grp = jnp.eye(B_GROUPS, dtype=F32)
    kron = lambda m: (grp[:, None, :, None] * m[None, :, None, :]).reshape(B_W, B_W)
    chan_cos = kron(jnp.cos(angc))
    chan_sin = kron(jnp.sin(angc))
    return {
        "cos": cos_t, "sin": sin_t,
        "a1": a1.astype(BF16),
        "tw_cos": tw_cos, "tw_sin": tw_sin,
        "l2": l2.astype(BF16),
        "chan_cos": chan_cos.astype(BF16), "chan_sin": chan_sin.astype(BF16),
    }


def _layer(x, mod, lp, tabs):
    gq, gk, gv, gf, gb, fu, aq, kt4, va, du, dvn, sz = _inproj(x, mod, lp, tabs)
    oa = _gla_band(gq, gk, gv, gf, gb)
    of, ob = _gla_rec(gq, gk, gv, gf, gb)
    yb = _fnet(fu, lp, tabs)
    ao = _attention(aq, kt4, va)
    return _outproj(x, mod, oa, of, ob, yb, ao, du, dvn, sz, lp)


def kernel(x_prompt, x_sample, c_prompt, c_sample, ada_w, ada_b, norm_pre_g, norm_post_g, w_in,
           gla_wg2_f, gla_bg_f, gla_wg2_b, gla_bg_b, gla_onorm_g, fnet_w, q_norm_g, k_norm_g,
           sgu_norm_g, sgu_w, sgu_b, w_out):
    bp, bs = x_prompt.shape[0], x_sample.shape[0]
    pad_rows = (-(bp + bs)) % 8
    c_all = jnp.concatenate([c_prompt, c_sample, jnp.zeros((pad_rows, D_MODEL), F32)], axis=0)
    mod = _adaln(c_all, ada_w, ada_b)
    tabs_p = _seq_tables(x_prompt.shape[1])
    tabs_s = _seq_tables(x_sample.shape[1])
    y_prompt, y_sample = x_prompt, x_sample
    for l in range(DEPTH):
        lp = _layer_params(l, None, norm_pre_g, norm_post_g, w_in, gla_wg2_f, gla_bg_f, gla_wg2_b,
                           gla_bg_b, gla_onorm_g, fnet_w, q_norm_g, k_norm_g, sgu_norm_g, sgu_w,
                           sgu_b, w_out)
        mod_p = mod[l, 0:bp].reshape(bp, 1, 3 * D_MODEL)
        mod_s = mod[l, bp:bp + bs].reshape(bs, 1, 3 * D_MODEL)
        y_prompt = _layer(y_prompt, mod_p, lp, tabs_p)
        y_sample = _layer(y_sample, mod_s, lp, tabs_s)
    return (y_prompt, y_sample)
```

```python
import math

import numpy as np
import jax
import jax.numpy as jnp
from jax import lax
from jax.experimental import pallas as pl
from jax.experimental.pallas import tpu as pltpu

F32 = jnp.float32
BF16 = jnp.bfloat16
HIGHEST = lax.Precision.HIGHEST

D_MODEL = 1024
DEPTH = 2
GRID_W = 64
A_HEADS, A_DK, A_DV = 4, 32, 64
A_W, A_KW = A_HEADS * A_DV, A_HEADS * A_DK
GLA_RANK = 16
GLA_GATE_NORM = 16.0
B_GROUPS, B_GC = 4, 64
B_W = B_GROUPS * B_GC
C_HEADS, C_KV, C_HD = 8, 2, 64
C_GRP = C_HEADS // C_KV
C_W, C_KVW = C_HEADS * C_HD, C_KV * C_HD
ROPE_THETA = 10000.0
D_GROUPS, D_GC = 4, 64
D_W = D_GROUPS * D_GC
SGU_CHUNK = 128
D_MIX = A_W + B_W + C_W + D_W
EPS = 1e-6
LOG2E = 1.4426950408889634

LANES = 128
VMEM_LIMIT_BYTES = 56 * 1024 * 1024

GQ, GK, GV, FU, AQ, AK, LR, AV, DU, DV, ZZ = 0, 128, 256, 512, 768, 1280, 1408, 1536, 1792, 2048, 2304
W_PACKED = 3584
AV_W = 2 * LANES
ONES_LANE = C_HD

TM_IN = 512
TM_OUT = 512
GLA_SUB = 16
GLA_BAND_TILE = 64
GLA_GRP = 128
GLA_SEG = 512
GLA_LOOKAHEAD = 3
FFT_N2 = 64
FFT_NB = 8
FFT_K1 = 8
FFT_K1_GROUPS = 2
ATT_TQ = 512
ATT_TK = 2048
ATT_UNROLL = 2


def _params(*sem):
    return pltpu.CompilerParams(dimension_semantics=sem, vmem_limit_bytes=VMEM_LIMIT_BYTES)


def _full(shape):
    n = len(shape)
    return pl.BlockSpec(shape, lambda *_: (0,) * n)


def _adaln_kernel(c_ref, w_ref, b_ref, o_ref):
    c = c_ref[...]
    sc = c / (1.0 + jnp.exp(-c))
    o_ref[0] = jnp.dot(sc, w_ref[0], precision=HIGHEST, preferred_element_type=F32) + b_ref[0]


def _adaln(c_all, ada_w, ada_b):
    rows = c_all.shape[0]
    ncol = 3 * D_MODEL // D_MODEL
    return pl.pallas_call(
        _adaln_kernel,
        out_shape=jax.ShapeDtypeStruct((DEPTH, rows, 3 * D_MODEL), F32),
        grid=(DEPTH, ncol),
        in_specs=[
            pl.BlockSpec((rows, D_MODEL), lambda l, j: (0, 0)),
            pl.BlockSpec((1, D_MODEL, D_MODEL), lambda l, j: (l, 0, j)),
            pl.BlockSpec((1, 1, D_MODEL), lambda l, j: (l, 0, j)),
        ],
        out_specs=pl.BlockSpec((1, rows, D_MODEL), lambda l, j: (l, 0, j)),
        compiler_params=_params("arbitrary", "arbitrary"),
        name="adaln",
    )(c_all, ada_w, ada_b.reshape(DEPTH, 1, 3 * D_MODEL))


def _rope(x, cos, sin_signed):
    width = x.shape[-1]
    lane = lax.broadcasted_iota(jnp.int32, x.shape, 1)
    first_half = (lane % C_HD) < (C_HD // 2)
    swapped = jnp.where(first_half, pltpu.roll(x, width - C_HD // 2, 1), pltpu.roll(x, C_HD // 2, 1))
    reps = width // LANES
    cos_w = jnp.concatenate([cos] * reps, axis=-1) if reps > 1 else cos
    sin_w = jnp.concatenate([sin_signed] * reps, axis=-1) if reps > 1 else sin_signed
    return x * cos_w + swapped * sin_w


def _inproj_kernel(x_ref, mod_ref, preg_ref, w_ref, wg2_ref, bg_ref, qg_ref, kg_ref, sg_ref,
                   cos_ref, sin_ref, bd_ref, vone_ref,
                   gq_ref, gk_ref, gv_ref, gf_ref, gb_ref, fu_ref, aq_ref, kt_ref, va_ref,
                   du_ref, dvn_ref, sz_ref):
    x = x_ref[0]
    ms = jnp.mean(x * x, axis=-1, keepdims=True)
    shift = mod_ref[0, :, 0:D_MODEL]
    scale = mod_ref[0, :, D_MODEL:2 * D_MODEL]
    h = (x * lax.rsqrt(ms + EPS)) * preg_ref[...] * (1.0 + scale) + shift
    hb = h.astype(BF16)

    def seg(off, width):
        return jnp.dot(hb, w_ref[:, off:off + width], preferred_element_type=F32)

    klr = seg(AK, C_KVW + LANES)
    q = seg(AQ, C_W)
    z = seg(ZZ, D_MIX)
    gqk = seg(GQ, 2 * A_KW)
    gv = seg(GV, A_W)
    fu = seg(FU, B_W)
    va = seg(AV, AV_W)
    du = seg(DU, D_W)
    dv = seg(DV, D_W)
    k = klr[:, :C_KVW]
    lr = klr[:, C_KVW:].astype(BF16)
    logits = jnp.dot(lr, wg2_ref[...], preferred_element_type=F32) + bg_ref[...]
    qms = jnp.dot((q * q).astype(BF16), bd_ref[...], preferred_element_type=F32)
    kms = jnp.dot((k * k).astype(BF16), bd_ref[0:C_KVW, 0:C_KVW], preferred_element_type=F32)

    gq_ref[0] = gqk[:, :A_KW] * (A_DK ** -0.5)
    gk_ref[0] = gqk[:, A_KW:]
    gv_ref[0] = gv
    logg = (jnp.minimum(logits, 0.0) - jnp.log1p(jnp.exp(-jnp.abs(logits)))) * (1.0 / GLA_GATE_NORM)
    gf_ref[0] = logg[:, :A_KW]
    gb_ref[0] = logg[:, A_KW:]

    fu_ref[0] = fu.astype(BF16)

    cos = cos_ref[...]
    sin = sin_ref[...]
    qn = q * lax.rsqrt(qms + EPS) * qg_ref[...]
    aq_ref[0] = (_rope(qn, cos, sin) * (C_HD ** -0.5 * LOG2E)).astype(BF16)
    kn = k * lax.rsqrt(kms + EPS) * kg_ref[...]
    kt = _rope(kn, cos, sin).T.astype(BF16)
    for g in range(C_KV):
        for r in range(C_GRP):
            kt_ref[0, g, r * C_HD:(r + 1) * C_HD, :] = kt[g * C_HD:(g + 1) * C_HD, :]
    vab = (va + vone_ref[...]).astype(BF16)
    va_ref[0, 0] = vab[:, :LANES]
    va_ref[0, 1] = vab[:, LANES:]

    du_ref[0] = du.astype(BF16)
    dms = jnp.mean(dv * dv, axis=-1, keepdims=True)
    dvn_ref[0] = (dv * lax.rsqrt(dms + EPS) * sg_ref[...]).astype(BF16)

    sz_ref[0] = (z / (1.0 + jnp.exp(-z))).astype(BF16)


def _inproj(x, mod, lp, tabs):
    bsz, n, _ = x.shape
    tm = min(TM_IN, n)
    grid = (bsz, n // tm)
    tok = lambda w: pl.BlockSpec((1, tm, w), lambda b, i: (b, i, 0))
    out_shapes = (
        jax.ShapeDtypeStruct((bsz, n, A_KW), F32),
        jax.ShapeDtypeStruct((bsz, n, A_KW), F32),
        jax.ShapeDtypeStruct((bsz, n, A_W), F32),
        jax.ShapeDtypeStruct((bsz, n, A_KW), F32),
        jax.ShapeDtypeStruct((bsz, n, A_KW), F32),
        jax.ShapeDtypeStruct((bsz, n, B_W), BF16),
        jax.ShapeDtypeStruct((bsz, n, C_W), BF16),
        jax.ShapeDtypeStruct((bsz, C_KV, C_GRP * C_HD, n), BF16),
        jax.ShapeDtypeStruct((bsz, C_KV, n, LANES), BF16),
        jax.ShapeDtypeStruct((bsz, n, D_W), BF16),
        jax.ShapeDtypeStruct((bsz, n, D_W), BF16),
        jax.ShapeDtypeStruct((bsz, n, D_MIX), BF16),
    )
    out_specs = (
        tok(A_KW), tok(A_KW), tok(A_W), tok(A_KW), tok(A_KW), tok(B_W), tok(C_W),
        pl.BlockSpec((1, C_KV, C_GRP * C_HD, tm), lambda b, i: (b, 0, 0, i)),
        pl.BlockSpec((1, C_KV, tm, LANES), lambda b, i: (b, 0, i, 0)),
        tok(D_W), tok(D_W), tok(D_MIX),
    )
    in_specs = [
        tok(D_MODEL),
        pl.BlockSpec((1, 1, 3 * D_MODEL), lambda b, i: (b, 0, 0)),
        _full((1, D_MODEL)),
        _full((D_MODEL, W_PACKED)),
        _full((LANES, 2 * A_KW)),
        _full((1, 2 * A_KW)),
        _full((1, C_W)),
        _full((1, C_KVW)),
        _full((1, D_W)),
        pl.BlockSpec((tm, LANES), lambda b, i: (i, 0)),
        pl.BlockSpec((tm, LANES), lambda b, i: (i, 0)),
        _full((C_W, C_W)),
        _full((1, 2 * LANES)),
    ]
    return pl.pallas_call(
        _inproj_kernel, out_shape=out_shapes, grid=grid, in_specs=in_specs, out_specs=out_specs,
        compiler_params=_params("parallel", "parallel"), name="inproj",
    )(x, mod, lp["pre_g"], lp["w_in"], lp["wg2"], lp["bg"], lp["qg"], lp["kg"], lp["sg"],
      tabs["cos"], tabs["sin"], lp["bd_q"], lp["vone"])


def _head_match(shape, rows_per_head, cols_per_head):
    return (lax.broadcasted_iota(jnp.int32, shape, 0) // rows_per_head
            == lax.broadcasted_iota(jnp.int32, shape, 1) // cols_per_head)


def _gla_band_kernel(q_ref, k_ref, v_ref, gf_ref, gb_ref, o_ref,
                     qf_ref, kf_ref, vf_ref, bf_ref, cb_ref, vs_ref, os_ref):
    rows = q_ref.shape[1]
    ts = rows // GLA_SUB
    n_slab = A_W // LANES
    every = lambda p: pl.ds(p, ts, stride=GLA_SUB)
    hsum = jnp.where(_head_match((A_KW, A_W), A_DK, A_DV), 1.0, 0.0).astype(BF16)
    for h in range(n_slab):
        vs_ref[h] = v_ref[0, :, h * LANES:(h + 1) * LANES]
    run = jnp.zeros((ts, A_KW), F32)
    for p in range(GLA_SUB):
        run = run + gf_ref[0, every(p), :] * LOG2E
        bf_ref[p] = run
        qf_ref[p] = q_ref[0, every(p), :]
        kf_ref[p] = k_ref[0, every(p), :]
        vf_ref[p] = jnp.concatenate([vs_ref[h, every(p), :] for h in range(n_slab)], axis=-1)
    run = jnp.zeros((ts, A_KW), F32)
    for p in reversed(range(GLA_SUB)):
        run = run + gb_ref[0, every(p), :] * LOG2E
        cb_ref[p] = run
    for p in range(GLA_SUB):
        q = qf_ref[p]
        prods = []
        for s in range(GLA_SUB):
            qk = q * kf_ref[s]
            if s < p:
                qk = qk * jnp.exp2(bf_ref[p] - bf_ref[s])
            elif s > p:
                qk = qk * jnp.exp2(cb_ref[p] - cb_ref[s])
            prods.append(qk.astype(BF16))
        a = jnp.dot(jnp.concatenate(prods, axis=0), hsum, preferred_element_type=F32)
        acc = a[0:ts] * vf_ref[0]
        for s in range(1, GLA_SUB):
            acc = acc + a[s * ts:(s + 1) * ts] * vf_ref[s]
        for h in range(n_slab):
            os_ref[h, every(p), :] = acc[:, h * LANES:(h + 1) * LANES]
    o_ref[0] = jnp.concatenate([os_ref[h] for h in range(n_slab)], axis=-1).astype(BF16)


def _gla_band(gq, gk, gv, gf, gb):
    bsz, n, _ = gq.shape
    rows = min(GLA_BAND_TILE * GLA_SUB, n)
    ts = rows // GLA_SUB
    blk = lambda w: pl.BlockSpec((1, rows, w), lambda b, i: (b, i, 0))
    return pl.pallas_call(
        _gla_band_kernel,
        out_shape=jax.ShapeDtypeStruct((bsz, n, A_W), BF16),
        grid=(bsz, n // rows),
        in_specs=[blk(A_KW), blk(A_KW), blk(A_W), blk(A_KW), blk(A_KW)],
        out_specs=blk(A_W),
        scratch_shapes=[
            pltpu.VMEM((GLA_SUB, ts, A_KW), F32), pltpu.VMEM((GLA_SUB, ts, A_KW), F32),
            pltpu.VMEM((GLA_SUB, ts, A_W), F32),
            pltpu.VMEM((GLA_SUB, ts, A_KW), F32), pltpu.VMEM((GLA_SUB, ts, A_KW), F32),
            pltpu.VMEM((A_W // LANES, rows, LANES), F32), pltpu.VMEM((A_W // LANES, rows, LANES), F32),
        ],
        compiler_params=_params("parallel", "parallel"), name="gla_band",
    )(gq, gk, gv, gf, gb)


def _split3(x):
    hi = x.astype(BF16)
    rest = x - hi.astype(F32)
    mid = rest.astype(BF16)
    lo = (rest - mid.astype(F32)).astype(BF16)
    return hi, mid, lo


def _gla_prep(q_ref, k_ref, g_ref, qt_ref, kh_ref, dect_ref, tot_ref, pad_ref, reverse):
    seg = q_ref.shape[1]
    n_sub = seg // GLA_SUB
    r_i = lax.broadcasted_iota(jnp.int32, (GLA_GRP, GLA_GRP), 0)
    c_i = lax.broadcasted_iota(jnp.int32, (GLA_GRP, GLA_GRP), 1)
    same = (r_i // GLA_SUB) == (c_i // GLA_SUB)
    tri = (c_i >= r_i) if reverse else (c_i <= r_i)
    sums_mat = jnp.concatenate([jnp.where(same & tri, 1.0, 0.0), jnp.where(same, 1.0, 0.0)], axis=0).astype(BF16)
    for r in range(seg // GLA_GRP):
        rows = slice(r * GLA_GRP, (r + 1) * GLA_GRP)
        g3 = jnp.concatenate(_split3(g_ref[0, rows, :]), axis=-1)
        sums = jnp.dot(sums_mat, g3, preferred_element_type=F32)
        sums = sums[:, 0:A_KW] + sums[:, A_KW:2 * A_KW] + sums[:, 2 * A_KW:]
        cum, tot = sums[:GLA_GRP], sums[GLA_GRP:]
        qt_ref[rows, :] = (q_ref[0, rows, :] * jnp.exp(cum)).astype(BF16)
        kh_ref[rows, :] = (k_ref[0, rows, :] * jnp.exp(tot - cum)).astype(BF16)
        tot_ref[rows, :] = tot
    pad_ref[...] = jnp.zeros_like(pad_ref)
    pad_ref[0:n_sub, :] = tot_ref[pl.ds(0, n_sub, stride=GLA_SUB), :]
    dect_ref[...] = jnp.exp(pad_ref[...].T)


def _gla_rec_kernel(qf_ref, kf_ref, vf_ref, gf_ref, qb_ref, kb_ref, vb_ref, gb_ref, of_ref, ob_ref,
                    sf_ref, sb_ref, qtf_ref, khf_ref, dtf_ref, qtb_ref, khb_ref, dtb_ref, tot_ref, pad_ref):
    @pl.when(pl.program_id(1) == 0)
    def _():
        sf_ref[...] = jnp.zeros_like(sf_ref)
        sb_ref[...] = jnp.zeros_like(sb_ref)

    seg = qf_ref.shape[1]
    n_sub = seg // GLA_SUB
    _gla_prep(qf_ref, kf_ref, gf_ref, qtf_ref, khf_ref, dtf_ref, tot_ref, pad_ref, reverse=False)
    _gla_prep(qb_ref, kb_ref, gb_ref, qtb_ref, khb_ref, dtb_ref, tot_ref, pad_ref, reverse=True)
    smask = _head_match((A_KW, A_W), A_DK, A_DV)

    def update(i, kh_ref, v_ref):
        rows = slice(i * GLA_SUB, (i + 1) * GLA_SUB)
        upd = lax.dot_general(kh_ref[rows, :], v_ref[0, rows, :].astype(BF16), (((0,), (0,)), ((), ())),
                              preferred_element_type=F32)
        return jnp.where(smask, upd, 0.0)

    def step(i, s, upd, qt_ref, dect_ref, o_ref):
        rows = slice(i * GLA_SUB, (i + 1) * GLA_SUB)
        o_ref[0, rows, :] = jnp.dot(qt_ref[rows, :], s.astype(BF16), preferred_element_type=F32).astype(BF16)
        return s * dect_ref[:, i:i + 1] + upd

    order_f = list(range(n_sub))
    order_b = list(reversed(range(n_sub)))
    upd_f = [update(i, khf_ref, vf_ref) for i in order_f[:GLA_LOOKAHEAD]]
    upd_b = [update(i, khb_ref, vb_ref) for i in order_b[:GLA_LOOKAHEAD]]
    s_f = sf_ref[...]
    s_b = sb_ref[...]
    for j in range(n_sub):
        if j + GLA_LOOKAHEAD < n_sub:
            upd_f.append(update(order_f[j + GLA_LOOKAHEAD], khf_ref, vf_ref))
            upd_b.append(update(order_b[j + GLA_LOOKAHEAD], khb_ref, vb_ref))
        s_f = step(order_f[j], s_f, upd_f[j], qtf_ref, dtf_ref, of_ref)
        s_b = step(order_b[j], s_b, upd_b[j], qtb_ref, dtb_ref, ob_ref)
    sf_ref[...] = s_f
    sb_ref[...] = s_b


def _gla_rec(gq, gk, gv, gf, gb):
    bsz, n, _ = gq.shape
    seg = min(GLA_SEG, n)
    nseg = n // seg
    assert seg // GLA_SUB <= LANES
    fwd = lambda w: pl.BlockSpec((1, seg, w), lambda b, s: (b, s, 0))
    bwd = lambda w: pl.BlockSpec((1, seg, w), lambda b, s: (b, nseg - 1 - s, 0))
    return pl.pallas_call(
        _gla_rec_kernel,
        out_shape=(jax.ShapeDtypeStruct((bsz, n, A_W), BF16), jax.ShapeDtypeStruct((bsz, n, A_W), BF16)),
        grid=(bsz, nseg),
        in_specs=[fwd(A_KW), fwd(A_KW), fwd(A_W), fwd(A_KW), bwd(A_KW), bwd(A_KW), bwd(A_W), bwd(A_KW)],
        out_specs=(fwd(A_W), bwd(A_W)),
        scratch_shapes=[
            pltpu.VMEM((A_KW, A_W), F32), pltpu.VMEM((A_KW, A_W), F32),
            pltpu.VMEM((seg, A_KW), BF16), pltpu.VMEM((seg, A_KW), BF16), pltpu.VMEM((A_KW, LANES), F32),
            pltpu.VMEM((seg, A_KW), BF16), pltpu.VMEM((seg, A_KW), BF16), pltpu.VMEM((A_KW, LANES), F32),
            pltpu.VMEM((seg, A_KW), F32), pltpu.VMEM((LANES, A_KW), F32),
        ],
        compiler_params=_params("parallel", "arbitrary"), name="gla_rec",
    )(gq, gk, gv, gf, gq, gk, gv, gb)


def _fft1_kernel(u_ref, a1_ref, cos_ref, sin_ref, t_ref):
    n1 = u_ref.shape[1]
    nb = cos_ref.shape[0]
    t = jnp.dot(a1_ref[...], u_ref[0], preferred_element_type=F32)
    for j in range(nb):
        cols = slice(j * B_W, (j + 1) * B_W)
        tr = t[:n1, cols]
        ti = t[n1:, cols]
        c = jnp.concatenate([cos_ref[j]] * (B_W // LANES), axis=-1)
        s = jnp.concatenate([sin_ref[j]] * (B_W // LANES), axis=-1)
        t_ref[0, 0, :, cols] = (tr * c + ti * s).astype(BF16)
        t_ref[0, 1, :, cols] = (ti * c - tr * s).astype(BF16)


def _fft2_kernel(t_ref, l_ref, cc_ref, cs_ref, w_ref, o_ref):
    n2 = t_ref.shape[3]
    half = FFT_K1 * n2
    outs = []
    for grp in range(t_ref.shape[2] // FFT_K1):
        k1s = slice(grp * FFT_K1, (grp + 1) * FFT_K1)
        x = jnp.concatenate([t_ref[0, 0, k1s].reshape(half, B_W), t_ref[0, 1, k1s].reshape(half, B_W)],
                            axis=0)
        g = jnp.dot(l_ref[...], x, preferred_element_type=F32)
        gr = g[:half].astype(BF16)
        gi = g[half:].astype(BF16)
        mixed = (jnp.dot(gr, cc_ref[...], preferred_element_type=F32)
                 + jnp.dot(gi, cs_ref[...], preferred_element_type=F32))
        y = jnp.dot(mixed.astype(BF16), w_ref[...], preferred_element_type=F32)
        outs.append(y.reshape(n2, FFT_K1, B_W))
    o_ref[0] = jnp.concatenate(outs, axis=1).astype(BF16)


def _fnet(fu, lp, tabs):
    bsz, n, _ = fu.shape
    n2 = FFT_N2
    n1 = n // n2
    nb = FFT_NB
    k1b = FFT_K1 * FFT_K1_GROUPS
    u2 = fu.reshape(bsz, n1, n2 * B_W)
    tt = pl.pallas_call(
        _fft1_kernel,
        out_shape=jax.ShapeDtypeStruct((bsz, 2, n1, n2 * B_W), BF16),
        grid=(n2 // nb, bsz),
        in_specs=[
            pl.BlockSpec((1, n1, nb * B_W), lambda j, b: (b, 0, j)),
            _full((2 * n1, n1)),
            pl.BlockSpec((nb, n1, LANES), lambda j, b: (j, 0, 0)),
            pl.BlockSpec((nb, n1, LANES), lambda j, b: (j, 0, 0)),
        ],
        out_specs=pl.BlockSpec((1, 2, n1, nb * B_W), lambda j, b: (b, 0, 0, j)),
        compiler_params=_params("parallel", "parallel"), name="fft_pass1",
    )(u2, tabs["a1"], tabs["tw_cos"], tabs["tw_sin"])
    t5 = tt.reshape(bsz, 2, n1, n2, B_W)
    y = pl.pallas_call(
        _fft2_kernel,
        out_shape=jax.ShapeDtypeStruct((bsz, n2, n1, B_W), BF16),
        grid=(bsz, n1 // k1b),
        in_specs=[
            pl.BlockSpec((1, 2, k1b, n2, B_W), lambda b, i: (b, 0, i, 0, 0)),
            _full((2 * FFT_K1 * n2, 2 * FFT_K1 * n2)),
            _full((B_W, B_W)), _full((B_W, B_W)), _full((B_W, B_W)),
        ],
        out_specs=pl.BlockSpec((1, n2, k1b, B_W), lambda b, i: (b, 0, i, 0)),
        compiler_params=_params("parallel", "parallel"), name="fft_pass2",
    )(t5, tabs["l2"], tabs["chan_cos"], tabs["chan_sin"], lp["fnet_w"])
    return y.reshape(bsz, n, B_W)


def _attn_kernel(q_ref, kt_ref, va_ref, o_ref, qm_ref, m_ref, acc_ref):
    tq = q_ref.shape[1]
    n = kt_ref.shape[3]
    tk = min(ATT_TK, n)
    q = q_ref[0]
    lane = lax.broadcasted_iota(jnp.int32, q.shape, 1)
    for h in range(C_GRP):
        qm_ref[h * tq:(h + 1) * tq, :] = jnp.where(lane // C_HD == h, q, jnp.zeros_like(q))
    m_ref[...] = jnp.full(m_ref.shape, -jnp.inf, F32)
    acc_ref[...] = jnp.zeros(acc_ref.shape, F32)

    def scores(c):
        c0 = pl.multiple_of(c * tk, tk)
        return jnp.dot(qm_ref[...], kt_ref[0, 0, :, pl.ds(c0, tk)], preferred_element_type=F32)

    def softmax_accumulate(c, s):
        c0 = pl.multiple_of(c * tk, tk)
        v = va_ref[0, 0, pl.ds(c0, tk), :]
        s_max = s[:, 0:LANES]
        for t in range(1, tk // LANES):
            s_max = jnp.maximum(s_max, s[:, t * LANES:(t + 1) * LANES])
        row_max = jnp.max(s_max, axis=-1, keepdims=True)
        m_old = m_ref[...]
        m_new = jnp.maximum(m_old, row_max)
        p = jnp.exp2(s - jnp.concatenate([m_new] * (tk // LANES), axis=-1))
        alpha = jnp.exp2(m_old - m_new)
        acc_ref[...] = alpha * acc_ref[...] + jnp.dot(p.astype(BF16), v, preferred_element_type=F32)
        m_ref[...] = m_new

    def step(c, carry):
        softmax_accumulate(c, scores(c))
        return carry

    lax.fori_loop(0, n // tk, step, 0, unroll=ATT_UNROLL)
    acc = acc_ref[...]
    inv = 1.0 / acc[:, ONES_LANE:ONES_LANE + 1]
    outs = [(acc[h * tq:(h + 1) * tq, 0:C_HD] * inv[h * tq:(h + 1) * tq]) for h in range(C_GRP)]
    o_ref[0] = jnp.concatenate(outs, axis=-1).astype(BF16)


def _attention(aq, kt4, va):
    bsz, n, _ = aq.shape
    tq = min(ATT_TQ, n)
    return pl.pallas_call(
        _attn_kernel,
        out_shape=jax.ShapeDtypeStruct((bsz, n, C_W), BF16),
        grid=(bsz, C_KV, n // tq),
        in_specs=[
            pl.BlockSpec((1, tq, C_GRP * C_HD), lambda b, g, i: (b, i, g)),
            pl.BlockSpec((1, 1, C_GRP * C_HD, n), lambda b, g, i: (b, g, 0, 0)),
            pl.BlockSpec((1, 1, n, LANES), lambda b, g, i: (b, g, 0, 0)),
        ],
        out_specs=pl.BlockSpec((1, tq, C_GRP * C_HD), lambda b, g, i: (b, i, g)),
        scratch_shapes=[
            pltpu.VMEM((C_GRP * tq, C_GRP * C_HD), BF16),
            pltpu.VMEM((C_GRP * tq, LANES), F32),
            pltpu.VMEM((C_GRP * tq, LANES), F32),
        ],
        compiler_params=_params("parallel", "parallel", "arbitrary"), name="attention",
    )(aq, kt4, va)


def _outproj_kernel(x_ref, mod_ref, oa_ref, of_ref, ob_ref, yb_ref, ao_ref, du_ref, dvn_ref, sz_ref,
                    w_ref, postg_ref, og_ref, bd_ref, sw_ref, sb_ref, o_ref):
    tm = x_ref.shape[1]
    sz = sz_ref[0].astype(F32)

    o = oa_ref[0].astype(F32) + of_ref[0].astype(F32) + ob_ref[0].astype(F32)
    oms = jnp.dot((o * o).astype(BF16), bd_ref[...], preferred_element_type=F32)
    out_a = o * lax.rsqrt(oms + EPS) * og_ref[...]
    y = jnp.dot((out_a * sz[:, 0:A_W]).astype(BF16), w_ref[0:A_W, :], preferred_element_type=F32)

    m_b = (yb_ref[0].astype(F32) * sz[:, A_W:A_W + B_W]).astype(BF16)
    y = y + jnp.dot(m_b, w_ref[A_W:A_W + B_W, :], preferred_element_type=F32)

    m_c = (ao_ref[0].astype(F32) * sz[:, A_W + B_W:A_W + B_W + C_W]).astype(BF16)
    y = y + jnp.dot(m_c, w_ref[A_W + B_W:A_W + B_W + C_W, :], preferred_element_type=F32)

    lane = lax.broadcasted_iota(jnp.int32, (SGU_CHUNK, D_W), 1)
    parts = []
    for c in range(tm // SGU_CHUNK):
        rows = slice(c * SGU_CHUNK, (c + 1) * SGU_CHUNK)
        r = jnp.dot(sw_ref[...], dvn_ref[0, rows, :], preferred_element_type=F32)
        mix = sb_ref[...]
        for g in range(D_GROUPS):
            mix = mix + jnp.where(lane // D_GC == g, r[g * SGU_CHUNK:(g + 1) * SGU_CHUNK, :], 0.0)
        parts.append(du_ref[0, rows, :].astype(F32) * mix)
    out_d = jnp.concatenate(parts, axis=0)
    m_d = (out_d * sz[:, A_W + B_W + C_W:]).astype(BF16)
    y = y + jnp.dot(m_d, w_ref[A_W + B_W + C_W:, :], preferred_element_type=F32)

    yms = jnp.mean(y * y, axis=-1, keepdims=True)
    yn = y * lax.rsqrt(yms + EPS) * postg_ref[...]
    gate = mod_ref[0, :, 2 * D_MODEL:3 * D_MODEL]
    o_ref[0] = x_ref[0] + gate * yn


def _outproj(x, mod, oa, of, ob, yb, ao, du, dvn, sz, lp):
    bsz, n, _ = x.shape
    tm = min(TM_OUT, n)
    tok = lambda w: pl.BlockSpec((1, tm, w), lambda b, i: (b, i, 0))
    return pl.pallas_call(
        _outproj_kernel,
        out_shape=jax.ShapeDtypeStruct((bsz, n, D_MODEL), F32),
        grid=(bsz, n // tm),
        in_specs=[
            tok(D_MODEL),
            pl.BlockSpec((1, 1, 3 * D_MODEL), lambda b, i: (b, 0, 0)),
            tok(A_W), tok(A_W), tok(A_W), tok(B_W), tok(C_W), tok(D_W), tok(D_W), tok(D_MIX),
            _full((D_MIX, D_MODEL)), _full((1, D_MODEL)), _full((1, A_W)), _full((A_W, A_W)),
            _full((D_GROUPS * SGU_CHUNK, SGU_CHUNK)), _full((SGU_CHUNK, D_W)),
        ],
        out_specs=tok(D_MODEL),
        compiler_params=_params("parallel", "parallel"), name="outproj",
    )(x, mod, oa, of, ob, yb, ao, du, dvn, sz,
      lp["w_out"], lp["post_g"], lp["og"], lp["bd_a"], lp["sgu_w"], lp["sgu_b"])


def _pack_w_in(w):
    sizes = [A_KW, A_KW, A_W, 2 * GLA_RANK, B_W, C_W, C_KVW, C_KVW, D_W, D_W, D_MIX]
    off = [0] + [int(o) for o in np.cumsum(sizes)]
    a_q, a_k, a_v, a_lr, b_u, c_q, c_k, c_v, d_u, d_v, z = [w[:, off[i]:off[i + 1]] for i in range(len(sizes))]

    def deinterleave(x, heads):
        r = x.reshape(D_MODEL, heads, C_HD // 2, 2)
        return jnp.concatenate([r[..., 0], r[..., 1]], axis=-1).reshape(D_MODEL, heads * C_HD)

    zeros = lambda width: jnp.zeros((D_MODEL, width), w.dtype)
    pieces = [a_q, a_k, a_v, b_u, deinterleave(c_q, C_HEADS), deinterleave(c_k, C_KV),
              a_lr, zeros(LANES - 2 * GLA_RANK)]
    for h in range(C_KV):
        pieces += [c_v[:, h * C_HD:(h + 1) * C_HD], zeros(LANES - C_HD)]
    pieces += [d_u, d_v, z]
    packed = jnp.concatenate(pieces, axis=1)
    assert packed.shape[1] == W_PACKED
    return packed.astype(BF16)


def _block_diag_mean(width, group):
    idx = np.arange(width) // group
    return jnp.asarray((idx[:, None] == idx[None, :]).astype(np.float32) / group, BF16)


def _layer_params(l, ada_unused, norm_pre_g, norm_post_g, w_in, gla_wg2_f, gla_bg_f, gla_wg2_b, gla_bg_b,
                  gla_onorm_g, fnet_w, q_norm_g, k_norm_g, sgu_norm_g, sgu_w, sgu_b, w_out):
    deint = np.concatenate([np.arange(0, C_HD, 2), np.arange(1, C_HD, 2)])
    w = _pack_w_in(w_in[l])
    wg2 = jnp.zeros((LANES, 2 * A_KW), F32)
    wg2 = wg2.at[0:GLA_RANK, 0:A_KW].set(gla_wg2_f[l])
    wg2 = wg2.at[GLA_RANK:2 * GLA_RANK, A_KW:].set(gla_wg2_b[l])
    vone = np.zeros((1, 2 * LANES), np.float32)
    vone[0, ONES_LANE] = 1.0
    vone[0, LANES + ONES_LANE] = 1.0
    return {
        "pre_g": norm_pre_g[l].reshape(1, D_MODEL),
        "post_g": norm_post_g[l].reshape(1, D_MODEL),
        "w_in": w,
        "wg2": wg2.astype(BF16),
        "bg": jnp.concatenate([gla_bg_f[l], gla_bg_b[l]]).reshape(1, 2 * A_KW),
        "qg": jnp.tile(q_norm_g[l][deint], C_HEADS).reshape(1, C_W),
        "kg": jnp.tile(k_norm_g[l][deint], C_KV).reshape(1, C_KVW),
        "sg": sgu_norm_g[l].reshape(1, D_W),
        "bd_q": _block_diag_mean(C_W, C_HD),
        "bd_a": _block_diag_mean(A_W, A_DV),
        "vone": jnp.asarray(vone),
        "og": jnp.tile(gla_onorm_g[l], A_HEADS).reshape(1, A_W),
        "fnet_w": fnet_w[l].astype(BF16),
        "sgu_w": sgu_w[l].reshape(D_GROUPS * SGU_CHUNK, SGU_CHUNK).astype(BF16),
        "sgu_b": jnp.repeat(sgu_b[l].T, D_GC, axis=1),
        "w_out": w_out[l].astype(BF16),
    }


def _seq_tables(n):
    rows = n // GRID_W
    row = jnp.repeat(jnp.arange(rows, dtype=F32), GRID_W)
    col = jnp.tile(jnp.arange(GRID_W, dtype=F32), rows)
    rope_axis = C_HD // 2
    freqs = ROPE_THETA ** (-jnp.arange(0, rope_axis, 2, dtype=F32) / rope_axis)
    ang = jnp.concatenate([row[:, None] * freqs, col[:, None] * freqs], axis=-1)
    cos, sin = jnp.cos(ang), jnp.sin(ang)
    cos_t = jnp.tile(jnp.concatenate([cos, cos], axis=-1), (1, LANES // C_HD))
    sin_t = jnp.tile(jnp.concatenate([-sin, sin], axis=-1), (1, LANES // C_HD))

    n2 = FFT_N2
    n1 = n // n2

    def dft_angles(rows, cols, period):
        prod = (jnp.arange(rows, dtype=jnp.int32)[:, None] * jnp.arange(cols, dtype=jnp.int32)[None, :]) % period
        return prod.astype(F32) * (2.0 * math.pi / period)

    a1_ang = dft_angles(n1, n1, n1)
    a1 = jnp.concatenate([jnp.cos(a1_ang), -jnp.sin(a1_ang)], axis=0)
    tw_ang = dft_angles(n2, n1, n)
    scale = 1.0 / math.sqrt(n * B_GC)
    tw_cos = jnp.broadcast_to((jnp.cos(tw_ang) * scale)[:, :, None], (n2, n1, LANES))
    tw_sin = jnp.broadcast_to((jnp.sin(tw_ang) * scale)[:, :, None], (n2, n1, LANES))
    ang2 = dft_angles(n2, n2, n2)
    c2, s2 = jnp.cos(ang2), jnp.sin(ang2)
    eye = jnp.eye(FFT_K1, dtype=F32)
    blk = lambda m: (m[:, None, None, :] * eye[None, :, :, None]).reshape(n2 * FFT_K1, FFT_K1 * n2)
    l2 = jnp.block([[blk(c2), blk(s2)], [blk(-s2), blk(c2)]])
    angc = dft_angles(B_GC, B_GC, B_GC)
    grp = jnp.eye(B_GROUPS, dtype=F32)
    kron = lambda m: (grp[:, None, :, None] * m[None, :, None, :]).reshape(B_W, B_W)
    chan_cos = kron(jnp.cos(angc))
    chan_sin = kron(jnp.sin(angc))
    return {
        "cos": cos_t, "sin": sin_t,
        "a1": a1.astype(BF16),
        "tw_cos": tw_cos, "tw_sin": tw_sin,
        "l2": l2.astype(BF16),
        "chan_cos": chan_cos.astype(BF16), "chan_sin": chan_sin.astype(BF16),
    }


def _layer(x, mod, lp, tabs):
    gq, gk, gv, gf, gb, fu, aq, kt4, va, du, dvn, sz = _inproj(x, mod, lp, tabs)
    oa = _gla_band(gq, gk, gv, gf, gb)
    of, ob = _gla_rec(gq, gk, gv, gf, gb)
    yb = _fnet(fu, lp, tabs)
    ao = _attention(aq, kt4, va)
    return _outproj(x, mod, oa, of, ob, yb, ao, du, dvn, sz, lp)


def kernel(x_prompt, x_sample, c_prompt, c_sample, ada_w, ada_b, norm_pre_g, norm_post_g, w_in,
           gla_wg2_f, gla_bg_f, gla_wg2_b, gla_bg_b, gla_onorm_g, fnet_w, q_norm_g, k_norm_g,
           sgu_norm_g, sgu_w, sgu_b, w_out):
    bp, bs = x_prompt.shape[0], x_sample.shape[0]
    pad_rows = (-(bp + bs)) % 8
    c_all = jnp.concatenate([c_prompt, c_sample, jnp.zeros((pad_rows, D_MODEL), F32)], axis=0)
    mod = _adaln(c_all, ada_w, ada_b)
    tabs_p = _seq_tables(x_prompt.shape[1])
    tabs_s = _seq_tables(x_sample.shape[1])
    y_prompt, y_sample = x_prompt, x_sample
    for l in range(DEPTH):
        lp = _layer_params(l, None, norm_pre_g, norm_post_g, w_in, gla_wg2_f, gla_bg_f, gla_wg2_b,
                           gla_bg_b, gla_onorm_g, fnet_w, q_norm_g, k_norm_g, sgu_norm_g, sgu_w,
                           sgu_b, w_out)
        mod_p = mod[l, 0:bp].reshape(bp, 1, 3 * D_MODEL)
        mod_s = mod[l, bp:bp + bs].reshape(bs, 1, 3 * D_MODEL)
        y_prompt = _layer(y_prompt, mod_p, lp, tabs_p)
        y_sample = _layer(y_sample, mod_s, lp, tabs_s)
    return (y_prompt, y_sample)
```

```python
import math

import numpy as np
import jax
import jax.numpy as jnp
from jax import lax
from jax.experimental import pallas as pl
from jax.experimental.pallas import tpu as pltpu

F32 = jnp.float32
BF16 = jnp.bfloat16
HIGHEST = lax.Precision.HIGHEST

D_MODEL = 1024
DEPTH = 2
GRID_W = 64
A_HEADS, A_DK, A_DV = 4, 32, 64
A_W, A_KW = A_HEADS * A_DV, A_HEADS * A_DK
GLA_RANK = 16
GLA_GATE_NORM = 16.0
B_GROUPS, B_GC = 4, 64
B_W = B_GROUPS * B_GC
C_HEADS, C_KV, C_HD = 8, 2, 64
C_GRP = C_HEADS // C_KV
C_W, C_KVW = C_HEADS * C_HD, C_KV * C_HD
ROPE_THETA = 10000.0
D_GROUPS, D_GC = 4, 64
D_W = D_GROUPS * D_GC
SGU_CHUNK = 128
D_MIX = A_W + B_W + C_W + D_W
EPS = 1e-6
LOG2E = 1.4426950408889634

LANES = 128
VMEM_LIMIT_BYTES = 56 * 1024 * 1024

GQ, GK, GV, FU, AQ, AK, LR, AV, DU, DV, ZZ = 0, 128, 256, 512, 768, 1280, 1408, 1536, 1792, 2048, 2304
W_PACKED = 3584
AV_W = 2 * LANES
ONES_LANE = C_HD

TM_IN = 512
TM_OUT = 512
GLA_SUB = 16
GLA_BAND_TILE = 64
GLA_GRP = 128
GLA_SEG = 512
GLA_LOOKAHEAD = 3
FFT_N2 = 64
FFT_NB = 8
FFT_K1 = 8
FFT_K1_GROUPS = 2
ATT_TQ = 512
ATT_TK = 2048
ATT_UNROLL = 2


def _params(*sem):
    return pltpu.CompilerParams(dimension_semantics=sem, vmem_limit_bytes=VMEM_LIMIT_BYTES)


def _full(shape):
    n = len(shape)
    return pl.BlockSpec(shape, lambda *_: (0,) * n)


def _adaln_kernel(c_ref, w_ref, b_ref, o_ref):
    c = c_ref[...]
    sc = c / (1.0 + jnp.exp(-c))
    o_ref[0] = jnp.dot(sc, w_ref[0], precision=HIGHEST, preferred_element_type=F32) + b_ref[0]


def _adaln(c_all, ada_w, ada_b):
    rows = c_all.shape[0]
    ncol = 3 * D_MODEL // D_MODEL
    return pl.pallas_call(
        _adaln_kernel,
        out_shape=jax.ShapeDtypeStruct((DEPTH, rows, 3 * D_MODEL), F32),
        grid=(DEPTH, ncol),
        in_specs=[
            pl.BlockSpec((rows, D_MODEL), lambda l, j: (0, 0)),
            pl.BlockSpec((1, D_MODEL, D_MODEL), lambda l, j: (l, 0, j)),
            pl.BlockSpec((1, 1, D_MODEL), lambda l, j: (l, 0, j)),
        ],
        out_specs=pl.BlockSpec((1, rows, D_MODEL), lambda l, j: (l, 0, j)),
        compiler_params=_params("arbitrary", "arbitrary"),
        name="adaln",
    )(c_all, ada_w, ada_b.reshape(DEPTH, 1, 3 * D_MODEL))


def _rope(x, cos, sin_signed):
    width = x.shape[-1]
    lane = lax.broadcasted_iota(jnp.int32, x.shape, 1)
    first_half = (lane % C_HD) < (C_HD // 2)
    swapped = jnp.where(first_half, pltpu.roll(x, width - C_HD // 2, 1), pltpu.roll(x, C_HD // 2, 1))
    reps = width // LANES
    cos_w = jnp.concatenate([cos] * reps, axis=-1) if reps > 1 else cos
    sin_w = jnp.concatenate([sin_signed] * reps, axis=-1) if reps > 1 else sin_signed
    return x * cos_w + swapped * sin_w


def _inproj_kernel(x_ref, mod_ref, preg_ref, w_ref, wg2_ref, bg_ref, qg_ref, kg_ref, sg_ref,
                   cos_ref, sin_ref, bd_ref, vone_ref,
                   gq_ref, gk_ref, gv_ref, gf_ref, gb_ref, fu_ref, aq_ref, kt_ref, va_ref,
                   du_ref, dvn_ref, sz_ref):
    x = x_ref[0]
    ms = jnp.mean(x * x, axis=-1, keepdims=True)
    shift = mod_ref[0, :, 0:D_MODEL]
    scale = mod_ref[0, :, D_MODEL:2 * D_MODEL]
    h = (x * lax.rsqrt(ms + EPS)) * preg_ref[...] * (1.0 + scale) + shift
    hb = h.astype(BF16)

    def seg(off, width):
        return jnp.dot(hb, w_ref[:, off:off + width], preferred_element_type=F32)

    klr = seg(AK, C_KVW + LANES)
    q = seg(AQ, C_W)
    z = seg(ZZ, D_MIX)
    gqk = seg(GQ, 2 * A_KW)
    gv = seg(GV, A_W)
    fu = seg(FU, B_W)
    va = seg(AV, AV_W)
    du = seg(DU, D_W)
    dv = seg(DV, D_W)
    k = klr[:, :C_KVW]
    lr = klr[:, C_KVW:].astype(BF16)
    logits = jnp.dot(lr, wg2_ref[...], preferred_element_type=F32) + bg_ref[...]
    qms = jnp.dot((q * q).astype(BF16), bd_ref[...], preferred_element_type=F32)
    kms = jnp.dot((k * k).astype(BF16), bd_ref[0:C_KVW, 0:C_KVW], preferred_element_type=F32)

    gq_ref[0] = gqk[:, :A_KW] * (A_DK ** -0.5)
    gk_ref[0] = gqk[:, A_KW:]
    gv_ref[0] = gv
    logg = (jnp.minimum(logits, 0.0) - jnp.log1p(jnp.exp(-jnp.abs(logits)))) * (1.0 / GLA_GATE_NORM)
    gf_ref[0] = logg[:, :A_KW]
    gb_ref[0] = logg[:, A_KW:]

    fu_ref[0] = fu.astype(BF16)

    cos = cos_ref[...]
    sin = sin_ref[...]
    qn = q * lax.rsqrt(qms + EPS) * qg_ref[...]
    aq_ref[0] = (_rope(qn, cos, sin) * (C_HD ** -0.5 * LOG2E)).astype(BF16)
    kn = k * lax.rsqrt(kms + EPS) * kg_ref[...]
    kt = _rope(kn, cos, sin).T.astype(BF16)
    for g in range(C_KV):
        for r in range(C_GRP):
            kt_ref[0, g, r * C_HD:(r + 1) * C_HD, :] = kt[g * C_HD:(g + 1) * C_HD, :]
    vab = (va + vone_ref[...]).astype(BF16)
    va_ref[0, 0] = vab[:, :LANES]
    va_ref[0, 1] = vab[:, LANES:]

    du_ref[0] = du.astype(BF16)
    dms = jnp.mean(dv * dv, axis=-1, keepdims=True)
    dvn_ref[0] = (dv * lax.rsqrt(dms + EPS) * sg_ref[...]).astype(BF16)

    sz_ref[0] = (z / (1.0 + jnp.exp(-z))).astype(BF16)


def _inproj(x, mod, lp, tabs):
    bsz, n, _ = x.shape
    tm = min(TM_IN, n)
    grid = (bsz, n // tm)
    tok = lambda w: pl.BlockSpec((1, tm, w), lambda b, i: (b, i, 0))
    out_shapes = (
        jax.ShapeDtypeStruct((bsz, n, A_KW), F32),
        jax.ShapeDtypeStruct((bsz, n, A_KW), F32),
        jax.ShapeDtypeStruct((bsz, n, A_W), F32),
        jax.ShapeDtypeStruct((bsz, n, A_KW), F32),
        jax.ShapeDtypeStruct((bsz, n, A_KW), F32),
        jax.ShapeDtypeStruct((bsz, n, B_W), BF16),
        jax.ShapeDtypeStruct((bsz, n, C_W), BF16),
        jax.ShapeDtypeStruct((bsz, C_KV, C_GRP * C_HD, n), BF16),
        jax.ShapeDtypeStruct((bsz, C_KV, n, LANES), BF16),
        jax.ShapeDtypeStruct((bsz, n, D_W), BF16),
        jax.ShapeDtypeStruct((bsz, n, D_W), BF16),
        jax.ShapeDtypeStruct((bsz, n, D_MIX), BF16),
    )
    out_specs = (
        tok(A_KW), tok(A_KW), tok(A_W), tok(A_KW), tok(A_KW), tok(B_W), tok(C_W),
        pl.BlockSpec((1, C_KV, C_GRP * C_HD, tm), lambda b, i: (b, 0, 0, i)),
        pl.BlockSpec((1, C_KV, tm, LANES), lambda b, i: (b, 0, i, 0)),
        tok(D_W), tok(D_W), tok(D_MIX),
    )
    in_specs = [
        tok(D_MODEL),
        pl.BlockSpec((1, 1, 3 * D_MODEL), lambda b, i: (b, 0, 0)),
        _full((1, D_MODEL)),
        _full((D_MODEL, W_PACKED)),
        _full((LANES, 2 * A_KW)),
        _full((1, 2 * A_KW)),
        _full((1, C_W)),
        _full((1, C_KVW)),
        _full((1, D_W)),
        pl.BlockSpec((tm, LANES), lambda b, i: (i, 0)),
        pl.BlockSpec((tm, LANES), lambda b, i: (i, 0)),
        _full((C_W, C_W)),
        _full((1, 2 * LANES)),
    ]
    return pl.pallas_call(
        _inproj_kernel, out_shape=out_shapes, grid=grid, in_specs=in_specs, out_specs=out_specs,
        compiler_params=_params("parallel", "parallel"), name="inproj",
    )(x, mod, lp["pre_g"], lp["w_in"], lp["wg2"], lp["bg"], lp["qg"], lp["kg"], lp["sg"],
      tabs["cos"], tabs["sin"], lp["bd_q"], lp["vone"])


def _head_match(shape, rows_per_head, cols_per_head):
    return (lax.broadcasted_iota(jnp.int32, shape, 0) // rows_per_head
            == lax.broadcasted_iota(jnp.int32, shape, 1) // cols_per_head)


def _gla_band_kernel(q_ref, k_ref, v_ref, gf_ref, gb_ref, o_ref,
                     qf_ref, kf_ref, vf_ref, bf_ref, cb_ref, vs_ref, os_ref):
    rows = q_ref.shape[1]
    ts = rows // GLA_SUB
    n_slab = A_W // LANES
    every = lambda p: pl.ds(p, ts, stride=GLA_SUB)
    hsum = jnp.where(_head_match((A_KW, A_W), A_DK, A_DV), 1.0, 0.0).astype(BF16)
    for h in range(n_slab):
        vs_ref[h] = v_ref[0, :, h * LANES:(h + 1) * LANES]
    run = jnp.zeros((ts, A_KW), F32)
    for p in range(GLA_SUB):
        run = run + gf_ref[0, every(p), :] * LOG2E
        bf_ref[p] = run
        qf_ref[p] = q_ref[0, every(p), :]
        kf_ref[p] = k_ref[0, every(p), :]
        vf_ref[p] = jnp.concatenate([vs_ref[h, every(p), :] for h in range(n_slab)], axis=-1)
    run = jnp.zeros((ts, A_KW), F32)
    for p in reversed(range(GLA_SUB)):
        run = run + gb_ref[0, every(p), :] * LOG2E
        cb_ref[p] = run
    for p in range(GLA_SUB):
        q = qf_ref[p]
        prods = []
        for s in range(GLA_SUB):
            qk = q * kf_ref[s]
            if s < p:
                qk = qk * jnp.exp2(bf_ref[p] - bf_ref[s])
            elif s > p:
                qk = qk * jnp.exp2(cb_ref[p] - cb_ref[s])
            prods.append(qk.astype(BF16))
        a = jnp.dot(jnp.concatenate(prods, axis=0), hsum, preferred_element_type=F32)
        acc = a[0:ts] * vf_ref[0]
        for s in range(1, GLA_SUB):
            acc = acc + a[s * ts:(s + 1) * ts] * vf_ref[s]
        for h in range(n_slab):
            os_ref[h, every(p), :] = acc[:, h * LANES:(h + 1) * LANES]
    o_ref[0] = jnp.concatenate([os_ref[h] for h in range(n_slab)], axis=-1).astype(BF16)


def _gla_band(gq, gk, gv, gf, gb):
    bsz, n, _ = gq.shape
    rows = min(GLA_BAND_TILE * GLA_SUB, n)
    ts = rows // GLA_SUB
    blk = lambda w: pl.BlockSpec((1, rows, w), lambda b, i: (b, i, 0))
    return pl.pallas_call(
        _gla_band_kernel,
        out_shape=jax.ShapeDtypeStruct((bsz, n, A_W), BF16),
        grid=(bsz, n // rows),
        in_specs=[blk(A_KW), blk(A_KW), blk(A_W), blk(A_KW), blk(A_KW)],
        out_specs=blk(A_W),
        scratch_shapes=[
            pltpu.VMEM((GLA_SUB, ts, A_KW), F32), pltpu.VMEM((GLA_SUB, ts, A_KW), F32),
            pltpu.VMEM((GLA_SUB, ts, A_W), F32),
            pltpu.VMEM((GLA_SUB, ts, A_KW), F32), pltpu.VMEM((GLA_SUB, ts, A_KW), F32),
            pltpu.VMEM((A_W // LANES, rows, LANES), F32), pltpu.VMEM((A_W // LANES, rows, LANES), F32),
        ],
        compiler_params=_params("parallel", "parallel"), name="gla_band",
    )(gq, gk, gv, gf, gb)


def _split3(x):
    hi = x.astype(BF16)
    rest = x - hi.astype(F32)
    mid = rest.astype(BF16)
    lo = (rest - mid.astype(F32)).astype(BF16)
    return hi, mid, lo


def _gla_prep(q_ref, k_ref, g_ref, qt_ref, kh_ref, dect_ref, tot_ref, pad_ref, reverse):
    seg = q_ref.shape[1]
    n_sub = seg // GLA_SUB
    r_i = lax.broadcasted_iota(jnp.int32, (GLA_GRP, GLA_GRP), 0)
    c_i = lax.broadcasted_iota(jnp.int32, (GLA_GRP, GLA_GRP), 1)
    same = (r_i // GLA_SUB) == (c_i // GLA_SUB)
    tri = (c_i >= r_i) if reverse else (c_i <= r_i)
    sums_mat = jnp.concatenate([jnp.where(same & tri, 1.0, 0.0), jnp.where(same, 1.0, 0.0)], axis=0).astype(BF16)
    for r in range(seg // GLA_GRP):
        rows = slice(r * GLA_GRP, (r + 1) * GLA_GRP)
        g3 = jnp.concatenate(_split3(g_ref[0, rows, :]), axis=-1)
        sums = jnp.dot(sums_mat, g3, preferred_element_type=F32)
        sums = sums[:, 0:A_KW] + sums[:, A_KW:2 * A_KW] + sums[:, 2 * A_KW:]
        cum, tot = sums[:GLA_GRP], sums[GLA_GRP:]
        qt_ref[rows, :] = (q_ref[0, rows, :] * jnp.exp(cum)).astype(BF16)
        kh_ref[rows, :] = (k_ref[0, rows, :] * jnp.exp(tot - cum)).astype(BF16)
        tot_ref[rows, :] = tot
    pad_ref[...] = jnp.zeros_like(pad_ref)
    pad_ref[0:n_sub, :] = tot_ref[pl.ds(0, n_sub, stride=GLA_SUB), :]
    dect_ref[...] = jnp.exp(pad_ref[...].T)


def _gla_rec_kernel(qf_ref, kf_ref, vf_ref, gf_ref, qb_ref, kb_ref, vb_ref, gb_ref, of_ref, ob_ref,
                    sf_ref, sb_ref, qtf_ref, khf_ref, dtf_ref, qtb_ref, khb_ref, dtb_ref, tot_ref, pad_ref):
    @pl.when(pl.program_id(1) == 0)
    def _():
        sf_ref[...] = jnp.zeros_like(sf_ref)
        sb_ref[...] = jnp.zeros_like(sb_ref)

    seg = qf_ref.shape[1]
    n_sub = seg // GLA_SUB
    _gla_prep(qf_ref, kf_ref, gf_ref, qtf_ref, khf_ref, dtf_ref, tot_ref, pad_ref, reverse=False)
    _gla_prep(qb_ref, kb_ref, gb_ref, qtb_ref, khb_ref, dtb_ref, tot_ref, pad_ref, reverse=True)
    smask = _head_match((A_KW, A_W), A_DK, A_DV)

    def update(i, kh_ref, v_ref):
        rows = slice(i * GLA_SUB, (i + 1) * GLA_SUB)
        upd = lax.dot_general(kh_ref[rows, :], v_ref[0, rows, :].astype(BF16), (((0,), (0,)), ((), ())),
                              preferred_element_type=F32)
        return jnp.where(smask, upd, 0.0)

    def step(i, s, upd, qt_ref, dect_ref, o_ref):
        rows = slice(i * GLA_SUB, (i + 1) * GLA_SUB)
        o_ref[0, rows, :] = jnp.dot(qt_ref[rows, :], s.astype(BF16), preferred_element_type=F32).astype(BF16)
        return s * dect_ref[:, i:i + 1] + upd

    order_f = list(range(n_sub))
    order_b = list(reversed(range(n_sub)))
    upd_f = [update(i, khf_ref, vf_ref) for i in order_f[:GLA_LOOKAHEAD]]
    upd_b = [update(i, khb_ref, vb_ref) for i in order_b[:GLA_LOOKAHEAD]]
    s_f = sf_ref[...]
    s_b = sb_ref[...]
    for j in range(n_sub):
        if j + GLA_LOOKAHEAD < n_sub:
            upd_f.append(update(order_f[j + GLA_LOOKAHEAD], khf_ref, vf_ref))
            upd_b.append(update(order_b[j + GLA_LOOKAHEAD], khb_ref, vb_ref))
        s_f = step(order_f[j], s_f, upd_f[j], qtf_ref, dtf_ref, of_ref)
        s_b = step(order_b[j], s_b, upd_b[j], qtb_ref, dtb_ref, ob_ref)
    sf_ref[...] = s_f
    sb_ref[...] = s_b


def _gla_rec(gq, gk, gv, gf, gb):
    bsz, n, _ = gq.shape
    seg = min(GLA_SEG, n)
    nseg = n // seg
    assert seg // GLA_SUB <= LANES
    fwd = lambda w: pl.BlockSpec((1, seg, w), lambda b, s: (b, s, 0))
    bwd = lambda w: pl.BlockSpec((1, seg, w), lambda b, s: (b, nseg - 1 - s, 0))
    return pl.pallas_call(
        _gla_rec_kernel,
        out_shape=(jax.ShapeDtypeStruct((bsz, n, A_W), BF16), jax.ShapeDtypeStruct((bsz, n, A_W), BF16)),
        grid=(bsz, nseg),
        in_specs=[fwd(A_KW), fwd(A_KW), fwd(A_W), fwd(A_KW), bwd(A_KW), bwd(A_KW), bwd(A_W), bwd(A_KW)],
        out_specs=(fwd(A_W), bwd(A_W)),
        scratch_shapes=[
            pltpu.VMEM((A_KW, A_W), F32), pltpu.VMEM((A_KW, A_W), F32),
            pltpu.VMEM((seg, A_KW), BF16), pltpu.VMEM((seg, A_KW), BF16), pltpu.VMEM((A_KW, LANES), F32),
            pltpu.VMEM((seg, A_KW), BF16), pltpu.VMEM((seg, A_KW), BF16), pltpu.VMEM((A_KW, LANES), F32),
            pltpu.VMEM((seg, A_KW), F32), pltpu.VMEM((LANES, A_KW), F32),
        ],
        compiler_params=_params("parallel", "arbitrary"), name="gla_rec",
    )(gq, gk, gv, gf, gq, gk, gv, gb)


def _fft1_kernel(u_ref, a1_ref, cos_ref, sin_ref, t_ref):
    n1 = u_ref.shape[1]
    nb = cos_ref.shape[0]
    t = jnp.dot(a1_ref[...], u_ref[0], preferred_element_type=F32)
    for j in range(nb):
        cols = slice(j * B_W, (j + 1) * B_W)
        tr = t[:n1, cols]
        ti = t[n1:, cols]
        c = jnp.concatenate([cos_ref[j]] * (B_W // LANES), axis=-1)
        s = jnp.concatenate([sin_ref[j]] * (B_W // LANES), axis=-1)
        t_ref[0, 0, :, cols] = (tr * c + ti * s).astype(BF16)
        t_ref[0, 1, :, cols] = (ti * c - tr * s).astype(BF16)


def _fft2_kernel(t_ref, l_ref, cc_ref, cs_ref, w_ref, o_ref):
    n2 = t_ref.shape[3]
    half = FFT_K1 * n2
    groups = range(t_ref.shape[2] // FFT_K1)
    xs = []
    for grp in groups:
        k1s = slice(grp * FFT_K1, (grp + 1) * FFT_K1)
        xs.append(jnp.concatenate([t_ref[0, 0, k1s].reshape(half, B_W), t_ref[0, 1, k1s].reshape(half, B_W)],
                                  axis=0))
    gs = [jnp.dot(l_ref[...], x, preferred_element_type=F32) for x in xs]
    mixed = [jnp.dot(g[:half].astype(BF16), cc_ref[...], preferred_element_type=F32)
             + jnp.dot(g[half:].astype(BF16), cs_ref[...], preferred_element_type=F32) for g in gs]
    ys = [jnp.dot(m.astype(BF16), w_ref[...], preferred_element_type=F32) for m in mixed]
    o_ref[0] = jnp.concatenate([y.reshape(n2, FFT_K1, B_W) for y in ys], axis=1).astype(BF16)


def _fnet(fu, lp, tabs):
    bsz, n, _ = fu.shape
    n2 = FFT_N2
    n1 = n // n2
    nb = FFT_NB
    k1b = FFT_K1 * FFT_K1_GROUPS
    u2 = fu.reshape(bsz, n1, n2 * B_W)
    tt = pl.pallas_call(
        _fft1_kernel,
        out_shape=jax.ShapeDtypeStruct((bsz, 2, n1, n2 * B_W), BF16),
        grid=(n2 // nb, bsz),
        in_specs=[
            pl.BlockSpec((1, n1, nb * B_W), lambda j, b: (b, 0, j)),
            _full((2 * n1, n1)),
            pl.BlockSpec((nb, n1, LANES), lambda j, b: (j, 0, 0)),
            pl.BlockSpec((nb, n1, LANES), lambda j, b: (j, 0, 0)),
        ],
        out_specs=pl.BlockSpec((1, 2, n1, nb * B_W), lambda j, b: (b, 0, 0, j)),
        compiler_params=_params("parallel", "parallel"), name="fft_pass1",
    )(u2, tabs["a1"], tabs["tw_cos"], tabs["tw_sin"])
    t5 = tt.reshape(bsz, 2, n1, n2, B_W)
    y = pl.pallas_call(
        _fft2_kernel,
        out_shape=jax.ShapeDtypeStruct((bsz, n2, n1, B_W), BF16),
        grid=(bsz, n1 // k1b),
        in_specs=[
            pl.BlockSpec((1, 2, k1b, n2, B_W), lambda b, i: (b, 0, i, 0, 0)),
            _full((2 * FFT_K1 * n2, 2 * FFT_K1 * n2)),
            _full((B_W, B_W)), _full((B_W, B_W)), _full((B_W, B_W)),
        ],
        out_specs=pl.BlockSpec((1, n2, k1b, B_W), lambda b, i: (b, 0, i, 0)),
        compiler_params=_params("parallel", "parallel"), name="fft_pass2",
    )(t5, tabs["l2"], tabs["chan_cos"], tabs["chan_sin"], lp["fnet_w"])
    return y.reshape(bsz, n, B_W)


def _attn_kernel(q_ref, kt_ref, va_ref, o_ref, qm_ref, m_ref, acc_ref):
    tq = q_ref.shape[1]
    n = kt_ref.shape[3]
    tk = min(ATT_TK, n)
    q = q_ref[0]
    lane = lax.broadcasted_iota(jnp.int32, q.shape, 1)
    for h in range(C_GRP):
        qm_ref[h * tq:(h + 1) * tq, :] = jnp.where(lane // C_HD == h, q, jnp.zeros_like(q))
    m_ref[...] = jnp.full(m_ref.shape, -jnp.inf, F32)
    acc_ref[...] = jnp.zeros(acc_ref.shape, F32)

    def scores(c):
        c0 = pl.multiple_of(c * tk, tk)
        return jnp.dot(qm_ref[...], kt_ref[0, 0, :, pl.ds(c0, tk)], preferred_element_type=F32)

    def softmax_accumulate(c, s):
        c0 = pl.multiple_of(c * tk, tk)
        v = va_ref[0, 0, pl.ds(c0, tk), :]
        s_max = s[:, 0:LANES]
        for t in range(1, tk // LANES):
            s_max = jnp.maximum(s_max, s[:, t * LANES:(t + 1) * LANES])
        row_max = jnp.max(s_max, axis=-1, keepdims=True)
        m_old = m_ref[...]
        m_new = jnp.maximum(m_old, row_max)
        p = jnp.exp2(s - jnp.concatenate([m_new] * (tk // LANES), axis=-1))
        alpha = jnp.exp2(m_old - m_new)
        acc_ref[...] = alpha * acc_ref[...] + jnp.dot(p.astype(BF16), v, preferred_element_type=F32)
        m_ref[...] = m_new

    def step(c, carry):
        softmax_accumulate(c, scores(c))
        return carry

    lax.fori_loop(0, n // tk, step, 0, unroll=ATT_UNROLL)
    acc = acc_ref[...]
    inv = 1.0 / acc[:, ONES_LANE:ONES_LANE + 1]
    outs = [(acc[h * tq:(h + 1) * tq, 0:C_HD] * inv[h * tq:(h + 1) * tq]) for h in range(C_GRP)]
    o_ref[0] = jnp.concatenate(outs, axis=-1).astype(BF16)


def _attention(aq, kt4, va):
    bsz, n, _ = aq.shape
    tq = min(ATT_TQ, n)
    return pl.pallas_call(
        _attn_kernel,
        out_shape=jax.ShapeDtypeStruct((bsz, n, C_W), BF16),
        grid=(bsz, C_KV, n // tq),
        in_specs=[
            pl.BlockSpec((1, tq, C_GRP * C_HD), lambda b, g, i: (b, i, g)),
            pl.BlockSpec((1, 1, C_GRP * C_HD, n), lambda b, g, i: (b, g, 0, 0)),
            pl.BlockSpec((1, 1, n, LANES), lambda b, g, i: (b, g, 0, 0)),
        ],
        out_specs=pl.BlockSpec((1, tq, C_GRP * C_HD), lambda b, g, i: (b, i, g)),
        scratch_shapes=[
            pltpu.VMEM((C_GRP * tq, C_GRP * C_HD), BF16),
            pltpu.VMEM((C_GRP * tq, LANES), F32),
            pltpu.VMEM((C_GRP * tq, LANES), F32),
        ],
        compiler_params=_params("parallel", "parallel", "arbitrary"), name="attention",
    )(aq, kt4, va)


def _outproj_kernel(x_ref, mod_ref, oa_ref, of_ref, ob_ref, yb_ref, ao_ref, du_ref, dvn_ref, sz_ref,
                    w_ref, postg_ref, og_ref, bd_ref, sw_ref, sb_ref, o_ref):
    tm = x_ref.shape[1]
    sz = sz_ref[0].astype(F32)

    o = oa_ref[0].astype(F32) + of_ref[0].astype(F32) + ob_ref[0].astype(F32)
    oms = jnp.dot((o * o).astype(BF16), bd_ref[...], preferred_element_type=F32)
    out_a = o * lax.rsqrt(oms + EPS) * og_ref[...]
    y = jnp.dot((out_a * sz[:, 0:A_W]).astype(BF16), w_ref[0:A_W, :], preferred_element_type=F32)

    m_b = (yb_ref[0].astype(F32) * sz[:, A_W:A_W + B_W]).astype(BF16)
    y = y + jnp.dot(m_b, w_ref[A_W:A_W + B_W, :], preferred_element_type=F32)

    m_c = (ao_ref[0].astype(F32) * sz[:, A_W + B_W:A_W + B_W + C_W]).astype(BF16)
    y = y + jnp.dot(m_c, w_ref[A_W + B_W:A_W + B_W + C_W, :], preferred_element_type=F32)

    lane = lax.broadcasted_iota(jnp.int32, (SGU_CHUNK, D_W), 1)
    parts = []
    for c in range(tm // SGU_CHUNK):
        rows = slice(c * SGU_CHUNK, (c + 1) * SGU_CHUNK)
        r = jnp.dot(sw_ref[...], dvn_ref[0, rows, :], preferred_element_type=F32)
        mix = sb_ref[...]
        for g in range(D_GROUPS):
            mix = mix + jnp.where(lane // D_GC == g, r[g * SGU_CHUNK:(g + 1) * SGU_CHUNK, :], 0.0)
        parts.append(du_ref[0, rows, :].astype(F32) * mix)
    out_d = jnp.concatenate(parts, axis=0)
    m_d = (out_d * sz[:, A_W + B_W + C_W:]).astype(BF16)
    y = y + jnp.dot(m_d, w_ref[A_W + B_W + C_W:, :], preferred_element_type=F32)

    yms = jnp.mean(y * y, axis=-1, keepdims=True)
    yn = y * lax.rsqrt(yms + EPS) * postg_ref[...]
    gate = mod_ref[0, :, 2 * D_MODEL:3 * D_MODEL]
    o_ref[0] = x_ref[0] + gate * yn


def _outproj(x, mod, oa, of, ob, yb, ao, du, dvn, sz, lp):
    bsz, n, _ = x.shape
    tm = min(TM_OUT, n)
    tok = lambda w: pl.BlockSpec((1, tm, w), lambda b, i: (b, i, 0))
    return pl.pallas_call(
        _outproj_kernel,
        out_shape=jax.ShapeDtypeStruct((bsz, n, D_MODEL), F32),
        grid=(bsz, n // tm),
        in_specs=[
            tok(D_MODEL),
            pl.BlockSpec((1, 1, 3 * D_MODEL), lambda b, i: (b, 0, 0)),
            tok(A_W), tok(A_W), tok(A_W), tok(B_W), tok(C_W), tok(D_W), tok(D_W), tok(D_MIX),
            _full((D_MIX, D_MODEL)), _full((1, D_MODEL)), _full((1, A_W)), _full((A_W, A_W)),
            _full((D_GROUPS * SGU_CHUNK, SGU_CHUNK)), _full((SGU_CHUNK, D_W)),
        ],
        out_specs=tok(D_MODEL),
        compiler_params=_params("parallel", "parallel"), name="outproj",
    )(x, mod, oa, of, ob, yb, ao, du, dvn, sz,
      lp["w_out"], lp["post_g"], lp["og"], lp["bd_a"], lp["sgu_w"], lp["sgu_b"])


def _pack_w_in(w):
    sizes = [A_KW, A_KW, A_W, 2 * GLA_RANK, B_W, C_W, C_KVW, C_KVW, D_W, D_W, D_MIX]
    off = [0] + [int(o) for o in np.cumsum(sizes)]
    a_q, a_k, a_v, a_lr, b_u, c_q, c_k, c_v, d_u, d_v, z = [w[:, off[i]:off[i + 1]] for i in range(len(sizes))]

    def deinterleave(x, heads):
        r = x.reshape(D_MODEL, heads, C_HD // 2, 2)
        return jnp.concatenate([r[..., 0], r[..., 1]], axis=-1).reshape(D_MODEL, heads * C_HD)

    zeros = lambda width: jnp.zeros((D_MODEL, width), w.dtype)
    pieces = [a_q, a_k, a_v, b_u, deinterleave(c_q, C_HEADS), deinterleave(c_k, C_KV),
              a_lr, zeros(LANES - 2 * GLA_RANK)]
    for h in range(C_KV):
        pieces += [c_v[:, h * C_HD:(h + 1) * C_HD], zeros(LANES - C_HD)]
    pieces += [d_u, d_v, z]
    packed = jnp.concatenate(pieces, axis=1)
    assert packed.shape[1] == W_PACKED
    return packed.astype(BF16)


def _block_diag_mean(width, group):
    idx = np.arange(width) // group
    return jnp.asarray((idx[:, None] == idx[None, :]).astype(np.float32) / group, BF16)


def _layer_params(l, ada_unused, norm_pre_g, norm_post_g, w_in, gla_wg2_f, gla_bg_f, gla_wg2_b, gla_bg_b,
                  gla_onorm_g, fnet_w, q_norm_g, k_norm_g, sgu_norm_g, sgu_w, sgu_b, w_out):
    deint = np.concatenate([np.arange(0, C_HD, 2), np.arange(1, C_HD, 2)])
    w = _pack_w_in(w_in[l])
    wg2 = jnp.zeros((LANES, 2 * A_KW), F32)
    wg2 = wg2.at[0:GLA_RANK, 0:A_KW].set(gla_wg2_f[l])
    wg2 = wg2.at[GLA_RANK:2 * GLA_RANK, A_KW:].set(gla_wg2_b[l])
    vone = np.zeros((1, 2 * LANES), np.float32)
    vone[0, ONES_LANE] = 1.0
    vone[0, LANES + ONES_LANE] = 1.0
    return {
        "pre_g": norm_pre_g[l].reshape(1, D_MODEL),
        "post_g": norm_post_g[l].reshape(1, D_MODEL),
        "w_in": w,
        "wg2": wg2.astype(BF16),
        "bg": jnp.concatenate([gla_bg_f[l], gla_bg_b[l]]).reshape(1, 2 * A_KW),
        "qg": jnp.tile(q_norm_g[l][deint], C_HEADS).reshape(1, C_W),
        "kg": jnp.tile(k_norm_g[l][deint], C_KV).reshape(1, C_KVW),
        "sg": sgu_norm_g[l].reshape(1, D_W),
        "bd_q": _block_diag_mean(C_W, C_HD),
        "bd_a": _block_diag_mean(A_W, A_DV),
        "vone": jnp.asarray(vone),
        "og": jnp.tile(gla_onorm_g[l], A_HEADS).reshape(1, A_W),
        "fnet_w": fnet_w[l].astype(BF16),
        "sgu_w": sgu_w[l].reshape(D_GROUPS * SGU_CHUNK, SGU_CHUNK).astype(BF16),
        "sgu_b": jnp.repeat(sgu_b[l].T, D_GC, axis=1),
        "w_out": w_out[l].astype(BF16),
    }


def _seq_tables(n):
    rows = n // GRID_W
    row = jnp.repeat(jnp.arange(rows, dtype=F32), GRID_W)
    col = jnp.tile(jnp.arange(GRID_W, dtype=F32), rows)
    rope_axis = C_HD // 2
    freqs = ROPE_THETA ** (-jnp.arange(0, rope_axis, 2, dtype=F32) / rope_axis)
    ang = jnp.concatenate([row[:, None] * freqs, col[:, None] * freqs], axis=-1)
    cos, sin = jnp.cos(ang), jnp.sin(ang)
    cos_t = jnp.tile(jnp.concatenate([cos, cos], axis=-1), (1, LANES // C_HD))
    sin_t = jnp.tile(jnp.concatenate([-sin, sin], axis=-1), (1, LANES // C_HD))

    n2 = FFT_N2
    n1 = n // n2

    def dft_angles(rows, cols, period):
        prod = (jnp.arange(rows, dtype=jnp.int32)[:, None] * jnp.arange(cols, dtype=jnp.int32)[None, :]) % period
        return prod.astype(F32) * (2.0 * math.pi / period)

    a1_ang = dft_angles(n1, n1, n1)
    a1 = jnp.concatenate([jnp.cos(a1_ang), -jnp.sin(a1_ang)], axis=0)
    tw_ang = dft_angles(n2, n1, n)
    scale = 1.0 / math.sqrt(n * B_GC)
    tw_cos = jnp.broadcast_to((jnp.cos(tw_ang) * scale)[:, :, None], (n2, n1, LANES))
    tw_sin = jnp.broadcast_to((jnp.sin(tw_ang) * scale)[:, :, None], (n2, n1, LANES))
    ang2 = dft_angles(n2, n2, n2)
    c2, s2 = jnp.cos(ang2), jnp.sin(ang2)
    eye = jnp.eye(FFT_K1, dtype=F32)
    blk = lambda m: (m[:, None, None, :] * eye[None, :, :, None]).reshape(n2 * FFT_K1, FFT_K1 * n2)
    l2 = jnp.block([[blk(c2), blk(s2)], [blk(-s2), blk(c2)]])
    angc = dft_angles(B_GC, B_GC, B_GC)
    grp = jnp.eye(B_GROUPS, dtype=F32)
    kron = lambda m: (grp[:, None, :, None] * m[None, :, None, :]).reshape(B_W, B_W)
    chan_cos = kron(jnp.cos(angc))
    chan_sin = kron(jnp.sin(angc))
    return {
        "cos": cos_t, "sin": sin_t,
        "a1": a1.astype(BF16),
        "tw_cos": tw_cos, "tw_sin": tw_sin,
        "l2": l2.astype(BF16),
        "chan_cos": chan_cos.astype(BF16), "chan_sin": chan_sin.astype(BF16),
    }


def _layer(x, mod, lp, tabs):
    gq, gk, gv, gf, gb, fu, aq, kt4, va, du, dvn, sz = _inproj(x, mod, lp, tabs)
    oa = _gla_band(gq, gk, gv, gf, gb)
    of, ob = _gla_rec(gq, gk, gv, gf, gb)
    yb = _fnet(fu, lp, tabs)
    ao = _attention(aq, kt4, va)
    return _outproj(x, mod, oa, of, ob, yb, ao, du, dvn, sz, lp)


def kernel(x_prompt, x_sample, c_prompt, c_sample, ada_w, ada_b, norm_pre_g, norm_post_g, w_in,
           gla_wg2_f, gla_bg_f, gla_wg2_b, gla_bg_b, gla_onorm_g, fnet_w, q_norm_g, k_norm_g,
           sgu_norm_g, sgu_w, sgu_b, w_out):
    bp, bs = x_prompt.shape[0], x_sample.shape[0]
    pad_rows = (-(bp + bs)) % 8
    c_all = jnp.concatenate([c_prompt, c_sample, jnp.zeros((pad_rows, D_MODEL), F32)], axis=0)
    mod = _adaln(c_all, ada_w, ada_b)
    tabs_p = _seq_tables(x_prompt.shape[1])
    tabs_s = _seq_tables(x_sample.shape[1])
    y_prompt, y_sample = x_prompt, x_sample
    for l in range(DEPTH):
        lp = _layer_params(l, None, norm_pre_g, norm_post_g, w_in, gla_wg2_f, gla_bg_f, gla_wg2_b,
                           gla_bg_b, gla_onorm_g, fnet_w, q_norm_g, k_norm_g, sgu_norm_g, sgu_w,
                           sgu_b, w_out)
        mod_p = mod[l, 0:bp].reshape(bp, 1, 3 * D_MODEL)
        mod_s = mod[l, bp:bp + bs].reshape(bs, 1, 3 * D_MODEL)
        y_prompt = _layer(y_prompt, mod_p, lp, tabs_p)
        y_sample = _layer(y_sample, mod_s, lp, tabs_s)
    return (y_prompt, y_sample)
```

```python
import math

import numpy as np
import jax
import jax.numpy as jnp
from jax import lax
from jax.experimental import pallas as pl
from jax.experimental.pallas import tpu as pltpu

F32 = jnp.float32
BF16 = jnp.bfloat16
HIGHEST = lax.Precision.HIGHEST

D_MODEL = 1024
DEPTH = 2
GRID_W = 64
A_HEADS, A_DK, A_DV = 4, 32, 64
A_W, A_KW = A_HEADS * A_DV, A_HEADS * A_DK
GLA_RANK = 16
GLA_GATE_NORM = 16.0
B_GROUPS, B_GC = 4, 64
B_W = B_GROUPS * B_GC
C_HEADS, C_KV, C_HD = 8, 2, 64
C_GRP = C_HEADS // C_KV
C_W, C_KVW = C_HEADS * C_HD, C_KV * C_HD
ROPE_THETA = 10000.0
D_GROUPS, D_GC = 4, 64
D_W = D_GROUPS * D_GC
SGU_CHUNK = 128
D_MIX = A_W + B_W + C_W + D_W
EPS = 1e-6
LOG2E = 1.4426950408889634

LANES = 128
VMEM_LIMIT_BYTES = 56 * 1024 * 1024

GQ, GK, GV, FU, AQ, AK, LR, AV, DU, DV, ZZ = 0, 128, 256, 512, 768, 1280, 1408, 1536, 1792, 2048, 2304
W_PACKED = 3584
AV_W = 2 * LANES
ONES_LANE = C_HD

TM_IN = 512
IN_ROW_PARTS = 2
TM_OUT = 512
GLA_SUB = 16
GLA_BAND_TILE = 64
GLA_GRP = 128
GLA_SEG = 512
GLA_LOOKAHEAD = 3
FFT_N2 = 64
FFT_NB = 8
FFT_K1 = 8
FFT_K1_GROUPS = 2
ATT_TQ = 512
ATT_TK = 2048
ATT_UNROLL = 2


def _params(*sem):
    return pltpu.CompilerParams(dimension_semantics=sem, vmem_limit_bytes=VMEM_LIMIT_BYTES)


def _full(shape):
    n = len(shape)
    return pl.BlockSpec(shape, lambda *_: (0,) * n)


def _adaln_kernel(c_ref, w_ref, b_ref, o_ref):
    c = c_ref[...]
    sc = c / (1.0 + jnp.exp(-c))
    o_ref[0] = jnp.dot(sc, w_ref[0], precision=HIGHEST, preferred_element_type=F32) + b_ref[0]


def _adaln(c_all, ada_w, ada_b):
    rows = c_all.shape[0]
    ncol = 3 * D_MODEL // D_MODEL
    return pl.pallas_call(
        _adaln_kernel,
        out_shape=jax.ShapeDtypeStruct((DEPTH, rows, 3 * D_MODEL), F32),
        grid=(DEPTH, ncol),
        in_specs=[
            pl.BlockSpec((rows, D_MODEL), lambda l, j: (0, 0)),
            pl.BlockSpec((1, D_MODEL, D_MODEL), lambda l, j: (l, 0, j)),
            pl.BlockSpec((1, 1, D_MODEL), lambda l, j: (l, 0, j)),
        ],
        out_specs=pl.BlockSpec((1, rows, D_MODEL), lambda l, j: (l, 0, j)),
        compiler_params=_params("arbitrary", "arbitrary"),
        name="adaln",
    )(c_all, ada_w, ada_b.reshape(DEPTH, 1, 3 * D_MODEL))


def _rope(x, cos, sin_signed):
    width = x.shape[-1]
    lane = lax.broadcasted_iota(jnp.int32, x.shape, 1)
    first_half = (lane % C_HD) < (C_HD // 2)
    swapped = jnp.where(first_half, pltpu.roll(x, width - C_HD // 2, 1), pltpu.roll(x, C_HD // 2, 1))
    reps = width // LANES
    cos_w = jnp.concatenate([cos] * reps, axis=-1) if reps > 1 else cos
    sin_w = jnp.concatenate([sin_signed] * reps, axis=-1) if reps > 1 else sin_signed
    return x * cos_w + swapped * sin_w


def _inproj_kernel(x_ref, mod_ref, preg_ref, w_ref, wg2_ref, bg_ref, qg_ref, kg_ref, sg_ref,
                   cos_ref, sin_ref, bd_ref, vone_ref,
                   gq_ref, gk_ref, gv_ref, gf_ref, gb_ref, fu_ref, aq_ref, kt_ref, va_ref,
                   du_ref, dvn_ref, sz_ref):
    tm = x_ref.shape[1]
    shift = mod_ref[0, :, 0:D_MODEL]
    gain = preg_ref[...] * (1.0 + mod_ref[0, :, D_MODEL:2 * D_MODEL])
    for part in range(IN_ROW_PARTS):
        rows = slice(part * tm // IN_ROW_PARTS, (part + 1) * tm // IN_ROW_PARTS)
        x = x_ref[0, rows, :]
        ms = jnp.mean(x * x, axis=-1, keepdims=True)
        hb = ((x * lax.rsqrt(ms + EPS)) * gain + shift).astype(BF16)

        def seg(off, width):
            return jnp.dot(hb, w_ref[:, off:off + width], preferred_element_type=F32)

        klr = seg(AK, C_KVW + LANES)
        q = seg(AQ, C_W)
        z = seg(ZZ, D_MIX)
        gqk = seg(GQ, 2 * A_KW)
        gv = seg(GV, A_W)
        fu = seg(FU, B_W)
        va = seg(AV, AV_W)
        du = seg(DU, D_W)
        dv = seg(DV, D_W)
        k = klr[:, :C_KVW]
        lr = klr[:, C_KVW:].astype(BF16)
        logits = jnp.dot(lr, wg2_ref[...], preferred_element_type=F32) + bg_ref[...]
        qms = jnp.dot((q * q).astype(BF16), bd_ref[...], preferred_element_type=F32)
        kms = jnp.dot((k * k).astype(BF16), bd_ref[0:C_KVW, 0:C_KVW], preferred_element_type=F32)

        gq_ref[0, rows, :] = gqk[:, :A_KW] * (A_DK ** -0.5)
        gk_ref[0, rows, :] = gqk[:, A_KW:]
        gv_ref[0, rows, :] = gv
        logg = (jnp.minimum(logits, 0.0) - jnp.log1p(jnp.exp(-jnp.abs(logits)))) * (1.0 / GLA_GATE_NORM)
        gf_ref[0, rows, :] = logg[:, :A_KW]
        gb_ref[0, rows, :] = logg[:, A_KW:]

        fu_ref[0, rows, :] = fu.astype(BF16)

        cos = cos_ref[rows, :]
        sin = sin_ref[rows, :]
        qn = q * lax.rsqrt(qms + EPS) * qg_ref[...]
        aq_ref[0, rows, :] = (_rope(qn, cos, sin) * (C_HD ** -0.5 * LOG2E)).astype(BF16)
        kn = k * lax.rsqrt(kms + EPS) * kg_ref[...]
        kt = _rope(kn, cos, sin).T.astype(BF16)
        for g in range(C_KV):
            for r in range(C_GRP):
                kt_ref[0, g, r * C_HD:(r + 1) * C_HD, rows] = kt[g * C_HD:(g + 1) * C_HD, :]
        vab = (va + vone_ref[...]).astype(BF16)
        va_ref[0, 0, rows, :] = vab[:, :LANES]
        va_ref[0, 1, rows, :] = vab[:, LANES:]

        du_ref[0, rows, :] = du.astype(BF16)
        dms = jnp.mean(dv * dv, axis=-1, keepdims=True)
        dvn_ref[0, rows, :] = (dv * lax.rsqrt(dms + EPS) * sg_ref[...]).astype(BF16)

        sz_ref[0, rows, :] = (z / (1.0 + jnp.exp(-z))).astype(BF16)


def _inproj(x, mod, lp, tabs):
    bsz, n, _ = x.shape
    tm = min(TM_IN, n)
    grid = (bsz, n // tm)
    tok = lambda w: pl.BlockSpec((1, tm, w), lambda b, i: (b, i, 0))
    out_shapes = (
        jax.ShapeDtypeStruct((bsz, n, A_KW), F32),
        jax.ShapeDtypeStruct((bsz, n, A_KW), F32),
        jax.ShapeDtypeStruct((bsz, n, A_W), F32),
        jax.ShapeDtypeStruct((bsz, n, A_KW), F32),
        jax.ShapeDtypeStruct((bsz, n, A_KW), F32),
        jax.ShapeDtypeStruct((bsz, n, B_W), BF16),
        jax.ShapeDtypeStruct((bsz, n, C_W), BF16),
        jax.ShapeDtypeStruct((bsz, C_KV, C_GRP * C_HD, n), BF16),
        jax.ShapeDtypeStruct((bsz, C_KV, n, LANES), BF16),
        jax.ShapeDtypeStruct((bsz, n, D_W), BF16),
        jax.ShapeDtypeStruct((bsz, n, D_W), BF16),
        jax.ShapeDtypeStruct((bsz, n, D_MIX), BF16),
    )
    out_specs = (
        tok(A_KW), tok(A_KW), tok(A_W), tok(A_KW), tok(A_KW), tok(B_W), tok(C_W),
        pl.BlockSpec((1, C_KV, C_GRP * C_HD, tm), lambda b, i: (b, 0, 0, i)),
        pl.BlockSpec((1, C_KV, tm, LANES), lambda b, i: (b, 0, i, 0)),
        tok(D_W), tok(D_W), tok(D_MIX),
    )
    in_specs = [
        tok(D_MODEL),
        pl.BlockSpec((1, 1, 3 * D_MODEL), lambda b, i: (b, 0, 0)),
        _full((1, D_MODEL)),
        _full((D_MODEL, W_PACKED)),
        _full((LANES, 2 * A_KW)),
        _full((1, 2 * A_KW)),
        _full((1, C_W)),
        _full((1, C_KVW)),
        _full((1, D_W)),
        pl.BlockSpec((tm, LANES), lambda b, i: (i, 0)),
        pl.BlockSpec((tm, LANES), lambda b, i: (i, 0)),
        _full((C_W, C_W)),
        _full((1, 2 * LANES)),
    ]
    return pl.pallas_call(
        _inproj_kernel, out_shape=out_shapes, grid=grid, in_specs=in_specs, out_specs=out_specs,
        compiler_params=_params("parallel", "parallel"), name="inproj",
    )(x, mod, lp["pre_g"], lp["w_in"], lp["wg2"], lp["bg"], lp["qg"], lp["kg"], lp["sg"],
      tabs["cos"], tabs["sin"], lp["bd_q"], lp["vone"])


def _head_match(shape, rows_per_head, cols_per_head):
    return (lax.broadcasted_iota(jnp.int32, shape, 0) // rows_per_head
            == lax.broadcasted_iota(jnp.int32, shape, 1) // cols_per_head)


def _gla_band_kernel(q_ref, k_ref, v_ref, gf_ref, gb_ref, o_ref,
                     qf_ref, kf_ref, vf_ref, bf_ref, cb_ref, vs_ref, os_ref):
    rows = q_ref.shape[1]
    ts = rows // GLA_SUB
    n_slab = A_W // LANES
    every = lambda p: pl.ds(p, ts, stride=GLA_SUB)
    hsum = jnp.where(_head_match((A_KW, A_W), A_DK, A_DV), 1.0, 0.0).astype(BF16)
    for h in range(n_slab):
        vs_ref[h] = v_ref[0, :, h * LANES:(h + 1) * LANES]
    run = jnp.zeros((ts, A_KW), F32)
    for p in range(GLA_SUB):
        run = run + gf_ref[0, every(p), :] * LOG2E
        bf_ref[p] = run
        qf_ref[p] = q_ref[0, every(p), :]
        kf_ref[p] = k_ref[0, every(p), :]
        vf_ref[p] = jnp.concatenate([vs_ref[h, every(p), :] for h in range(n_slab)], axis=-1)
    run = jnp.zeros((ts, A_KW), F32)
    for p in reversed(range(GLA_SUB)):
        run = run + gb_ref[0, every(p), :] * LOG2E
        cb_ref[p] = run
    for p in range(GLA_SUB):
        q = qf_ref[p]
        prods = []
        for s in range(GLA_SUB):
            qk = q * kf_ref[s]
            if s < p:
                qk = qk * jnp.exp2(bf_ref[p] - bf_ref[s])
            elif s > p:
                qk = qk * jnp.exp2(cb_ref[p] - cb_ref[s])
            prods.append(qk.astype(BF16))
        a = jnp.dot(jnp.concatenate(prods, axis=0), hsum, preferred_element_type=F32)
        acc = a[0:ts] * vf_ref[0]
        for s in range(1, GLA_SUB):
            acc = acc + a[s * ts:(s + 1) * ts] * vf_ref[s]
        for h in range(n_slab):
            os_ref[h, every(p), :] = acc[:, h * LANES:(h + 1) * LANES]
    o_ref[0] = jnp.concatenate([os_ref[h] for h in range(n_slab)], axis=-1).astype(BF16)


def _gla_band(gq, gk, gv, gf, gb):
    bsz, n, _ = gq.shape
    rows = min(GLA_BAND_TILE * GLA_SUB, n)
    ts = rows // GLA_SUB
    blk = lambda w: pl.BlockSpec((1, rows, w), lambda b, i: (b, i, 0))
    return pl.pallas_call(
        _gla_band_kernel,
        out_shape=jax.ShapeDtypeStruct((bsz, n, A_W), BF16),
        grid=(bsz, n // rows),
        in_specs=[blk(A_KW), blk(A_KW), blk(A_W), blk(A_KW), blk(A_KW)],
        out_specs=blk(A_W),
        scratch_shapes=[
            pltpu.VMEM((GLA_SUB, ts, A_KW), F32), pltpu.VMEM((GLA_SUB, ts, A_KW), F32),
            pltpu.VMEM((GLA_SUB, ts, A_W), F32),
            pltpu.VMEM((GLA_SUB, ts, A_KW), F32), pltpu.VMEM((GLA_SUB, ts, A_KW), F32),
            pltpu.VMEM((A_W // LANES, rows, LANES), F32), pltpu.VMEM((A_W // LANES, rows, LANES), F32),
        ],
        compiler_params=_params("parallel", "parallel"), name="gla_band",
    )(gq, gk, gv, gf, gb)


def _split3(x):
    hi = x.astype(BF16)
    rest = x - hi.astype(F32)
    mid = rest.astype(BF16)
    lo = (rest - mid.astype(F32)).astype(BF16)
    return hi, mid, lo


def _gla_prep(q_ref, k_ref, g_ref, qt_ref, kh_ref, dect_ref, tot_ref, pad_ref, reverse):
    seg = q_ref.shape[1]
    n_sub = seg // GLA_SUB
    r_i = lax.broadcasted_iota(jnp.int32, (GLA_GRP, GLA_GRP), 0)
    c_i = lax.broadcasted_iota(jnp.int32, (GLA_GRP, GLA_GRP), 1)
    same = (r_i // GLA_SUB) == (c_i // GLA_SUB)
    tri = (c_i >= r_i) if reverse else (c_i <= r_i)
    sums_mat = jnp.concatenate([jnp.where(same & tri, 1.0, 0.0), jnp.where(same, 1.0, 0.0)], axis=0).astype(BF16)
    for r in range(seg // GLA_GRP):
        rows = slice(r * GLA_GRP, (r + 1) * GLA_GRP)
        g3 = jnp.concatenate(_split3(g_ref[0, rows, :]), axis=-1)
        sums = jnp.dot(sums_mat, g3, preferred_element_type=F32)
        sums = sums[:, 0:A_KW] + sums[:, A_KW:2 * A_KW] + sums[:, 2 * A_KW:]
        cum, tot = sums[:GLA_GRP], sums[GLA_GRP:]
        qt_ref[rows, :] = (q_ref[0, rows, :] * jnp.exp(cum)).astype(BF16)
        kh_ref[rows, :] = (k_ref[0, rows, :] * jnp.exp(tot - cum)).astype(BF16)
        tot_ref[rows, :] = tot
    pad_ref[...] = jnp.zeros_like(pad_ref)
    pad_ref[0:n_sub, :] = tot_ref[pl.ds(0, n_sub, stride=GLA_SUB), :]
    dect_ref[...] = jnp.exp(pad_ref[...].T)


def _gla_rec_kernel(qf_ref, kf_ref, vf_ref, gf_ref, qb_ref, kb_ref, vb_ref, gb_ref, of_ref, ob_ref,
                    sf_ref, sb_ref, qtf_ref, khf_ref, dtf_ref, qtb_ref, khb_ref, dtb_ref, tot_ref, pad_ref):
    @pl.when(pl.program_id(1) == 0)
    def _():
        sf_ref[...] = jnp.zeros_like(sf_ref)
        sb_ref[...] = jnp.zeros_like(sb_ref)

    seg = qf_ref.shape[1]
    n_sub = seg // GLA_SUB
    _gla_prep(qf_ref, kf_ref, gf_ref, qtf_ref, khf_ref, dtf_ref, tot_ref, pad_ref, reverse=False)
    _gla_prep(qb_ref, kb_ref, gb_ref, qtb_ref, khb_ref, dtb_ref, tot_ref, pad_ref, reverse=True)
    smask = _head_match((A_KW, A_W), A_DK, A_DV)

    def update(i, kh_ref, v_ref):
        rows = slice(i * GLA_SUB, (i + 1) * GLA_SUB)
        upd = lax.dot_general(kh_ref[rows, :], v_ref[0, rows, :].astype(BF16), (((0,), (0,)), ((), ())),
                              preferred_element_type=F32)
        return jnp.where(smask, upd, 0.0)

    def step(i, s, upd, qt_ref, dect_ref, o_ref):
        rows = slice(i * GLA_SUB, (i + 1) * GLA_SUB)
        o_ref[0, rows, :] = jnp.dot(qt_ref[rows, :], s.astype(BF16), preferred_element_type=F32).astype(BF16)
        return s * dect_ref[:, i:i + 1] + upd

    order_f = list(range(n_sub))
    order_b = list(reversed(range(n_sub)))
    upd_f = [update(i, khf_ref, vf_ref) for i in order_f[:GLA_LOOKAHEAD]]
    upd_b = [update(i, khb_ref, vb_ref) for i in order_b[:GLA_LOOKAHEAD]]
    s_f = sf_ref[...]
    s_b = sb_ref[...]
    for j in range(n_sub):
        if j + GLA_LOOKAHEAD < n_sub:
            upd_f.append(update(order_f[j + GLA_LOOKAHEAD], khf_ref, vf_ref))
            upd_b.append(update(order_b[j + GLA_LOOKAHEAD], khb_ref, vb_ref))
        s_f = step(order_f[j], s_f, upd_f[j], qtf_ref, dtf_ref, of_ref)
        s_b = step(order_b[j], s_b, upd_b[j], qtb_ref, dtb_ref, ob_ref)
    sf_ref[...] = s_f
    sb_ref[...] = s_b


def _gla_rec(gq, gk, gv, gf, gb):
    bsz, n, _ = gq.shape
    seg = min(GLA_SEG, n)
    nseg = n // seg
    assert seg // GLA_SUB <= LANES
    fwd = lambda w: pl.BlockSpec((1, seg, w), lambda b, s: (b, s, 0))
    bwd = lambda w: pl.BlockSpec((1, seg, w), lambda b, s: (b, nseg - 1 - s, 0))
    return pl.pallas_call(
        _gla_rec_kernel,
        out_shape=(jax.ShapeDtypeStruct((bsz, n, A_W), BF16), jax.ShapeDtypeStruct((bsz, n, A_W), BF16)),
        grid=(bsz, nseg),
        in_specs=[fwd(A_KW), fwd(A_KW), fwd(A_W), fwd(A_KW), bwd(A_KW), bwd(A_KW), bwd(A_W), bwd(A_KW)],
        out_specs=(fwd(A_W), bwd(A_W)),
        scratch_shapes=[
            pltpu.VMEM((A_KW, A_W), F32), pltpu.VMEM((A_KW, A_W), F32),
            pltpu.VMEM((seg, A_KW), BF16), pltpu.VMEM((seg, A_KW), BF16), pltpu.VMEM((A_KW, LANES), F32),
            pltpu.VMEM((seg, A_KW), BF16), pltpu.VMEM((seg, A_KW), BF16), pltpu.VMEM((A_KW, LANES), F32),
            pltpu.VMEM((seg, A_KW), F32), pltpu.VMEM((LANES, A_KW), F32),
        ],
        compiler_params=_params("parallel", "arbitrary"), name="gla_rec",
    )(gq, gk, gv, gf, gq, gk, gv, gb)


def _fft1_kernel(u_ref, a1_ref, cos_ref, sin_ref, t_ref):
    n1 = u_ref.shape[1]
    nb = cos_ref.shape[0]
    t = jnp.dot(a1_ref[...], u_ref[0], preferred_element_type=F32)
    for j in range(nb):
        cols = slice(j * B_W, (j + 1) * B_W)
        tr = t[:n1, cols]
        ti = t[n1:, cols]
        c = jnp.concatenate([cos_ref[j]] * (B_W // LANES), axis=-1)
        s = jnp.concatenate([sin_ref[j]] * (B_W // LANES), axis=-1)
        t_ref[0, 0, :, cols] = (tr * c + ti * s).astype(BF16)
        t_ref[0, 1, :, cols] = (ti * c - tr * s).astype(BF16)


def _fft2_kernel(t_ref, l_ref, cc_ref, cs_ref, w_ref, o_ref):
    n2 = t_ref.shape[3]
    half = FFT_K1 * n2
    groups = range(t_ref.shape[2] // FFT_K1)
    xs = []
    for grp in groups:
        k1s = slice(grp * FFT_K1, (grp + 1) * FFT_K1)
        xs.append(jnp.concatenate([t_ref[0, 0, k1s].reshape(half, B_W), t_ref[0, 1, k1s].reshape(half, B_W)],
                                  axis=0))
    gs = [jnp.dot(l_ref[...], x, preferred_element_type=F32) for x in xs]
    mixed = [jnp.dot(g[:half].astype(BF16), cc_ref[...], preferred_element_type=F32)
             + jnp.dot(g[half:].astype(BF16), cs_ref[...], preferred_element_type=F32) for g in gs]
    ys = [jnp.dot(m.astype(BF16), w_ref[...], preferred_element_type=F32) for m in mixed]
    o_ref[0] = jnp.concatenate([y.reshape(n2, FFT_K1, B_W) for y in ys], axis=1).astype(BF16)


def _fnet(fu, lp, tabs):
    bsz, n, _ = fu.shape
    n2 = FFT_N2
    n1 = n // n2
    nb = FFT_NB
    k1b = FFT_K1 * FFT_K1_GROUPS
    u2 = fu.reshape(bsz, n1, n2 * B_W)
    tt = pl.pallas_call(
        _fft1_kernel,
        out_shape=jax.ShapeDtypeStruct((bsz, 2, n1, n2 * B_W), BF16),
        grid=(n2 // nb, bsz),
        in_specs=[
            pl.BlockSpec((1, n1, nb * B_W), lambda j, b: (b, 0, j)),
            _full((2 * n1, n1)),
            pl.BlockSpec((nb, n1, LANES), lambda j, b: (j, 0, 0)),
            pl.BlockSpec((nb, n1, LANES), lambda j, b: (j, 0, 0)),
        ],
        out_specs=pl.BlockSpec((1, 2, n1, nb * B_W), lambda j, b: (b, 0, 0, j)),
        compiler_params=_params("parallel", "parallel"), name="fft_pass1",
    )(u2, tabs["a1"], tabs["tw_cos"], tabs["tw_sin"])
    t5 = tt.reshape(bsz, 2, n1, n2, B_W)
    y = pl.pallas_call(
        _fft2_kernel,
        out_shape=jax.ShapeDtypeStruct((bsz, n2, n1, B_W), BF16),
        grid=(bsz, n1 // k1b),
        in_specs=[
            pl.BlockSpec((1, 2, k1b, n2, B_W), lambda b, i: (b, 0, i, 0, 0)),
            _full((2 * FFT_K1 * n2, 2 * FFT_K1 * n2)),
            _full((B_W, B_W)), _full((B_W, B_W)), _full((B_W, B_W)),
        ],
        out_specs=pl.BlockSpec((1, n2, k1b, B_W), lambda b, i: (b, 0, i, 0)),
        compiler_params=_params("parallel", "parallel"), name="fft_pass2",
    )(t5, tabs["l2"], tabs["chan_cos"], tabs["chan_sin"], lp["fnet_w"])
    return y.reshape(bsz, n, B_W)


def _attn_kernel(q_ref, kt_ref, va_ref, o_ref, qm_ref, m_ref, acc_ref):
    tq = q_ref.shape[1]
    n = kt_ref.shape[3]
    tk = min(ATT_TK, n)
    q = q_ref[0]
    lane = lax.broadcasted_iota(jnp.int32, q.shape, 1)
    for h in range(C_GRP):
        qm_ref[h * tq:(h + 1) * tq, :] = jnp.where(lane // C_HD == h, q, jnp.zeros_like(q))
    m_ref[...] = jnp.full(m_ref.shape, -jnp.inf, F32)
    acc_ref[...] = jnp.zeros(acc_ref.shape, F32)

    def scores(c):
        c0 = pl.multiple_of(c * tk, tk)
        return jnp.dot(qm_ref[...], kt_ref[0, 0, :, pl.ds(c0, tk)], preferred_element_type=F32)

    def softmax_accumulate(c, s):
        c0 = pl.multiple_of(c * tk, tk)
        v = va_ref[0, 0, pl.ds(c0, tk), :]
        s_max = s[:, 0:LANES]
        for t in range(1, tk // LANES):
            s_max = jnp.maximum(s_max, s[:, t * LANES:(t + 1) * LANES])
        row_max = jnp.max(s_max, axis=-1, keepdims=True)
        m_old = m_ref[...]
        m_new = jnp.maximum(m_old, row_max)
        p = jnp.exp2(s - jnp.concatenate([m_new] * (tk // LANES), axis=-1))
        alpha = jnp.exp2(m_old - m_new)
        acc_ref[...] = alpha * acc_ref[...] + jnp.dot(p.astype(BF16), v, preferred_element_type=F32)
        m_ref[...] = m_new

    def step(c, carry):
        softmax_accumulate(c, scores(c))
        return carry

    lax.fori_loop(0, n // tk, step, 0, unroll=ATT_UNROLL)
    acc = acc_ref[...]
    inv = 1.0 / acc[:, ONES_LANE:ONES_LANE + 1]
    outs = [(acc[h * tq:(h + 1) * tq, 0:C_HD] * inv[h * tq:(h + 1) * tq]) for h in range(C_GRP)]
    o_ref[0] = jnp.concatenate(outs, axis=-1).astype(BF16)


def _attention(aq, kt4, va):
    bsz, n, _ = aq.shape
    tq = min(ATT_TQ, n)
    return pl.pallas_call(
        _attn_kernel,
        out_shape=jax.ShapeDtypeStruct((bsz, n, C_W), BF16),
        grid=(bsz, C_KV, n // tq),
        in_specs=[
            pl.BlockSpec((1, tq, C_GRP * C_HD), lambda b, g, i: (b, i, g)),
            pl.BlockSpec((1, 1, C_GRP * C_HD, n), lambda b, g, i: (b, g, 0, 0)),
            pl.BlockSpec((1, 1, n, LANES), lambda b, g, i: (b, g, 0, 0)),
        ],
        out_specs=pl.BlockSpec((1, tq, C_GRP * C_HD), lambda b, g, i: (b, i, g)),
        scratch_shapes=[
            pltpu.VMEM((C_GRP * tq, C_GRP * C_HD), BF16),
            pltpu.VMEM((C_GRP * tq, LANES), F32),
            pltpu.VMEM((C_GRP * tq, LANES), F32),
        ],
        compiler_params=_params("parallel", "parallel", "arbitrary"), name="attention",
    )(aq, kt4, va)


def _outproj_kernel(x_ref, mod_ref, oa_ref, of_ref, ob_ref, yb_ref, ao_ref, du_ref, dvn_ref, sz_ref,
                    w_ref, postg_ref, og_ref, bd_ref, sw_ref, sb_ref, o_ref):
    tm = x_ref.shape[1]
    sz = sz_ref[0].astype(F32)

    o = oa_ref[0].astype(F32) + of_ref[0].astype(F32) + ob_ref[0].astype(F32)
    oms = jnp.dot((o * o).astype(BF16), bd_ref[...], preferred_element_type=F32)
    out_a = o * lax.rsqrt(oms + EPS) * og_ref[...]
    y = jnp.dot((out_a * sz[:, 0:A_W]).astype(BF16), w_ref[0:A_W, :], preferred_element_type=F32)

    m_b = (yb_ref[0].astype(F32) * sz[:, A_W:A_W + B_W]).astype(BF16)
    y = y + jnp.dot(m_b, w_ref[A_W:A_W + B_W, :], preferred_element_type=F32)

    m_c = (ao_ref[0].astype(F32) * sz[:, A_W + B_W:A_W + B_W + C_W]).astype(BF16)
    y = y + jnp.dot(m_c, w_ref[A_W + B_W:A_W + B_W + C_W, :], preferred_element_type=F32)

    lane = lax.broadcasted_iota(jnp.int32, (SGU_CHUNK, D_W), 1)
    parts = []
    for c in range(tm // SGU_CHUNK):
        rows = slice(c * SGU_CHUNK, (c + 1) * SGU_CHUNK)
        r = jnp.dot(sw_ref[...], dvn_ref[0, rows, :], preferred_element_type=F32)
        mix = sb_ref[...]
        for g in range(D_GROUPS):
            mix = mix + jnp.where(lane // D_GC == g, r[g * SGU_CHUNK:(g + 1) * SGU_CHUNK, :], 0.0)
        parts.append(du_ref[0, rows, :].astype(F32) * mix)
    out_d = jnp.concatenate(parts, axis=0)
    m_d = (out_d * sz[:, A_W + B_W + C_W:]).astype(BF16)
    y = y + jnp.dot(m_d, w_ref[A_W + B_W + C_W:, :], preferred_element_type=F32)

    yms = jnp.mean(y * y, axis=-1, keepdims=True)
    yn = y * lax.rsqrt(yms + EPS) * postg_ref[...]
    gate = mod_ref[0, :, 2 * D_MODEL:3 * D_MODEL]
    o_ref[0] = x_ref[0] + gate * yn


def _outproj(x, mod, oa, of, ob, yb, ao, du, dvn, sz, lp):
    bsz, n, _ = x.shape
    tm = min(TM_OUT, n)
    tok = lambda w: pl.BlockSpec((1, tm, w), lambda b, i: (b, i, 0))
    return pl.pallas_call(
        _outproj_kernel,
        out_shape=jax.ShapeDtypeStruct((bsz, n, D_MODEL), F32),
        grid=(bsz, n // tm),
        in_specs=[
            tok(D_MODEL),
            pl.BlockSpec((1, 1, 3 * D_MODEL), lambda b, i: (b, 0, 0)),
            tok(A_W), tok(A_W), tok(A_W), tok(B_W), tok(C_W), tok(D_W), tok(D_W), tok(D_MIX),
            _full((D_MIX, D_MODEL)), _full((1, D_MODEL)), _full((1, A_W)), _full((A_W, A_W)),
            _full((D_GROUPS * SGU_CHUNK, SGU_CHUNK)), _full((SGU_CHUNK, D_W)),
        ],
        out_specs=tok(D_MODEL),
        compiler_params=_params("parallel", "parallel"), name="outproj",
    )(x, mod, oa, of, ob, yb, ao, du, dvn, sz,
      lp["w_out"], lp["post_g"], lp["og"], lp["bd_a"], lp["sgu_w"], lp["sgu_b"])


def _pack_w_in(w):
    sizes = [A_KW, A_KW, A_W, 2 * GLA_RANK, B_W, C_W, C_KVW, C_KVW, D_W, D_W, D_MIX]
    off = [0] + [int(o) for o in np.cumsum(sizes)]
    a_q, a_k, a_v, a_lr, b_u, c_q, c_k, c_v, d_u, d_v, z = [w[:, off[i]:off[i + 1]] for i in range(len(sizes))]

    def deinterleave(x, heads):
        r = x.reshape(D_MODEL, heads, C_HD // 2, 2)
        return jnp.concatenate([r[..., 0], r[..., 1]], axis=-1).reshape(D_MODEL, heads * C_HD)

    zeros = lambda width: jnp.zeros((D_MODEL, width), w.dtype)
    pieces = [a_q, a_k, a_v, b_u, deinterleave(c_q, C_HEADS), deinterleave(c_k, C_KV),
              a_lr, zeros(LANES - 2 * GLA_RANK)]
    for h in range(C_KV):
        pieces += [c_v[:, h * C_HD:(h + 1) * C_HD], zeros(LANES - C_HD)]
    pieces += [d_u, d_v, z]
    packed = jnp.concatenate(pieces, axis=1)
    assert packed.shape[1] == W_PACKED
    return packed.astype(BF16)


def _block_diag_mean(width, group):
    idx = np.arange(width) // group
    return jnp.asarray((idx[:, None] == idx[None, :]).astype(np.float32) / group, BF16)


def _layer_params(l, ada_unused, norm_pre_g, norm_post_g, w_in, gla_wg2_f, gla_bg_f, gla_wg2_b, gla_bg_b,
                  gla_onorm_g, fnet_w, q_norm_g, k_norm_g, sgu_norm_g, sgu_w, sgu_b, w_out):
    deint = np.concatenate([np.arange(0, C_HD, 2), np.arange(1, C_HD, 2)])
    w = _pack_w_in(w_in[l])
    wg2 = jnp.zeros((LANES, 2 * A_KW), F32)
    wg2 = wg2.at[0:GLA_RANK, 0:A_KW].set(gla_wg2_f[l])
    wg2 = wg2.at[GLA_RANK:2 * GLA_RANK, A_KW:].set(gla_wg2_b[l])
    vone = np.zeros((1, 2 * LANES), np.float32)
    vone[0, ONES_LANE] = 1.0
    vone[0, LANES + ONES_LANE] = 1.0
    return {
        "pre_g": norm_pre_g[l].reshape(1, D_MODEL),
        "post_g": norm_post_g[l].reshape(1, D_MODEL),
        "w_in": w,
        "wg2": wg2.astype(BF16),
        "bg": jnp.concatenate([gla_bg_f[l], gla_bg_b[l]]).reshape(1, 2 * A_KW),
        "qg": jnp.tile(q_norm_g[l][deint], C_HEADS).reshape(1, C_W),
        "kg": jnp.tile(k_norm_g[l][deint], C_KV).reshape(1, C_KVW),
        "sg": sgu_norm_g[l].reshape(1, D_W),
        "bd_q": _block_diag_mean(C_W, C_HD),
        "bd_a": _block_diag_mean(A_W, A_DV),
        "vone": jnp.asarray(vone),
        "og": jnp.tile(gla_onorm_g[l], A_HEADS).reshape(1, A_W),
        "fnet_w": fnet_w[l].astype(BF16),
        "sgu_w": sgu_w[l].reshape(D_GROUPS * SGU_CHUNK, SGU_CHUNK).astype(BF16),
        "sgu_b": jnp.repeat(sgu_b[l].T, D_GC, axis=1),
        "w_out": w_out[l].astype(BF16),
    }


def _seq_tables(n):
    rows = n // GRID_W
    row = jnp.repeat(jnp.arange(rows, dtype=F32), GRID_W)
    col = jnp.tile(jnp.arange(GRID_W, dtype=F32), rows)
    rope_axis = C_HD // 2
    freqs = ROPE_THETA ** (-jnp.arange(0, rope_axis, 2, dtype=F32) / rope_axis)
    ang = jnp.concatenate([row[:, None] * freqs, col[:, None] * freqs], axis=-1)
    cos, sin = jnp.cos(ang), jnp.sin(ang)
    cos_t = jnp.tile(jnp.concatenate([cos, cos], axis=-1), (1, LANES // C_HD))
    sin_t = jnp.tile(jnp.concatenate([-sin, sin], axis=-1), (1, LANES // C_HD))

    n2 = FFT_N2
    n1 = n // n2

    def dft_angles(rows, cols, period):
        prod = (jnp.arange(rows, dtype=jnp.int32)[:, None] * jnp.arange(cols, dtype=jnp.int32)[None, :]) % period
        return prod.astype(F32) * (2.0 * math.pi / period)

    a1_ang = dft_angles(n1, n1, n1)
    a1 = jnp.concatenate([jnp.cos(a1_ang), -jnp.sin(a1_ang)], axis=0)
    tw_ang = dft_angles(n2, n1, n)
    scale = 1.0 / math.sqrt(n * B_GC)
    tw_cos = jnp.broadcast_to((jnp.cos(tw_ang) * scale)[:, :, None], (n2, n1, LANES))
    tw_sin = jnp.broadcast_to((jnp.sin(tw_ang) * scale)[:, :, None], (n2, n1, LANES))
    ang2 = dft_angles(n2, n2, n2)
    c2, s2 = jnp.cos(ang2), jnp.sin(ang2)
    eye = jnp.eye(FFT_K1, dtype=F32)
    blk = lambda m: (m[:, None, None, :] * eye[None, :, :, None]).reshape(n2 * FFT_K1, FFT_K1 * n2)
    l2 = jnp.block([[blk(c2), blk(s2)], [blk(-s2), blk(c2)]])
    angc = dft_angles(B_GC, B_GC, B_GC)
    grp = jnp.eye(B_GROUPS, dtype=F32)
    kron = lambda m: (grp[:, None, :, None] * m[None, :, None, :]).reshape(B_W, B_W)
    chan_cos = kron(jnp.cos(angc))
    chan_sin = kron(jnp.sin(angc))
    return {
        "cos": cos_t, "sin": sin_t,
        "a1": a1.astype(BF16),
        "tw_cos": tw_cos, "tw_sin": tw_sin,
        "l2": l2.astype(BF16),
        "chan_cos": chan_cos.astype(BF16), "chan_sin": chan_sin.astype(BF16),
    }


def _layer(x, mod, lp, tabs):
    gq, gk, gv, gf, gb, fu, aq, kt4, va, du, dvn, sz = _inproj(x, mod, lp, tabs)
    oa = _gla_band(gq, gk, gv, gf, gb)
    of, ob = _gla_rec(gq, gk, gv, gf, gb)
    yb = _fnet(fu, lp, tabs)
    ao = _attention(aq, kt4, va)
    return _outproj(x, mod, oa, of, ob, yb, ao, du, dvn, sz, lp)


def kernel(x_prompt, x_sample, c_prompt, c_sample, ada_w, ada_b, norm_pre_g, norm_post_g, w_in,
           gla_wg2_f, gla_bg_f, gla_wg2_b, gla_bg_b, gla_onorm_g, fnet_w, q_norm_g, k_norm_g,
           sgu_norm_g, sgu_w, sgu_b, w_out):
    bp, bs = x_prompt.shape[0], x_sample.shape[0]
    pad_rows = (-(bp + bs)) % 8
    c_all = jnp.concatenate([c_prompt, c_sample, jnp.zeros((pad_rows, D_MODEL), F32)], axis=0)
    mod = _adaln(c_all, ada_w, ada_b)
    tabs_p = _seq_tables(x_prompt.shape[1])
    tabs_s = _seq_tables(x_sample.shape[1])
    y_prompt, y_sample = x_prompt, x_sample
    for l in range(DEPTH):
        lp = _layer_params(l, None, norm_pre_g, norm_post_g, w_in, gla_wg2_f, gla_bg_f, gla_wg2_b,
                           gla_bg_b, gla_onorm_g, fnet_w, q_norm_g, k_norm_g, sgu_norm_g, sgu_w,
                           sgu_b, w_out)
        mod_p = mod[l, 0:bp].reshape(bp, 1, 3 * D_MODEL)
        mod_s = mod[l, bp:bp + bs].reshape(bs, 1, 3 * D_MODEL)
        y_prompt = _layer(y_prompt, mod_p, lp, tabs_p)
        y_sample = _layer(y_sample, mod_s, lp, tabs_s)
    return (y_prompt, y_sample)
```

```python
import math

import numpy as np
import jax
import jax.numpy as jnp
from jax import lax
from jax.experimental import pallas as pl
from jax.experimental.pallas import tpu as pltpu

F32 = jnp.float32
BF16 = jnp.bfloat16
HIGHEST = lax.Precision.HIGHEST

D_MODEL = 1024
DEPTH = 2
GRID_W = 64
A_HEADS, A_DK, A_DV = 4, 32, 64
A_W, A_KW = A_HEADS * A_DV, A_HEADS * A_DK
GLA_RANK = 16
GLA_GATE_NORM = 16.0
B_GROUPS, B_GC = 4, 64
B_W = B_GROUPS * B_GC
C_HEADS, C_KV, C_HD = 8, 2, 64
C_GRP = C_HEADS // C_KV
C_W, C_KVW = C_HEADS * C_HD, C_KV * C_HD
ROPE_THETA = 10000.0
D_GROUPS, D_GC = 4, 64
D_W = D_GROUPS * D_GC
SGU_CHUNK = 128
D_MIX = A_W + B_W + C_W + D_W
EPS = 1e-6
LOG2E = 1.4426950408889634

LANES = 128
VMEM_LIMIT_BYTES = 56 * 1024 * 1024

GQ, GK, GV, FU, AQ, AK, LR, AV, DU, DV, ZZ = 0, 128, 256, 512, 768, 1280, 1408, 1536, 1792, 2048, 2304
W_PACKED = 3584
AV_W = 2 * LANES
ONES_LANE = C_HD

TM_IN = 512
IN_ROW_PARTS = 2
TM_OUT = 1024
GLA_SUB = 16
GLA_BAND_TILE = 64
GLA_GRP = 128
GLA_SEG = 512
GLA_LOOKAHEAD = 3
FFT_N2 = 64
FFT_NB = 16
FFT_K1 = 8
FFT_K1_GROUPS = 2
ATT_TQ = 512
ATT_TK = 2048
ATT_UNROLL = 2


def _params(*sem):
    return pltpu.CompilerParams(dimension_semantics=sem, vmem_limit_bytes=VMEM_LIMIT_BYTES)


def _full(shape):
    n = len(shape)
    return pl.BlockSpec(shape, lambda *_: (0,) * n)


def _adaln_kernel(c_ref, w_ref, b_ref, o_ref):
    c = c_ref[...]
    sc = c / (1.0 + jnp.exp(-c))
    o_ref[0] = jnp.dot(sc, w_ref[0], precision=HIGHEST, preferred_element_type=F32) + b_ref[0]


def _adaln(c_all, ada_w, ada_b):
    rows = c_all.shape[0]
    ncol = 3 * D_MODEL // D_MODEL
    return pl.pallas_call(
        _adaln_kernel,
        out_shape=jax.ShapeDtypeStruct((DEPTH, rows, 3 * D_MODEL), F32),
        grid=(DEPTH, ncol),
        in_specs=[
            pl.BlockSpec((rows, D_MODEL), lambda l, j: (0, 0)),
            pl.BlockSpec((1, D_MODEL, D_MODEL), lambda l, j: (l, 0, j)),
            pl.BlockSpec((1, 1, D_MODEL), lambda l, j: (l, 0, j)),
        ],
        out_specs=pl.BlockSpec((1, rows, D_MODEL), lambda l, j: (l, 0, j)),
        compiler_params=_params("arbitrary", "arbitrary"),
        name="adaln",
    )(c_all, ada_w, ada_b.reshape(DEPTH, 1, 3 * D_MODEL))


def _rope(x, cos, sin_signed):
    width = x.shape[-1]
    lane = lax.broadcasted_iota(jnp.int32, x.shape, 1)
    first_half = (lane % C_HD) < (C_HD // 2)
    swapped = jnp.where(first_half, pltpu.roll(x, width - C_HD // 2, 1), pltpu.roll(x, C_HD // 2, 1))
    reps = width // LANES
    cos_w = jnp.concatenate([cos] * reps, axis=-1) if reps > 1 else cos
    sin_w = jnp.concatenate([sin_signed] * reps, axis=-1) if reps > 1 else sin_signed
    return x * cos_w + swapped * sin_w


def _inproj_kernel(x_ref, mod_ref, preg_ref, w_ref, wg2_ref, bg_ref, qg_ref, kg_ref, sg_ref,
                   cos_ref, sin_ref, bd_ref, vone_ref,
                   gq_ref, gk_ref, gv_ref, gf_ref, gb_ref, fu_ref, aq_ref, kt_ref, va_ref,
                   du_ref, dvn_ref, sz_ref):
    tm = x_ref.shape[1]
    shift = mod_ref[0, :, 0:D_MODEL]
    gain = preg_ref[...] * (1.0 + mod_ref[0, :, D_MODEL:2 * D_MODEL])
    for part in range(IN_ROW_PARTS):
        rows = slice(part * tm // IN_ROW_PARTS, (part + 1) * tm // IN_ROW_PARTS)
        x = x_ref[0, rows, :]
        ms = jnp.mean(x * x, axis=-1, keepdims=True)
        hb = ((x * lax.rsqrt(ms + EPS)) * gain + shift).astype(BF16)

        def seg(off, width):
            return jnp.dot(hb, w_ref[:, off:off + width], preferred_element_type=F32)

        klr = seg(AK, C_KVW + LANES)
        q = seg(AQ, C_W)
        z = seg(ZZ, D_MIX)
        gqk = seg(GQ, 2 * A_KW)
        gv = seg(GV, A_W)
        fu = seg(FU, B_W)
        va = seg(AV, AV_W)
        du = seg(DU, D_W)
        dv = seg(DV, D_W)
        k = klr[:, :C_KVW]
        lr = klr[:, C_KVW:].astype(BF16)
        logits = jnp.dot(lr, wg2_ref[...], preferred_element_type=F32) + bg_ref[...]
        qms = jnp.dot((q * q).astype(BF16), bd_ref[...], preferred_element_type=F32)
        kms = jnp.dot((k * k).astype(BF16), bd_ref[0:C_KVW, 0:C_KVW], preferred_element_type=F32)

        gq_ref[0, rows, :] = gqk[:, :A_KW] * (A_DK ** -0.5)
        gk_ref[0, rows, :] = gqk[:, A_KW:]
        gv_ref[0, rows, :] = gv
        logg = (jnp.minimum(logits, 0.0) - jnp.log1p(jnp.exp(-jnp.abs(logits)))) * (1.0 / GLA_GATE_NORM)
        gf_ref[0, rows, :] = logg[:, :A_KW]
        gb_ref[0, rows, :] = logg[:, A_KW:]

        fu_ref[0, rows, :] = fu.astype(BF16)

        cos = cos_ref[rows, :]
        sin = sin_ref[rows, :]
        qn = q * lax.rsqrt(qms + EPS) * qg_ref[...]
        aq_ref[0, rows, :] = (_rope(qn, cos, sin) * (C_HD ** -0.5 * LOG2E)).astype(BF16)
        kn = k * lax.rsqrt(kms + EPS) * kg_ref[...]
        kt = _rope(kn, cos, sin).T.astype(BF16)
        for g in range(C_KV):
            for r in range(C_GRP):
                kt_ref[0, g, r * C_HD:(r + 1) * C_HD, rows] = kt[g * C_HD:(g + 1) * C_HD, :]
        vab = (va + vone_ref[...]).astype(BF16)
        va_ref[0, 0, rows, :] = vab[:, :LANES]
        va_ref[0, 1, rows, :] = vab[:, LANES:]

        du_ref[0, rows, :] = du.astype(BF16)
        dms = jnp.mean(dv * dv, axis=-1, keepdims=True)
        dvn_ref[0, rows, :] = (dv * lax.rsqrt(dms + EPS) * sg_ref[...]).astype(BF16)

        sz_ref[0, rows, :] = (z / (1.0 + jnp.exp(-z))).astype(BF16)


def _inproj(x, mod, lp, tabs):
    bsz, n, _ = x.shape
    tm = min(TM_IN, n)
    grid = (bsz, n // tm)
    tok = lambda w: pl.BlockSpec((1, tm, w), lambda b, i: (b, i, 0))
    out_shapes = (
        jax.ShapeDtypeStruct((bsz, n, A_KW), F32),
        jax.ShapeDtypeStruct((bsz, n, A_KW), F32),
        jax.ShapeDtypeStruct((bsz, n, A_W), F32),
        jax.ShapeDtypeStruct((bsz, n, A_KW), F32),
        jax.ShapeDtypeStruct((bsz, n, A_KW), F32),
        jax.ShapeDtypeStruct((bsz, n, B_W), BF16),
        jax.ShapeDtypeStruct((bsz, n, C_W), BF16),
        jax.ShapeDtypeStruct((bsz, C_KV, C_GRP * C_HD, n), BF16),
        jax.ShapeDtypeStruct((bsz, C_KV, n, LANES), BF16),
        jax.ShapeDtypeStruct((bsz, n, D_W), BF16),
        jax.ShapeDtypeStruct((bsz, n, D_W), BF16),
        jax.ShapeDtypeStruct((bsz, n, D_MIX), BF16),
    )
    out_specs = (
        tok(A_KW), tok(A_KW), tok(A_W), tok(A_KW), tok(A_KW), tok(B_W), tok(C_W),
        pl.BlockSpec((1, C_KV, C_GRP * C_HD, tm), lambda b, i: (b, 0, 0, i)),
        pl.BlockSpec((1, C_KV, tm, LANES), lambda b, i: (b, 0, i, 0)),
        tok(D_W), tok(D_W), tok(D_MIX),
    )
    in_specs = [
        tok(D_MODEL),
        pl.BlockSpec((1, 1, 3 * D_MODEL), lambda b, i: (b, 0, 0)),
        _full((1, D_MODEL)),
        _full((D_MODEL, W_PACKED)),
        _full((LANES, 2 * A_KW)),
        _full((1, 2 * A_KW)),
        _full((1, C_W)),
        _full((1, C_KVW)),
        _full((1, D_W)),
        pl.BlockSpec((tm, LANES), lambda b, i: (i, 0)),
        pl.BlockSpec((tm, LANES), lambda b, i: (i, 0)),
        _full((C_W, C_W)),
        _full((1, 2 * LANES)),
    ]
    return pl.pallas_call(
        _inproj_kernel, out_shape=out_shapes, grid=grid, in_specs=in_specs, out_specs=out_specs,
        compiler_params=_params("parallel", "parallel"), name="inproj",
    )(x, mod, lp["pre_g"], lp["w_in"], lp["wg2"], lp["bg"], lp["qg"], lp["kg"], lp["sg"],
      tabs["cos"], tabs["sin"], lp["bd_q"], lp["vone"])


def _head_match(shape, rows_per_head, cols_per_head):
    return (lax.broadcasted_iota(jnp.int32, shape, 0) // rows_per_head
            == lax.broadcasted_iota(jnp.int32, shape, 1) // cols_per_head)


def _gla_band_kernel(q_ref, k_ref, v_ref, gf_ref, gb_ref, o_ref,
                     qf_ref, kf_ref, vf_ref, bf_ref, cb_ref, vs_ref, os_ref):
    rows = q_ref.shape[1]
    ts = rows // GLA_SUB
    n_slab = A_W // LANES
    every = lambda p: pl.ds(p, ts, stride=GLA_SUB)
    hsum = jnp.where(_head_match((A_KW, A_W), A_DK, A_DV), 1.0, 0.0).astype(BF16)
    for h in range(n_slab):
        vs_ref[h] = v_ref[0, :, h * LANES:(h + 1) * LANES]
    run = jnp.zeros((ts, A_KW), F32)
    for p in range(GLA_SUB):
        run = run + gf_ref[0, every(p), :] * LOG2E
        bf_ref[p] = run
        qf_ref[p] = q_ref[0, every(p), :]
        kf_ref[p] = k_ref[0, every(p), :]
        vf_ref[p] = jnp.concatenate([vs_ref[h, every(p), :] for h in range(n_slab)], axis=-1)
    run = jnp.zeros((ts, A_KW), F32)
    for p in reversed(range(GLA_SUB)):
        run = run + gb_ref[0, every(p), :] * LOG2E
        cb_ref[p] = run
    for p in range(GLA_SUB):
        q = qf_ref[p]
        prods = []
        for s in range(GLA_SUB):
            qk = q * kf_ref[s]
            if s < p:
                qk = qk * jnp.exp2(bf_ref[p] - bf_ref[s])
            elif s > p:
                qk = qk * jnp.exp2(cb_ref[p] - cb_ref[s])
            prods.append(qk.astype(BF16))
        a = jnp.dot(jnp.concatenate(prods, axis=0), hsum, preferred_element_type=F32)
        acc = a[0:ts] * vf_ref[0]
        for s in range(1, GLA_SUB):
            acc = acc + a[s * ts:(s + 1) * ts] * vf_ref[s]
        for h in range(n_slab):
            os_ref[h, every(p), :] = acc[:, h * LANES:(h + 1) * LANES]
    o_ref[0] = jnp.concatenate([os_ref[h] for h in range(n_slab)], axis=-1).astype(BF16)


def _gla_band(gq, gk, gv, gf, gb):
    bsz, n, _ = gq.shape
    rows = min(GLA_BAND_TILE * GLA_SUB, n)
    ts = rows // GLA_SUB
    blk = lambda w: pl.BlockSpec((1, rows, w), lambda b, i: (b, i, 0))
    return pl.pallas_call(
        _gla_band_kernel,
        out_shape=jax.ShapeDtypeStruct((bsz, n, A_W), BF16),
        grid=(bsz, n // rows),
        in_specs=[blk(A_KW), blk(A_KW), blk(A_W), blk(A_KW), blk(A_KW)],
        out_specs=blk(A_W),
        scratch_shapes=[
            pltpu.VMEM((GLA_SUB, ts, A_KW), F32), pltpu.VMEM((GLA_SUB, ts, A_KW), F32),
            pltpu.VMEM((GLA_SUB, ts, A_W), F32),
            pltpu.VMEM((GLA_SUB, ts, A_KW), F32), pltpu.VMEM((GLA_SUB, ts, A_KW), F32),
            pltpu.VMEM((A_W // LANES, rows, LANES), F32), pltpu.VMEM((A_W // LANES, rows, LANES), F32),
        ],
        compiler_params=_params("parallel", "parallel"), name="gla_band",
    )(gq, gk, gv, gf, gb)


def _split3(x):
    hi = x.astype(BF16)
    rest = x - hi.astype(F32)
    mid = rest.astype(BF16)
    lo = (rest - mid.astype(F32)).astype(BF16)
    return hi, mid, lo


def _gla_prep(q_ref, k_ref, g_ref, qt_ref, kh_ref, dect_ref, tot_ref, pad_ref, reverse):
    seg = q_ref.shape[1]
    n_sub = seg // GLA_SUB
    r_i = lax.broadcasted_iota(jnp.int32, (GLA_GRP, GLA_GRP), 0)
    c_i = lax.broadcasted_iota(jnp.int32, (GLA_GRP, GLA_GRP), 1)
    same = (r_i // GLA_SUB) == (c_i // GLA_SUB)
    tri = (c_i >= r_i) if reverse else (c_i <= r_i)
    sums_mat = jnp.concatenate([jnp.where(same & tri, 1.0, 0.0), jnp.where(same, 1.0, 0.0)], axis=0).astype(BF16)
    for r in range(seg // GLA_GRP):
        rows = slice(r * GLA_GRP, (r + 1) * GLA_GRP)
        g3 = jnp.concatenate(_split3(g_ref[0, rows, :]), axis=-1)
        sums = jnp.dot(sums_mat, g3, preferred_element_type=F32)
        sums = sums[:, 0:A_KW] + sums[:, A_KW:2 * A_KW] + sums[:, 2 * A_KW:]
        cum, tot = sums[:GLA_GRP], sums[GLA_GRP:]
        qt_ref[rows, :] = (q_ref[0, rows, :] * jnp.exp(cum)).astype(BF16)
        kh_ref[rows, :] = (k_ref[0, rows, :] * jnp.exp(tot - cum)).astype(BF16)
        tot_ref[rows, :] = tot
    pad_ref[...] = jnp.zeros_like(pad_ref)
    pad_ref[0:n_sub, :] = tot_ref[pl.ds(0, n_sub, stride=GLA_SUB), :]
    dect_ref[...] = jnp.exp(pad_ref[...].T)


def _gla_rec_kernel(qf_ref, kf_ref, vf_ref, gf_ref, qb_ref, kb_ref, vb_ref, gb_ref, of_ref, ob_ref,
                    sf_ref, sb_ref, qtf_ref, khf_ref, dtf_ref, qtb_ref, khb_ref, dtb_ref, tot_ref, pad_ref):
    @pl.when(pl.program_id(1) == 0)
    def _():
        sf_ref[...] = jnp.zeros_like(sf_ref)
        sb_ref[...] = jnp.zeros_like(sb_ref)

    seg = qf_ref.shape[1]
    n_sub = seg // GLA_SUB
    _gla_prep(qf_ref, kf_ref, gf_ref, qtf_ref, khf_ref, dtf_ref, tot_ref, pad_ref, reverse=False)
    _gla_prep(qb_ref, kb_ref, gb_ref, qtb_ref, khb_ref, dtb_ref, tot_ref, pad_ref, reverse=True)
    smask = _head_match((A_KW, A_W), A_DK, A_DV)

    def update(i, kh_ref, v_ref):
        rows = slice(i * GLA_SUB, (i + 1) * GLA_SUB)
        upd = lax.dot_general(kh_ref[rows, :], v_ref[0, rows, :].astype(BF16), (((0,), (0,)), ((), ())),
                              preferred_element_type=F32)
        return jnp.where(smask, upd, 0.0)

    def step(i, s, upd, qt_ref, dect_ref, o_ref):
        rows = slice(i * GLA_SUB, (i + 1) * GLA_SUB)
        o_ref[0, rows, :] = jnp.dot(qt_ref[rows, :], s.astype(BF16), preferred_element_type=F32).astype(BF16)
        return s * dect_ref[:, i:i + 1] + upd

    order_f = list(range(n_sub))
    order_b = list(reversed(range(n_sub)))
    upd_f = [update(i, khf_ref, vf_ref) for i in order_f[:GLA_LOOKAHEAD]]
    upd_b = [update(i, khb_ref, vb_ref) for i in order_b[:GLA_LOOKAHEAD]]
    s_f = sf_ref[...]
    s_b = sb_ref[...]
    for j in range(n_sub):
        if j + GLA_LOOKAHEAD < n_sub:
            upd_f.append(update(order_f[j + GLA_LOOKAHEAD], khf_ref, vf_ref))
            upd_b.append(update(order_b[j + GLA_LOOKAHEAD], khb_ref, vb_ref))
        s_f = step(order_f[j], s_f, upd_f[j], qtf_ref, dtf_ref, of_ref)
        s_b = step(order_b[j], s_b, upd_b[j], qtb_ref, dtb_ref, ob_ref)
    sf_ref[...] = s_f
    sb_ref[...] = s_b


def _gla_rec(gq, gk, gv, gf, gb):
    bsz, n, _ = gq.shape
    seg = min(GLA_SEG, n)
    nseg = n // seg
    assert seg // GLA_SUB <= LANES
    fwd = lambda w: pl.BlockSpec((1, seg, w), lambda b, s: (b, s, 0))
    bwd = lambda w: pl.BlockSpec((1, seg, w), lambda b, s: (b, nseg - 1 - s, 0))
    return pl.pallas_call(
        _gla_rec_kernel,
        out_shape=(jax.ShapeDtypeStruct((bsz, n, A_W), BF16), jax.ShapeDtypeStruct((bsz, n, A_W), BF16)),
        grid=(bsz, nseg),
        in_specs=[fwd(A_KW), fwd(A_KW), fwd(A_W), fwd(A_KW), bwd(A_KW), bwd(A_KW), bwd(A_W), bwd(A_KW)],
        out_specs=(fwd(A_W), bwd(A_W)),
        scratch_shapes=[
            pltpu.VMEM((A_KW, A_W), F32), pltpu.VMEM((A_KW, A_W), F32),
            pltpu.VMEM((seg, A_KW), BF16), pltpu.VMEM((seg, A_KW), BF16), pltpu.VMEM((A_KW, LANES), F32),
            pltpu.VMEM((seg, A_KW), BF16), pltpu.VMEM((seg, A_KW), BF16), pltpu.VMEM((A_KW, LANES), F32),
            pltpu.VMEM((seg, A_KW), F32), pltpu.VMEM((LANES, A_KW), F32),
        ],
        compiler_params=_params("parallel", "arbitrary"), name="gla_rec",
    )(gq, gk, gv, gf, gq, gk, gv, gb)


def _fft1_kernel(u_ref, a1_ref, cos_ref, sin_ref, t_ref):
    n1 = u_ref.shape[1]
    nb = cos_ref.shape[0]
    t = jnp.dot(a1_ref[...], u_ref[0], preferred_element_type=F32)
    for j in range(nb):
        cols = slice(j * B_W, (j + 1) * B_W)
        tr = t[:n1, cols]
        ti = t[n1:, cols]
        c = jnp.concatenate([cos_ref[j]] * (B_W // LANES), axis=-1)
        s = jnp.concatenate([sin_ref[j]] * (B_W // LANES), axis=-1)
        t_ref[0, 0, :, cols] = (tr * c + ti * s).astype(BF16)
        t_ref[0, 1, :, cols] = (ti * c - tr * s).astype(BF16)


def _fft2_kernel(t_ref, l_ref, cc_ref, cs_ref, w_ref, o_ref):
    n2 = t_ref.shape[3]
    half = FFT_K1 * n2
    groups = range(t_ref.shape[2] // FFT_K1)
    xs = []
    for grp in groups:
        k1s = slice(grp * FFT_K1, (grp + 1) * FFT_K1)
        xs.append(jnp.concatenate([t_ref[0, 0, k1s].reshape(half, B_W), t_ref[0, 1, k1s].reshape(half, B_W)],
                                  axis=0))
    gs = [jnp.dot(l_ref[...], x, preferred_element_type=F32) for x in xs]
    mixed = [jnp.dot(g[:half].astype(BF16), cc_ref[...], preferred_element_type=F32)
             + jnp.dot(g[half:].astype(BF16), cs_ref[...], preferred_element_type=F32) for g in gs]
    ys = [jnp.dot(m.astype(BF16), w_ref[...], preferred_element_type=F32) for m in mixed]
    o_ref[0] = jnp.concatenate([y.reshape(n2, FFT_K1, B_W) for y in ys], axis=1).astype(BF16)


def _fnet(fu, lp, tabs):
    bsz, n, _ = fu.shape
    n2 = FFT_N2
    n1 = n // n2
    nb = FFT_NB
    k1b = FFT_K1 * FFT_K1_GROUPS
    u2 = fu.reshape(bsz, n1, n2 * B_W)
    tt = pl.pallas_call(
        _fft1_kernel,
        out_shape=jax.ShapeDtypeStruct((bsz, 2, n1, n2 * B_W), BF16),
        grid=(n2 // nb, bsz),
        in_specs=[
            pl.BlockSpec((1, n1, nb * B_W), lambda j, b: (b, 0, j)),
            _full((2 * n1, n1)),
            pl.BlockSpec((nb, n1, LANES), lambda j, b: (j, 0, 0)),
            pl.BlockSpec((nb, n1, LANES), lambda j, b: (j, 0, 0)),
        ],
        out_specs=pl.BlockSpec((1, 2, n1, nb * B_W), lambda j, b: (b, 0, 0, j)),
        compiler_params=_params("parallel", "parallel"), name="fft_pass1",
    )(u2, tabs["a1"], tabs["tw_cos"], tabs["tw_sin"])
    t5 = tt.reshape(bsz, 2, n1, n2, B_W)
    y = pl.pallas_call(
        _fft2_kernel,
        out_shape=jax.ShapeDtypeStruct((bsz, n2, n1, B_W), BF16),
        grid=(bsz, n1 // k1b),
        in_specs=[
            pl.BlockSpec((1, 2, k1b, n2, B_W), lambda b, i: (b, 0, i, 0, 0)),
            _full((2 * FFT_K1 * n2, 2 * FFT_K1 * n2)),
            _full((B_W, B_W)), _full((B_W, B_W)), _full((B_W, B_W)),
        ],
        out_specs=pl.BlockSpec((1, n2, k1b, B_W), lambda b, i: (b, 0, i, 0)),
        compiler_params=_params("parallel", "parallel"), name="fft_pass2",
    )(t5, tabs["l2"], tabs["chan_cos"], tabs["chan_sin"], lp["fnet_w"])
    return y.reshape(bsz, n, B_W)


def _attn_kernel(q_ref, kt_ref, va_ref, o_ref, qm_ref, m_ref, acc_ref):
    tq = q_ref.shape[1]
    n = kt_ref.shape[3]
    tk = min(ATT_TK, n)
    q = q_ref[0]
    lane = lax.broadcasted_iota(jnp.int32, q.shape, 1)
    for h in range(C_GRP):
        qm_ref[h * tq:(h + 1) * tq, :] = jnp.where(lane // C_HD == h, q, jnp.zeros_like(q))
    m_ref[...] = jnp.full(m_ref.shape, -jnp.inf, F32)
    acc_ref[...] = jnp.zeros(acc_ref.shape, F32)

    def scores(c):
        c0 = pl.multiple_of(c * tk, tk)
        return jnp.dot(qm_ref[...], kt_ref[0, 0, :, pl.ds(c0, tk)], preferred_element_type=F32)

    def softmax_accumulate(c, s):
        c0 = pl.multiple_of(c * tk, tk)
        v = va_ref[0, 0, pl.ds(c0, tk), :]
        s_max = s[:, 0:LANES]
        for t in range(1, tk // LANES):
            s_max = jnp.maximum(s_max, s[:, t * LANES:(t + 1) * LANES])
        row_max = jnp.max(s_max, axis=-1, keepdims=True)
        m_old = m_ref[...]
        m_new = jnp.maximum(m_old, row_max)
        p = jnp.exp2(s - jnp.concatenate([m_new] * (tk // LANES), axis=-1))
        alpha = jnp.exp2(m_old - m_new)
        acc_ref[...] = alpha * acc_ref[...] + jnp.dot(p.astype(BF16), v, preferred_element_type=F32)
        m_ref[...] = m_new

    def step(c, carry):
        softmax_accumulate(c, scores(c))
        return carry

    lax.fori_loop(0, n // tk, step, 0, unroll=ATT_UNROLL)
    acc = acc_ref[...]
    inv = 1.0 / acc[:, ONES_LANE:ONES_LANE + 1]
    outs = [(acc[h * tq:(h + 1) * tq, 0:C_HD] * inv[h * tq:(h + 1) * tq]) for h in range(C_GRP)]
    o_ref[0] = jnp.concatenate(outs, axis=-1).astype(BF16)


def _attention(aq, kt4, va):
    bsz, n, _ = aq.shape
    tq = min(ATT_TQ, n)
    return pl.pallas_call(
        _attn_kernel,
        out_shape=jax.ShapeDtypeStruct((bsz, n, C_W), BF16),
        grid=(bsz, C_KV, n // tq),
        in_specs=[
            pl.BlockSpec((1, tq, C_GRP * C_HD), lambda b, g, i: (b, i, g)),
            pl.BlockSpec((1, 1, C_GRP * C_HD, n), lambda b, g, i: (b, g, 0, 0)),
            pl.BlockSpec((1, 1, n, LANES), lambda b, g, i: (b, g, 0, 0)),
        ],
        out_specs=pl.BlockSpec((1, tq, C_GRP * C_HD), lambda b, g, i: (b, i, g)),
        scratch_shapes=[
            pltpu.VMEM((C_GRP * tq, C_GRP * C_HD), BF16),
            pltpu.VMEM((C_GRP * tq, LANES), F32),
            pltpu.VMEM((C_GRP * tq, LANES), F32),
        ],
        compiler_params=_params("parallel", "parallel", "arbitrary"), name="attention",
    )(aq, kt4, va)


def _outproj_kernel(x_ref, mod_ref, oa_ref, of_ref, ob_ref, yb_ref, ao_ref, du_ref, dvn_ref, sz_ref,
                    w_ref, postg_ref, og_ref, bd_ref, sw_ref, sb_ref, o_ref):
    tm = x_ref.shape[1]
    sz = sz_ref[0].astype(F32)
    off_b, off_c, off_d = A_W, A_W + B_W, A_W + B_W + C_W

    sgu = [jnp.dot(sw_ref[...], dvn_ref[0, c * SGU_CHUNK:(c + 1) * SGU_CHUNK, :], preferred_element_type=F32)
           for c in range(tm // SGU_CHUNK)]
    m_c = (ao_ref[0].astype(F32) * sz[:, off_c:off_d]).astype(BF16)
    y = jnp.dot(m_c, w_ref[off_c:off_d, :], preferred_element_type=F32)
    m_b = (yb_ref[0].astype(F32) * sz[:, off_b:off_c]).astype(BF16)
    y = y + jnp.dot(m_b, w_ref[off_b:off_c, :], preferred_element_type=F32)

    o = oa_ref[0].astype(F32) + of_ref[0].astype(F32) + ob_ref[0].astype(F32)
    oms = jnp.dot((o * o).astype(BF16), bd_ref[...], preferred_element_type=F32)
    out_a = o * lax.rsqrt(oms + EPS) * og_ref[...]
    y = y + jnp.dot((out_a * sz[:, 0:off_b]).astype(BF16), w_ref[0:off_b, :], preferred_element_type=F32)

    lane = lax.broadcasted_iota(jnp.int32, (SGU_CHUNK, D_W), 1)
    parts = []
    for c, r in enumerate(sgu):
        rows = slice(c * SGU_CHUNK, (c + 1) * SGU_CHUNK)
        mix = sb_ref[...]
        for g in range(D_GROUPS):
            mix = mix + jnp.where(lane // D_GC == g, r[g * SGU_CHUNK:(g + 1) * SGU_CHUNK, :], 0.0)
        parts.append(du_ref[0, rows, :].astype(F32) * mix)
    out_d = jnp.concatenate(parts, axis=0)
    m_d = (out_d * sz[:, off_d:]).astype(BF16)
    y = y + jnp.dot(m_d, w_ref[off_d:, :], preferred_element_type=F32)

    yms = jnp.mean(y * y, axis=-1, keepdims=True)
    yn = y * lax.rsqrt(yms + EPS) * postg_ref[...]
    gate = mod_ref[0, :, 2 * D_MODEL:3 * D_MODEL]
    o_ref[0] = x_ref[0] + gate * yn


def _outproj(x, mod, oa, of, ob, yb, ao, du, dvn, sz, lp):
    bsz, n, _ = x.shape
    tm = min(TM_OUT, n)
    tok = lambda w: pl.BlockSpec((1, tm, w), lambda b, i: (b, i, 0))
    return pl.pallas_call(
        _outproj_kernel,
        out_shape=jax.ShapeDtypeStruct((bsz, n, D_MODEL), F32),
        grid=(bsz, n // tm),
        in_specs=[
            tok(D_MODEL),
            pl.BlockSpec((1, 1, 3 * D_MODEL), lambda b, i: (b, 0, 0)),
            tok(A_W), tok(A_W), tok(A_W), tok(B_W), tok(C_W), tok(D_W), tok(D_W), tok(D_MIX),
            _full((D_MIX, D_MODEL)), _full((1, D_MODEL)), _full((1, A_W)), _full((A_W, A_W)),
            _full((D_GROUPS * SGU_CHUNK, SGU_CHUNK)), _full((SGU_CHUNK, D_W)),
        ],
        out_specs=tok(D_MODEL),
        compiler_params=_params("parallel", "parallel"), name="outproj",
    )(x, mod, oa, of, ob, yb, ao, du, dvn, sz,
      lp["w_out"], lp["post_g"], lp["og"], lp["bd_a"], lp["sgu_w"], lp["sgu_b"])


def _pack_w_in(w):
    sizes = [A_KW, A_KW, A_W, 2 * GLA_RANK, B_W, C_W, C_KVW, C_KVW, D_W, D_W, D_MIX]
    off = [0] + [int(o) for o in np.cumsum(sizes)]
    a_q, a_k, a_v, a_lr, b_u, c_q, c_k, c_v, d_u, d_v, z = [w[:, off[i]:off[i + 1]] for i in range(len(sizes))]

    def deinterleave(x, heads):
        r = x.reshape(D_MODEL, heads, C_HD // 2, 2)
        return jnp.concatenate([r[..., 0], r[..., 1]], axis=-1).reshape(D_MODEL, heads * C_HD)

    zeros = lambda width: jnp.zeros((D_MODEL, width), w.dtype)
    pieces = [a_q, a_k, a_v, b_u, deinterleave(c_q, C_HEADS), deinterleave(c_k, C_KV),
              a_lr, zeros(LANES - 2 * GLA_RANK)]
    for h in range(C_KV):
        pieces += [c_v[:, h * C_HD:(h + 1) * C_HD], zeros(LANES - C_HD)]
    pieces += [d_u, d_v, z]
    packed = jnp.concatenate(pieces, axis=1)
    assert packed.shape[1] == W_PACKED
    return packed.astype(BF16)


def _block_diag_mean(width, group):
    idx = np.arange(width) // group
    return jnp.asarray((idx[:, None] == idx[None, :]).astype(np.float32) / group, BF16)


def _layer_params(l, ada_unused, norm_pre_g, norm_post_g, w_in, gla_wg2_f, gla_bg_f, gla_wg2_b, gla_bg_b,
                  gla_onorm_g, fnet_w, q_norm_g, k_norm_g, sgu_norm_g, sgu_w, sgu_b, w_out):
    deint = np.concatenate([np.arange(0, C_HD, 2), np.arange(1, C_HD, 2)])
    w = _pack_w_in(w_in[l])
    wg2 = jnp.zeros((LANES, 2 * A_KW), F32)
    wg2 = wg2.at[0:GLA_RANK, 0:A_KW].set(gla_wg2_f[l])
    wg2 = wg2.at[GLA_RANK:2 * GLA_RANK, A_KW:].set(gla_wg2_b[l])
    vone = np.zeros((1, 2 * LANES), np.float32)
    vone[0, ONES_LANE] = 1.0
    vone[0, LANES + ONES_LANE] = 1.0
    return {
        "pre_g": norm_pre_g[l].reshape(1, D_MODEL),
        "post_g": norm_post_g[l].reshape(1, D_MODEL),
        "w_in": w,
        "wg2": wg2.astype(BF16),
        "bg": jnp.concatenate([gla_bg_f[l], gla_bg_b[l]]).reshape(1, 2 * A_KW),
        "qg": jnp.tile(q_norm_g[l][deint], C_HEADS).reshape(1, C_W),
        "kg": jnp.tile(k_norm_g[l][deint], C_KV).reshape(1, C_KVW),
        "sg": sgu_norm_g[l].reshape(1, D_W),
        "bd_q": _block_diag_mean(C_W, C_HD),
        "bd_a": _block_diag_mean(A_W, A_DV),
        "vone": jnp.asarray(vone),
        "og": jnp.tile(gla_onorm_g[l], A_HEADS).reshape(1, A_W),
        "fnet_w": fnet_w[l].astype(BF16),
        "sgu_w": sgu_w[l].reshape(D_GROUPS * SGU_CHUNK, SGU_CHUNK).astype(BF16),
        "sgu_b": jnp.repeat(sgu_b[l].T, D_GC, axis=1),
        "w_out": w_out[l].astype(BF16),
    }


def _seq_tables(n):
    rows = n // GRID_W
    row = jnp.repeat(jnp.arange(rows, dtype=F32), GRID_W)
    col = jnp.tile(jnp.arange(GRID_W, dtype=F32), rows)
    rope_axis = C_HD // 2
    freqs = ROPE_THETA ** (-jnp.arange(0, rope_axis, 2, dtype=F32) / rope_axis)
    ang = jnp.concatenate([row[:, None] * freqs, col[:, None] * freqs], axis=-1)
    cos, sin = jnp.cos(ang), jnp.sin(ang)
    cos_t = jnp.tile(jnp.concatenate([cos, cos], axis=-1), (1, LANES // C_HD))
    sin_t = jnp.tile(jnp.concatenate([-sin, sin], axis=-1), (1, LANES // C_HD))

    n2 = FFT_N2
    n1 = n // n2

    def dft_angles(rows, cols, period):
        prod = (jnp.arange(rows, dtype=jnp.int32)[:, None] * jnp.arange(cols, dtype=jnp.int32)[None, :]) % period
        return prod.astype(F32) * (2.0 * math.pi / period)

    a1_ang = dft_angles(n1, n1, n1)
    a1 = jnp.concatenate([jnp.cos(a1_ang), -jnp.sin(a1_ang)], axis=0)
    tw_ang = dft_angles(n2, n1, n)
    scale = 1.0 / math.sqrt(n * B_GC)
    tw_cos = jnp.broadcast_to((jnp.cos(tw_ang) * scale)[:, :, None], (n2, n1, LANES))
    tw_sin = jnp.broadcast_to((jnp.sin(tw_ang) * scale)[:, :, None], (n2, n1, LANES))
    ang2 = dft_angles(n2, n2, n2)
    c2, s2 = jnp.cos(ang2), jnp.sin(ang2)
    eye = jnp.eye(FFT_K1, dtype=F32)
    blk = lambda m: (m[:, None, None, :] * eye[None, :, :, None]).reshape(n2 * FFT_K1, FFT_K1 * n2)
    l2 = jnp.block([[blk(c2), blk(s2)], [blk(-s2), blk(c2)]])
    angc = dft_angles(B_GC, B_GC, B_GC)
    grp = jnp.eye(B_GROUPS, dtype=F32)
    kron = lambda m: (grp[:, None, :, None] * m[None, :, None, :]).reshape(B_W, B_W)
    chan_cos = kron(jnp.cos(angc))
    chan_sin = kron(jnp.sin(angc))
    return {
        "cos": cos_t, "sin": sin_t,
        "a1": a1.astype(BF16),
        "tw_cos": tw_cos, "tw_sin": tw_sin,
        "l2": l2.astype(BF16),
        "chan_cos": chan_cos.astype(BF16), "chan_sin": chan_sin.astype(BF16),
    }


def _layer(x, mod, lp, tabs):
    gq, gk, gv, gf, gb, fu, aq, kt4, va, du, dvn, sz = _inproj(x, mod, lp, tabs)
    oa = _gla_band(gq, gk, gv, gf, gb)
    of, ob = _gla_rec(gq, gk, gv, gf, gb)
    yb = _fnet(fu, lp, tabs)
    ao = _attention(aq, kt4, va)
    return _outproj(x, mod, oa, of, ob, yb, ao, du, dvn, sz, lp)


def kernel(x_prompt, x_sample, c_prompt, c_sample, ada_w, ada_b, norm_pre_g, norm_post_g, w_in,
           gla_wg2_f, gla_bg_f, gla_wg2_b, gla_bg_b, gla_onorm_g, fnet_w, q_norm_g, k_norm_g,
           sgu_norm_g, sgu_w, sgu_b, w_out):
    bp, bs = x_prompt.shape[0], x_sample.shape[0]
    pad_rows = (-(bp + bs)) % 8
    c_all = jnp.concatenate([c_prompt, c_sample, jnp.zeros((pad_rows, D_MODEL), F32)], axis=0)
    mod = _adaln(c_all, ada_w, ada_b)
    tabs_p = _seq_tables(x_prompt.shape[1])
    tabs_s = _seq_tables(x_sample.shape[1])
    y_prompt, y_sample = x_prompt, x_sample
    for l in range(DEPTH):
        lp = _layer_params(l, None, norm_pre_g, norm_post_g, w_in, gla_wg2_f, gla_bg_f, gla_wg2_b,
                           gla_bg_b, gla_onorm_g, fnet_w, q_norm_g, k_norm_g, sgu_norm_g, sgu_w,
                           sgu_b, w_out)
        mod_p = mod[l, 0:bp].reshape(bp, 1, 3 * D_MODEL)
        mod_s = mod[l, bp:bp + bs].reshape(bs, 1, 3 * D_MODEL)
        y_prompt = _layer(y_prompt, mod_p, lp, tabs_p)
        y_sample = _layer(y_sample, mod_s, lp, tabs_s)
    return (y_prompt, y_sample)
```

```python
import math

import numpy as np
import jax
import jax.numpy as jnp
from jax import lax
from jax.experimental import pallas as pl
from jax.experimental.pallas import tpu as pltpu

F32 = jnp.float32
BF16 = jnp.bfloat16
HIGHEST = lax.Precision.HIGHEST

D_MODEL = 1024
DEPTH = 2
GRID_W = 64
A_HEADS, A_DK, A_DV = 4, 32, 64
A_W, A_KW = A_HEADS * A_DV, A_HEADS * A_DK
GLA_RANK = 16
GLA_GATE_NORM = 16.0
B_GROUPS, B_GC = 4, 64
B_W = B_GROUPS * B_GC
C_HEADS, C_KV, C_HD = 8, 2, 64
C_GRP = C_HEADS // C_KV
C_W, C_KVW = C_HEADS * C_HD, C_KV * C_HD
ROPE_THETA = 10000.0
D_GROUPS, D_GC = 4, 64
D_W = D_GROUPS * D_GC
SGU_CHUNK = 128
D_MIX = A_W + B_W + C_W + D_W
EPS = 1e-6
LOG2E = 1.4426950408889634

LANES = 128
VMEM_LIMIT_BYTES = 56 * 1024 * 1024

GQ, GK, GV, FU, AQ, AK, LR, AV, DU, DV, ZZ = 0, 128, 256, 512, 768, 1280, 1408, 1536, 1792, 2048, 2304
W_PACKED = 3584
AV_W = 2 * LANES
ONES_LANE = C_HD

TM_IN = 512
IN_ROW_PARTS = 2
TM_OUT = 1024
GLA_SUB = 16
GLA_BAND_TILE = 64
GLA_GRP = 128
GLA_SEG = 1024
GLA_LOOKAHEAD = 3
FFT_N2 = 64
FFT_NB = 16
FFT_K1 = 8
FFT_K1_GROUPS = 2
ATT_TQ = 512
ATT_TK = 2048
ATT_UNROLL = 2


def _params(*sem):
    return pltpu.CompilerParams(dimension_semantics=sem, vmem_limit_bytes=VMEM_LIMIT_BYTES)


def _full(shape):
    n = len(shape)
    return pl.BlockSpec(shape, lambda *_: (0,) * n)


def _adaln_kernel(c_ref, w_ref, b_ref, o_ref):
    c = c_ref[...]
    sc = c / (1.0 + jnp.exp(-c))
    o_ref[0] = jnp.dot(sc, w_ref[0], precision=HIGHEST, preferred_element_type=F32) + b_ref[0]


def _adaln(c_all, ada_w, ada_b):
    rows = c_all.shape[0]
    ncol = 3 * D_MODEL // D_MODEL
    return pl.pallas_call(
        _adaln_kernel,
        out_shape=jax.ShapeDtypeStruct((DEPTH, rows, 3 * D_MODEL), F32),
        grid=(DEPTH, ncol),
        in_specs=[
            pl.BlockSpec((rows, D_MODEL), lambda l, j: (0, 0)),
            pl.BlockSpec((1, D_MODEL, D_MODEL), lambda l, j: (l, 0, j)),
            pl.BlockSpec((1, 1, D_MODEL), lambda l, j: (l, 0, j)),
        ],
        out_specs=pl.BlockSpec((1, rows, D_MODEL), lambda l, j: (l, 0, j)),
        compiler_params=_params("arbitrary", "arbitrary"),
        name="adaln",
    )(c_all, ada_w, ada_b.reshape(DEPTH, 1, 3 * D_MODEL))


def _rope(x, cos, sin_signed):
    width = x.shape[-1]
    lane = lax.broadcasted_iota(jnp.int32, x.shape, 1)
    first_half = (lane % C_HD) < (C_HD // 2)
    swapped = jnp.where(first_half, pltpu.roll(x, width - C_HD // 2, 1), pltpu.roll(x, C_HD // 2, 1))
    reps = width // LANES
    cos_w = jnp.concatenate([cos] * reps, axis=-1) if reps > 1 else cos
    sin_w = jnp.concatenate([sin_signed] * reps, axis=-1) if reps > 1 else sin_signed
    return x * cos_w + swapped * sin_w


def _inproj_kernel(x_ref, mod_ref, preg_ref, w_ref, wg2_ref, bg_ref, qg_ref, kg_ref, sg_ref,
                   cos_ref, sin_ref, bd_ref, vone_ref,
                   gq_ref, gk_ref, gv_ref, gf_ref, gb_ref, fu_ref, aq_ref, kt_ref, va_ref,
                   du_ref, dvn_ref, sz_ref):
    tm = x_ref.shape[1]
    shift = mod_ref[0, :, 0:D_MODEL]
    gain = preg_ref[...] * (1.0 + mod_ref[0, :, D_MODEL:2 * D_MODEL])
    for part in range(IN_ROW_PARTS):
        rows = slice(part * tm // IN_ROW_PARTS, (part + 1) * tm // IN_ROW_PARTS)
        x = x_ref[0, rows, :]
        ms = jnp.mean(x * x, axis=-1, keepdims=True)
        hb = ((x * lax.rsqrt(ms + EPS)) * gain + shift).astype(BF16)

        def seg(off, width):
            return jnp.dot(hb, w_ref[:, off:off + width], preferred_element_type=F32)

        klr = seg(AK, C_KVW + LANES)
        q = seg(AQ, C_W)
        z = seg(ZZ, D_MIX)
        gqk = seg(GQ, 2 * A_KW)
        gv = seg(GV, A_W)
        fu = seg(FU, B_W)
        va = seg(AV, AV_W)
        du = seg(DU, D_W)
        dv = seg(DV, D_W)
        k = klr[:, :C_KVW]
        lr = klr[:, C_KVW:].astype(BF16)
        logits = jnp.dot(lr, wg2_ref[...], preferred_element_type=F32) + bg_ref[...]
        qms = jnp.dot((q * q).astype(BF16), bd_ref[...], preferred_element_type=F32)
        kms = jnp.dot((k * k).astype(BF16), bd_ref[0:C_KVW, 0:C_KVW], preferred_element_type=F32)

        gq_ref[0, rows, :] = gqk[:, :A_KW] * (A_DK ** -0.5)
        gk_ref[0, rows, :] = gqk[:, A_KW:]
        gv_ref[0, rows, :] = gv
        logg = (jnp.minimum(logits, 0.0) - jnp.log1p(jnp.exp(-jnp.abs(logits)))) * (1.0 / GLA_GATE_NORM)
        gf_ref[0, rows, :] = logg[:, :A_KW]
        gb_ref[0, rows, :] = logg[:, A_KW:]

        fu_ref[0, rows, :] = fu.astype(BF16)

        cos = cos_ref[rows, :]
        sin = sin_ref[rows, :]
        qn = q * lax.rsqrt(qms + EPS) * qg_ref[...]
        aq_ref[0, rows, :] = (_rope(qn, cos, sin) * (C_HD ** -0.5 * LOG2E)).astype(BF16)
        kn = k * lax.rsqrt(kms + EPS) * kg_ref[...]
        kt = _rope(kn, cos, sin).T.astype(BF16)
        for g in range(C_KV):
            for r in range(C_GRP):
                kt_ref[0, g, r * C_HD:(r + 1) * C_HD, rows] = kt[g * C_HD:(g + 1) * C_HD, :]
        vab = (va + vone_ref[...]).astype(BF16)
        va_ref[0, 0, rows, :] = vab[:, :LANES]
        va_ref[0, 1, rows, :] = vab[:, LANES:]

        du_ref[0, rows, :] = du.astype(BF16)
        dms = jnp.mean(dv * dv, axis=-1, keepdims=True)
        dvn_ref[0, rows, :] = (dv * lax.rsqrt(dms + EPS) * sg_ref[...]).astype(BF16)

        sz_ref[0, rows, :] = (z / (1.0 + jnp.exp(-z))).astype(BF16)


def _inproj(x, mod, lp, tabs):
    bsz, n, _ = x.shape
    tm = min(TM_IN, n)
    grid = (bsz, n // tm)
    tok = lambda w: pl.BlockSpec((1, tm, w), lambda b, i: (b, i, 0))
    out_shapes = (
        jax.ShapeDtypeStruct((bsz, n, A_KW), F32),
        jax.ShapeDtypeStruct((bsz, n, A_KW), F32),
        jax.ShapeDtypeStruct((bsz, n, A_W), F32),
        jax.ShapeDtypeStruct((bsz, n, A_KW), F32),
        jax.ShapeDtypeStruct((bsz, n, A_KW), F32),
        jax.ShapeDtypeStruct((bsz, n, B_W), BF16),
        jax.ShapeDtypeStruct((bsz, n, C_W), BF16),
        jax.ShapeDtypeStruct((bsz, C_KV, C_GRP * C_HD, n), BF16),
        jax.ShapeDtypeStruct((bsz, C_KV, n, LANES), BF16),
        jax.ShapeDtypeStruct((bsz, n, D_W), BF16),
        jax.ShapeDtypeStruct((bsz, n, D_W), BF16),
        jax.ShapeDtypeStruct((bsz, n, D_MIX), BF16),
    )
    out_specs = (
        tok(A_KW), tok(A_KW), tok(A_W), tok(A_KW), tok(A_KW), tok(B_W), tok(C_W),
        pl.BlockSpec((1, C_KV, C_GRP * C_HD, tm), lambda b, i: (b, 0, 0, i)),
        pl.BlockSpec((1, C_KV, tm, LANES), lambda b, i: (b, 0, i, 0)),
        tok(D_W), tok(D_W), tok(D_MIX),
    )
    in_specs = [
        tok(D_MODEL),
        pl.BlockSpec((1, 1, 3 * D_MODEL), lambda b, i: (b, 0, 0)),
        _full((1, D_MODEL)),
        _full((D_MODEL, W_PACKED)),
        _full((LANES, 2 * A_KW)),
        _full((1, 2 * A_KW)),
        _full((1, C_W)),
        _full((1, C_KVW)),
        _full((1, D_W)),
        pl.BlockSpec((tm, LANES), lambda b, i: (i, 0)),
        pl.BlockSpec((tm, LANES), lambda b, i: (i, 0)),
        _full((C_W, C_W)),
        _full((1, 2 * LANES)),
    ]
    return pl.pallas_call(
        _inproj_kernel, out_shape=out_shapes, grid=grid, in_specs=in_specs, out_specs=out_specs,
        compiler_params=_params("parallel", "parallel"), name="inproj",
    )(x, mod, lp["pre_g"], lp["w_in"], lp["wg2"], lp["bg"], lp["qg"], lp["kg"], lp["sg"],
      tabs["cos"], tabs["sin"], lp["bd_q"], lp["vone"])


def _head_match(shape, rows_per_head, cols_per_head):
    return (lax.broadcasted_iota(jnp.int32, shape, 0) // rows_per_head
            == lax.broadcasted_iota(jnp.int32, shape, 1) // cols_per_head)


def _gla_band_kernel(q_ref, k_ref, v_ref, gf_ref, gb_ref, of_ref, ob_ref, o_ref,
                     qf_ref, kf_ref, vf_ref, bf_ref, cb_ref, vs_ref, os_ref):
    rows = q_ref.shape[1]
    ts = rows // GLA_SUB
    n_slab = A_W // LANES
    every = lambda p: pl.ds(p, ts, stride=GLA_SUB)
    hsum = jnp.where(_head_match((A_KW, A_W), A_DK, A_DV), 1.0, 0.0).astype(BF16)
    for h in range(n_slab):
        vs_ref[h] = v_ref[0, :, h * LANES:(h + 1) * LANES]
    run = jnp.zeros((ts, A_KW), F32)
    for p in range(GLA_SUB):
        run = run + gf_ref[0, every(p), :] * LOG2E
        bf_ref[p] = run
        qf_ref[p] = q_ref[0, every(p), :]
        kf_ref[p] = k_ref[0, every(p), :]
        vf_ref[p] = jnp.concatenate([vs_ref[h, every(p), :] for h in range(n_slab)], axis=-1)
    run = jnp.zeros((ts, A_KW), F32)
    for p in reversed(range(GLA_SUB)):
        run = run + gb_ref[0, every(p), :] * LOG2E
        cb_ref[p] = run
    for p in range(GLA_SUB):
        q = qf_ref[p]
        prods = []
        for s in range(GLA_SUB):
            qk = q * kf_ref[s]
            if s < p:
                qk = qk * jnp.exp2(bf_ref[p] - bf_ref[s])
            elif s > p:
                qk = qk * jnp.exp2(cb_ref[p] - cb_ref[s])
            prods.append(qk.astype(BF16))
        a = jnp.dot(jnp.concatenate(prods, axis=0), hsum, preferred_element_type=F32)
        acc = a[0:ts] * vf_ref[0]
        for s in range(1, GLA_SUB):
            acc = acc + a[s * ts:(s + 1) * ts] * vf_ref[s]
        for h in range(n_slab):
            os_ref[h, every(p), :] = acc[:, h * LANES:(h + 1) * LANES]
    band = jnp.concatenate([os_ref[h] for h in range(n_slab)], axis=-1)
    o_ref[0] = (band + of_ref[0].astype(F32) + ob_ref[0].astype(F32)).astype(BF16)


def _gla_band(gq, gk, gv, gf, gb, of, ob):
    bsz, n, _ = gq.shape
    rows = min(GLA_BAND_TILE * GLA_SUB, n)
    ts = rows // GLA_SUB
    blk = lambda w: pl.BlockSpec((1, rows, w), lambda b, i: (b, i, 0))
    return pl.pallas_call(
        _gla_band_kernel,
        out_shape=jax.ShapeDtypeStruct((bsz, n, A_W), BF16),
        grid=(bsz, n // rows),
        in_specs=[blk(A_KW), blk(A_KW), blk(A_W), blk(A_KW), blk(A_KW), blk(A_W), blk(A_W)],
        out_specs=blk(A_W),
        scratch_shapes=[
            pltpu.VMEM((GLA_SUB, ts, A_KW), F32), pltpu.VMEM((GLA_SUB, ts, A_KW), F32),
            pltpu.VMEM((GLA_SUB, ts, A_W), F32),
            pltpu.VMEM((GLA_SUB, ts, A_KW), F32), pltpu.VMEM((GLA_SUB, ts, A_KW), F32),
            pltpu.VMEM((A_W // LANES, rows, LANES), F32), pltpu.VMEM((A_W // LANES, rows, LANES), F32),
        ],
        compiler_params=_params("parallel", "parallel"), name="gla_band",
    )(gq, gk, gv, gf, gb, of, ob)


def _split3(x):
    hi = x.astype(BF16)
    rest = x - hi.astype(F32)
    mid = rest.astype(BF16)
    lo = (rest - mid.astype(F32)).astype(BF16)
    return hi, mid, lo


def _gla_prep(q_ref, k_ref, g_ref, qt_ref, kh_ref, dect_ref, tot_ref, pad_ref, reverse):
    seg = q_ref.shape[1]
    n_sub = seg // GLA_SUB
    r_i = lax.broadcasted_iota(jnp.int32, (GLA_GRP, GLA_GRP), 0)
    c_i = lax.broadcasted_iota(jnp.int32, (GLA_GRP, GLA_GRP), 1)
    same = (r_i // GLA_SUB) == (c_i // GLA_SUB)
    tri = (c_i >= r_i) if reverse else (c_i <= r_i)
    sums_mat = jnp.concatenate([jnp.where(same & tri, 1.0, 0.0), jnp.where(same, 1.0, 0.0)], axis=0).astype(BF16)
    for r in range(seg // GLA_GRP):
        rows = slice(r * GLA_GRP, (r + 1) * GLA_GRP)
        g3 = jnp.concatenate(_split3(g_ref[0, rows, :]), axis=-1)
        sums = jnp.dot(sums_mat, g3, preferred_element_type=F32)
        sums = sums[:, 0:A_KW] + sums[:, A_KW:2 * A_KW] + sums[:, 2 * A_KW:]
        cum, tot = sums[:GLA_GRP], sums[GLA_GRP:]
        qt_ref[rows, :] = (q_ref[0, rows, :] * jnp.exp(cum)).astype(BF16)
        kh_ref[rows, :] = (k_ref[0, rows, :] * jnp.exp(tot - cum)).astype(BF16)
        tot_ref[rows, :] = tot
    pad_ref[...] = jnp.zeros_like(pad_ref)
    pad_ref[0:n_sub, :] = tot_ref[pl.ds(0, n_sub, stride=GLA_SUB), :]
    dect_ref[...] = jnp.exp(pad_ref[...].T)


def _gla_rec_kernel(qf_ref, kf_ref, vf_ref, gf_ref, qb_ref, kb_ref, vb_ref, gb_ref, of_ref, ob_ref,
                    sf_ref, sb_ref, qtf_ref, khf_ref, dtf_ref, qtb_ref, khb_ref, dtb_ref, tot_ref, pad_ref):
    @pl.when(pl.program_id(1) == 0)
    def _():
        sf_ref[...] = jnp.zeros_like(sf_ref)
        sb_ref[...] = jnp.zeros_like(sb_ref)

    seg = qf_ref.shape[1]
    n_sub = seg // GLA_SUB
    _gla_prep(qf_ref, kf_ref, gf_ref, qtf_ref, khf_ref, dtf_ref, tot_ref, pad_ref, reverse=False)
    _gla_prep(qb_ref, kb_ref, gb_ref, qtb_ref, khb_ref, dtb_ref, tot_ref, pad_ref, reverse=True)
    smask = _head_match((A_KW, A_W), A_DK, A_DV)

    def update(i, kh_ref, v_ref):
        rows = slice(i * GLA_SUB, (i + 1) * GLA_SUB)
        upd = lax.dot_general(kh_ref[rows, :], v_ref[0, rows, :].astype(BF16), (((0,), (0,)), ((), ())),
                              preferred_element_type=F32)
        return jnp.where(smask, upd, 0.0)

    def step(i, s, upd, qt_ref, dect_ref, o_ref):
        rows = slice(i * GLA_SUB, (i + 1) * GLA_SUB)
        o_ref[0, rows, :] = jnp.dot(qt_ref[rows, :], s.astype(BF16), preferred_element_type=F32).astype(BF16)
        return s * dect_ref[:, i:i + 1] + upd

    order_f = list(range(n_sub))
    order_b = list(reversed(range(n_sub)))
    upd_f = [update(i, khf_ref, vf_ref) for i in order_f[:GLA_LOOKAHEAD]]
    upd_b = [update(i, khb_ref, vb_ref) for i in order_b[:GLA_LOOKAHEAD]]
    s_f = sf_ref[...]
    s_b = sb_ref[...]
    for j in range(n_sub):
        if j + GLA_LOOKAHEAD < n_sub:
            upd_f.append(update(order_f[j + GLA_LOOKAHEAD], khf_ref, vf_ref))
            upd_b.append(update(order_b[j + GLA_LOOKAHEAD], khb_ref, vb_ref))
        s_f = step(order_f[j], s_f, upd_f[j], qtf_ref, dtf_ref, of_ref)
        s_b = step(order_b[j], s_b, upd_b[j], qtb_ref, dtb_ref, ob_ref)
    sf_ref[...] = s_f
    sb_ref[...] = s_b


def _gla_rec(gq, gk, gv, gf, gb):
    bsz, n, _ = gq.shape
    seg = min(GLA_SEG, n)
    nseg = n // seg
    assert seg // GLA_SUB <= LANES
    fwd = lambda w: pl.BlockSpec((1, seg, w), lambda b, s: (b, s, 0))
    bwd = lambda w: pl.BlockSpec((1, seg, w), lambda b, s: (b, nseg - 1 - s, 0))
    return pl.pallas_call(
        _gla_rec_kernel,
        out_shape=(jax.ShapeDtypeStruct((bsz, n, A_W), BF16), jax.ShapeDtypeStruct((bsz, n, A_W), BF16)),
        grid=(bsz, nseg),
        in_specs=[fwd(A_KW), fwd(A_KW), fwd(A_W), fwd(A_KW), bwd(A_KW), bwd(A_KW), bwd(A_W), bwd(A_KW)],
        out_specs=(fwd(A_W), bwd(A_W)),
        scratch_shapes=[
            pltpu.VMEM((A_KW, A_W), F32), pltpu.VMEM((A_KW, A_W), F32),
            pltpu.VMEM((seg, A_KW), BF16), pltpu.VMEM((seg, A_KW), BF16), pltpu.VMEM((A_KW, LANES), F32),
            pltpu.VMEM((seg, A_KW), BF16), pltpu.VMEM((seg, A_KW), BF16), pltpu.VMEM((A_KW, LANES), F32),
            pltpu.VMEM((seg, A_KW), F32), pltpu.VMEM((LANES, A_KW), F32),
        ],
        compiler_params=_params("parallel", "arbitrary"), name="gla_rec",
    )(gq, gk, gv, gf, gq, gk, gv, gb)


def _fft1_kernel(u_ref, a1_ref, cos_ref, sin_ref, t_ref):
    n1 = u_ref.shape[1]
    nb = cos_ref.shape[0]
    t = jnp.dot(a1_ref[...], u_ref[0], preferred_element_type=F32)
    for j in range(nb):
        cols = slice(j * B_W, (j + 1) * B_W)
        tr = t[:n1, cols]
        ti = t[n1:, cols]
        c = jnp.concatenate([cos_ref[j]] * (B_W // LANES), axis=-1)
        s = jnp.concatenate([sin_ref[j]] * (B_W // LANES), axis=-1)
        t_ref[0, 0, :, cols] = (tr * c + ti * s).astype(BF16)
        t_ref[0, 1, :, cols] = (ti * c - tr * s).astype(BF16)


def _fft2_kernel(t_ref, l_ref, cc_ref, cs_ref, w_ref, o_ref):
    n2 = t_ref.shape[3]
    half = FFT_K1 * n2
    groups = range(t_ref.shape[2] // FFT_K1)
    xs = []
    for grp in groups:
        k1s = slice(grp * FFT_K1, (grp + 1) * FFT_K1)
        xs.append(jnp.concatenate([t_ref[0, 0, k1s].reshape(half, B_W), t_ref[0, 1, k1s].reshape(half, B_W)],
                                  axis=0))
    gs = [jnp.dot(l_ref[...], x, preferred_element_type=F32) for x in xs]
    mixed = [jnp.dot(g[:half].astype(BF16), cc_ref[...], preferred_element_type=F32)
             + jnp.dot(g[half:].astype(BF16), cs_ref[...], preferred_element_type=F32) for g in gs]
    ys = [jnp.dot(m.astype(BF16), w_ref[...], preferred_element_type=F32) for m in mixed]
    o_ref[0] = jnp.concatenate([y.reshape(n2, FFT_K1, B_W) for y in ys], axis=1).astype(BF16)


def _fnet(fu, lp, tabs):
    bsz, n, _ = fu.shape
    n2 = FFT_N2
    n1 = n // n2
    nb = FFT_NB
    k1b = FFT_K1 * FFT_K1_GROUPS
    u2 = fu.reshape(bsz, n1, n2 * B_W)
    tt = pl.pallas_call(
        _fft1_kernel,
        out_shape=jax.ShapeDtypeStruct((bsz, 2, n1, n2 * B_W), BF16),
        grid=(n2 // nb, bsz),
        in_specs=[
            pl.BlockSpec((1, n1, nb * B_W), lambda j, b: (b, 0, j)),
            _full((2 * n1, n1)),
            pl.BlockSpec((nb, n1, LANES), lambda j, b: (j, 0, 0)),
            pl.BlockSpec((nb, n1, LANES), lambda j, b: (j, 0, 0)),
        ],
        out_specs=pl.BlockSpec((1, 2, n1, nb * B_W), lambda j, b: (b, 0, 0, j)),
        compiler_params=_params("parallel", "parallel"), name="fft_pass1",
    )(u2, tabs["a1"], tabs["tw_cos"], tabs["tw_sin"])
    t5 = tt.reshape(bsz, 2, n1, n2, B_W)
    y = pl.pallas_call(
        _fft2_kernel,
        out_shape=jax.ShapeDtypeStruct((bsz, n2, n1, B_W), BF16),
        grid=(bsz, n1 // k1b),
        in_specs=[
            pl.BlockSpec((1, 2, k1b, n2, B_W), lambda b, i: (b, 0, i, 0, 0)),
            _full((2 * FFT_K1 * n2, 2 * FFT_K1 * n2)),
            _full((B_W, B_W)), _full((B_W, B_W)), _full((B_W, B_W)),
        ],
        out_specs=pl.BlockSpec((1, n2, k1b, B_W), lambda b, i: (b, 0, i, 0)),
        compiler_params=_params("parallel", "parallel"), name="fft_pass2",
    )(t5, tabs["l2"], tabs["chan_cos"], tabs["chan_sin"], lp["fnet_w"])
    return y.reshape(bsz, n, B_W)


def _attn_kernel(q_ref, kt_ref, va_ref, o_ref, qm_ref, m_ref, acc_ref):
    tq = q_ref.shape[1]
    n = kt_ref.shape[3]
    tk = min(ATT_TK, n)
    q = q_ref[0]
    lane = lax.broadcasted_iota(jnp.int32, q.shape, 1)
    for h in range(C_GRP):
        qm_ref[h * tq:(h + 1) * tq, :] = jnp.where(lane // C_HD == h, q, jnp.zeros_like(q))
    m_ref[...] = jnp.full(m_ref.shape, -jnp.inf, F32)
    acc_ref[...] = jnp.zeros(acc_ref.shape, F32)

    def scores(c):
        c0 = pl.multiple_of(c * tk, tk)
        return jnp.dot(qm_ref[...], kt_ref[0, 0, :, pl.ds(c0, tk)], preferred_element_type=F32)

    def softmax_accumulate(c, s):
        c0 = pl.multiple_of(c * tk, tk)
        v = va_ref[0, 0, pl.ds(c0, tk), :]
        s_max = s[:, 0:LANES]
        for t in range(1, tk // LANES):
            s_max = jnp.maximum(s_max, s[:, t * LANES:(t + 1) * LANES])
        row_max = jnp.max(s_max, axis=-1, keepdims=True)
        m_old = m_ref[...]
        m_new = jnp.maximum(m_old, row_max)
        p = jnp.exp2(s - jnp.concatenate([m_new] * (tk // LANES), axis=-1))
        alpha = jnp.exp2(m_old - m_new)
        acc_ref[...] = alpha * acc_ref[...] + jnp.dot(p.astype(BF16), v, preferred_element_type=F32)
        m_ref[...] = m_new

    def step(c, carry):
        softmax_accumulate(c, scores(c))
        return carry

    lax.fori_loop(0, n // tk, step, 0, unroll=ATT_UNROLL)
    acc = acc_ref[...]
    inv = 1.0 / acc[:, ONES_LANE:ONES_LANE + 1]
    outs = [(acc[h * tq:(h + 1) * tq, 0:C_HD] * inv[h * tq:(h + 1) * tq]) for h in range(C_GRP)]
    o_ref[0] = jnp.concatenate(outs, axis=-1).astype(BF16)


def _attention(aq, kt4, va):
    bsz, n, _ = aq.shape
    tq = min(ATT_TQ, n)
    return pl.pallas_call(
        _attn_kernel,
        out_shape=jax.ShapeDtypeStruct((bsz, n, C_W), BF16),
        grid=(bsz, C_KV, n // tq),
        in_specs=[
            pl.BlockSpec((1, tq, C_GRP * C_HD), lambda b, g, i: (b, i, g)),
            pl.BlockSpec((1, 1, C_GRP * C_HD, n), lambda b, g, i: (b, g, 0, 0)),
            pl.BlockSpec((1, 1, n, LANES), lambda b, g, i: (b, g, 0, 0)),
        ],
        out_specs=pl.BlockSpec((1, tq, C_GRP * C_HD), lambda b, g, i: (b, i, g)),
        scratch_shapes=[
            pltpu.VMEM((C_GRP * tq, C_GRP * C_HD), BF16),
            pltpu.VMEM((C_GRP * tq, LANES), F32),
            pltpu.VMEM((C_GRP * tq, LANES), F32),
        ],
        compiler_params=_params("parallel", "parallel", "arbitrary"), name="attention",
    )(aq, kt4, va)


def _outproj_kernel(x_ref, mod_ref, oa_ref, yb_ref, ao_ref, du_ref, dvn_ref, sz_ref,
                    w_ref, postg_ref, og_ref, bd_ref, sw_ref, sb_ref, o_ref):
    tm = x_ref.shape[1]
    sz = sz_ref[0].astype(F32)
    off_b, off_c, off_d = A_W, A_W + B_W, A_W + B_W + C_W

    sgu = [jnp.dot(sw_ref[...], dvn_ref[0, c * SGU_CHUNK:(c + 1) * SGU_CHUNK, :], preferred_element_type=F32)
           for c in range(tm // SGU_CHUNK)]
    m_c = (ao_ref[0].astype(F32) * sz[:, off_c:off_d]).astype(BF16)
    y = jnp.dot(m_c, w_ref[off_c:off_d, :], preferred_element_type=F32)
    m_b = (yb_ref[0].astype(F32) * sz[:, off_b:off_c]).astype(BF16)
    y = y + jnp.dot(m_b, w_ref[off_b:off_c, :], preferred_element_type=F32)

    o = oa_ref[0].astype(F32)
    oms = jnp.dot((o * o).astype(BF16), bd_ref[...], preferred_element_type=F32)
    out_a = o * lax.rsqrt(oms + EPS) * og_ref[...]
    y = y + jnp.dot((out_a * sz[:, 0:off_b]).astype(BF16), w_ref[0:off_b, :], preferred_element_type=F32)

    lane = lax.broadcasted_iota(jnp.int32, (SGU_CHUNK, D_W), 1)
    parts = []
    for c, r in enumerate(sgu):
        rows = slice(c * SGU_CHUNK, (c + 1) * SGU_CHUNK)
        mix = sb_ref[...]
        for g in range(D_GROUPS):
            mix = mix + jnp.where(lane // D_GC == g, r[g * SGU_CHUNK:(g + 1) * SGU_CHUNK, :], 0.0)
        parts.append(du_ref[0, rows, :].astype(F32) * mix)
    out_d = jnp.concatenate(parts, axis=0)
    m_d = (out_d * sz[:, off_d:]).astype(BF16)
    y = y + jnp.dot(m_d, w_ref[off_d:, :], preferred_element_type=F32)

    yms = jnp.mean(y * y, axis=-1, keepdims=True)
    yn = y * lax.rsqrt(yms + EPS) * postg_ref[...]
    gate = mod_ref[0, :, 2 * D_MODEL:3 * D_MODEL]
    o_ref[0] = x_ref[0] + gate * yn


def _outproj(x, mod, oa, yb, ao, du, dvn, sz, lp):
    bsz, n, _ = x.shape
    tm = min(TM_OUT, n)
    tok = lambda w: pl.BlockSpec((1, tm, w), lambda b, i: (b, i, 0))
    return pl.pallas_call(
        _outproj_kernel,
        out_shape=jax.ShapeDtypeStruct((bsz, n, D_MODEL), F32),
        grid=(bsz, n // tm),
        in_specs=[
            tok(D_MODEL),
            pl.BlockSpec((1, 1, 3 * D_MODEL), lambda b, i: (b, 0, 0)),
            tok(A_W), tok(B_W), tok(C_W), tok(D_W), tok(D_W), tok(D_MIX),
            _full((D_MIX, D_MODEL)), _full((1, D_MODEL)), _full((1, A_W)), _full((A_W, A_W)),
            _full((D_GROUPS * SGU_CHUNK, SGU_CHUNK)), _full((SGU_CHUNK, D_W)),
        ],
        out_specs=tok(D_MODEL),
        compiler_params=_params("parallel", "parallel"), name="outproj",
    )(x, mod, oa, yb, ao, du, dvn, sz,
      lp["w_out"], lp["post_g"], lp["og"], lp["bd_a"], lp["sgu_w"], lp["sgu_b"])


def _pack_w_in(w):
    sizes = [A_KW, A_KW, A_W, 2 * GLA_RANK, B_W, C_W, C_KVW, C_KVW, D_W, D_W, D_MIX]
    off = [0] + [int(o) for o in np.cumsum(sizes)]
    a_q, a_k, a_v, a_lr, b_u, c_q, c_k, c_v, d_u, d_v, z = [w[:, off[i]:off[i + 1]] for i in range(len(sizes))]

    def deinterleave(x, heads):
        r = x.reshape(D_MODEL, heads, C_HD // 2, 2)
        return jnp.concatenate([r[..., 0], r[..., 1]], axis=-1).reshape(D_MODEL, heads * C_HD)

    zeros = lambda width: jnp.zeros((D_MODEL, width), w.dtype)
    pieces = [a_q, a_k, a_v, b_u, deinterleave(c_q, C_HEADS), deinterleave(c_k, C_KV),
              a_lr, zeros(LANES - 2 * GLA_RANK)]
    for h in range(C_KV):
        pieces += [c_v[:, h * C_HD:(h + 1) * C_HD], zeros(LANES - C_HD)]
    pieces += [d_u, d_v, z]
    packed = jnp.concatenate(pieces, axis=1)
    assert packed.shape[1] == W_PACKED
    return packed.astype(BF16)


def _block_diag_mean(width, group):
    idx = np.arange(width) // group
    return jnp.asarray((idx[:, None] == idx[None, :]).astype(np.float32) / group, BF16)


def _layer_params(l, ada_unused, norm_pre_g, norm_post_g, w_in, gla_wg2_f, gla_bg_f, gla_wg2_b, gla_bg_b,
                  gla_onorm_g, fnet_w, q_norm_g, k_norm_g, sgu_norm_g, sgu_w, sgu_b, w_out):
    deint = np.concatenate([np.arange(0, C_HD, 2), np.arange(1, C_HD, 2)])
    w = _pack_w_in(w_in[l])
    wg2 = jnp.zeros((LANES, 2 * A_KW), F32)
    wg2 = wg2.at[0:GLA_RANK, 0:A_KW].set(gla_wg2_f[l])
    wg2 = wg2.at[GLA_RANK:2 * GLA_RANK, A_KW:].set(gla_wg2_b[l])
    vone = np.zeros((1, 2 * LANES), np.float32)
    vone[0, ONES_LANE] = 1.0
    vone[0, LANES + ONES_LANE] = 1.0
    return {
        "pre_g": norm_pre_g[l].reshape(1, D_MODEL),
        "post_g": norm_post_g[l].reshape(1, D_MODEL),
        "w_in": w,
        "wg2": wg2.astype(BF16),
        "bg": jnp.concatenate([gla_bg_f[l], gla_bg_b[l]]).reshape(1, 2 * A_KW),
        "qg": jnp.tile(q_norm_g[l][deint], C_HEADS).reshape(1, C_W),
        "kg": jnp.tile(k_norm_g[l][deint], C_KV).reshape(1, C_KVW),
        "sg": sgu_norm_g[l].reshape(1, D_W),
        "bd_q": _block_diag_mean(C_W, C_HD),
        "bd_a": _block_diag_mean(A_W, A_DV),
        "vone": jnp.asarray(vone),
        "og": jnp.tile(gla_onorm_g[l], A_HEADS).reshape(1, A_W),
        "fnet_w": fnet_w[l].astype(BF16),
        "sgu_w": sgu_w[l].reshape(D_GROUPS * SGU_CHUNK, SGU_CHUNK).astype(BF16),
        "sgu_b": jnp.repeat(sgu_b[l].T, D_GC, axis=1),
        "w_out": w_out[l].astype(BF16),
    }


def _seq_tables(n):
    rows = n // GRID_W
    row = jnp.repeat(jnp.arange(rows, dtype=F32), GRID_W)
    col = jnp.tile(jnp.arange(GRID_W, dtype=F32), rows)
    rope_axis = C_HD // 2
    freqs = ROPE_THETA ** (-jnp.arange(0, rope_axis, 2, dtype=F32) / rope_axis)
    ang = jnp.concatenate([row[:, None] * freqs, col[:, None] * freqs], axis=-1)
    cos, sin = jnp.cos(ang), jnp.sin(ang)
    cos_t = jnp.tile(jnp.concatenate([cos, cos], axis=-1), (1, LANES // C_HD))
    sin_t = jnp.tile(jnp.concatenate([-sin, sin], axis=-1), (1, LANES // C_HD))

    n2 = FFT_N2
    n1 = n // n2

    def dft_angles(rows, cols, period):
        prod = (jnp.arange(rows, dtype=jnp.int32)[:, None] * jnp.arange(cols, dtype=jnp.int32)[None, :]) % period
        return prod.astype(F32) * (2.0 * math.pi / period)

    a1_ang = dft_angles(n1, n1, n1)
    a1 = jnp.concatenate([jnp.cos(a1_ang), -jnp.sin(a1_ang)], axis=0)
    tw_ang = dft_angles(n2, n1, n)
    scale = 1.0 / math.sqrt(n * B_GC)
    tw_cos = jnp.broadcast_to((jnp.cos(tw_ang) * scale)[:, :, None], (n2, n1, LANES))
    tw_sin = jnp.broadcast_to((jnp.sin(tw_ang) * scale)[:, :, None], (n2, n1, LANES))
    ang2 = dft_angles(n2, n2, n2)
    c2, s2 = jnp.cos(ang2), jnp.sin(ang2)
    eye = jnp.eye(FFT_K1, dtype=F32)
    blk = lambda m: (m[:, None, None, :] * eye[None, :, :, None]).reshape(n2 * FFT_K1, FFT_K1 * n2)
    l2 = jnp.block([[blk(c2), blk(s2)], [blk(-s2), blk(c2)]])
    angc = dft_angles(B_GC, B_GC, B_GC)
    grp = jnp.eye(B_GROUPS, dtype=F32)
    kron = lambda m: (grp[:, None, :, None] * m[None, :, None, :]).reshape(B_W, B_W)
    chan_cos = kron(jnp.cos(angc))
    chan_sin = kron(jnp.sin(angc))
    return {
        "cos": cos_t, "sin": sin_t,
        "a1": a1.astype(BF16),
        "tw_cos": tw_cos, "tw_sin": tw_sin,
        "l2": l2.astype(BF16),
        "chan_cos": chan_cos.astype(BF16), "chan_sin": chan_sin.astype(BF16),
    }


def _layer(x, mod, lp, tabs):
    gq, gk, gv, gf, gb, fu, aq, kt4, va, du, dvn, sz = _inproj(x, mod, lp, tabs)
    of, ob = _gla_rec(gq, gk, gv, gf, gb)
    oa = _gla_band(gq, gk, gv, gf, gb, of, ob)
    yb = _fnet(fu, lp, tabs)
    ao = _attention(aq, kt4, va)
    return _outproj(x, mod, oa, yb, ao, du, dvn, sz, lp)


def kernel(x_prompt, x_sample, c_prompt, c_sample, ada_w, ada_b, norm_pre_g, norm_post_g, w_in,
           gla_wg2_f, gla_bg_f, gla_wg2_b, gla_bg_b, gla_onorm_g, fnet_w, q_norm_g, k_norm_g,
           sgu_norm_g, sgu_w, sgu_b, w_out):
    bp, bs = x_prompt.shape[0], x_sample.shape[0]
    pad_rows = (-(bp + bs)) % 8
    c_all = jnp.concatenate([c_prompt, c_sample, jnp.zeros((pad_rows, D_MODEL), F32)], axis=0)
    mod = _adaln(c_all, ada_w, ada_b)
    tabs_p = _seq_tables(x_prompt.shape[1])
    tabs_s = _seq_tables(x_sample.shape[1])
    y_prompt, y_sample = x_prompt, x_sample
    for l in range(DEPTH):
        lp = _layer_params(l, None, norm_pre_g, norm_post_g, w_in, gla_wg2_f, gla_bg_f, gla_wg2_b,
                           gla_bg_b, gla_onorm_g, fnet_w, q_norm_g, k_norm_g, sgu_norm_g, sgu_w,
                           sgu_b, w_out)
        mod_p = mod[l, 0:bp].reshape(bp, 1, 3 * D_MODEL)
        mod_s = mod[l, bp:bp + bs].reshape(bs, 1, 3 * D_MODEL)
        y_prompt = _layer(y_prompt, mod_p, lp, tabs_p)
        y_sample = _layer(y_sample, mod_s, lp, tabs_s)
    return (y_prompt, y_sample)
```

```python
import math

import numpy as np
import jax
import jax.numpy as jnp
from jax import lax
from jax.experimental import pallas as pl
from jax.experimental.pallas import tpu as pltpu

F32 = jnp.float32
BF16 = jnp.bfloat16
HIGHEST = lax.Precision.HIGHEST

D_MODEL = 1024
DEPTH = 2
GRID_W = 64
A_HEADS, A_DK, A_DV = 4, 32, 64
A_W, A_KW = A_HEADS * A_DV, A_HEADS * A_DK
GLA_RANK = 16
GLA_GATE_NORM = 16.0
B_GROUPS, B_GC = 4, 64
B_W = B_GROUPS * B_GC
C_HEADS, C_KV, C_HD = 8, 2, 64
C_GRP = C_HEADS // C_KV
C_W, C_KVW = C_HEADS * C_HD, C_KV * C_HD
ROPE_THETA = 10000.0
D_GROUPS, D_GC = 4, 64
D_W = D_GROUPS * D_GC
SGU_CHUNK = 128
D_MIX = A_W + B_W + C_W + D_W
EPS = 1e-6
LOG2E = 1.4426950408889634

LANES = 128
VMEM_LIMIT_BYTES = 56 * 1024 * 1024

GQ, GK, GV, FU, AQ, AK, LR, AV, DU, DV, ZZ = 0, 128, 256, 512, 768, 1280, 1408, 1536, 1792, 2048, 2304
W_PACKED = 3584
AV_W = 2 * LANES
ONES_LANE = C_HD

TM_IN = 1024
IN_ROW_PARTS = 4
TM_OUT = 1024
GLA_SUB = 16
GLA_BAND_TILE = 64
GLA_PITCH = GLA_SUB + 4
GLA_GRP = 128
GLA_SEG = 1024
GLA_LOOKAHEAD = 3
FFT_N2 = 64
FFT_NB = 16
FFT_K1 = 8
FFT_K1_GROUPS = 2
ATT_TQ = 512
ATT_TK = 2048
ATT_UNROLL = 2


def _params(*sem):
    return pltpu.CompilerParams(dimension_semantics=sem, vmem_limit_bytes=VMEM_LIMIT_BYTES)


def _full(shape):
    n = len(shape)
    return pl.BlockSpec(shape, lambda *_: (0,) * n)


def _adaln_kernel(c_ref, w_ref, b_ref, o_ref):
    c = c_ref[...]
    sc = c / (1.0 + jnp.exp(-c))
    o_ref[0] = jnp.dot(sc, w_ref[0], precision=HIGHEST, preferred_element_type=F32) + b_ref[0]


def _adaln(c_all, ada_w, ada_b):
    rows = c_all.shape[0]
    ncol = 3 * D_MODEL // D_MODEL
    return pl.pallas_call(
        _adaln_kernel,
        out_shape=jax.ShapeDtypeStruct((DEPTH, rows, 3 * D_MODEL), F32),
        grid=(DEPTH, ncol),
        in_specs=[
            pl.BlockSpec((rows, D_MODEL), lambda l, j: (0, 0)),
            pl.BlockSpec((1, D_MODEL, D_MODEL), lambda l, j: (l, 0, j)),
            pl.BlockSpec((1, 1, D_MODEL), lambda l, j: (l, 0, j)),
        ],
        out_specs=pl.BlockSpec((1, rows, D_MODEL), lambda l, j: (l, 0, j)),
        compiler_params=_params("arbitrary", "arbitrary"),
        name="adaln",
    )(c_all, ada_w, ada_b.reshape(DEPTH, 1, 3 * D_MODEL))


def _rope(x, cos, sin_signed):
    width = x.shape[-1]
    lane = lax.broadcasted_iota(jnp.int32, x.shape, 1)
    first_half = (lane % C_HD) < (C_HD // 2)
    swapped = jnp.where(first_half, pltpu.roll(x, width - C_HD // 2, 1), pltpu.roll(x, C_HD // 2, 1))
    reps = width // LANES
    cos_w = jnp.concatenate([cos] * reps, axis=-1) if reps > 1 else cos
    sin_w = jnp.concatenate([sin_signed] * reps, axis=-1) if reps > 1 else sin_signed
    return x * cos_w + swapped * sin_w


def _inproj_kernel(x_ref, mod_ref, preg_ref, w_ref, wg2_ref, bg_ref, qg_ref, kg_ref, sg_ref,
                   cos_ref, sin_ref, bd_ref, vone_ref,
                   gq_ref, gk_ref, gv_ref, gf_ref, gb_ref, fu_ref, aq_ref, kt_ref, va_ref,
                   du_ref, dvn_ref, sz_ref):
    tm = x_ref.shape[1]
    shift = mod_ref[0, :, 0:D_MODEL]
    gain = preg_ref[...] * (1.0 + mod_ref[0, :, D_MODEL:2 * D_MODEL])
    for part in range(IN_ROW_PARTS):
        rows = slice(part * tm // IN_ROW_PARTS, (part + 1) * tm // IN_ROW_PARTS)
        x = x_ref[0, rows, :]
        ms = jnp.mean(x * x, axis=-1, keepdims=True)
        hb = ((x * lax.rsqrt(ms + EPS)) * gain + shift).astype(BF16)

        def seg(off, width):
            return jnp.dot(hb, w_ref[:, off:off + width], preferred_element_type=F32)

        klr = seg(AK, C_KVW + LANES)
        q = seg(AQ, C_W)
        z = seg(ZZ, D_MIX)
        gqk = seg(GQ, 2 * A_KW)
        gv = seg(GV, A_W)
        fu = seg(FU, B_W)
        va = seg(AV, AV_W)
        du = seg(DU, D_W)
        dv = seg(DV, D_W)
        k = klr[:, :C_KVW]
        lr = klr[:, C_KVW:].astype(BF16)
        logits = jnp.dot(lr, wg2_ref[...], preferred_element_type=F32) + bg_ref[...]
        qms = jnp.dot((q * q).astype(BF16), bd_ref[...], preferred_element_type=F32)
        kms = jnp.dot((k * k).astype(BF16), bd_ref[0:C_KVW, 0:C_KVW], preferred_element_type=F32)

        gq_ref[0, rows, :] = gqk[:, :A_KW] * (A_DK ** -0.5)
        gk_ref[0, rows, :] = gqk[:, A_KW:]
        gv_ref[0, rows, :] = gv
        logg = (jnp.minimum(logits, 0.0) - jnp.log1p(jnp.exp(-jnp.abs(logits)))) * (1.0 / GLA_GATE_NORM)
        gf_ref[0, rows, :] = logg[:, :A_KW]
        gb_ref[0, rows, :] = logg[:, A_KW:]

        fu_ref[0, rows, :] = fu.astype(BF16)

        cos = cos_ref[rows, :]
        sin = sin_ref[rows, :]
        qn = q * lax.rsqrt(qms + EPS) * qg_ref[...]
        aq_ref[0, rows, :] = (_rope(qn, cos, sin) * (C_HD ** -0.5 * LOG2E)).astype(BF16)
        kn = k * lax.rsqrt(kms + EPS) * kg_ref[...]
        kt = _rope(kn, cos, sin).T.astype(BF16)
        for g in range(C_KV):
            for r in range(C_GRP):
                kt_ref[0, g, r * C_HD:(r + 1) * C_HD, rows] = kt[g * C_HD:(g + 1) * C_HD, :]
        vab = (va + vone_ref[...]).astype(BF16)
        va_ref[0, 0, rows, :] = vab[:, :LANES]
        va_ref[0, 1, rows, :] = vab[:, LANES:]

        du_ref[0, rows, :] = du.astype(BF16)
        dms = jnp.mean(dv * dv, axis=-1, keepdims=True)
        dvn_ref[0, rows, :] = (dv * lax.rsqrt(dms + EPS) * sg_ref[...]).astype(BF16)

        sz_ref[0, rows, :] = (z / (1.0 + jnp.exp(-z))).astype(BF16)


def _inproj(x, mod, lp, tabs):
    bsz, n, _ = x.shape
    tm = min(TM_IN, n)
    grid = (bsz, n // tm)
    tok = lambda w: pl.BlockSpec((1, tm, w), lambda b, i: (b, i, 0))
    out_shapes = (
        jax.ShapeDtypeStruct((bsz, n, A_KW), F32),
        jax.ShapeDtypeStruct((bsz, n, A_KW), F32),
        jax.ShapeDtypeStruct((bsz, n, A_W), F32),
        jax.ShapeDtypeStruct((bsz, n, A_KW), F32),
        jax.ShapeDtypeStruct((bsz, n, A_KW), F32),
        jax.ShapeDtypeStruct((bsz, n, B_W), BF16),
        jax.ShapeDtypeStruct((bsz, n, C_W), BF16),
        jax.ShapeDtypeStruct((bsz, C_KV, C_GRP * C_HD, n), BF16),
        jax.ShapeDtypeStruct((bsz, C_KV, n, LANES), BF16),
        jax.ShapeDtypeStruct((bsz, n, D_W), BF16),
        jax.ShapeDtypeStruct((bsz, n, D_W), BF16),
        jax.ShapeDtypeStruct((bsz, n, D_MIX), BF16),
    )
    out_specs = (
        tok(A_KW), tok(A_KW), tok(A_W), tok(A_KW), tok(A_KW), tok(B_W), tok(C_W),
        pl.BlockSpec((1, C_KV, C_GRP * C_HD, tm), lambda b, i: (b, 0, 0, i)),
        pl.BlockSpec((1, C_KV, tm, LANES), lambda b, i: (b, 0, i, 0)),
        tok(D_W), tok(D_W), tok(D_MIX),
    )
    in_specs = [
        tok(D_MODEL),
        pl.BlockSpec((1, 1, 3 * D_MODEL), lambda b, i: (b, 0, 0)),
        _full((1, D_MODEL)),
        _full((D_MODEL, W_PACKED)),
        _full((LANES, 2 * A_KW)),
        _full((1, 2 * A_KW)),
        _full((1, C_W)),
        _full((1, C_KVW)),
        _full((1, D_W)),
        pl.BlockSpec((tm, LANES), lambda b, i: (i, 0)),
        pl.BlockSpec((tm, LANES), lambda b, i: (i, 0)),
        _full((C_W, C_W)),
        _full((1, 2 * LANES)),
    ]
    return pl.pallas_call(
        _inproj_kernel, out_shape=out_shapes, grid=grid, in_specs=in_specs, out_specs=out_specs,
        compiler_params=_params("parallel", "parallel"), name="inproj",
    )(x, mod, lp["pre_g"], lp["w_in"], lp["wg2"], lp["bg"], lp["qg"], lp["kg"], lp["sg"],
      tabs["cos"], tabs["sin"], lp["bd_q"], lp["vone"])


def _head_match(shape, rows_per_head, cols_per_head):
    return (lax.broadcasted_iota(jnp.int32, shape, 0) // rows_per_head
            == lax.broadcasted_iota(jnp.int32, shape, 1) // cols_per_head)


def _gla_band_kernel(q_ref, k_ref, v_ref, gf_ref, gb_ref, of_ref, ob_ref, o_ref,
                     qf_ref, kf_ref, vf_ref, bf_ref, cb_ref, pit_ref, os_ref):
    rows = q_ref.shape[1]
    ts = rows // GLA_SUB
    n_slab = A_W // LANES
    every = lambda p: pl.ds(p, ts, stride=GLA_PITCH)
    hsum = jnp.where(_head_match((A_KW, A_W), A_DK, A_DV), 1.0, 0.0).astype(BF16)
    slab_q, slab_k, slab_gf, slab_gb, slab_v = 0, 1, 2, 3, 4
    for i in range(ts):
        src = slice(i * GLA_SUB, (i + 1) * GLA_SUB)
        dst = slice(i * GLA_PITCH, i * GLA_PITCH + GLA_SUB)
        pit_ref[slab_q, dst, :] = q_ref[0, src, :]
        pit_ref[slab_k, dst, :] = k_ref[0, src, :]
        pit_ref[slab_gf, dst, :] = gf_ref[0, src, :]
        pit_ref[slab_gb, dst, :] = gb_ref[0, src, :]
        for h in range(n_slab):
            pit_ref[slab_v + h, dst, :] = v_ref[0, src, h * LANES:(h + 1) * LANES]
    run = jnp.zeros((ts, A_KW), F32)
    for p in range(GLA_SUB):
        run = run + pit_ref[slab_gf, every(p), :] * LOG2E
        bf_ref[p] = run
        qf_ref[p] = pit_ref[slab_q, every(p), :]
        kf_ref[p] = pit_ref[slab_k, every(p), :]
        vf_ref[p] = jnp.concatenate([pit_ref[slab_v + h, every(p), :] for h in range(n_slab)], axis=-1)
    run = jnp.zeros((ts, A_KW), F32)
    for p in reversed(range(GLA_SUB)):
        run = run + pit_ref[slab_gb, every(p), :] * LOG2E
        cb_ref[p] = run
    for p in range(GLA_SUB):
        q = qf_ref[p]
        prods = []
        for s in range(GLA_SUB):
            qk = q * kf_ref[s]
            if s < p:
                qk = qk * jnp.exp2(bf_ref[p] - bf_ref[s])
            elif s > p:
                qk = qk * jnp.exp2(cb_ref[p] - cb_ref[s])
            prods.append(qk.astype(BF16))
        a = jnp.dot(jnp.concatenate(prods, axis=0), hsum, preferred_element_type=F32)
        acc = a[0:ts] * vf_ref[0]
        for s in range(1, GLA_SUB):
            acc = acc + a[s * ts:(s + 1) * ts] * vf_ref[s]
        for h in range(n_slab):
            os_ref[h, every(p), :] = acc[:, h * LANES:(h + 1) * LANES]
    for i in range(ts):
        src = slice(i * GLA_PITCH, i * GLA_PITCH + GLA_SUB)
        dst = slice(i * GLA_SUB, (i + 1) * GLA_SUB)
        band = jnp.concatenate([os_ref[h, src, :] for h in range(n_slab)], axis=-1)
        o_ref[0, dst, :] = (band + of_ref[0, dst, :].astype(F32) + ob_ref[0, dst, :].astype(F32)).astype(BF16)


def _gla_band(gq, gk, gv, gf, gb, of, ob):
    bsz, n, _ = gq.shape
    rows = min(GLA_BAND_TILE * GLA_SUB, n)
    ts = rows // GLA_SUB
    blk = lambda w: pl.BlockSpec((1, rows, w), lambda b, i: (b, i, 0))
    return pl.pallas_call(
        _gla_band_kernel,
        out_shape=jax.ShapeDtypeStruct((bsz, n, A_W), BF16),
        grid=(bsz, n // rows),
        in_specs=[blk(A_KW), blk(A_KW), blk(A_W), blk(A_KW), blk(A_KW), blk(A_W), blk(A_W)],
        out_specs=blk(A_W),
        scratch_shapes=[
            pltpu.VMEM((GLA_SUB, ts, A_KW), F32), pltpu.VMEM((GLA_SUB, ts, A_KW), F32),
            pltpu.VMEM((GLA_SUB, ts, A_W), F32),
            pltpu.VMEM((GLA_SUB, ts, A_KW), F32), pltpu.VMEM((GLA_SUB, ts, A_KW), F32),
            pltpu.VMEM((4 + A_W // LANES, ts * GLA_PITCH, LANES), F32),
            pltpu.VMEM((A_W // LANES, ts * GLA_PITCH, LANES), F32),
        ],
        compiler_params=_params("parallel", "parallel"), name="gla_band",
    )(gq, gk, gv, gf, gb, of, ob)


def _split3(x):
    hi = x.astype(BF16)
    rest = x - hi.astype(F32)
    mid = rest.astype(BF16)
    lo = (rest - mid.astype(F32)).astype(BF16)
    return hi, mid, lo


def _gla_prep(q_ref, k_ref, g_ref, qt_ref, kh_ref, dect_ref, tot_ref, pad_ref, reverse):
    seg = q_ref.shape[1]
    n_sub = seg // GLA_SUB
    r_i = lax.broadcasted_iota(jnp.int32, (GLA_GRP, GLA_GRP), 0)
    c_i = lax.broadcasted_iota(jnp.int32, (GLA_GRP, GLA_GRP), 1)
    same = (r_i // GLA_SUB) == (c_i // GLA_SUB)
    tri = (c_i >= r_i) if reverse else (c_i <= r_i)
    sums_mat = jnp.concatenate([jnp.where(same & tri, 1.0, 0.0), jnp.where(same, 1.0, 0.0)], axis=0).astype(BF16)
    for r in range(seg // GLA_GRP):
        rows = slice(r * GLA_GRP, (r + 1) * GLA_GRP)
        g3 = jnp.concatenate(_split3(g_ref[0, rows, :]), axis=-1)
        sums = jnp.dot(sums_mat, g3, preferred_element_type=F32)
        sums = sums[:, 0:A_KW] + sums[:, A_KW:2 * A_KW] + sums[:, 2 * A_KW:]
        cum, tot = sums[:GLA_GRP], sums[GLA_GRP:]
        qt_ref[rows, :] = (q_ref[0, rows, :] * jnp.exp(cum)).astype(BF16)
        kh_ref[rows, :] = (k_ref[0, rows, :] * jnp.exp(tot - cum)).astype(BF16)
        tot_ref[rows, :] = tot
    pad_ref[...] = jnp.zeros_like(pad_ref)
    pad_ref[0:n_sub, :] = tot_ref[pl.ds(0, n_sub, stride=GLA_SUB), :]
    dect_ref[...] = jnp.exp(pad_ref[...].T)


def _gla_rec_kernel(qf_ref, kf_ref, vf_ref, gf_ref, qb_ref, kb_ref, vb_ref, gb_ref, of_ref, ob_ref,
                    sf_ref, sb_ref, qtf_ref, khf_ref, dtf_ref, qtb_ref, khb_ref, dtb_ref, tot_ref, pad_ref):
    @pl.when(pl.program_id(1) == 0)
    def _():
        sf_ref[...] = jnp.zeros_like(sf_ref)
        sb_ref[...] = jnp.zeros_like(sb_ref)

    seg = qf_ref.shape[1]
    n_sub = seg // GLA_SUB
    _gla_prep(qf_ref, kf_ref, gf_ref, qtf_ref, khf_ref, dtf_ref, tot_ref, pad_ref, reverse=False)
    _gla_prep(qb_ref, kb_ref, gb_ref, qtb_ref, khb_ref, dtb_ref, tot_ref, pad_ref, reverse=True)
    smask = _head_match((A_KW, A_W), A_DK, A_DV)

    def update(i, kh_ref, v_ref):
        rows = slice(i * GLA_SUB, (i + 1) * GLA_SUB)
        upd = lax.dot_general(kh_ref[rows, :], v_ref[0, rows, :].astype(BF16), (((0,), (0,)), ((), ())),
                              preferred_element_type=F32)
        return jnp.where(smask, upd, 0.0)

    def step(i, s, upd, qt_ref, dect_ref, o_ref):
        rows = slice(i * GLA_SUB, (i + 1) * GLA_SUB)
        o_ref[0, rows, :] = jnp.dot(qt_ref[rows, :], s.astype(BF16), preferred_element_type=F32).astype(BF16)
        return s * dect_ref[:, i:i + 1] + upd

    order_f = list(range(n_sub))
    order_b = list(reversed(range(n_sub)))
    upd_f = [update(i, khf_ref, vf_ref) for i in order_f[:GLA_LOOKAHEAD]]
    upd_b = [update(i, khb_ref, vb_ref) for i in order_b[:GLA_LOOKAHEAD]]
    s_f = sf_ref[...]
    s_b = sb_ref[...]
    for j in range(n_sub):
        if j + GLA_LOOKAHEAD < n_sub:
            upd_f.append(update(order_f[j + GLA_LOOKAHEAD], khf_ref, vf_ref))
            upd_b.append(update(order_b[j + GLA_LOOKAHEAD], khb_ref, vb_ref))
        s_f = step(order_f[j], s_f, upd_f[j], qtf_ref, dtf_ref, of_ref)
        s_b = step(order_b[j], s_b, upd_b[j], qtb_ref, dtb_ref, ob_ref)
    sf_ref[...] = s_f
    sb_ref[...] = s_b


def _gla_rec(gq, gk, gv, gf, gb):
    bsz, n, _ = gq.shape
    seg = min(GLA_SEG, n)
    nseg = n // seg
    assert seg // GLA_SUB <= LANES
    fwd = lambda w: pl.BlockSpec((1, seg, w), lambda b, s: (b, s, 0))
    bwd = lambda w: pl.BlockSpec((1, seg, w), lambda b, s: (b, nseg - 1 - s, 0))
    return pl.pallas_call(
        _gla_rec_kernel,
        out_shape=(jax.ShapeDtypeStruct((bsz, n, A_W), BF16), jax.ShapeDtypeStruct((bsz, n, A_W), BF16)),
        grid=(bsz, nseg),
        in_specs=[fwd(A_KW), fwd(A_KW), fwd(A_W), fwd(A_KW), bwd(A_KW), bwd(A_KW), bwd(A_W), bwd(A_KW)],
        out_specs=(fwd(A_W), bwd(A_W)),
        scratch_shapes=[
            pltpu.VMEM((A_KW, A_W), F32), pltpu.VMEM((A_KW, A_W), F32),
            pltpu.VMEM((seg, A_KW), BF16), pltpu.VMEM((seg, A_KW), BF16), pltpu.VMEM((A_KW, LANES), F32),
            pltpu.VMEM((seg, A_KW), BF16), pltpu.VMEM((seg, A_KW), BF16), pltpu.VMEM((A_KW, LANES), F32),
            pltpu.VMEM((seg, A_KW), F32), pltpu.VMEM((LANES, A_KW), F32),
        ],
        compiler_params=_params("parallel", "arbitrary"), name="gla_rec",
    )(gq, gk, gv, gf, gq, gk, gv, gb)


def _fft1_kernel(u_ref, a1_ref, cos_ref, sin_ref, t_ref):
    n1 = u_ref.shape[1]
    nb = cos_ref.shape[0]
    t = jnp.dot(a1_ref[...], u_ref[0], preferred_element_type=F32)
    for j in range(nb):
        cols = slice(j * B_W, (j + 1) * B_W)
        tr = t[:n1, cols]
        ti = t[n1:, cols]
        c = jnp.concatenate([cos_ref[j]] * (B_W // LANES), axis=-1)
        s = jnp.concatenate([sin_ref[j]] * (B_W // LANES), axis=-1)
        t_ref[0, 0, :, cols] = (tr * c + ti * s).astype(BF16)
        t_ref[0, 1, :, cols] = (ti * c - tr * s).astype(BF16)


def _fft2_kernel(t_ref, l_ref, cc_ref, cs_ref, w_ref, o_ref):
    n2 = t_ref.shape[3]
    half = FFT_K1 * n2
    groups = range(t_ref.shape[2] // FFT_K1)
    xs = []
    for grp in groups:
        k1s = slice(grp * FFT_K1, (grp + 1) * FFT_K1)
        xs.append(jnp.concatenate([t_ref[0, 0, k1s].reshape(half, B_W), t_ref[0, 1, k1s].reshape(half, B_W)],
                                  axis=0))
    gs = [jnp.dot(l_ref[...], x, preferred_element_type=F32) for x in xs]
    mixed = [jnp.dot(g[:half].astype(BF16), cc_ref[...], preferred_element_type=F32)
             + jnp.dot(g[half:].astype(BF16), cs_ref[...], preferred_element_type=F32) for g in gs]
    ys = [jnp.dot(m.astype(BF16), w_ref[...], preferred_element_type=F32) for m in mixed]
    o_ref[0] = jnp.concatenate([y.reshape(n2, FFT_K1, B_W) for y in ys], axis=1).astype(BF16)


def _fnet(fu, lp, tabs):
    bsz, n, _ = fu.shape
    n2 = FFT_N2
    n1 = n // n2
    nb = FFT_NB
    k1b = FFT_K1 * FFT_K1_GROUPS
    u2 = fu.reshape(bsz, n1, n2 * B_W)
    tt = pl.pallas_call(
        _fft1_kernel,
        out_shape=jax.ShapeDtypeStruct((bsz, 2, n1, n2 * B_W), BF16),
        grid=(n2 // nb, bsz),
        in_specs=[
            pl.BlockSpec((1, n1, nb * B_W), lambda j, b: (b, 0, j)),
            _full((2 * n1, n1)),
            pl.BlockSpec((nb, n1, LANES), lambda j, b: (j, 0, 0)),
            pl.BlockSpec((nb, n1, LANES), lambda j, b: (j, 0, 0)),
        ],
        out_specs=pl.BlockSpec((1, 2, n1, nb * B_W), lambda j, b: (b, 0, 0, j)),
        compiler_params=_params("parallel", "parallel"), name="fft_pass1",
    )(u2, tabs["a1"], tabs["tw_cos"], tabs["tw_sin"])
    t5 = tt.reshape(bsz, 2, n1, n2, B_W)
    y = pl.pallas_call(
        _fft2_kernel,
        out_shape=jax.ShapeDtypeStruct((bsz, n2, n1, B_W), BF16),
        grid=(bsz, n1 // k1b),
        in_specs=[
            pl.BlockSpec((1, 2, k1b, n2, B_W), lambda b, i: (b, 0, i, 0, 0)),
            _full((2 * FFT_K1 * n2, 2 * FFT_K1 * n2)),
            _full((B_W, B_W)), _full((B_W, B_W)), _full((B_W, B_W)),
        ],
        out_specs=pl.BlockSpec((1, n2, k1b, B_W), lambda b, i: (b, 0, i, 0)),
        compiler_params=_params("parallel", "parallel"), name="fft_pass2",
    )(t5, tabs["l2"], tabs["chan_cos"], tabs["chan_sin"], lp["fnet_w"])
    return y.reshape(bsz, n, B_W)


def _attn_kernel(q_ref, kt_ref, va_ref, o_ref, qm_ref, m_ref, acc_ref):
    tq = q_ref.shape[1]
    n = kt_ref.shape[3]
    tk = min(ATT_TK, n)
    q = q_ref[0]
    lane = lax.broadcasted_iota(jnp.int32, q.shape, 1)
    for h in range(C_GRP):
        qm_ref[h * tq:(h + 1) * tq, :] = jnp.where(lane // C_HD == h, q, jnp.zeros_like(q))
    m_ref[...] = jnp.full(m_ref.shape, -jnp.inf, F32)
    acc_ref[...] = jnp.zeros(acc_ref.shape, F32)

    def scores(c):
        c0 = pl.multiple_of(c * tk, tk)
        return jnp.dot(qm_ref[...], kt_ref[0, 0, :, pl.ds(c0, tk)], preferred_element_type=F32)

    def softmax_accumulate(c, s):
        c0 = pl.multiple_of(c * tk, tk)
        v = va_ref[0, 0, pl.ds(c0, tk), :]
        s_max = s[:, 0:LANES]
        for t in range(1, tk // LANES):
            s_max = jnp.maximum(s_max, s[:, t * LANES:(t + 1) * LANES])
        row_max = jnp.max(s_max, axis=-1, keepdims=True)
        m_old = m_ref[...]
        m_new = jnp.maximum(m_old, row_max)
        p = jnp.exp2(s - jnp.concatenate([m_new] * (tk // LANES), axis=-1))
        alpha = jnp.exp2(m_old - m_new)
        acc_ref[...] = alpha * acc_ref[...] + jnp.dot(p.astype(BF16), v, preferred_element_type=F32)
        m_ref[...] = m_new

    def step(c, carry):
        softmax_accumulate(c, scores(c))
        return carry

    lax.fori_loop(0, n // tk, step, 0, unroll=ATT_UNROLL)
    acc = acc_ref[...]
    inv = 1.0 / acc[:, ONES_LANE:ONES_LANE + 1]
    outs = [(acc[h * tq:(h + 1) * tq, 0:C_HD] * inv[h * tq:(h + 1) * tq]) for h in range(C_GRP)]
    o_ref[0] = jnp.concatenate(outs, axis=-1).astype(BF16)


def _attention(aq, kt4, va):
    bsz, n, _ = aq.shape
    tq = min(ATT_TQ, n)
    return pl.pallas_call(
        _attn_kernel,
        out_shape=jax.ShapeDtypeStruct((bsz, n, C_W), BF16),
        grid=(bsz, C_KV, n // tq),
        in_specs=[
            pl.BlockSpec((1, tq, C_GRP * C_HD), lambda b, g, i: (b, i, g)),
            pl.BlockSpec((1, 1, C_GRP * C_HD, n), lambda b, g, i: (b, g, 0, 0)),
            pl.BlockSpec((1, 1, n, LANES), lambda b, g, i: (b, g, 0, 0)),
        ],
        out_specs=pl.BlockSpec((1, tq, C_GRP * C_HD), lambda b, g, i: (b, i, g)),
        scratch_shapes=[
            pltpu.VMEM((C_GRP * tq, C_GRP * C_HD), BF16),
            pltpu.VMEM((C_GRP * tq, LANES), F32),
            pltpu.VMEM((C_GRP * tq, LANES), F32),
        ],
        compiler_params=_params("parallel", "parallel", "arbitrary"), name="attention",
    )(aq, kt4, va)


def _outproj_kernel(x_ref, mod_ref, oa_ref, yb_ref, ao_ref, du_ref, dvn_ref, sz_ref,
                    w_ref, postg_ref, og_ref, bd_ref, sw_ref, sb_ref, o_ref):
    tm = x_ref.shape[1]
    sz = sz_ref[0].astype(F32)
    off_b, off_c, off_d = A_W, A_W + B_W, A_W + B_W + C_W

    sgu = [jnp.dot(sw_ref[...], dvn_ref[0, c * SGU_CHUNK:(c + 1) * SGU_CHUNK, :], preferred_element_type=F32)
           for c in range(tm // SGU_CHUNK)]
    m_c = (ao_ref[0].astype(F32) * sz[:, off_c:off_d]).astype(BF16)
    y = jnp.dot(m_c, w_ref[off_c:off_d, :], preferred_element_type=F32)
    m_b = (yb_ref[0].astype(F32) * sz[:, off_b:off_c]).astype(BF16)
    y = y + jnp.dot(m_b, w_ref[off_b:off_c, :], preferred_element_type=F32)

    o = oa_ref[0].astype(F32)
    oms = jnp.dot((o * o).astype(BF16), bd_ref[...], preferred_element_type=F32)
    out_a = o * lax.rsqrt(oms + EPS) * og_ref[...]
    y = y + jnp.dot((out_a * sz[:, 0:off_b]).astype(BF16), w_ref[0:off_b, :], preferred_element_type=F32)

    lane = lax.broadcasted_iota(jnp.int32, (SGU_CHUNK, D_W), 1)
    parts = []
    for c, r in enumerate(sgu):
        rows = slice(c * SGU_CHUNK, (c + 1) * SGU_CHUNK)
        mix = sb_ref[...]
        for g in range(D_GROUPS):
            mix = mix + jnp.where(lane // D_GC == g, r[g * SGU_CHUNK:(g + 1) * SGU_CHUNK, :], 0.0)
        parts.append(du_ref[0, rows, :].astype(F32) * mix)
    out_d = jnp.concatenate(parts, axis=0)
    m_d = (out_d * sz[:, off_d:]).astype(BF16)
    y = y + jnp.dot(m_d, w_ref[off_d:, :], preferred_element_type=F32)

    yms = jnp.mean(y * y, axis=-1, keepdims=True)
    yn = y * lax.rsqrt(yms + EPS) * postg_ref[...]
    gate = mod_ref[0, :, 2 * D_MODEL:3 * D_MODEL]
    o_ref[0] = x_ref[0] + gate * yn


def _outproj(x, mod, oa, yb, ao, du, dvn, sz, lp):
    bsz, n, _ = x.shape
    tm = min(TM_OUT, n)
    tok = lambda w: pl.BlockSpec((1, tm, w), lambda b, i: (b, i, 0))
    return pl.pallas_call(
        _outproj_kernel,
        out_shape=jax.ShapeDtypeStruct((bsz, n, D_MODEL), F32),
        grid=(bsz, n // tm),
        in_specs=[
            tok(D_MODEL),
            pl.BlockSpec((1, 1, 3 * D_MODEL), lambda b, i: (b, 0, 0)),
            tok(A_W), tok(B_W), tok(C_W), tok(D_W), tok(D_W), tok(D_MIX),
            _full((D_MIX, D_MODEL)), _full((1, D_MODEL)), _full((1, A_W)), _full((A_W, A_W)),
            _full((D_GROUPS * SGU_CHUNK, SGU_CHUNK)), _full((SGU_CHUNK, D_W)),
        ],
        out_specs=tok(D_MODEL),
        compiler_params=_params("parallel", "parallel"), name="outproj",
    )(x, mod, oa, yb, ao, du, dvn, sz,
      lp["w_out"], lp["post_g"], lp["og"], lp["bd_a"], lp["sgu_w"], lp["sgu_b"])


def _pack_w_in(w):
    sizes = [A_KW, A_KW, A_W, 2 * GLA_RANK, B_W, C_W, C_KVW, C_KVW, D_W, D_W, D_MIX]
    off = [0] + [int(o) for o in np.cumsum(sizes)]
    a_q, a_k, a_v, a_lr, b_u, c_q, c_k, c_v, d_u, d_v, z = [w[:, off[i]:off[i + 1]] for i in range(len(sizes))]

    def deinterleave(x, heads):
        r = x.reshape(D_MODEL, heads, C_HD // 2, 2)
        return jnp.concatenate([r[..., 0], r[..., 1]], axis=-1).reshape(D_MODEL, heads * C_HD)

    zeros = lambda width: jnp.zeros((D_MODEL, width), w.dtype)
    pieces = [a_q, a_k, a_v, b_u, deinterleave(c_q, C_HEADS), deinterleave(c_k, C_KV),
              a_lr, zeros(LANES - 2 * GLA_RANK)]
    for h in range(C_KV):
        pieces += [c_v[:, h * C_HD:(h + 1) * C_HD], zeros(LANES - C_HD)]
    pieces += [d_u, d_v, z]
    packed = jnp.concatenate(pieces, axis=1)
    assert packed.shape[1] == W_PACKED
    return packed.astype(BF16)


def _block_diag_mean(width, group):
    idx = np.arange(width) // group
    return jnp.asarray((idx[:, None] == idx[None, :]).astype(np.float32) / group, BF16)


def _layer_params(l, ada_unused, norm_pre_g, norm_post_g, w_in, gla_wg2_f, gla_bg_f, gla_wg2_b, gla_bg_b,
                  gla_onorm_g, fnet_w, q_norm_g, k_norm_g, sgu_norm_g, sgu_w, sgu_b, w_out):
    deint = np.concatenate([np.arange(0, C_HD, 2), np.arange(1, C_HD, 2)])
    w = _pack_w_in(w_in[l])
    wg2 = jnp.zeros((LANES, 2 * A_KW), F32)
    wg2 = wg2.at[0:GLA_RANK, 0:A_KW].set(gla_wg2_f[l])
    wg2 = wg2.at[GLA_RANK:2 * GLA_RANK, A_KW:].set(gla_wg2_b[l])
    vone = np.zeros((1, 2 * LANES), np.float32)
    vone[0, ONES_LANE] = 1.0
    vone[0, LANES + ONES_LANE] = 1.0
    return {
        "pre_g": norm_pre_g[l].reshape(1, D_MODEL),
        "post_g": norm_post_g[l].reshape(1, D_MODEL),
        "w_in": w,
        "wg2": wg2.astype(BF16),
        "bg": jnp.concatenate([gla_bg_f[l], gla_bg_b[l]]).reshape(1, 2 * A_KW),
        "qg": jnp.tile(q_norm_g[l][deint], C_HEADS).reshape(1, C_W),
        "kg": jnp.tile(k_norm_g[l][deint], C_KV).reshape(1, C_KVW),
        "sg": sgu_norm_g[l].reshape(1, D_W),
        "bd_q": _block_diag_mean(C_W, C_HD),
        "bd_a": _block_diag_mean(A_W, A_DV),
        "vone": jnp.asarray(vone),
        "og": jnp.tile(gla_onorm_g[l], A_HEADS).reshape(1, A_W),
        "fnet_w": fnet_w[l].astype(BF16),
        "sgu_w": sgu_w[l].reshape(D_GROUPS * SGU_CHUNK, SGU_CHUNK).astype(BF16),
        "sgu_b": jnp.repeat(sgu_b[l].T, D_GC, axis=1),
        "w_out": w_out[l].astype(BF16),
    }


def _seq_tables(n):
    rows = n // GRID_W
    row = jnp.repeat(jnp.arange(rows, dtype=F32), GRID_W)
    col = jnp.tile(jnp.arange(GRID_W, dtype=F32), rows)
    rope_axis = C_HD // 2
    freqs = ROPE_THETA ** (-jnp.arange(0, rope_axis, 2, dtype=F32) / rope_axis)
    ang = jnp.concatenate([row[:, None] * freqs, col[:, None] * freqs], axis=-1)
    cos, sin = jnp.cos(ang), jnp.sin(ang)
    cos_t = jnp.tile(jnp.concatenate([cos, cos], axis=-1), (1, LANES // C_HD))
    sin_t = jnp.tile(jnp.concatenate([-sin, sin], axis=-1), (1, LANES // C_HD))

    n2 = FFT_N2
    n1 = n // n2

    def dft_angles(rows, cols, period):
        prod = (jnp.arange(rows, dtype=jnp.int32)[:, None] * jnp.arange(cols, dtype=jnp.int32)[None, :]) % period
        return prod.astype(F32) * (2.0 * math.pi / period)

    a1_ang = dft_angles(n1, n1, n1)
    a1 = jnp.concatenate([jnp.cos(a1_ang), -jnp.sin(a1_ang)], axis=0)
    tw_ang = dft_angles(n2, n1, n)
    scale = 1.0 / math.sqrt(n * B_GC)
    tw_cos = jnp.broadcast_to((jnp.cos(tw_ang) * scale)[:, :, None], (n2, n1, LANES))
    tw_sin = jnp.broadcast_to((jnp.sin(tw_ang) * scale)[:, :, None], (n2, n1, LANES))
    ang2 = dft_angles(n2, n2, n2)
    c2, s2 = jnp.cos(ang2), jnp.sin(ang2)
    eye = jnp.eye(FFT_K1, dtype=F32)
    blk = lambda m: (m[:, None, None, :] * eye[None, :, :, None]).reshape(n2 * FFT_K1, FFT_K1 * n2)
    l2 = jnp.block([[blk(c2), blk(s2)], [blk(-s2), blk(c2)]])
    angc = dft_angles(B_GC, B_GC, B_GC)
    grp = jnp.eye(B_GROUPS, dtype=F32)
    kron = lambda m: (grp[:, None, :, None] * m[None, :, None, :]).reshape(B_W, B_W)
    chan_cos = kron(jnp.cos(angc))
    chan_sin = kron(jnp.sin(angc))
    return {
        "cos": cos_t, "sin": sin_t,
        "a1": a1.astype(BF16),
        "tw_cos": tw_cos, "tw_sin": tw_sin,
        "l2": l2.astype(BF16),
        "chan_cos": chan_cos.astype(BF16), "chan_sin": chan_sin.astype(BF16),
    }


def _layer(x, mod, lp, tabs):
    gq, gk, gv, gf, gb, fu, aq, kt4, va, du, dvn, sz = _inproj(x, mod, lp, tabs)
    of, ob = _gla_rec(gq, gk, gv, gf, gb)
    oa = _gla_band(gq, gk, gv, gf, gb, of, ob)
    yb = _fnet(fu, lp, tabs)
    ao = _attention(aq, kt4, va)
    return _outproj(x, mod, oa, yb, ao, du, dvn, sz, lp)


def kernel(x_prompt, x_sample, c_prompt, c_sample, ada_w, ada_b, norm_pre_g, norm_post_g, w_in,
           gla_wg2_f, gla_bg_f, gla_wg2_b, gla_bg_b, gla_onorm_g, fnet_w, q_norm_g, k_norm_g,
           sgu_norm_g, sgu_w, sgu_b, w_out):
    bp, bs = x_prompt.shape[0], x_sample.shape[0]
    pad_rows = (-(bp + bs)) % 8
    c_all = jnp.concatenate([c_prompt, c_sample, jnp.zeros((pad_rows, D_MODEL), F32)], axis=0)
    mod = _adaln(c_all, ada_w, ada_b)
    tabs_p = _seq_tables(x_prompt.shape[1])
    tabs_s = _seq_tables(x_sample.shape[1])
    y_prompt, y_sample = x_prompt, x_sample
    for l in range(DEPTH):
        lp = _layer_params(l, None, norm_pre_g, norm_post_g, w_in, gla_wg2_f, gla_bg_f, gla_wg2_b,
                           gla_bg_b, gla_onorm_g, fnet_w, q_norm_g, k_norm_g, sgu_norm_g, sgu_w,
                           sgu_b, w_out)
        mod_p = mod[l, 0:bp].reshape(bp, 1, 3 * D_MODEL)
        mod_s = mod[l, bp:bp + bs].reshape(bs, 1, 3 * D_MODEL)
        y_prompt = _layer(y_prompt, mod_p, lp, tabs_p)
        y_sample = _layer(y_sample, mod_s, lp, tabs_s)
    return (y_prompt, y_sample)
```

```python
import math

import numpy as np
import jax
import jax.numpy as jnp
from jax import lax
from jax.experimental import pallas as pl
from jax.experimental.pallas import tpu as pltpu

F32 = jnp.float32
BF16 = jnp.bfloat16
HIGHEST = lax.Precision.HIGHEST

D_MODEL = 1024
DEPTH = 2
GRID_W = 64
A_HEADS, A_DK, A_DV = 4, 32, 64
A_W, A_KW = A_HEADS * A_DV, A_HEADS * A_DK
GLA_RANK = 16
GLA_GATE_NORM = 16.0
B_GROUPS, B_GC = 4, 64
B_W = B_GROUPS * B_GC
C_HEADS, C_KV, C_HD = 8, 2, 64
C_GRP = C_HEADS // C_KV
C_W, C_KVW = C_HEADS * C_HD, C_KV * C_HD
ROPE_THETA = 10000.0
D_GROUPS, D_GC = 4, 64
D_W = D_GROUPS * D_GC
SGU_CHUNK = 128
D_MIX = A_W + B_W + C_W + D_W
EPS = 1e-6
LOG2E = 1.4426950408889634

LANES = 128
VMEM_LIMIT_BYTES = 56 * 1024 * 1024

GQ, GK, GV, FU, AQ, AK, LR, AV, DU, DV, ZZ = 0, 128, 256, 512, 768, 1280, 1408, 1536, 1792, 2048, 2304
W_PACKED = 3584
AV_W = 2 * LANES
ONES_LANE = C_HD

TM_IN = 1024
IN_ROW_PARTS = 4
TM_OUT = 1024
GLA_SUB = 16
GLA_BAND_TILE = 64
GLA_PITCH = GLA_SUB + 4
GLA_GRP = 128
GLA_SEG = 1024
GLA_LOOKAHEAD = 3
FFT_N2 = 64
FFT_NB = 16
FFT_K1 = 8
FFT_K1_GROUPS = 2
ATT_TQ = 512
ATT_TK = 2048
ATT_UNROLL = 2


def _params(*sem):
    return pltpu.CompilerParams(dimension_semantics=sem, vmem_limit_bytes=VMEM_LIMIT_BYTES)


def _full(shape):
    n = len(shape)
    return pl.BlockSpec(shape, lambda *_: (0,) * n)


def _of_layer(l, shape):
    n = len(shape)
    return pl.BlockSpec((1,) + tuple(shape), lambda *_: (l,) + (0,) * n)


def _adaln_kernel(c_ref, w_ref, b_ref, o_ref):
    c = c_ref[...]
    sc = c / (1.0 + jnp.exp(-c))
    o_ref[0] = jnp.dot(sc, w_ref[0], precision=HIGHEST, preferred_element_type=F32) + b_ref[0]


def _adaln(c_all, ada_w, ada_b):
    rows = c_all.shape[0]
    ncol = 3 * D_MODEL // D_MODEL
    return pl.pallas_call(
        _adaln_kernel,
        out_shape=jax.ShapeDtypeStruct((DEPTH, rows, 3 * D_MODEL), F32),
        grid=(DEPTH, ncol),
        in_specs=[
            pl.BlockSpec((rows, D_MODEL), lambda l, j: (0, 0)),
            pl.BlockSpec((1, D_MODEL, D_MODEL), lambda l, j: (l, 0, j)),
            pl.BlockSpec((1, 1, D_MODEL), lambda l, j: (l, 0, j)),
        ],
        out_specs=pl.BlockSpec((1, rows, D_MODEL), lambda l, j: (l, 0, j)),
        compiler_params=_params("arbitrary", "arbitrary"),
        name="adaln",
    )(c_all, ada_w, ada_b.reshape(DEPTH, 1, 3 * D_MODEL))


def _rope(x, cos, sin_signed):
    width = x.shape[-1]
    lane = lax.broadcasted_iota(jnp.int32, x.shape, 1)
    first_half = (lane % C_HD) < (C_HD // 2)
    swapped = jnp.where(first_half, pltpu.roll(x, width - C_HD // 2, 1), pltpu.roll(x, C_HD // 2, 1))
    reps = width // LANES
    cos_w = jnp.concatenate([cos] * reps, axis=-1) if reps > 1 else cos
    sin_w = jnp.concatenate([sin_signed] * reps, axis=-1) if reps > 1 else sin_signed
    return x * cos_w + swapped * sin_w


def _inproj_kernel(x_ref, mod_ref, preg_ref, w_ref, wg2_ref, bg_ref, qg_ref, kg_ref, sg_ref,
                   cos_ref, sin_ref, bd_ref, vone_ref,
                   gq_ref, gk_ref, gv_ref, gf_ref, gb_ref, fu_ref, aq_ref, kt_ref, va_ref,
                   du_ref, dvn_ref, sz_ref):
    tm = x_ref.shape[1]
    shift = mod_ref[0, 0, :, 0:D_MODEL]
    gain = preg_ref[0] * (1.0 + mod_ref[0, 0, :, D_MODEL:2 * D_MODEL])
    for part in range(IN_ROW_PARTS):
        rows = slice(part * tm // IN_ROW_PARTS, (part + 1) * tm // IN_ROW_PARTS)
        x = x_ref[0, rows, :]
        ms = jnp.mean(x * x, axis=-1, keepdims=True)
        hb = ((x * lax.rsqrt(ms + EPS)) * gain + shift).astype(BF16)

        def seg(off, width):
            return jnp.dot(hb, w_ref[0, :, off:off + width], preferred_element_type=F32)

        klr = seg(AK, C_KVW + LANES)
        q = seg(AQ, C_W)
        z = seg(ZZ, D_MIX)
        gqk = seg(GQ, 2 * A_KW)
        gv = seg(GV, A_W)
        fu = seg(FU, B_W)
        va = seg(AV, AV_W)
        du = seg(DU, D_W)
        dv = seg(DV, D_W)
        k = klr[:, :C_KVW]
        lr = klr[:, C_KVW:].astype(BF16)
        logits = jnp.dot(lr, wg2_ref[0], preferred_element_type=F32) + bg_ref[0]
        qms = jnp.dot((q * q).astype(BF16), bd_ref[...], preferred_element_type=F32)
        kms = jnp.dot((k * k).astype(BF16), bd_ref[0:C_KVW, 0:C_KVW], preferred_element_type=F32)

        gq_ref[0, rows, :] = gqk[:, :A_KW] * (A_DK ** -0.5)
        gk_ref[0, rows, :] = gqk[:, A_KW:]
        gv_ref[0, rows, :] = gv
        logg = (jnp.minimum(logits, 0.0) - jnp.log1p(jnp.exp(-jnp.abs(logits)))) * (1.0 / GLA_GATE_NORM)
        gf_ref[0, rows, :] = logg[:, :A_KW]
        gb_ref[0, rows, :] = logg[:, A_KW:]

        fu_ref[0, rows, :] = fu.astype(BF16)

        cos = cos_ref[rows, :]
        sin = sin_ref[rows, :]
        qn = q * lax.rsqrt(qms + EPS) * qg_ref[0]
        aq_ref[0, rows, :] = (_rope(qn, cos, sin) * (C_HD ** -0.5 * LOG2E)).astype(BF16)
        kn = k * lax.rsqrt(kms + EPS) * kg_ref[0]
        kt = _rope(kn, cos, sin).T.astype(BF16)
        for g in range(C_KV):
            for r in range(C_GRP):
                kt_ref[0, g, r * C_HD:(r + 1) * C_HD, rows] = kt[g * C_HD:(g + 1) * C_HD, :]
        vab = (va + vone_ref[...]).astype(BF16)
        va_ref[0, 0, rows, :] = vab[:, :LANES]
        va_ref[0, 1, rows, :] = vab[:, LANES:]

        du_ref[0, rows, :] = du.astype(BF16)
        dms = jnp.mean(dv * dv, axis=-1, keepdims=True)
        dvn_ref[0, rows, :] = (dv * lax.rsqrt(dms + EPS) * sg_ref[0]).astype(BF16)

        sz_ref[0, rows, :] = (z / (1.0 + jnp.exp(-z))).astype(BF16)


def _inproj(x, mod, row0, l, pp, tabs):
    bsz, n, _ = x.shape
    tm = min(TM_IN, n)
    grid = (bsz, n // tm)
    tok = lambda w: pl.BlockSpec((1, tm, w), lambda b, i: (b, i, 0))
    out_shapes = (
        jax.ShapeDtypeStruct((bsz, n, A_KW), F32),
        jax.ShapeDtypeStruct((bsz, n, A_KW), F32),
        jax.ShapeDtypeStruct((bsz, n, A_W), F32),
        jax.ShapeDtypeStruct((bsz, n, A_KW), F32),
        jax.ShapeDtypeStruct((bsz, n, A_KW), F32),
        jax.ShapeDtypeStruct((bsz, n, B_W), BF16),
        jax.ShapeDtypeStruct((bsz, n, C_W), BF16),
        jax.ShapeDtypeStruct((bsz, C_KV, C_GRP * C_HD, n), BF16),
        jax.ShapeDtypeStruct((bsz, C_KV, n, LANES), BF16),
        jax.ShapeDtypeStruct((bsz, n, D_W), BF16),
        jax.ShapeDtypeStruct((bsz, n, D_W), BF16),
        jax.ShapeDtypeStruct((bsz, n, D_MIX), BF16),
    )
    out_specs = (
        tok(A_KW), tok(A_KW), tok(A_W), tok(A_KW), tok(A_KW), tok(B_W), tok(C_W),
        pl.BlockSpec((1, C_KV, C_GRP * C_HD, tm), lambda b, i: (b, 0, 0, i)),
        pl.BlockSpec((1, C_KV, tm, LANES), lambda b, i: (b, 0, i, 0)),
        tok(D_W), tok(D_W), tok(D_MIX),
    )
    in_specs = [
        tok(D_MODEL),
        pl.BlockSpec((1, 1, 1, 3 * D_MODEL), lambda b, i: (l, row0 + b, 0, 0)),
        _of_layer(l, (1, D_MODEL)),
        _of_layer(l, (D_MODEL, W_PACKED)),
        _of_layer(l, (LANES, 2 * A_KW)),
        _of_layer(l, (1, 2 * A_KW)),
        _of_layer(l, (1, C_W)),
        _of_layer(l, (1, C_KVW)),
        _of_layer(l, (1, D_W)),
        pl.BlockSpec((tm, LANES), lambda b, i: (i, 0)),
        pl.BlockSpec((tm, LANES), lambda b, i: (i, 0)),
        _full((C_W, C_W)),
        _full((1, 2 * LANES)),
    ]
    return pl.pallas_call(
        _inproj_kernel, out_shape=out_shapes, grid=grid, in_specs=in_specs, out_specs=out_specs,
        compiler_params=_params("parallel", "parallel"), name="inproj",
    )(x, mod, pp["pre_g"], pp["w_in"], pp["wg2"], pp["bg"], pp["qg"], pp["kg"], pp["sg"],
      tabs["cos"], tabs["sin"], pp["bd_q"], pp["vone"])


def _head_match(shape, rows_per_head, cols_per_head):
    return (lax.broadcasted_iota(jnp.int32, shape, 0) // rows_per_head
            == lax.broadcasted_iota(jnp.int32, shape, 1) // cols_per_head)


def _gla_band_kernel(q_ref, k_ref, v_ref, gf_ref, gb_ref, of_ref, ob_ref, o_ref,
                     qf_ref, kf_ref, vf_ref, bf_ref, cb_ref, pit_ref, os_ref):
    rows = q_ref.shape[1]
    ts = rows // GLA_SUB
    n_slab = A_W // LANES
    every = lambda p: pl.ds(p, ts, stride=GLA_PITCH)
    hsum = jnp.where(_head_match((A_KW, A_W), A_DK, A_DV), 1.0, 0.0).astype(BF16)
    slab_q, slab_k, slab_gf, slab_gb, slab_v = 0, 1, 2, 3, 4
    for i in range(ts):
        src = slice(i * GLA_SUB, (i + 1) * GLA_SUB)
        dst = slice(i * GLA_PITCH, i * GLA_PITCH + GLA_SUB)
        pit_ref[slab_q, dst, :] = q_ref[0, src, :]
        pit_ref[slab_k, dst, :] = k_ref[0, src, :]
        pit_ref[slab_gf, dst, :] = gf_ref[0, src, :]
        pit_ref[slab_gb, dst, :] = gb_ref[0, src, :]
        for h in range(n_slab):
            pit_ref[slab_v + h, dst, :] = v_ref[0, src, h * LANES:(h + 1) * LANES]
    run = jnp.zeros((ts, A_KW), F32)
    for p in range(GLA_SUB):
        run = run + pit_ref[slab_gf, every(p), :] * LOG2E
        bf_ref[p] = run
        qf_ref[p] = pit_ref[slab_q, every(p), :]
        kf_ref[p] = pit_ref[slab_k, every(p), :]
        vf_ref[p] = jnp.concatenate([pit_ref[slab_v + h, every(p), :] for h in range(n_slab)], axis=-1)
    run = jnp.zeros((ts, A_KW), F32)
    for p in reversed(range(GLA_SUB)):
        run = run + pit_ref[slab_gb, every(p), :] * LOG2E
        cb_ref[p] = run
    for p in range(GLA_SUB):
        q = qf_ref[p]
        prods = []
        for s in range(GLA_SUB):
            qk = q * kf_ref[s]
            if s < p:
                qk = qk * jnp.exp2(bf_ref[p] - bf_ref[s])
            elif s > p:
                qk = qk * jnp.exp2(cb_ref[p] - cb_ref[s])
            prods.append(qk.astype(BF16))
        a = jnp.dot(jnp.concatenate(prods, axis=0), hsum, preferred_element_type=F32)
        acc = a[0:ts] * vf_ref[0]
        for s in range(1, GLA_SUB):
            acc = acc + a[s * ts:(s + 1) * ts] * vf_ref[s]
        for h in range(n_slab):
            os_ref[h, every(p), :] = acc[:, h * LANES:(h + 1) * LANES]
    for i in range(ts):
        src = slice(i * GLA_PITCH, i * GLA_PITCH + GLA_SUB)
        dst = slice(i * GLA_SUB, (i + 1) * GLA_SUB)
        band = jnp.concatenate([os_ref[h, src, :] for h in range(n_slab)], axis=-1)
        o_ref[0, dst, :] = (band + of_ref[0, dst, :].astype(F32) + ob_ref[0, dst, :].astype(F32)).astype(BF16)


def _gla_band(gq, gk, gv, gf, gb, of, ob):
    bsz, n, _ = gq.shape
    rows = min(GLA_BAND_TILE * GLA_SUB, n)
    ts = rows // GLA_SUB
    blk = lambda w: pl.BlockSpec((1, rows, w), lambda b, i: (b, i, 0))
    return pl.pallas_call(
        _gla_band_kernel,
        out_shape=jax.ShapeDtypeStruct((bsz, n, A_W), BF16),
        grid=(bsz, n // rows),
        in_specs=[blk(A_KW), blk(A_KW), blk(A_W), blk(A_KW), blk(A_KW), blk(A_W), blk(A_W)],
        out_specs=blk(A_W),
        scratch_shapes=[
            pltpu.VMEM((GLA_SUB, ts, A_KW), F32), pltpu.VMEM((GLA_SUB, ts, A_KW), F32),
            pltpu.VMEM((GLA_SUB, ts, A_W), F32),
            pltpu.VMEM((GLA_SUB, ts, A_KW), F32), pltpu.VMEM((GLA_SUB, ts, A_KW), F32),
            pltpu.VMEM((4 + A_W // LANES, ts * GLA_PITCH, LANES), F32),
            pltpu.VMEM((A_W // LANES, ts * GLA_PITCH, LANES), F32),
        ],
        compiler_params=_params("parallel", "parallel"), name="gla_band",
    )(gq, gk, gv, gf, gb, of, ob)


def _split3(x):
    hi = x.astype(BF16)
    rest = x - hi.astype(F32)
    mid = rest.astype(BF16)
    lo = (rest - mid.astype(F32)).astype(BF16)
    return hi, mid, lo


def _gla_prep(q_ref, k_ref, g_ref, qt_ref, kh_ref, dect_ref, tot_ref, pad_ref, reverse):
    seg = q_ref.shape[1]
    n_sub = seg // GLA_SUB
    r_i = lax.broadcasted_iota(jnp.int32, (GLA_GRP, GLA_GRP), 0)
    c_i = lax.broadcasted_iota(jnp.int32, (GLA_GRP, GLA_GRP), 1)
    same = (r_i // GLA_SUB) == (c_i // GLA_SUB)
    tri = (c_i >= r_i) if reverse else (c_i <= r_i)
    sums_mat = jnp.concatenate([jnp.where(same & tri, 1.0, 0.0), jnp.where(same, 1.0, 0.0)], axis=0).astype(BF16)
    for r in range(seg // GLA_GRP):
        rows = slice(r * GLA_GRP, (r + 1) * GLA_GRP)
        g3 = jnp.concatenate(_split3(g_ref[0, rows, :]), axis=-1)
        sums = jnp.dot(sums_mat, g3, preferred_element_type=F32)
        sums = sums[:, 0:A_KW] + sums[:, A_KW:2 * A_KW] + sums[:, 2 * A_KW:]
        cum, tot = sums[:GLA_GRP], sums[GLA_GRP:]
        qt_ref[rows, :] = (q_ref[0, rows, :] * jnp.exp(cum)).astype(BF16)
        kh_ref[rows, :] = (k_ref[0, rows, :] * jnp.exp(tot - cum)).astype(BF16)
        tot_ref[rows, :] = tot
    pad_ref[...] = jnp.zeros_like(pad_ref)
    pad_ref[0:n_sub, :] = tot_ref[pl.ds(0, n_sub, stride=GLA_SUB), :]
    dect_ref[...] = jnp.exp(pad_ref[...].T)


def _gla_rec_kernel(qf_ref, kf_ref, vf_ref, gf_ref, qb_ref, kb_ref, vb_ref, gb_ref, of_ref, ob_ref,
                    sf_ref, sb_ref, qtf_ref, khf_ref, dtf_ref, qtb_ref, khb_ref, dtb_ref, tot_ref, pad_ref):
    @pl.when(pl.program_id(1) == 0)
    def _():
        sf_ref[...] = jnp.zeros_like(sf_ref)
        sb_ref[...] = jnp.zeros_like(sb_ref)

    seg = qf_ref.shape[1]
    n_sub = seg // GLA_SUB
    _gla_prep(qf_ref, kf_ref, gf_ref, qtf_ref, khf_ref, dtf_ref, tot_ref, pad_ref, reverse=False)
    _gla_prep(qb_ref, kb_ref, gb_ref, qtb_ref, khb_ref, dtb_ref, tot_ref, pad_ref, reverse=True)
    smask = _head_match((A_KW, A_W), A_DK, A_DV)

    def update(i, kh_ref, v_ref):
        rows = slice(i * GLA_SUB, (i + 1) * GLA_SUB)
        upd = lax.dot_general(kh_ref[rows, :], v_ref[0, rows, :].astype(BF16), (((0,), (0,)), ((), ())),
                              preferred_element_type=F32)
        return jnp.where(smask, upd, 0.0)

    def step(i, s, upd, qt_ref, dect_ref, o_ref):
        rows = slice(i * GLA_SUB, (i + 1) * GLA_SUB)
        o_ref[0, rows, :] = jnp.dot(qt_ref[rows, :], s.astype(BF16), preferred_element_type=F32).astype(BF16)
        return s * dect_ref[:, i:i + 1] + upd

    order_f = list(range(n_sub))
    order_b = list(reversed(range(n_sub)))
    upd_f = [update(i, khf_ref, vf_ref) for i in order_f[:GLA_LOOKAHEAD]]
    upd_b = [update(i, khb_ref, vb_ref) for i in order_b[:GLA_LOOKAHEAD]]
    s_f = sf_ref[...]
    s_b = sb_ref[...]
    for j in range(n_sub):
        if j + GLA_LOOKAHEAD < n_sub:
            upd_f.append(update(order_f[j + GLA_LOOKAHEAD], khf_ref, vf_ref))
            upd_b.append(update(order_b[j + GLA_LOOKAHEAD], khb_ref, vb_ref))
        s_f = step(order_f[j], s_f, upd_f[j], qtf_ref, dtf_ref, of_ref)
        s_b = step(order_b[j], s_b, upd_b[j], qtb_ref, dtb_ref, ob_ref)
    sf_ref[...] = s_f
    sb_ref[...] = s_b


def _gla_rec(gq, gk, gv, gf, gb):
    bsz, n, _ = gq.shape
    seg = min(GLA_SEG, n)
    nseg = n // seg
    assert seg // GLA_SUB <= LANES
    fwd = lambda w: pl.BlockSpec((1, seg, w), lambda b, s: (b, s, 0))
    bwd = lambda w: pl.BlockSpec((1, seg, w), lambda b, s: (b, nseg - 1 - s, 0))
    return pl.pallas_call(
        _gla_rec_kernel,
        out_shape=(jax.ShapeDtypeStruct((bsz, n, A_W), BF16), jax.ShapeDtypeStruct((bsz, n, A_W), BF16)),
        grid=(bsz, nseg),
        in_specs=[fwd(A_KW), fwd(A_KW), fwd(A_W), fwd(A_KW), bwd(A_KW), bwd(A_KW), bwd(A_W), bwd(A_KW)],
        out_specs=(fwd(A_W), bwd(A_W)),
        scratch_shapes=[
            pltpu.VMEM((A_KW, A_W), F32), pltpu.VMEM((A_KW, A_W), F32),
            pltpu.VMEM((seg, A_KW), BF16), pltpu.VMEM((seg, A_KW), BF16), pltpu.VMEM((A_KW, LANES), F32),
            pltpu.VMEM((seg, A_KW), BF16), pltpu.VMEM((seg, A_KW), BF16), pltpu.VMEM((A_KW, LANES), F32),
            pltpu.VMEM((seg, A_KW), F32), pltpu.VMEM((LANES, A_KW), F32),
        ],
        compiler_params=_params("parallel", "arbitrary"), name="gla_rec",
    )(gq, gk, gv, gf, gq, gk, gv, gb)


def _fft1_kernel(u_ref, a1_ref, cos_ref, sin_ref, t_ref):
    n1 = u_ref.shape[1]
    nb = cos_ref.shape[0]
    t = jnp.dot(a1_ref[...], u_ref[0], preferred_element_type=F32)
    for j in range(nb):
        cols = slice(j * B_W, (j + 1) * B_W)
        tr = t[:n1, cols]
        ti = t[n1:, cols]
        c = jnp.concatenate([cos_ref[j]] * (B_W // LANES), axis=-1)
        s = jnp.concatenate([sin_ref[j]] * (B_W // LANES), axis=-1)
        t_ref[0, 0, :, cols] = (tr * c + ti * s).astype(BF16)
        t_ref[0, 1, :, cols] = (ti * c - tr * s).astype(BF16)


def _fft2_kernel(t_ref, l_ref, cc_ref, cs_ref, w_ref, o_ref):
    n2 = t_ref.shape[3]
    half = FFT_K1 * n2
    groups = range(t_ref.shape[2] // FFT_K1)
    xs = []
    for grp in groups:
        k1s = slice(grp * FFT_K1, (grp + 1) * FFT_K1)
        xs.append(jnp.concatenate([t_ref[0, 0, k1s].reshape(half, B_W), t_ref[0, 1, k1s].reshape(half, B_W)],
                                  axis=0))
    gs = [jnp.dot(l_ref[...], x, preferred_element_type=F32) for x in xs]
    mixed = [jnp.dot(g[:half].astype(BF16), cc_ref[...], preferred_element_type=F32)
             + jnp.dot(g[half:].astype(BF16), cs_ref[...], preferred_element_type=F32) for g in gs]
    ys = [jnp.dot(m.astype(BF16), w_ref[0], preferred_element_type=F32) for m in mixed]
    o_ref[0] = jnp.concatenate([y.reshape(n2, FFT_K1, B_W) for y in ys], axis=1).astype(BF16)


def _fnet(fu, l, pp, tabs):
    bsz, n, _ = fu.shape
    n2 = FFT_N2
    n1 = n // n2
    nb = FFT_NB
    k1b = FFT_K1 * FFT_K1_GROUPS
    u2 = fu.reshape(bsz, n1, n2 * B_W)
    tt = pl.pallas_call(
        _fft1_kernel,
        out_shape=jax.ShapeDtypeStruct((bsz, 2, n1, n2 * B_W), BF16),
        grid=(n2 // nb, bsz),
        in_specs=[
            pl.BlockSpec((1, n1, nb * B_W), lambda j, b: (b, 0, j)),
            _full((2 * n1, n1)),
            pl.BlockSpec((nb, n1, LANES), lambda j, b: (j, 0, 0)),
            pl.BlockSpec((nb, n1, LANES), lambda j, b: (j, 0, 0)),
        ],
        out_specs=pl.BlockSpec((1, 2, n1, nb * B_W), lambda j, b: (b, 0, 0, j)),
        compiler_params=_params("parallel", "parallel"), name="fft_pass1",
    )(u2, tabs["a1"], tabs["tw_cos"], tabs["tw_sin"])
    t5 = tt.reshape(bsz, 2, n1, n2, B_W)
    y = pl.pallas_call(
        _fft2_kernel,
        out_shape=jax.ShapeDtypeStruct((bsz, n2, n1, B_W), BF16),
        grid=(bsz, n1 // k1b),
        in_specs=[
            pl.BlockSpec((1, 2, k1b, n2, B_W), lambda b, i: (b, 0, i, 0, 0)),
            _full((2 * FFT_K1 * n2, 2 * FFT_K1 * n2)),
            _full((B_W, B_W)), _full((B_W, B_W)), _of_layer(l, (B_W, B_W)),
        ],
        out_specs=pl.BlockSpec((1, n2, k1b, B_W), lambda b, i: (b, 0, i, 0)),
        compiler_params=_params("parallel", "parallel"), name="fft_pass2",
    )(t5, tabs["l2"], tabs["chan_cos"], tabs["chan_sin"], pp["fnet_w"])
    return y.reshape(bsz, n, B_W)


def _attn_kernel(q_ref, kt_ref, va_ref, o_ref, qm_ref, m_ref, acc_ref):
    tq = q_ref.shape[1]
    n = kt_ref.shape[3]
    tk = min(ATT_TK, n)
    q = q_ref[0]
    lane = lax.broadcasted_iota(jnp.int32, q.shape, 1)
    for h in range(C_GRP):
        qm_ref[h * tq:(h + 1) * tq, :] = jnp.where(lane // C_HD == h, q, jnp.zeros_like(q))
    m_ref[...] = jnp.full(m_ref.shape, -jnp.inf, F32)
    acc_ref[...] = jnp.zeros(acc_ref.shape, F32)

    def scores(c):
        c0 = pl.multiple_of(c * tk, tk)
        return jnp.dot(qm_ref[...], kt_ref[0, 0, :, pl.ds(c0, tk)], preferred_element_type=F32)

    def softmax_accumulate(c, s):
        c0 = pl.multiple_of(c * tk, tk)
        v = va_ref[0, 0, pl.ds(c0, tk), :]
        s_max = s[:, 0:LANES]
        for t in range(1, tk // LANES):
            s_max = jnp.maximum(s_max, s[:, t * LANES:(t + 1) * LANES])
        row_max = jnp.max(s_max, axis=-1, keepdims=True)
        m_old = m_ref[...]
        m_new = jnp.maximum(m_old, row_max)
        p = jnp.exp2(s - jnp.concatenate([m_new] * (tk // LANES), axis=-1))
        alpha = jnp.exp2(m_old - m_new)
        acc_ref[...] = alpha * acc_ref[...] + jnp.dot(p.astype(BF16), v, preferred_element_type=F32)
        m_ref[...] = m_new

    def step(c, carry):
        softmax_accumulate(c, scores(c))
        return carry

    lax.fori_loop(0, n // tk, step, 0, unroll=ATT_UNROLL)
    acc = acc_ref[...]
    inv = 1.0 / acc[:, ONES_LANE:ONES_LANE + 1]
    outs = [(acc[h * tq:(h + 1) * tq, 0:C_HD] * inv[h * tq:(h + 1) * tq]) for h in range(C_GRP)]
    o_ref[0] = jnp.concatenate(outs, axis=-1).astype(BF16)


def _attention(aq, kt4, va):
    bsz, n, _ = aq.shape
    tq = min(ATT_TQ, n)
    return pl.pallas_call(
        _attn_kernel,
        out_shape=jax.ShapeDtypeStruct((bsz, n, C_W), BF16),
        grid=(bsz, C_KV, n // tq),
        in_specs=[
            pl.BlockSpec((1, tq, C_GRP * C_HD), lambda b, g, i: (b, i, g)),
            pl.BlockSpec((1, 1, C_GRP * C_HD, n), lambda b, g, i: (b, g, 0, 0)),
            pl.BlockSpec((1, 1, n, LANES), lambda b, g, i: (b, g, 0, 0)),
        ],
        out_specs=pl.BlockSpec((1, tq, C_GRP * C_HD), lambda b, g, i: (b, i, g)),
        scratch_shapes=[
            pltpu.VMEM((C_GRP * tq, C_GRP * C_HD), BF16),
            pltpu.VMEM((C_GRP * tq, LANES), F32),
            pltpu.VMEM((C_GRP * tq, LANES), F32),
        ],
        compiler_params=_params("parallel", "parallel", "arbitrary"), name="attention",
    )(aq, kt4, va)


def _outproj_kernel(x_ref, mod_ref, oa_ref, yb_ref, ao_ref, du_ref, dvn_ref, sz_ref,
                    w_ref, postg_ref, og_ref, bd_ref, sw_ref, sb_ref, o_ref):
    tm = x_ref.shape[1]
    sz = sz_ref[0].astype(F32)
    off_b, off_c, off_d = A_W, A_W + B_W, A_W + B_W + C_W

    sgu = [jnp.dot(sw_ref[0], dvn_ref[0, c * SGU_CHUNK:(c + 1) * SGU_CHUNK, :], preferred_element_type=F32)
           for c in range(tm // SGU_CHUNK)]
    m_c = (ao_ref[0].astype(F32) * sz[:, off_c:off_d]).astype(BF16)
    y = jnp.dot(m_c, w_ref[0, off_c:off_d, :], preferred_element_type=F32)
    m_b = (yb_ref[0].astype(F32) * sz[:, off_b:off_c]).astype(BF16)
    y = y + jnp.dot(m_b, w_ref[0, off_b:off_c, :], preferred_element_type=F32)

    o = oa_ref[0].astype(F32)
    oms = jnp.dot((o * o).astype(BF16), bd_ref[...], preferred_element_type=F32)
    out_a = o * lax.rsqrt(oms + EPS) * og_ref[0]
    y = y + jnp.dot((out_a * sz[:, 0:off_b]).astype(BF16), w_ref[0, 0:off_b, :], preferred_element_type=F32)

    lane = lax.broadcasted_iota(jnp.int32, (SGU_CHUNK, D_W), 1)
    parts = []
    for c, r in enumerate(sgu):
        rows = slice(c * SGU_CHUNK, (c + 1) * SGU_CHUNK)
        mix = sb_ref[0]
        for g in range(D_GROUPS):
            mix = mix + jnp.where(lane // D_GC == g, r[g * SGU_CHUNK:(g + 1) * SGU_CHUNK, :], 0.0)
        parts.append(du_ref[0, rows, :].astype(F32) * mix)
    out_d = jnp.concatenate(parts, axis=0)
    m_d = (out_d * sz[:, off_d:]).astype(BF16)
    y = y + jnp.dot(m_d, w_ref[0, off_d:, :], preferred_element_type=F32)

    yms = jnp.mean(y * y, axis=-1, keepdims=True)
    yn = y * lax.rsqrt(yms + EPS) * postg_ref[0]
    gate = mod_ref[0, 0, :, 2 * D_MODEL:3 * D_MODEL]
    o_ref[0] = x_ref[0] + gate * yn


def _outproj(x, mod, row0, l, oa, yb, ao, du, dvn, sz, pp):
    bsz, n, _ = x.shape
    tm = min(TM_OUT, n)
    tok = lambda w: pl.BlockSpec((1, tm, w), lambda b, i: (b, i, 0))
    return pl.pallas_call(
        _outproj_kernel,
        out_shape=jax.ShapeDtypeStruct((bsz, n, D_MODEL), F32),
        grid=(bsz, n // tm),
        in_specs=[
            tok(D_MODEL),
            pl.BlockSpec((1, 1, 1, 3 * D_MODEL), lambda b, i: (l, row0 + b, 0, 0)),
            tok(A_W), tok(B_W), tok(C_W), tok(D_W), tok(D_W), tok(D_MIX),
            _of_layer(l, (D_MIX, D_MODEL)), _of_layer(l, (1, D_MODEL)), _of_layer(l, (1, A_W)), _full((A_W, A_W)),
            _of_layer(l, (D_GROUPS * SGU_CHUNK, SGU_CHUNK)), _of_layer(l, (SGU_CHUNK, D_W)),
        ],
        out_specs=tok(D_MODEL),
        compiler_params=_params("parallel", "parallel"), name="outproj",
    )(x, mod, oa, yb, ao, du, dvn, sz,
      pp["w_out"], pp["post_g"], pp["og"], pp["bd_a"], pp["sgu_w"], pp["sgu_b"])


def _pack_w_in(w):
    sizes = [A_KW, A_KW, A_W, 2 * GLA_RANK, B_W, C_W, C_KVW, C_KVW, D_W, D_W, D_MIX]
    off = [0] + [int(o) for o in np.cumsum(sizes)]
    a_q, a_k, a_v, a_lr, b_u, c_q, c_k, c_v, d_u, d_v, z = [w[:, off[i]:off[i + 1]] for i in range(len(sizes))]

    def deinterleave(x, heads):
        r = x.reshape(D_MODEL, heads, C_HD // 2, 2)
        return jnp.concatenate([r[..., 0], r[..., 1]], axis=-1).reshape(D_MODEL, heads * C_HD)

    zeros = lambda width: jnp.zeros((D_MODEL, width), w.dtype)
    pieces = [a_q, a_k, a_v, b_u, deinterleave(c_q, C_HEADS), deinterleave(c_k, C_KV),
              a_lr, zeros(LANES - 2 * GLA_RANK)]
    for h in range(C_KV):
        pieces += [c_v[:, h * C_HD:(h + 1) * C_HD], zeros(LANES - C_HD)]
    pieces += [d_u, d_v, z]
    packed = jnp.concatenate(pieces, axis=1)
    assert packed.shape[1] == W_PACKED
    return packed.astype(BF16)


def _block_diag_mean(width, group):
    idx = np.arange(width) // group
    return jnp.asarray((idx[:, None] == idx[None, :]).astype(np.float32) / group, BF16)


def _stacked_params(norm_pre_g, norm_post_g, w_in, gla_wg2_f, gla_bg_f, gla_wg2_b, gla_bg_b,
                    gla_onorm_g, fnet_w, q_norm_g, k_norm_g, sgu_norm_g, sgu_w, sgu_b, w_out):
    depth = w_in.shape[0]
    deint = np.concatenate([np.arange(0, C_HD, 2), np.arange(1, C_HD, 2)])
    wg2 = jnp.zeros((depth, LANES, 2 * A_KW), F32)
    wg2 = wg2.at[:, 0:GLA_RANK, 0:A_KW].set(gla_wg2_f)
    wg2 = wg2.at[:, GLA_RANK:2 * GLA_RANK, A_KW:].set(gla_wg2_b)
    vone = np.zeros((1, 2 * LANES), np.float32)
    vone[0, ONES_LANE] = 1.0
    vone[0, LANES + ONES_LANE] = 1.0
    return {
        "pre_g": norm_pre_g.reshape(depth, 1, D_MODEL),
        "post_g": norm_post_g.reshape(depth, 1, D_MODEL),
        "w_in": jax.vmap(_pack_w_in)(w_in),
        "wg2": wg2.astype(BF16),
        "bg": jnp.concatenate([gla_bg_f, gla_bg_b], axis=-1).reshape(depth, 1, 2 * A_KW),
        "qg": jnp.tile(q_norm_g[:, deint], (1, C_HEADS)).reshape(depth, 1, C_W),
        "kg": jnp.tile(k_norm_g[:, deint], (1, C_KV)).reshape(depth, 1, C_KVW),
        "sg": sgu_norm_g.reshape(depth, 1, D_W),
        "bd_q": _block_diag_mean(C_W, C_HD),
        "bd_a": _block_diag_mean(A_W, A_DV),
        "vone": jnp.asarray(vone),
        "og": jnp.tile(gla_onorm_g, (1, A_HEADS)).reshape(depth, 1, A_W),
        "fnet_w": fnet_w.astype(BF16),
        "sgu_w": sgu_w.reshape(depth, D_GROUPS * SGU_CHUNK, SGU_CHUNK).astype(BF16),
        "sgu_b": jnp.repeat(jnp.swapaxes(sgu_b, 1, 2), D_GC, axis=2),
        "w_out": w_out.astype(BF16),
    }


def _seq_tables(n):
    rows = n // GRID_W
    row = jnp.repeat(jnp.arange(rows, dtype=F32), GRID_W)
    col = jnp.tile(jnp.arange(GRID_W, dtype=F32), rows)
    rope_axis = C_HD // 2
    freqs = ROPE_THETA ** (-jnp.arange(0, rope_axis, 2, dtype=F32) / rope_axis)
    ang = jnp.concatenate([row[:, None] * freqs, col[:, None] * freqs], axis=-1)
    cos, sin = jnp.cos(ang), jnp.sin(ang)
    cos_t = jnp.tile(jnp.concatenate([cos, cos], axis=-1), (1, LANES // C_HD))
    sin_t = jnp.tile(jnp.concatenate([-sin, sin], axis=-1), (1, LANES // C_HD))

    n2 = FFT_N2
    n1 = n // n2

    def dft_angles(rows, cols, period):
        prod = (jnp.arange(rows, dtype=jnp.int32)[:, None] * jnp.arange(cols, dtype=jnp.int32)[None, :]) % period
        return prod.astype(F32) * (2.0 * math.pi / period)

    a1_ang = dft_angles(n1, n1, n1)
    a1 = jnp.concatenate([jnp.cos(a1_ang), -jnp.sin(a1_ang)], axis=0)
    tw_ang = dft_angles(n2, n1, n)
    scale = 1.0 / math.sqrt(n * B_GC)
    tw_cos = jnp.broadcast_to((jnp.cos(tw_ang) * scale)[:, :, None], (n2, n1, LANES))
    tw_sin = jnp.broadcast_to((jnp.sin(tw_ang) * scale)[:, :, None], (n2, n1, LANES))
    ang2 = dft_angles(n2, n2, n2)
    c2, s2 = jnp.cos(ang2), jnp.sin(ang2)
    eye = jnp.eye(FFT_K1, dtype=F32)
    blk = lambda m: (m[:, None, None, :] * eye[None, :, :, None]).reshape(n2 * FFT_K1, FFT_K1 * n2)
    l2 = jnp.block([[blk(c2), blk(s2)], [blk(-s2), blk(c2)]])
    angc = dft_angles(B_GC, B_GC, B_GC)
    grp = jnp.eye(B_GROUPS, dtype=F32)
    kron = lambda m: (grp[:, None, :, None] * m[None, :, None, :]).reshape(B_W, B_W)
    chan_cos = kron(jnp.cos(angc))
    chan_sin = kron(jnp.sin(angc))
    return {
        "cos": cos_t, "sin": sin_t,
        "a1": a1.astype(BF16),
        "tw_cos": tw_cos, "tw_sin": tw_sin,
        "l2": l2.astype(BF16),
        "chan_cos": chan_cos.astype(BF16), "chan_sin": chan_sin.astype(BF16),
    }


def _layer(x, mod, row0, l, pp, tabs):
    gq, gk, gv, gf, gb, fu, aq, kt4, va, du, dvn, sz = _inproj(x, mod, row0, l, pp, tabs)
    of, ob = _gla_rec(gq, gk, gv, gf, gb)
    oa = _gla_band(gq, gk, gv, gf, gb, of, ob)
    yb = _fnet(fu, l, pp, tabs)
    ao = _attention(aq, kt4, va)
    return _outproj(x, mod, row0, l, oa, yb, ao, du, dvn, sz, pp)


def kernel(x_prompt, x_sample, c_prompt, c_sample, ada_w, ada_b, norm_pre_g, norm_post_g, w_in,
           gla_wg2_f, gla_bg_f, gla_wg2_b, gla_bg_b, gla_onorm_g, fnet_w, q_norm_g, k_norm_g,
           sgu_norm_g, sgu_w, sgu_b, w_out):
    bp, bs = x_prompt.shape[0], x_sample.shape[0]
    pad_rows = (-(bp + bs)) % 8
    c_all = jnp.concatenate([c_prompt, c_sample, jnp.zeros((pad_rows, D_MODEL), F32)], axis=0)
    mod = _adaln(c_all, ada_w, ada_b)
    mod = mod.reshape(DEPTH, mod.shape[1], 1, 3 * D_MODEL)
    pp = _stacked_params(norm_pre_g, norm_post_g, w_in, gla_wg2_f, gla_bg_f, gla_wg2_b, gla_bg_b,
                         gla_onorm_g, fnet_w, q_norm_g, k_norm_g, sgu_norm_g, sgu_w, sgu_b, w_out)
    tabs_p = _seq_tables(x_prompt.shape[1])
    tabs_s = _seq_tables(x_sample.shape[1])
    y_prompt, y_sample = x_prompt, x_sample
    for l in range(DEPTH):
        y_prompt = _layer(y_prompt, mod, 0, l, pp, tabs_p)
        y_sample = _layer(y_sample, mod, bp, l, pp, tabs_s)
    return (y_prompt, y_sample)
```

```python
import math

import numpy as np
import jax
import jax.numpy as jnp
from jax import lax
from jax.experimental import pallas as pl
from jax.experimental.pallas import tpu as pltpu

F32 = jnp.float32
BF16 = jnp.bfloat16
HIGHEST = lax.Precision.HIGHEST

D_MODEL = 1024
DEPTH = 2
GRID_W = 64
A_HEADS, A_DK, A_DV = 4, 32, 64
A_W, A_KW = A_HEADS * A_DV, A_HEADS * A_DK
GLA_RANK = 16
GLA_GATE_NORM = 16.0
B_GROUPS, B_GC = 4, 64
B_W = B_GROUPS * B_GC
C_HEADS, C_KV, C_HD = 8, 2, 64
C_GRP = C_HEADS // C_KV
C_W, C_KVW = C_HEADS * C_HD, C_KV * C_HD
ROPE_THETA = 10000.0
D_GROUPS, D_GC = 4, 64
D_W = D_GROUPS * D_GC
SGU_CHUNK = 128
D_MIX = A_W + B_W + C_W + D_W
EPS = 1e-6
LOG2E = 1.4426950408889634

LANES = 128
VMEM_LIMIT_BYTES = 56 * 1024 * 1024

GQ, GK, GV, FU, AQ, AK, LR, AV, DU, DV, ZZ = 0, 128, 256, 512, 768, 1280, 1408, 1536, 1792, 2048, 2304
W_PACKED = 3584
AV_W = 2 * LANES
ONES_LANE = C_HD

TM_IN = 1024
IN_ROW_PARTS = 4
TM_OUT = 1024
GLA_SUB = 16
GLA_BAND_TILE = 64
GLA_PITCH = GLA_SUB + 4
GLA_GRP = 128
GLA_SEG = 1024
GLA_LOOKAHEAD = 3
FFT_N2 = 64
FFT_NB = 16
FFT_K1 = 8
FFT_K1_GROUPS = 2
ATT_TQ = 512
ATT_TK = 2048
ATT_UNROLL = 2


def _params(*sem):
    return pltpu.CompilerParams(dimension_semantics=sem, vmem_limit_bytes=VMEM_LIMIT_BYTES)


def _full(shape):
    n = len(shape)
    return pl.BlockSpec(shape, lambda *_: (0,) * n)


def _of_layer(l, shape):
    n = len(shape)
    return pl.BlockSpec((1,) + tuple(shape), lambda *_: (l,) + (0,) * n)


def _adaln_kernel(c_ref, w_ref, b_ref, o_ref):
    c = c_ref[...]
    sc = c / (1.0 + jnp.exp(-c))
    o_ref[0] = jnp.dot(sc, w_ref[0], precision=HIGHEST, preferred_element_type=F32) + b_ref[0]


def _adaln(c_all, ada_w, ada_b):
    rows = c_all.shape[0]
    ncol = 3 * D_MODEL // D_MODEL
    return pl.pallas_call(
        _adaln_kernel,
        out_shape=jax.ShapeDtypeStruct((DEPTH, rows, 3 * D_MODEL), F32),
        grid=(DEPTH, ncol),
        in_specs=[
            pl.BlockSpec((rows, D_MODEL), lambda l, j: (0, 0)),
            pl.BlockSpec((1, D_MODEL, D_MODEL), lambda l, j: (l, 0, j)),
            pl.BlockSpec((1, 1, D_MODEL), lambda l, j: (l, 0, j)),
        ],
        out_specs=pl.BlockSpec((1, rows, D_MODEL), lambda l, j: (l, 0, j)),
        compiler_params=_params("arbitrary", "arbitrary"),
        name="adaln",
    )(c_all, ada_w, ada_b.reshape(DEPTH, 1, 3 * D_MODEL))


def _rope(x, cos, sin_signed):
    width = x.shape[-1]
    lane = lax.broadcasted_iota(jnp.int32, x.shape, 1)
    first_half = (lane % C_HD) < (C_HD // 2)
    swapped = jnp.where(first_half, pltpu.roll(x, width - C_HD // 2, 1), pltpu.roll(x, C_HD // 2, 1))
    reps = width // LANES
    cos_w = jnp.concatenate([cos] * reps, axis=-1) if reps > 1 else cos
    sin_w = jnp.concatenate([sin_signed] * reps, axis=-1) if reps > 1 else sin_signed
    return x * cos_w + swapped * sin_w


def _inproj_kernel(x_ref, mod_ref, preg_ref, w_ref, wg2_ref, bg_ref, qg_ref, kg_ref, sg_ref,
                   cos_ref, sin_ref, bd_ref, vone_ref,
                   gq_ref, gk_ref, gv_ref, gf_ref, gb_ref, fu_ref, aq_ref, kt_ref, va_ref,
                   du_ref, dvn_ref, sz_ref):
    tm = x_ref.shape[1]
    shift = mod_ref[0, 0, :, 0:D_MODEL]
    gain = preg_ref[0] * (1.0 + mod_ref[0, 0, :, D_MODEL:2 * D_MODEL])
    for part in range(IN_ROW_PARTS):
        rows = slice(part * tm // IN_ROW_PARTS, (part + 1) * tm // IN_ROW_PARTS)
        x = x_ref[0, rows, :]
        ms = jnp.mean(x * x, axis=-1, keepdims=True)
        hb = ((x * lax.rsqrt(ms + EPS)) * gain + shift).astype(BF16)

        def seg(off, width):
            return jnp.dot(hb, w_ref[0, :, off:off + width], preferred_element_type=F32)

        klr = seg(AK, C_KVW + LANES)
        q = seg(AQ, C_W)
        z = seg(ZZ, D_MIX)
        gqk = seg(GQ, 2 * A_KW)
        gv = seg(GV, A_W)
        fu = seg(FU, B_W)
        va = seg(AV, AV_W)
        du = seg(DU, D_W)
        dv = seg(DV, D_W)
        k = klr[:, :C_KVW]
        lr = klr[:, C_KVW:].astype(BF16)
        logits = jnp.dot(lr, wg2_ref[0], preferred_element_type=F32) + bg_ref[0]
        qms = jnp.dot((q * q).astype(BF16), bd_ref[...], preferred_element_type=F32)
        kms = jnp.dot((k * k).astype(BF16), bd_ref[0:C_KVW, 0:C_KVW], preferred_element_type=F32)

        gq_ref[0, rows, :] = gqk[:, :A_KW] * (A_DK ** -0.5)
        gk_ref[0, rows, :] = gqk[:, A_KW:]
        gv_ref[0, rows, :] = gv
        logg = (jnp.minimum(logits, 0.0) - jnp.log1p(jnp.exp(-jnp.abs(logits)))) * (1.0 / GLA_GATE_NORM)
        gf_ref[0, rows, :] = logg[:, :A_KW]
        gb_ref[0, rows, :] = logg[:, A_KW:]

        fu_ref[0, rows, :] = fu.astype(BF16)

        cos = cos_ref[rows, :]
        sin = sin_ref[rows, :]
        qn = q * lax.rsqrt(qms + EPS) * qg_ref[0]
        aq_ref[0, rows, :] = (_rope(qn, cos, sin) * (C_HD ** -0.5 * LOG2E)).astype(BF16)
        kn = k * lax.rsqrt(kms + EPS) * kg_ref[0]
        kt = _rope(kn, cos, sin).T.astype(BF16)
        for g in range(C_KV):
            for r in range(C_GRP):
                kt_ref[0, g, r * C_HD:(r + 1) * C_HD, rows] = kt[g * C_HD:(g + 1) * C_HD, :]
        vab = (va + vone_ref[...]).astype(BF16)
        va_ref[0, 0, rows, :] = vab[:, :LANES]
        va_ref[0, 1, rows, :] = vab[:, LANES:]

        du_ref[0, rows, :] = du.astype(BF16)
        dms = jnp.mean(dv * dv, axis=-1, keepdims=True)
        dvn_ref[0, rows, :] = (dv * lax.rsqrt(dms + EPS) * sg_ref[0]).astype(BF16)

        sz_ref[0, rows, :] = (z / (1.0 + jnp.exp(-z))).astype(BF16)


def _inproj(x, mod, row0, l, pp, tabs):
    bsz, n, _ = x.shape
    tm = min(TM_IN, n)
    grid = (bsz, n // tm)
    tok = lambda w: pl.BlockSpec((1, tm, w), lambda b, i: (b, i, 0))
    out_shapes = (
        jax.ShapeDtypeStruct((bsz, n, A_KW), F32),
        jax.ShapeDtypeStruct((bsz, n, A_KW), F32),
        jax.ShapeDtypeStruct((bsz, n, A_W), F32),
        jax.ShapeDtypeStruct((bsz, n, A_KW), F32),
        jax.ShapeDtypeStruct((bsz, n, A_KW), F32),
        jax.ShapeDtypeStruct((bsz, n, B_W), BF16),
        jax.ShapeDtypeStruct((bsz, n, C_W), BF16),
        jax.ShapeDtypeStruct((bsz, C_KV, C_GRP * C_HD, n), BF16),
        jax.ShapeDtypeStruct((bsz, C_KV, n, LANES), BF16),
        jax.ShapeDtypeStruct((bsz, n, D_W), BF16),
        jax.ShapeDtypeStruct((bsz, n, D_W), BF16),
        jax.ShapeDtypeStruct((bsz, n, D_MIX), BF16),
    )
    out_specs = (
        tok(A_KW), tok(A_KW), tok(A_W), tok(A_KW), tok(A_KW), tok(B_W), tok(C_W),
        pl.BlockSpec((1, C_KV, C_GRP * C_HD, tm), lambda b, i: (b, 0, 0, i)),
        pl.BlockSpec((1, C_KV, tm, LANES), lambda b, i: (b, 0, i, 0)),
        tok(D_W), tok(D_W), tok(D_MIX),
    )
    in_specs = [
        tok(D_MODEL),
        pl.BlockSpec((1, 1, 1, 3 * D_MODEL), lambda b, i: (l, row0 + b, 0, 0)),
        _of_layer(l, (1, D_MODEL)),
        _of_layer(l, (D_MODEL, W_PACKED)),
        _of_layer(l, (LANES, 2 * A_KW)),
        _of_layer(l, (1, 2 * A_KW)),
        _of_layer(l, (1, C_W)),
        _of_layer(l, (1, C_KVW)),
        _of_layer(l, (1, D_W)),
        pl.BlockSpec((tm, LANES), lambda b, i: (i, 0)),
        pl.BlockSpec((tm, LANES), lambda b, i: (i, 0)),
        _full((C_W, C_W)),
        _full((1, 2 * LANES)),
    ]
    return pl.pallas_call(
        _inproj_kernel, out_shape=out_shapes, grid=grid, in_specs=in_specs, out_specs=out_specs,
        compiler_params=_params("parallel", "parallel"), name="inproj",
    )(x, mod, pp["pre_g"], pp["w_in"], pp["wg2"], pp["bg"], pp["qg"], pp["kg"], pp["sg"],
      tabs["cos"], tabs["sin"], pp["bd_q"], pp["vone"])


def _head_match(shape, rows_per_head, cols_per_head):
    return (lax.broadcasted_iota(jnp.int32, shape, 0) // rows_per_head
            == lax.broadcasted_iota(jnp.int32, shape, 1) // cols_per_head)


def _gla_band_kernel(q_ref, k_ref, v_ref, gf_ref, gb_ref, of_ref, ob_ref, o_ref,
                     qf_ref, kf_ref, vf_ref, bf_ref, cb_ref, pit_ref, os_ref):
    rows = q_ref.shape[1]
    ts = rows // GLA_SUB
    n_slab = A_W // LANES
    every = lambda p: pl.ds(p, ts, stride=GLA_PITCH)
    hsum = jnp.where(_head_match((A_KW, A_W), A_DK, A_DV), 1.0, 0.0).astype(BF16)
    slab_q, slab_k, slab_gf, slab_gb, slab_v = 0, 1, 2, 3, 4
    for i in range(ts):
        src = slice(i * GLA_SUB, (i + 1) * GLA_SUB)
        dst = slice(i * GLA_PITCH, i * GLA_PITCH + GLA_SUB)
        pit_ref[slab_q, dst, :] = q_ref[0, src, :]
        pit_ref[slab_k, dst, :] = k_ref[0, src, :]
        pit_ref[slab_gf, dst, :] = gf_ref[0, src, :]
        pit_ref[slab_gb, dst, :] = gb_ref[0, src, :]
        for h in range(n_slab):
            pit_ref[slab_v + h, dst, :] = v_ref[0, src, h * LANES:(h + 1) * LANES]
    run = jnp.zeros((ts, A_KW), F32)
    for p in range(GLA_SUB):
        run = run + pit_ref[slab_gf, every(p), :] * LOG2E
        bf_ref[p] = run
        qf_ref[p] = pit_ref[slab_q, every(p), :]
        kf_ref[p] = pit_ref[slab_k, every(p), :]
        vf_ref[p] = jnp.concatenate([pit_ref[slab_v + h, every(p), :] for h in range(n_slab)], axis=-1)
    run = jnp.zeros((ts, A_KW), F32)
    for p in reversed(range(GLA_SUB)):
        run = run + pit_ref[slab_gb, every(p), :] * LOG2E
        cb_ref[p] = run
    for p in range(GLA_SUB):
        q = qf_ref[p]
        prods = []
        for s in range(GLA_SUB):
            qk = q * kf_ref[s]
            if s < p:
                qk = qk * jnp.exp2(bf_ref[p] - bf_ref[s])
            elif s > p:
                qk = qk * jnp.exp2(cb_ref[p] - cb_ref[s])
            prods.append(qk.astype(BF16))
        a = jnp.dot(jnp.concatenate(prods, axis=0), hsum, preferred_element_type=F32)
        acc = a[0:ts] * vf_ref[0]
        for s in range(1, GLA_SUB):
            acc = acc + a[s * ts:(s + 1) * ts] * vf_ref[s]
        for h in range(n_slab):
            os_ref[h, every(p), :] = acc[:, h * LANES:(h + 1) * LANES]
    for i in range(ts):
        src = slice(i * GLA_PITCH, i * GLA_PITCH + GLA_SUB)
        dst = slice(i * GLA_SUB, (i + 1) * GLA_SUB)
        band = jnp.concatenate([os_ref[h, src, :] for h in range(n_slab)], axis=-1)
        o_ref[0, dst, :] = (band + of_ref[0, dst, :].astype(F32) + ob_ref[0, dst, :].astype(F32)).astype(BF16)


def _gla_band(gq, gk, gv, gf, gb, of, ob):
    bsz, n, _ = gq.shape
    rows = min(GLA_BAND_TILE * GLA_SUB, n)
    ts = rows // GLA_SUB
    blk = lambda w: pl.BlockSpec((1, rows, w), lambda b, i: (b, i, 0))
    return pl.pallas_call(
        _gla_band_kernel,
        out_shape=jax.ShapeDtypeStruct((bsz, n, A_W), BF16),
        grid=(bsz, n // rows),
        in_specs=[blk(A_KW), blk(A_KW), blk(A_W), blk(A_KW), blk(A_KW), blk(A_W), blk(A_W)],
        out_specs=blk(A_W),
        scratch_shapes=[
            pltpu.VMEM((GLA_SUB, ts, A_KW), F32), pltpu.VMEM((GLA_SUB, ts, A_KW), F32),
            pltpu.VMEM((GLA_SUB, ts, A_W), F32),
            pltpu.VMEM((GLA_SUB, ts, A_KW), F32), pltpu.VMEM((GLA_SUB, ts, A_KW), F32),
            pltpu.VMEM((4 + A_W // LANES, ts * GLA_PITCH, LANES), F32),
            pltpu.VMEM((A_W // LANES, ts * GLA_PITCH, LANES), F32),
        ],
        compiler_params=_params("parallel", "parallel"), name="gla_band",
    )(gq, gk, gv, gf, gb, of, ob)


def _split3(x):
    hi = x.astype(BF16)
    rest = x - hi.astype(F32)
    mid = rest.astype(BF16)
    lo = (rest - mid.astype(F32)).astype(BF16)
    return hi, mid, lo


def _gla_prep(q_ref, k_ref, g_ref, qt_ref, kh_ref, dect_ref, tot_ref, pad_ref, reverse):
    seg = q_ref.shape[1]
    n_sub = seg // GLA_SUB
    r_i = lax.broadcasted_iota(jnp.int32, (GLA_GRP, GLA_GRP), 0)
    c_i = lax.broadcasted_iota(jnp.int32, (GLA_GRP, GLA_GRP), 1)
    same = (r_i // GLA_SUB) == (c_i // GLA_SUB)
    tri = (c_i >= r_i) if reverse else (c_i <= r_i)
    sums_mat = jnp.concatenate([jnp.where(same & tri, 1.0, 0.0), jnp.where(same, 1.0, 0.0)], axis=0).astype(BF16)
    for r in range(seg // GLA_GRP):
        rows = slice(r * GLA_GRP, (r + 1) * GLA_GRP)
        g3 = jnp.concatenate(_split3(g_ref[0, rows, :]), axis=-1)
        sums = jnp.dot(sums_mat, g3, preferred_element_type=F32)
        sums = sums[:, 0:A_KW] + sums[:, A_KW:2 * A_KW] + sums[:, 2 * A_KW:]
        cum, tot = sums[:GLA_GRP], sums[GLA_GRP:]
        qt_ref[rows, :] = (q_ref[0, rows, :] * jnp.exp(cum)).astype(BF16)
        kh_ref[rows, :] = (k_ref[0, rows, :] * jnp.exp(tot - cum)).astype(BF16)
        tot_ref[rows, :] = tot
    pad_ref[...] = jnp.zeros_like(pad_ref)
    pad_ref[0:n_sub, :] = tot_ref[pl.ds(0, n_sub, stride=GLA_SUB), :]
    dect_ref[...] = jnp.exp(pad_ref[...].T)


def _gla_rec_kernel(qf_ref, kf_ref, vf_ref, gf_ref, qb_ref, kb_ref, vb_ref, gb_ref, of_ref, ob_ref,
                    sf_ref, sb_ref, qtf_ref, khf_ref, dtf_ref, qtb_ref, khb_ref, dtb_ref, tot_ref, pad_ref):
    @pl.when(pl.program_id(1) == 0)
    def _():
        sf_ref[...] = jnp.zeros_like(sf_ref)
        sb_ref[...] = jnp.zeros_like(sb_ref)

    seg = qf_ref.shape[1]
    n_sub = seg // GLA_SUB
    _gla_prep(qf_ref, kf_ref, gf_ref, qtf_ref, khf_ref, dtf_ref, tot_ref, pad_ref, reverse=False)
    _gla_prep(qb_ref, kb_ref, gb_ref, qtb_ref, khb_ref, dtb_ref, tot_ref, pad_ref, reverse=True)
    smask = _head_match((A_KW, A_W), A_DK, A_DV)

    def update(i, kh_ref, v_ref):
        rows = slice(i * GLA_SUB, (i + 1) * GLA_SUB)
        upd = lax.dot_general(kh_ref[rows, :], v_ref[0, rows, :].astype(BF16), (((0,), (0,)), ((), ())),
                              preferred_element_type=F32)
        return jnp.where(smask, upd, 0.0)

    def step(i, s, upd, qt_ref, dect_ref, o_ref):
        rows = slice(i * GLA_SUB, (i + 1) * GLA_SUB)
        o_ref[0, rows, :] = jnp.dot(qt_ref[rows, :], s.astype(BF16), preferred_element_type=F32).astype(BF16)
        return s * dect_ref[:, i:i + 1] + upd

    order_f = list(range(n_sub))
    order_b = list(reversed(range(n_sub)))
    upd_f = [update(i, khf_ref, vf_ref) for i in order_f[:GLA_LOOKAHEAD]]
    upd_b = [update(i, khb_ref, vb_ref) for i in order_b[:GLA_LOOKAHEAD]]
    s_f = sf_ref[...]
    s_b = sb_ref[...]
    for j in range(n_sub):
        if j + GLA_LOOKAHEAD < n_sub:
            upd_f.append(update(order_f[j + GLA_LOOKAHEAD], khf_ref, vf_ref))
            upd_b.append(update(order_b[j + GLA_LOOKAHEAD], khb_ref, vb_ref))
        s_f = step(order_f[j], s_f, upd_f[j], qtf_ref, dtf_ref, of_ref)
        s_b = step(order_b[j], s_b, upd_b[j], qtb_ref, dtb_ref, ob_ref)
    sf_ref[...] = s_f
    sb_ref[...] = s_b


def _gla_rec(gq, gk, gv, gf, gb):
    bsz, n, _ = gq.shape
    seg = min(GLA_SEG, n)
    nseg = n // seg
    assert seg // GLA_SUB <= LANES
    fwd = lambda w: pl.BlockSpec((1, seg, w), lambda b, s: (b, s, 0))
    bwd = lambda w: pl.BlockSpec((1, seg, w), lambda b, s: (b, nseg - 1 - s, 0))
    return pl.pallas_call(
        _gla_rec_kernel,
        out_shape=(jax.ShapeDtypeStruct((bsz, n, A_W), BF16), jax.ShapeDtypeStruct((bsz, n, A_W), BF16)),
        grid=(bsz, nseg),
        in_specs=[fwd(A_KW), fwd(A_KW), fwd(A_W), fwd(A_KW), bwd(A_KW), bwd(A_KW), bwd(A_W), bwd(A_KW)],
        out_specs=(fwd(A_W), bwd(A_W)),
        scratch_shapes=[
            pltpu.VMEM((A_KW, A_W), F32), pltpu.VMEM((A_KW, A_W), F32),
            pltpu.VMEM((seg, A_KW), BF16), pltpu.VMEM((seg, A_KW), BF16), pltpu.VMEM((A_KW, LANES), F32),
            pltpu.VMEM((seg, A_KW), BF16), pltpu.VMEM((seg, A_KW), BF16), pltpu.VMEM((A_KW, LANES), F32),
            pltpu.VMEM((seg, A_KW), F32), pltpu.VMEM((LANES, A_KW), F32),
        ],
        compiler_params=_params("parallel", "arbitrary"), name="gla_rec",
    )(gq, gk, gv, gf, gq, gk, gv, gb)


def _fft1_kernel(u_ref, a1_ref, cos_ref, sin_ref, t_ref):
    n1 = u_ref.shape[1]
    nb = cos_ref.shape[0]
    t = jnp.dot(a1_ref[...], u_ref[0], preferred_element_type=F32)
    for j in range(nb):
        cols = slice(j * B_W, (j + 1) * B_W)
        tr = t[:n1, cols]
        ti = t[n1:, cols]
        c = jnp.concatenate([cos_ref[j]] * (B_W // LANES), axis=-1)
        s = jnp.concatenate([sin_ref[j]] * (B_W // LANES), axis=-1)
        t_ref[0, 0, :, cols] = (tr * c + ti * s).astype(BF16)
        t_ref[0, 1, :, cols] = (ti * c - tr * s).astype(BF16)


def _fft2_kernel(t_ref, l_ref, cc_ref, cs_ref, w_ref, o_ref):
    n2 = t_ref.shape[3]
    half = FFT_K1 * n2
    groups = range(t_ref.shape[2] // FFT_K1)
    xs = []
    for grp in groups:
        k1s = slice(grp * FFT_K1, (grp + 1) * FFT_K1)
        xs.append(jnp.concatenate([t_ref[0, 0, k1s].reshape(half, B_W), t_ref[0, 1, k1s].reshape(half, B_W)],
                                  axis=0))
    gs = [jnp.dot(l_ref[...], x, preferred_element_type=F32) for x in xs]
    mixed = [jnp.dot(g[:half].astype(BF16), cc_ref[...], preferred_element_type=F32)
             + jnp.dot(g[half:].astype(BF16), cs_ref[...], preferred_element_type=F32) for g in gs]
    ys = [jnp.dot(m.astype(BF16), w_ref[0], preferred_element_type=F32) for m in mixed]
    o_ref[0] = jnp.concatenate([y.reshape(n2, FFT_K1, B_W) for y in ys], axis=1).astype(BF16)


def _fnet(fu, l, pp, tabs):
    bsz, n, _ = fu.shape
    n2 = FFT_N2
    n1 = n // n2
    nb = FFT_NB
    k1b = FFT_K1 * FFT_K1_GROUPS
    u2 = fu.reshape(bsz, n1, n2 * B_W)
    tt = pl.pallas_call(
        _fft1_kernel,
        out_shape=jax.ShapeDtypeStruct((bsz, 2, n1, n2 * B_W), BF16),
        grid=(n2 // nb, bsz),
        in_specs=[
            pl.BlockSpec((1, n1, nb * B_W), lambda j, b: (b, 0, j)),
            _full((2 * n1, n1)),
            pl.BlockSpec((nb, n1, LANES), lambda j, b: (j, 0, 0)),
            pl.BlockSpec((nb, n1, LANES), lambda j, b: (j, 0, 0)),
        ],
        out_specs=pl.BlockSpec((1, 2, n1, nb * B_W), lambda j, b: (b, 0, 0, j)),
        compiler_params=_params("parallel", "parallel"), name="fft_pass1",
    )(u2, tabs["a1"], tabs["tw_cos"], tabs["tw_sin"])
    t5 = tt.reshape(bsz, 2, n1, n2, B_W)
    y = pl.pallas_call(
        _fft2_kernel,
        out_shape=jax.ShapeDtypeStruct((bsz, n2, n1, B_W), BF16),
        grid=(bsz, n1 // k1b),
        in_specs=[
            pl.BlockSpec((1, 2, k1b, n2, B_W), lambda b, i: (b, 0, i, 0, 0)),
            _full((2 * FFT_K1 * n2, 2 * FFT_K1 * n2)),
            _full((B_W, B_W)), _full((B_W, B_W)), _of_layer(l, (B_W, B_W)),
        ],
        out_specs=pl.BlockSpec((1, n2, k1b, B_W), lambda b, i: (b, 0, i, 0)),
        compiler_params=_params("parallel", "parallel"), name="fft_pass2",
    )(t5, tabs["l2"], tabs["chan_cos"], tabs["chan_sin"], pp["fnet_w"])
    return y.reshape(bsz, n, B_W)


def _attn_kernel(q_ref, kt_ref, va_ref, o_ref, qm_ref, m_ref, acc_ref):
    tq = q_ref.shape[1]
    n = kt_ref.shape[3]
    tk = min(ATT_TK, n)
    q = q_ref[0]
    lane = lax.broadcasted_iota(jnp.int32, q.shape, 1)
    for h in range(C_GRP):
        qm_ref[h * tq:(h + 1) * tq, :] = jnp.where(lane // C_HD == h, q, jnp.zeros_like(q))

    def scores(c):
        c0 = pl.multiple_of(c * tk, tk)
        return jnp.dot(qm_ref[...], kt_ref[0, 0, :, pl.ds(c0, tk)], preferred_element_type=F32)

    def softmax_accumulate(c, s, first):
        c0 = pl.multiple_of(c * tk, tk)
        v = va_ref[0, 0, pl.ds(c0, tk), :]
        s_max = s[:, 0:LANES]
        for t in range(1, tk // LANES):
            s_max = jnp.maximum(s_max, s[:, t * LANES:(t + 1) * LANES])
        row_max = jnp.max(s_max, axis=-1, keepdims=True)
        if first:
            m_new = jnp.broadcast_to(row_max, m_ref.shape)
        else:
            m_old = m_ref[...]
            m_new = jnp.maximum(m_old, row_max)
        p = jnp.exp2(s - jnp.concatenate([m_new] * (tk // LANES), axis=-1))
        pv = jnp.dot(p.astype(BF16), v, preferred_element_type=F32)
        acc_ref[...] = pv if first else jnp.exp2(m_old - m_new) * acc_ref[...] + pv
        m_ref[...] = m_new

    def step(c, carry):
        softmax_accumulate(c, scores(c), first=False)
        return carry

    softmax_accumulate(0, scores(0), first=True)
    lax.fori_loop(1, n // tk, step, 0, unroll=ATT_UNROLL)
    acc = acc_ref[...]
    inv = 1.0 / acc[:, ONES_LANE:ONES_LANE + 1]
    outs = [(acc[h * tq:(h + 1) * tq, 0:C_HD] * inv[h * tq:(h + 1) * tq]) for h in range(C_GRP)]
    o_ref[0] = jnp.concatenate(outs, axis=-1).astype(BF16)


def _attention(aq, kt4, va):
    bsz, n, _ = aq.shape
    tq = min(ATT_TQ, n)
    return pl.pallas_call(
        _attn_kernel,
        out_shape=jax.ShapeDtypeStruct((bsz, n, C_W), BF16),
        grid=(bsz, C_KV, n // tq),
        in_specs=[
            pl.BlockSpec((1, tq, C_GRP * C_HD), lambda b, g, i: (b, i, g)),
            pl.BlockSpec((1, 1, C_GRP * C_HD, n), lambda b, g, i: (b, g, 0, 0)),
            pl.BlockSpec((1, 1, n, LANES), lambda b, g, i: (b, g, 0, 0)),
        ],
        out_specs=pl.BlockSpec((1, tq, C_GRP * C_HD), lambda b, g, i: (b, i, g)),
        scratch_shapes=[
            pltpu.VMEM((C_GRP * tq, C_GRP * C_HD), BF16),
            pltpu.VMEM((C_GRP * tq, LANES), F32),
            pltpu.VMEM((C_GRP * tq, LANES), F32),
        ],
        compiler_params=_params("parallel", "parallel", "arbitrary"), name="attention",
    )(aq, kt4, va)


def _outproj_kernel(x_ref, mod_ref, oa_ref, yb_ref, ao_ref, du_ref, dvn_ref, sz_ref,
                    w_ref, postg_ref, og_ref, bd_ref, sw_ref, sb_ref, o_ref):
    tm = x_ref.shape[1]
    sz = sz_ref[0].astype(F32)
    off_b, off_c, off_d = A_W, A_W + B_W, A_W + B_W + C_W

    sgu = [jnp.dot(sw_ref[0], dvn_ref[0, c * SGU_CHUNK:(c + 1) * SGU_CHUNK, :], preferred_element_type=F32)
           for c in range(tm // SGU_CHUNK)]
    m_c = (ao_ref[0].astype(F32) * sz[:, off_c:off_d]).astype(BF16)
    y = jnp.dot(m_c, w_ref[0, off_c:off_d, :], preferred_element_type=F32)
    m_b = (yb_ref[0].astype(F32) * sz[:, off_b:off_c]).astype(BF16)
    y = y + jnp.dot(m_b, w_ref[0, off_b:off_c, :], preferred_element_type=F32)

    o = oa_ref[0].astype(F32)
    oms = jnp.dot((o * o).astype(BF16), bd_ref[...], preferred_element_type=F32)
    out_a = o * lax.rsqrt(oms + EPS) * og_ref[0]
    y = y + jnp.dot((out_a * sz[:, 0:off_b]).astype(BF16), w_ref[0, 0:off_b, :], preferred_element_type=F32)

    lane = lax.broadcasted_iota(jnp.int32, (SGU_CHUNK, D_W), 1)
    parts = []
    for c, r in enumerate(sgu):
        rows = slice(c * SGU_CHUNK, (c + 1) * SGU_CHUNK)
        mix = sb_ref[0]
        for g in range(D_GROUPS):
            mix = mix + jnp.where(lane // D_GC == g, r[g * SGU_CHUNK:(g + 1) * SGU_CHUNK, :], 0.0)
        parts.append(du_ref[0, rows, :].astype(F32) * mix)
    out_d = jnp.concatenate(parts, axis=0)
    m_d = (out_d * sz[:, off_d:]).astype(BF16)
    y = y + jnp.dot(m_d, w_ref[0, off_d:, :], preferred_element_type=F32)

    yms = jnp.mean(y * y, axis=-1, keepdims=True)
    yn = y * lax.rsqrt(yms + EPS) * postg_ref[0]
    gate = mod_ref[0, 0, :, 2 * D_MODEL:3 * D_MODEL]
    o_ref[0] = x_ref[0] + gate * yn


def _outproj(x, mod, row0, l, oa, yb, ao, du, dvn, sz, pp):
    bsz, n, _ = x.shape
    tm = min(TM_OUT, n)
    tok = lambda w: pl.BlockSpec((1, tm, w), lambda b, i: (b, i, 0))
    return pl.pallas_call(
        _outproj_kernel,
        out_shape=jax.ShapeDtypeStruct((bsz, n, D_MODEL), F32),
        grid=(bsz, n // tm),
        in_specs=[
            tok(D_MODEL),
            pl.BlockSpec((1, 1, 1, 3 * D_MODEL), lambda b, i: (l, row0 + b, 0, 0)),
            tok(A_W), tok(B_W), tok(C_W), tok(D_W), tok(D_W), tok(D_MIX),
            _of_layer(l, (D_MIX, D_MODEL)), _of_layer(l, (1, D_MODEL)), _of_layer(l, (1, A_W)), _full((A_W, A_W)),
            _of_layer(l, (D_GROUPS * SGU_CHUNK, SGU_CHUNK)), _of_layer(l, (SGU_CHUNK, D_W)),
        ],
        out_specs=tok(D_MODEL),
        compiler_params=_params("parallel", "parallel"), name="outproj",
    )(x, mod, oa, yb, ao, du, dvn, sz,
      pp["w_out"], pp["post_g"], pp["og"], pp["bd_a"], pp["sgu_w"], pp["sgu_b"])


def _pack_w_in(w):
    sizes = [A_KW, A_KW, A_W, 2 * GLA_RANK, B_W, C_W, C_KVW, C_KVW, D_W, D_W, D_MIX]
    off = [0] + [int(o) for o in np.cumsum(sizes)]
    a_q, a_k, a_v, a_lr, b_u, c_q, c_k, c_v, d_u, d_v, z = [w[:, off[i]:off[i + 1]] for i in range(len(sizes))]

    def deinterleave(x, heads):
        r = x.reshape(D_MODEL, heads, C_HD // 2, 2)
        return jnp.concatenate([r[..., 0], r[..., 1]], axis=-1).reshape(D_MODEL, heads * C_HD)

    zeros = lambda width: jnp.zeros((D_MODEL, width), w.dtype)
    pieces = [a_q, a_k, a_v, b_u, deinterleave(c_q, C_HEADS), deinterleave(c_k, C_KV),
              a_lr, zeros(LANES - 2 * GLA_RANK)]
    for h in range(C_KV):
        pieces += [c_v[:, h * C_HD:(h + 1) * C_HD], zeros(LANES - C_HD)]
    pieces += [d_u, d_v, z]
    packed = jnp.concatenate(pieces, axis=1)
    assert packed.shape[1] == W_PACKED
    return packed.astype(BF16)


def _block_diag_mean(width, group):
    idx = np.arange(width) // group
    return jnp.asarray((idx[:, None] == idx[None, :]).astype(np.float32) / group, BF16)


def _stacked_params(norm_pre_g, norm_post_g, w_in, gla_wg2_f, gla_bg_f, gla_wg2_b, gla_bg_b,
                    gla_onorm_g, fnet_w, q_norm_g, k_norm_g, sgu_norm_g, sgu_w, sgu_b, w_out):
    depth = w_in.shape[0]
    deint = np.concatenate([np.arange(0, C_HD, 2), np.arange(1, C_HD, 2)])
    wg2 = jnp.zeros((depth, LANES, 2 * A_KW), F32)
    wg2 = wg2.at[:, 0:GLA_RANK, 0:A_KW].set(gla_wg2_f)
    wg2 = wg2.at[:, GLA_RANK:2 * GLA_RANK, A_KW:].set(gla_wg2_b)
    vone = np.zeros((1, 2 * LANES), np.float32)
    vone[0, ONES_LANE] = 1.0
    vone[0, LANES + ONES_LANE] = 1.0
    return {
        "pre_g": norm_pre_g.reshape(depth, 1, D_MODEL),
        "post_g": norm_post_g.reshape(depth, 1, D_MODEL),
        "w_in": jax.vmap(_pack_w_in)(w_in),
        "wg2": wg2.astype(BF16),
        "bg": jnp.concatenate([gla_bg_f, gla_bg_b], axis=-1).reshape(depth, 1, 2 * A_KW),
        "qg": jnp.tile(q_norm_g[:, deint], (1, C_HEADS)).reshape(depth, 1, C_W),
        "kg": jnp.tile(k_norm_g[:, deint], (1, C_KV)).reshape(depth, 1, C_KVW),
        "sg": sgu_norm_g.reshape(depth, 1, D_W),
        "bd_q": _block_diag_mean(C_W, C_HD),
        "bd_a": _block_diag_mean(A_W, A_DV),
        "vone": jnp.asarray(vone),
        "og": jnp.tile(gla_onorm_g, (1, A_HEADS)).reshape(depth, 1, A_W),
        "fnet_w": fnet_w.astype(BF16),
        "sgu_w": sgu_w.reshape(depth, D_GROUPS * SGU_CHUNK, SGU_CHUNK).astype(BF16),
        "sgu_b": jnp.repeat(jnp.swapaxes(sgu_b, 1, 2), D_GC, axis=2),
        "w_out": w_out.astype(BF16),
    }


def _seq_tables(n):
    rows = n // GRID_W
    row = jnp.repeat(jnp.arange(rows, dtype=F32), GRID_W)
    col = jnp.tile(jnp.arange(GRID_W, dtype=F32), rows)
    rope_axis = C_HD // 2
    freqs = ROPE_THETA ** (-jnp.arange(0, rope_axis, 2, dtype=F32) / rope_axis)
    ang = jnp.concatenate([row[:, None] * freqs, col[:, None] * freqs], axis=-1)
    cos, sin = jnp.cos(ang), jnp.sin(ang)
    cos_t = jnp.tile(jnp.concatenate([cos, cos], axis=-1), (1, LANES // C_HD))
    sin_t = jnp.tile(jnp.concatenate([-sin, sin], axis=-1), (1, LANES // C_HD))

    n2 = FFT_N2
    n1 = n // n2

    def dft_angles(rows, cols, period):
        prod = (jnp.arange(rows, dtype=jnp.int32)[:, None] * jnp.arange(cols, dtype=jnp.int32)[None, :]) % period
        return prod.astype(F32) * (2.0 * math.pi / period)

    a1_ang = dft_angles(n1, n1, n1)
    a1 = jnp.concatenate([jnp.cos(a1_ang), -jnp.sin(a1_ang)], axis=0)
    tw_ang = dft_angles(n2, n1, n)
    scale = 1.0 / math.sqrt(n * B_GC)
    tw_cos = jnp.broadcast_to((jnp.cos(tw_ang) * scale)[:, :, None], (n2, n1, LANES))
    tw_sin = jnp.broadcast_to((jnp.sin(tw_ang) * scale)[:, :, None], (n2, n1, LANES))
    ang2 = dft_angles(n2, n2, n2)
    c2, s2 = jnp.cos(ang2), jnp.sin(ang2)
    eye = jnp.eye(FFT_K1, dtype=F32)
    blk = lambda m: (m[:, None, None, :] * eye[None, :, :, None]).reshape(n2 * FFT_K1, FFT_K1 * n2)
    l2 = jnp.block([[blk(c2), blk(s2)], [blk(-s2), blk(c2)]])
    angc = dft_angles(B_GC, B_GC, B_GC)
    grp = jnp.eye(B_GROUPS, dtype=F32)
    kron = lambda m: (grp[:, None, :, None] * m[None, :, None, :]).reshape(B_W, B_W)
    chan_cos = kron(jnp.cos(angc))
    chan_sin = kron(jnp.sin(angc))
    return {
        "cos": cos_t, "sin": sin_t,
        "a1": a1.astype(BF16),
        "tw_cos": tw_cos, "tw_sin": tw_sin,
        "l2": l2.astype(BF16),
        "chan_cos": chan_cos.astype(BF16), "chan_sin": chan_sin.astype(BF16),
    }


def _layer(x, mod, row0, l, pp, tabs):
    gq, gk, gv, gf, gb, fu, aq, kt4, va, du, dvn, sz = _inproj(x, mod, row0, l, pp, tabs)
    of, ob = _gla_rec(gq, gk, gv, gf, gb)
    oa = _gla_band(gq, gk, gv, gf, gb, of, ob)
    yb = _fnet(fu, l, pp, tabs)
    ao = _attention(aq, kt4, va)
    return _outproj(x, mod, row0, l, oa, yb, ao, du, dvn, sz, pp)


def kernel(x_prompt, x_sample, c_prompt, c_sample, ada_w, ada_b, norm_pre_g, norm_post_g, w_in,
           gla_wg2_f, gla_bg_f, gla_wg2_b, gla_bg_b, gla_onorm_g, fnet_w, q_norm_g, k_norm_g,
           sgu_norm_g, sgu_w, sgu_b, w_out):
    bp, bs = x_prompt.shape[0], x_sample.shape[0]
    pad_rows = (-(bp + bs)) % 8
    c_all = jnp.concatenate([c_prompt, c_sample, jnp.zeros((pad_rows, D_MODEL), F32)], axis=0)
    mod = _adaln(c_all, ada_w, ada_b)
    mod = mod.reshape(DEPTH, mod.shape[1], 1, 3 * D_MODEL)
    pp = _stacked_params(norm_pre_g, norm_post_g, w_in, gla_wg2_f, gla_bg_f, gla_wg2_b, gla_bg_b,
                         gla_onorm_g, fnet_w, q_norm_g, k_norm_g, sgu_norm_g, sgu_w, sgu_b, w_out)
    tabs_p = _seq_tables(x_prompt.shape[1])
    tabs_s = _seq_tables(x_sample.shape[1])
    y_prompt, y_sample = x_prompt, x_sample
    for l in range(DEPTH):
        y_prompt = _layer(y_prompt, mod, 0, l, pp, tabs_p)
        y_sample = _layer(y_sample, mod, bp, l, pp, tabs_s)
    return (y_prompt, y_sample)
```

```python
import math

import numpy as np
import jax
import jax.numpy as jnp
from jax import lax
from jax.experimental import pallas as pl
from jax.experimental.pallas import tpu as pltpu

F32 = jnp.float32
BF16 = jnp.bfloat16
HIGHEST = lax.Precision.HIGHEST

D_MODEL = 1024
DEPTH = 2
GRID_W = 64
A_HEADS, A_DK, A_DV = 4, 32, 64
A_W, A_KW = A_HEADS * A_DV, A_HEADS * A_DK
GLA_RANK = 16
GLA_GATE_NORM = 16.0
B_GROUPS, B_GC = 4, 64
B_W = B_GROUPS * B_GC
C_HEADS, C_KV, C_HD = 8, 2, 64
C_GRP = C_HEADS // C_KV
C_W, C_KVW = C_HEADS * C_HD, C_KV * C_HD
ROPE_THETA = 10000.0
D_GROUPS, D_GC = 4, 64
D_W = D_GROUPS * D_GC
SGU_CHUNK = 128
D_MIX = A_W + B_W + C_W + D_W
EPS = 1e-6
LOG2E = 1.4426950408889634

LANES = 128
VMEM_LIMIT_BYTES = 56 * 1024 * 1024

GQ, GK, GV, FU, AQ, AK, LR, AV, DU, DV, ZZ = 0, 128, 256, 512, 768, 1280, 1408, 1536, 1792, 2048, 2304
W_PACKED = 3584
AV_W = 2 * LANES
ONES_LANE = C_HD

TM_IN = 1024
IN_ROW_PARTS = 4
TM_OUT = 1024
GLA_SUB = 16
GLA_BAND_TILE = 64
GLA_PITCH = GLA_SUB + 4
GLA_GRP = 128
GLA_SEG = 1024
GLA_LOOKAHEAD = 3
FFT_N2 = 64
FFT_NB = 16
FFT_K1 = 8
FFT_K1_GROUPS = 2
ATT_TQ = 512
ATT_TK = 2048
ATT_UNROLL = 2


def _params(*sem):
    return pltpu.CompilerParams(dimension_semantics=sem, vmem_limit_bytes=VMEM_LIMIT_BYTES)


def _full(shape):
    n = len(shape)
    return pl.BlockSpec(shape, lambda *_: (0,) * n)


def _of_layer(l, shape):
    n = len(shape)
    return pl.BlockSpec((1,) + tuple(shape), lambda *_: (l,) + (0,) * n)


def _adaln_kernel(c_ref, w_ref, b_ref, o_ref):
    c = c_ref[...]
    sc = c / (1.0 + jnp.exp(-c))
    o_ref[0] = jnp.dot(sc, w_ref[0], precision=HIGHEST, preferred_element_type=F32) + b_ref[0]


def _adaln(c_all, ada_w, ada_b):
    rows = c_all.shape[0]
    ncol = 3 * D_MODEL // D_MODEL
    return pl.pallas_call(
        _adaln_kernel,
        out_shape=jax.ShapeDtypeStruct((DEPTH, rows, 3 * D_MODEL), F32),
        grid=(DEPTH, ncol),
        in_specs=[
            pl.BlockSpec((rows, D_MODEL), lambda l, j: (0, 0)),
            pl.BlockSpec((1, D_MODEL, D_MODEL), lambda l, j: (l, 0, j)),
            pl.BlockSpec((1, 1, D_MODEL), lambda l, j: (l, 0, j)),
        ],
        out_specs=pl.BlockSpec((1, rows, D_MODEL), lambda l, j: (l, 0, j)),
        compiler_params=_params("arbitrary", "arbitrary"),
        name="adaln",
    )(c_all, ada_w, ada_b.reshape(DEPTH, 1, 3 * D_MODEL))


def _rope(x, cos, sin_signed):
    width = x.shape[-1]
    lane = lax.broadcasted_iota(jnp.int32, x.shape, 1)
    first_half = (lane % C_HD) < (C_HD // 2)
    swapped = jnp.where(first_half, pltpu.roll(x, width - C_HD // 2, 1), pltpu.roll(x, C_HD // 2, 1))
    reps = width // LANES
    cos_w = jnp.concatenate([cos] * reps, axis=-1) if reps > 1 else cos
    sin_w = jnp.concatenate([sin_signed] * reps, axis=-1) if reps > 1 else sin_signed
    return x * cos_w + swapped * sin_w


def _inproj_kernel(x_ref, mod_ref, preg_ref, w_ref, wg2_ref, bg_ref, qg_ref, kg_ref, sg_ref,
                   cos_ref, sin_ref, bd_ref, vone_ref,
                   gq_ref, gk_ref, gv_ref, gf_ref, gb_ref, fu_ref, aq_ref, kt_ref, va_ref,
                   du_ref, dvn_ref, sz_ref):
    tm = x_ref.shape[1]
    shift = mod_ref[0, 0, :, 0:D_MODEL]
    gain = preg_ref[0] * (1.0 + mod_ref[0, 0, :, D_MODEL:2 * D_MODEL])
    for part in range(IN_ROW_PARTS):
        rows = slice(part * tm // IN_ROW_PARTS, (part + 1) * tm // IN_ROW_PARTS)
        x = x_ref[0, rows, :]
        ms = jnp.mean(x * x, axis=-1, keepdims=True)
        hb = ((x * lax.rsqrt(ms + EPS)) * gain + shift).astype(BF16)

        def seg(off, width):
            return jnp.dot(hb, w_ref[0, :, off:off + width], preferred_element_type=F32)

        klr = seg(AK, C_KVW + LANES)
        q = seg(AQ, C_W)
        z = seg(ZZ, D_MIX)
        gqk = seg(GQ, 2 * A_KW)
        gv = seg(GV, A_W)
        fu = seg(FU, B_W)
        va = seg(AV, AV_W)
        du = seg(DU, D_W)
        dv = seg(DV, D_W)
        k = klr[:, :C_KVW]
        lr = klr[:, C_KVW:].astype(BF16)
        logits = jnp.dot(lr, wg2_ref[0], preferred_element_type=F32) + bg_ref[0]
        qms = jnp.dot((q * q).astype(BF16), bd_ref[...], preferred_element_type=F32)
        kms = jnp.dot((k * k).astype(BF16), bd_ref[0:C_KVW, 0:C_KVW], preferred_element_type=F32)

        gq_ref[0, rows, :] = gqk[:, :A_KW] * (A_DK ** -0.5)
        gk_ref[0, rows, :] = gqk[:, A_KW:]
        gv_ref[0, rows, :] = gv
        logg = (jnp.minimum(logits, 0.0) - jnp.log1p(jnp.exp(-jnp.abs(logits)))) * (1.0 / GLA_GATE_NORM)
        gf_ref[0, rows, :] = logg[:, :A_KW]
        gb_ref[0, rows, :] = logg[:, A_KW:]

        fu_ref[0, rows, :] = fu.astype(BF16)

        cos = cos_ref[rows, :]
        sin = sin_ref[rows, :]
        qn = q * lax.rsqrt(qms + EPS) * qg_ref[0]
        aq_ref[0, rows, :] = (_rope(qn, cos, sin) * (C_HD ** -0.5 * LOG2E)).astype(BF16)
        kn = k * lax.rsqrt(kms + EPS) * kg_ref[0]
        kt = _rope(kn, cos, sin).T.astype(BF16)
        for g in range(C_KV):
            for r in range(C_GRP):
                kt_ref[0, g, r * C_HD:(r + 1) * C_HD, rows] = kt[g * C_HD:(g + 1) * C_HD, :]
        vab = (va + vone_ref[...]).astype(BF16)
        va_ref[0, 0, rows, :] = vab[:, :LANES]
        va_ref[0, 1, rows, :] = vab[:, LANES:]

        du_ref[0, rows, :] = du.astype(BF16)
        dms = jnp.mean(dv * dv, axis=-1, keepdims=True)
        dvn_ref[0, rows, :] = (dv * lax.rsqrt(dms + EPS) * sg_ref[0]).astype(BF16)

        sz_ref[0, rows, :] = (z / (1.0 + jnp.exp(-z))).astype(BF16)


def _inproj(x, mod, row0, l, pp, tabs):
    bsz, n, _ = x.shape
    tm = min(TM_IN, n)
    grid = (bsz, n // tm)
    tok = lambda w: pl.BlockSpec((1, tm, w), lambda b, i: (b, i, 0))
    out_shapes = (
        jax.ShapeDtypeStruct((bsz, n, A_KW), F32),
        jax.ShapeDtypeStruct((bsz, n, A_KW), F32),
        jax.ShapeDtypeStruct((bsz, n, A_W), F32),
        jax.ShapeDtypeStruct((bsz, n, A_KW), F32),
        jax.ShapeDtypeStruct((bsz, n, A_KW), F32),
        jax.ShapeDtypeStruct((bsz, n, B_W), BF16),
        jax.ShapeDtypeStruct((bsz, n, C_W), BF16),
        jax.ShapeDtypeStruct((bsz, C_KV, C_GRP * C_HD, n), BF16),
        jax.ShapeDtypeStruct((bsz, C_KV, n, LANES), BF16),
        jax.ShapeDtypeStruct((bsz, n, D_W), BF16),
        jax.ShapeDtypeStruct((bsz, n, D_W), BF16),
        jax.ShapeDtypeStruct((bsz, n, D_MIX), BF16),
    )
    out_specs = (
        tok(A_KW), tok(A_KW), tok(A_W), tok(A_KW), tok(A_KW), tok(B_W), tok(C_W),
        pl.BlockSpec((1, C_KV, C_GRP * C_HD, tm), lambda b, i: (b, 0, 0, i)),
        pl.BlockSpec((1, C_KV, tm, LANES), lambda b, i: (b, 0, i, 0)),
        tok(D_W), tok(D_W), tok(D_MIX),
    )
    in_specs = [
        tok(D_MODEL),
        pl.BlockSpec((1, 1, 1, 3 * D_MODEL), lambda b, i: (l, row0 + b, 0, 0)),
        _of_layer(l, (1, D_MODEL)),
        _of_layer(l, (D_MODEL, W_PACKED)),
        _of_layer(l, (LANES, 2 * A_KW)),
        _of_layer(l, (1, 2 * A_KW)),
        _of_layer(l, (1, C_W)),
        _of_layer(l, (1, C_KVW)),
        _of_layer(l, (1, D_W)),
        pl.BlockSpec((tm, LANES), lambda b, i: (i, 0)),
        pl.BlockSpec((tm, LANES), lambda b, i: (i, 0)),
        _full((C_W, C_W)),
        _full((1, 2 * LANES)),
    ]
    return pl.pallas_call(
        _inproj_kernel, out_shape=out_shapes, grid=grid, in_specs=in_specs, out_specs=out_specs,
        compiler_params=_params("parallel", "parallel"), name="inproj",
    )(x, mod, pp["pre_g"], pp["w_in"], pp["wg2"], pp["bg"], pp["qg"], pp["kg"], pp["sg"],
      tabs["cos"], tabs["sin"], pp["bd_q"], pp["vone"])


def _head_match(shape, rows_per_head, cols_per_head):
    return (lax.broadcasted_iota(jnp.int32, shape, 0) // rows_per_head
            == lax.broadcasted_iota(jnp.int32, shape, 1) // cols_per_head)


def _gla_band_kernel(q_ref, k_ref, v_ref, gf_ref, gb_ref, of_ref, ob_ref, o_ref,
                     qf_ref, kf_ref, vf_ref, bf_ref, cb_ref, pit_ref, os_ref):
    rows = q_ref.shape[1]
    ts = rows // GLA_SUB
    n_slab = A_W // LANES
    every = lambda p: pl.ds(p, ts, stride=GLA_PITCH)
    hsum = jnp.where(_head_match((A_KW, A_W), A_DK, A_DV), 1.0, 0.0).astype(BF16)
    slab_q, slab_k, slab_gf, slab_gb, slab_v = 0, 1, 2, 3, 4
    for i in range(ts):
        src = slice(i * GLA_SUB, (i + 1) * GLA_SUB)
        dst = slice(i * GLA_PITCH, i * GLA_PITCH + GLA_SUB)
        pit_ref[slab_q, dst, :] = q_ref[0, src, :]
        pit_ref[slab_k, dst, :] = k_ref[0, src, :]
        pit_ref[slab_gf, dst, :] = gf_ref[0, src, :]
        pit_ref[slab_gb, dst, :] = gb_ref[0, src, :]
        for h in range(n_slab):
            pit_ref[slab_v + h, dst, :] = v_ref[0, src, h * LANES:(h + 1) * LANES]
    run = jnp.zeros((ts, A_KW), F32)
    for p in range(GLA_SUB):
        run = run + pit_ref[slab_gf, every(p), :] * LOG2E
        bf_ref[p] = run
        qf_ref[p] = pit_ref[slab_q, every(p), :]
        kf_ref[p] = pit_ref[slab_k, every(p), :]
        vf_ref[p] = jnp.concatenate([pit_ref[slab_v + h, every(p), :] for h in range(n_slab)], axis=-1)
    run = jnp.zeros((ts, A_KW), F32)
    for p in reversed(range(GLA_SUB)):
        run = run + pit_ref[slab_gb, every(p), :] * LOG2E
        cb_ref[p] = run
    for p in range(GLA_SUB):
        q = qf_ref[p]
        prods = []
        for s in range(GLA_SUB):
            qk = q * kf_ref[s]
            if s < p:
                qk = qk * jnp.exp2(bf_ref[p] - bf_ref[s])
            elif s > p:
                qk = qk * jnp.exp2(cb_ref[p] - cb_ref[s])
            prods.append(qk.astype(BF16))
        a = jnp.dot(jnp.concatenate(prods, axis=0), hsum, preferred_element_type=F32)
        acc = a[0:ts] * vf_ref[0]
        for s in range(1, GLA_SUB):
            acc = acc + a[s * ts:(s + 1) * ts] * vf_ref[s]
        for h in range(n_slab):
            os_ref[h, every(p), :] = acc[:, h * LANES:(h + 1) * LANES]
    for i in range(ts):
        src = slice(i * GLA_PITCH, i * GLA_PITCH + GLA_SUB)
        dst = slice(i * GLA_SUB, (i + 1) * GLA_SUB)
        band = jnp.concatenate([os_ref[h, src, :] for h in range(n_slab)], axis=-1)
        o_ref[0, dst, :] = (band + of_ref[0, dst, :].astype(F32) + ob_ref[0, dst, :].astype(F32)).astype(BF16)


def _gla_band(gq, gk, gv, gf, gb, of, ob):
    bsz, n, _ = gq.shape
    rows = min(GLA_BAND_TILE * GLA_SUB, n)
    ts = rows // GLA_SUB
    blk = lambda w: pl.BlockSpec((1, rows, w), lambda b, i: (b, i, 0))
    return pl.pallas_call(
        _gla_band_kernel,
        out_shape=jax.ShapeDtypeStruct((bsz, n, A_W), BF16),
        grid=(bsz, n // rows),
        in_specs=[blk(A_KW), blk(A_KW), blk(A_W), blk(A_KW), blk(A_KW), blk(A_W), blk(A_W)],
        out_specs=blk(A_W),
        scratch_shapes=[
            pltpu.VMEM((GLA_SUB, ts, A_KW), F32), pltpu.VMEM((GLA_SUB, ts, A_KW), F32),
            pltpu.VMEM((GLA_SUB, ts, A_W), F32),
            pltpu.VMEM((GLA_SUB, ts, A_KW), F32), pltpu.VMEM((GLA_SUB, ts, A_KW), F32),
            pltpu.VMEM((4 + A_W // LANES, ts * GLA_PITCH, LANES), F32),
            pltpu.VMEM((A_W // LANES, ts * GLA_PITCH, LANES), F32),
        ],
        compiler_params=_params("parallel", "parallel"), name="gla_band",
    )(gq, gk, gv, gf, gb, of, ob)


def _split3(x):
    hi = x.astype(BF16)
    rest = x - hi.astype(F32)
    mid = rest.astype(BF16)
    lo = (rest - mid.astype(F32)).astype(BF16)
    return hi, mid, lo


def _gla_prep(q_ref, k_ref, g_ref, qt_ref, kh_ref, dect_ref, tot_ref, pad_ref, reverse):
    seg = q_ref.shape[1]
    n_sub = seg // GLA_SUB
    r_i = lax.broadcasted_iota(jnp.int32, (GLA_GRP, GLA_GRP), 0)
    c_i = lax.broadcasted_iota(jnp.int32, (GLA_GRP, GLA_GRP), 1)
    same = (r_i // GLA_SUB) == (c_i // GLA_SUB)
    tri = (c_i >= r_i) if reverse else (c_i <= r_i)
    sums_mat = jnp.concatenate([jnp.where(same & tri, 1.0, 0.0), jnp.where(same, 1.0, 0.0)], axis=0).astype(BF16)
    for r in range(seg // GLA_GRP):
        rows = slice(r * GLA_GRP, (r + 1) * GLA_GRP)
        g3 = jnp.concatenate(_split3(g_ref[0, rows, :]), axis=-1)
        sums = jnp.dot(sums_mat, g3, preferred_element_type=F32)
        sums = sums[:, 0:A_KW] + sums[:, A_KW:2 * A_KW] + sums[:, 2 * A_KW:]
        cum, tot = sums[:GLA_GRP], sums[GLA_GRP:]
        qt_ref[rows, :] = (q_ref[0, rows, :] * jnp.exp(cum)).astype(BF16)
        kh_ref[rows, :] = (k_ref[0, rows, :] * jnp.exp(tot - cum)).astype(BF16)
        tot_ref[rows, :] = tot
    pad_ref[...] = jnp.zeros_like(pad_ref)
    pad_ref[0:n_sub, :] = tot_ref[pl.ds(0, n_sub, stride=GLA_SUB), :]
    dect_ref[...] = jnp.exp(pad_ref[...].T)


def _gla_rec_kernel(qf_ref, kf_ref, vf_ref, gf_ref, qb_ref, kb_ref, vb_ref, gb_ref, of_ref, ob_ref,
                    sf_ref, sb_ref, qtf_ref, khf_ref, dtf_ref, qtb_ref, khb_ref, dtb_ref, tot_ref, pad_ref):
    @pl.when(pl.program_id(1) == 0)
    def _():
        sf_ref[...] = jnp.zeros_like(sf_ref)
        sb_ref[...] = jnp.zeros_like(sb_ref)

    seg = qf_ref.shape[1]
    n_sub = seg // GLA_SUB
    _gla_prep(qf_ref, kf_ref, gf_ref, qtf_ref, khf_ref, dtf_ref, tot_ref, pad_ref, reverse=False)
    _gla_prep(qb_ref, kb_ref, gb_ref, qtb_ref, khb_ref, dtb_ref, tot_ref, pad_ref, reverse=True)
    smask = _head_match((A_KW, A_W), A_DK, A_DV)

    def update(i, kh_ref, v_ref):
        rows = slice(i * GLA_SUB, (i + 1) * GLA_SUB)
        upd = lax.dot_general(kh_ref[rows, :], v_ref[0, rows, :].astype(BF16), (((0,), (0,)), ((), ())),
                              preferred_element_type=F32)
        return jnp.where(smask, upd, 0.0)

    def step(i, s, upd, qt_ref, dect_ref, o_ref):
        rows = slice(i * GLA_SUB, (i + 1) * GLA_SUB)
        o_ref[0, rows, :] = jnp.dot(qt_ref[rows, :], s.astype(BF16), preferred_element_type=F32).astype(BF16)
        return s * dect_ref[:, i:i + 1] + upd

    order_f = list(range(n_sub))
    order_b = list(reversed(range(n_sub)))
    upd_f = [update(i, khf_ref, vf_ref) for i in order_f[:GLA_LOOKAHEAD]]
    upd_b = [update(i, khb_ref, vb_ref) for i in order_b[:GLA_LOOKAHEAD]]
    s_f = sf_ref[...]
    s_b = sb_ref[...]
    for j in range(n_sub):
        if j + GLA_LOOKAHEAD < n_sub:
            upd_f.append(update(order_f[j + GLA_LOOKAHEAD], khf_ref, vf_ref))
            upd_b.append(update(order_b[j + GLA_LOOKAHEAD], khb_ref, vb_ref))
        s_f = step(order_f[j], s_f, upd_f[j], qtf_ref, dtf_ref, of_ref)
        s_b = step(order_b[j], s_b, upd_b[j], qtb_ref, dtb_ref, ob_ref)
    sf_ref[...] = s_f
    sb_ref[...] = s_b


def _gla_rec(gq, gk, gv, gf, gb):
    bsz, n, _ = gq.shape
    seg = min(GLA_SEG, n)
    nseg = n // seg
    assert seg // GLA_SUB <= LANES
    fwd = lambda w: pl.BlockSpec((1, seg, w), lambda b, s: (b, s, 0))
    bwd = lambda w: pl.BlockSpec((1, seg, w), lambda b, s: (b, nseg - 1 - s, 0))
    return pl.pallas_call(
        _gla_rec_kernel,
        out_shape=(jax.ShapeDtypeStruct((bsz, n, A_W), BF16), jax.ShapeDtypeStruct((bsz, n, A_W), BF16)),
        grid=(bsz, nseg),
        in_specs=[fwd(A_KW), fwd(A_KW), fwd(A_W), fwd(A_KW), bwd(A_KW), bwd(A_KW), bwd(A_W), bwd(A_KW)],
        out_specs=(fwd(A_W), bwd(A_W)),
        scratch_shapes=[
            pltpu.VMEM((A_KW, A_W), F32), pltpu.VMEM((A_KW, A_W), F32),
            pltpu.VMEM((seg, A_KW), BF16), pltpu.VMEM((seg, A_KW), BF16), pltpu.VMEM((A_KW, LANES), F32),
            pltpu.VMEM((seg, A_KW), BF16), pltpu.VMEM((seg, A_KW), BF16), pltpu.VMEM((A_KW, LANES), F32),
            pltpu.VMEM((seg, A_KW), F32), pltpu.VMEM((LANES, A_KW), F32),
        ],
        compiler_params=_params("parallel", "arbitrary"), name="gla_rec",
    )(gq, gk, gv, gf, gq, gk, gv, gb)


def _fft1_kernel(u_ref, a1_ref, cos_ref, sin_ref, t_ref):
    n1 = u_ref.shape[1]
    nb = cos_ref.shape[0]
    t = jnp.dot(a1_ref[...], u_ref[0], preferred_element_type=F32)
    for j in range(nb):
        cols = slice(j * B_W, (j + 1) * B_W)
        tr = t[:n1, cols]
        ti = t[n1:, cols]
        c = jnp.concatenate([cos_ref[j]] * (B_W // LANES), axis=-1)
        s = jnp.concatenate([sin_ref[j]] * (B_W // LANES), axis=-1)
        t_ref[0, 0, :, cols] = tr * c + ti * s
        t_ref[0, 1, :, cols] = ti * c - tr * s


def _fft2_kernel(t_ref, l_ref, cc_ref, cs_ref, w_ref, o_ref):
    n2 = t_ref.shape[3] // B_W
    half = FFT_K1 * n2
    groups = range(t_ref.shape[2] // FFT_K1)
    xs = []
    for grp in groups:
        k1s = slice(grp * FFT_K1, (grp + 1) * FFT_K1)
        xs.append(jnp.concatenate([t_ref[0, part, k1s, j * B_W:(j + 1) * B_W]
                                   for part in range(2) for j in range(n2)], axis=0).astype(BF16))
    gs = [jnp.dot(l_ref[...], x, preferred_element_type=F32) for x in xs]
    mixed = [jnp.dot(g[:half].astype(BF16), cc_ref[...], preferred_element_type=F32)
             + jnp.dot(g[half:].astype(BF16), cs_ref[...], preferred_element_type=F32) for g in gs]
    ys = [jnp.dot(m.astype(BF16), w_ref[0], preferred_element_type=F32) for m in mixed]
    o_ref[0] = jnp.concatenate([y.reshape(n2, FFT_K1, B_W) for y in ys], axis=1).astype(BF16)


def _fnet(fu, l, pp, tabs):
    bsz, n, _ = fu.shape
    n2 = FFT_N2
    n1 = n // n2
    nb = FFT_NB
    k1b = FFT_K1 * FFT_K1_GROUPS
    u2 = fu.reshape(bsz, n1, n2 * B_W)
    tt = pl.pallas_call(
        _fft1_kernel,
        out_shape=jax.ShapeDtypeStruct((bsz, 2, n1, n2 * B_W), F32),
        grid=(n2 // nb, bsz),
        in_specs=[
            pl.BlockSpec((1, n1, nb * B_W), lambda j, b: (b, 0, j)),
            _full((2 * n1, n1)),
            pl.BlockSpec((nb, n1, LANES), lambda j, b: (j, 0, 0)),
            pl.BlockSpec((nb, n1, LANES), lambda j, b: (j, 0, 0)),
        ],
        out_specs=pl.BlockSpec((1, 2, n1, nb * B_W), lambda j, b: (b, 0, 0, j)),
        compiler_params=_params("parallel", "parallel"), name="fft_pass1",
    )(u2, tabs["a1"], tabs["tw_cos"], tabs["tw_sin"])
    y = pl.pallas_call(
        _fft2_kernel,
        out_shape=jax.ShapeDtypeStruct((bsz, n2, n1, B_W), BF16),
        grid=(bsz, n1 // k1b),
        in_specs=[
            pl.BlockSpec((1, 2, k1b, n2 * B_W), lambda b, i: (b, 0, i, 0)),
            _full((2 * FFT_K1 * n2, 2 * FFT_K1 * n2)),
            _full((B_W, B_W)), _full((B_W, B_W)), _of_layer(l, (B_W, B_W)),
        ],
        out_specs=pl.BlockSpec((1, n2, k1b, B_W), lambda b, i: (b, 0, i, 0)),
        compiler_params=_params("parallel", "parallel"), name="fft_pass2",
    )(tt, tabs["l2"], tabs["chan_cos"], tabs["chan_sin"], pp["fnet_w"])
    return y.reshape(bsz, n, B_W)


def _attn_kernel(q_ref, kt_ref, va_ref, o_ref, qm_ref, m_ref, acc_ref):
    tq = q_ref.shape[1]
    n = kt_ref.shape[3]
    tk = min(ATT_TK, n)
    q = q_ref[0]
    lane = lax.broadcasted_iota(jnp.int32, q.shape, 1)
    for h in range(C_GRP):
        qm_ref[h * tq:(h + 1) * tq, :] = jnp.where(lane // C_HD == h, q, jnp.zeros_like(q))

    def scores(c):
        c0 = pl.multiple_of(c * tk, tk)
        return jnp.dot(qm_ref[...], kt_ref[0, 0, :, pl.ds(c0, tk)], preferred_element_type=F32)

    def softmax_accumulate(c, s, first):
        c0 = pl.multiple_of(c * tk, tk)
        v = va_ref[0, 0, pl.ds(c0, tk), :]
        s_max = s[:, 0:LANES]
        for t in range(1, tk // LANES):
            s_max = jnp.maximum(s_max, s[:, t * LANES:(t + 1) * LANES])
        row_max = jnp.max(s_max, axis=-1, keepdims=True)
        if first:
            m_new = jnp.broadcast_to(row_max, m_ref.shape)
        else:
            m_old = m_ref[...]
            m_new = jnp.maximum(m_old, row_max)
        p = jnp.exp2(s - jnp.concatenate([m_new] * (tk // LANES), axis=-1))
        pv = jnp.dot(p.astype(BF16), v, preferred_element_type=F32)
        acc_ref[...] = pv if first else jnp.exp2(m_old - m_new) * acc_ref[...] + pv
        m_ref[...] = m_new

    def step(c, carry):
        softmax_accumulate(c, scores(c), first=False)
        return carry

    softmax_accumulate(0, scores(0), first=True)
    lax.fori_loop(1, n // tk, step, 0, unroll=ATT_UNROLL)
    acc = acc_ref[...]
    inv = 1.0 / acc[:, ONES_LANE:ONES_LANE + 1]
    outs = [(acc[h * tq:(h + 1) * tq, 0:C_HD] * inv[h * tq:(h + 1) * tq]) for h in range(C_GRP)]
    o_ref[0] = jnp.concatenate(outs, axis=-1).astype(BF16)


def _attention(aq, kt4, va):
    bsz, n, _ = aq.shape
    tq = min(ATT_TQ, n)
    return pl.pallas_call(
        _attn_kernel,
        out_shape=jax.ShapeDtypeStruct((bsz, n, C_W), BF16),
        grid=(bsz, C_KV, n // tq),
        in_specs=[
            pl.BlockSpec((1, tq, C_GRP * C_HD), lambda b, g, i: (b, i, g)),
            pl.BlockSpec((1, 1, C_GRP * C_HD, n), lambda b, g, i: (b, g, 0, 0)),
            pl.BlockSpec((1, 1, n, LANES), lambda b, g, i: (b, g, 0, 0)),
        ],
        out_specs=pl.BlockSpec((1, tq, C_GRP * C_HD), lambda b, g, i: (b, i, g)),
        scratch_shapes=[
            pltpu.VMEM((C_GRP * tq, C_GRP * C_HD), BF16),
            pltpu.VMEM((C_GRP * tq, LANES), F32),
            pltpu.VMEM((C_GRP * tq, LANES), F32),
        ],
        compiler_params=_params("parallel", "parallel", "arbitrary"), name="attention",
    )(aq, kt4, va)


def _outproj_kernel(x_ref, mod_ref, oa_ref, yb_ref, ao_ref, du_ref, dvn_ref, sz_ref,
                    w_ref, postg_ref, og_ref, bd_ref, sw_ref, sb_ref, o_ref):
    tm = x_ref.shape[1]
    sz = sz_ref[0].astype(F32)
    off_b, off_c, off_d = A_W, A_W + B_W, A_W + B_W + C_W

    sgu = [jnp.dot(sw_ref[0], dvn_ref[0, c * SGU_CHUNK:(c + 1) * SGU_CHUNK, :], preferred_element_type=F32)
           for c in range(tm // SGU_CHUNK)]
    m_c = (ao_ref[0].astype(F32) * sz[:, off_c:off_d]).astype(BF16)
    y = jnp.dot(m_c, w_ref[0, off_c:off_d, :], preferred_element_type=F32)
    m_b = (yb_ref[0].astype(F32) * sz[:, off_b:off_c]).astype(BF16)
    y = y + jnp.dot(m_b, w_ref[0, off_b:off_c, :], preferred_element_type=F32)

    o = oa_ref[0].astype(F32)
    oms = jnp.dot((o * o).astype(BF16), bd_ref[...], preferred_element_type=F32)
    out_a = o * lax.rsqrt(oms + EPS) * og_ref[0]
    y = y + jnp.dot((out_a * sz[:, 0:off_b]).astype(BF16), w_ref[0, 0:off_b, :], preferred_element_type=F32)

    lane = lax.broadcasted_iota(jnp.int32, (SGU_CHUNK, D_W), 1)
    parts = []
    for c, r in enumerate(sgu):
        rows = slice(c * SGU_CHUNK, (c + 1) * SGU_CHUNK)
        mix = sb_ref[0]
        for g in range(D_GROUPS):
            mix = mix + jnp.where(lane // D_GC == g, r[g * SGU_CHUNK:(g + 1) * SGU_CHUNK, :], 0.0)
        parts.append(du_ref[0, rows, :].astype(F32) * mix)
    out_d = jnp.concatenate(parts, axis=0)
    m_d = (out_d * sz[:, off_d:]).astype(BF16)
    y = y + jnp.dot(m_d, w_ref[0, off_d:, :], preferred_element_type=F32)

    yms = jnp.mean(y * y, axis=-1, keepdims=True)
    yn = y * lax.rsqrt(yms + EPS) * postg_ref[0]
    gate = mod_ref[0, 0, :, 2 * D_MODEL:3 * D_MODEL]
    o_ref[0] = x_ref[0] + gate * yn


def _outproj(x, mod, row0, l, oa, yb, ao, du, dvn, sz, pp):
    bsz, n, _ = x.shape
    tm = min(TM_OUT, n)
    tok = lambda w: pl.BlockSpec((1, tm, w), lambda b, i: (b, i, 0))
    return pl.pallas_call(
        _outproj_kernel,
        out_shape=jax.ShapeDtypeStruct((bsz, n, D_MODEL), F32),
        grid=(bsz, n // tm),
        in_specs=[
            tok(D_MODEL),
            pl.BlockSpec((1, 1, 1, 3 * D_MODEL), lambda b, i: (l, row0 + b, 0, 0)),
            tok(A_W), tok(B_W), tok(C_W), tok(D_W), tok(D_W), tok(D_MIX),
            _of_layer(l, (D_MIX, D_MODEL)), _of_layer(l, (1, D_MODEL)), _of_layer(l, (1, A_W)), _full((A_W, A_W)),
            _of_layer(l, (D_GROUPS * SGU_CHUNK, SGU_CHUNK)), _of_layer(l, (SGU_CHUNK, D_W)),
        ],
        out_specs=tok(D_MODEL),
        compiler_params=_params("parallel", "parallel"), name="outproj",
    )(x, mod, oa, yb, ao, du, dvn, sz,
      pp["w_out"], pp["post_g"], pp["og"], pp["bd_a"], pp["sgu_w"], pp["sgu_b"])


def _pack_w_in(w):
    sizes = [A_KW, A_KW, A_W, 2 * GLA_RANK, B_W, C_W, C_KVW, C_KVW, D_W, D_W, D_MIX]
    off = [0] + [int(o) for o in np.cumsum(sizes)]
    a_q, a_k, a_v, a_lr, b_u, c_q, c_k, c_v, d_u, d_v, z = [w[:, off[i]:off[i + 1]] for i in range(len(sizes))]

    def deinterleave(x, heads):
        r = x.reshape(D_MODEL, heads, C_HD // 2, 2)
        return jnp.concatenate([r[..., 0], r[..., 1]], axis=-1).reshape(D_MODEL, heads * C_HD)

    zeros = lambda width: jnp.zeros((D_MODEL, width), w.dtype)
    pieces = [a_q, a_k, a_v, b_u, deinterleave(c_q, C_HEADS), deinterleave(c_k, C_KV),
              a_lr, zeros(LANES - 2 * GLA_RANK)]
    for h in range(C_KV):
        pieces += [c_v[:, h * C_HD:(h + 1) * C_HD], zeros(LANES - C_HD)]
    pieces += [d_u, d_v, z]
    packed = jnp.concatenate(pieces, axis=1)
    assert packed.shape[1] == W_PACKED
    return packed.astype(BF16)


def _block_diag_mean(width, group):
    idx = np.arange(width) // group
    return jnp.asarray((idx[:, None] == idx[None, :]).astype(np.float32) / group, BF16)


def _stacked_params(norm_pre_g, norm_post_g, w_in, gla_wg2_f, gla_bg_f, gla_wg2_b, gla_bg_b,
                    gla_onorm_g, fnet_w, q_norm_g, k_norm_g, sgu_norm_g, sgu_w, sgu_b, w_out):
    depth = w_in.shape[0]
    deint = np.concatenate([np.arange(0, C_HD, 2), np.arange(1, C_HD, 2)])
    wg2 = jnp.zeros((depth, LANES, 2 * A_KW), F32)
    wg2 = wg2.at[:, 0:GLA_RANK, 0:A_KW].set(gla_wg2_f)
    wg2 = wg2.at[:, GLA_RANK:2 * GLA_RANK, A_KW:].set(gla_wg2_b)
    vone = np.zeros((1, 2 * LANES), np.float32)
    vone[0, ONES_LANE] = 1.0
    vone[0, LANES + ONES_LANE] = 1.0
    return {
        "pre_g": norm_pre_g.reshape(depth, 1, D_MODEL),
        "post_g": norm_post_g.reshape(depth, 1, D_MODEL),
        "w_in": jax.vmap(_pack_w_in)(w_in),
        "wg2": wg2.astype(BF16),
        "bg": jnp.concatenate([gla_bg_f, gla_bg_b], axis=-1).reshape(depth, 1, 2 * A_KW),
        "qg": jnp.tile(q_norm_g[:, deint], (1, C_HEADS)).reshape(depth, 1, C_W),
        "kg": jnp.tile(k_norm_g[:, deint], (1, C_KV)).reshape(depth, 1, C_KVW),
        "sg": sgu_norm_g.reshape(depth, 1, D_W),
        "bd_q": _block_diag_mean(C_W, C_HD),
        "bd_a": _block_diag_mean(A_W, A_DV),
        "vone": jnp.asarray(vone),
        "og": jnp.tile(gla_onorm_g, (1, A_HEADS)).reshape(depth, 1, A_W),
        "fnet_w": fnet_w.astype(BF16),
        "sgu_w": sgu_w.reshape(depth, D_GROUPS * SGU_CHUNK, SGU_CHUNK).astype(BF16),
        "sgu_b": jnp.repeat(jnp.swapaxes(sgu_b, 1, 2), D_GC, axis=2),
        "w_out": w_out.astype(BF16),
    }


def _seq_tables(n):
    rows = n // GRID_W
    row = jnp.repeat(jnp.arange(rows, dtype=F32), GRID_W)
    col = jnp.tile(jnp.arange(GRID_W, dtype=F32), rows)
    rope_axis = C_HD // 2
    freqs = ROPE_THETA ** (-jnp.arange(0, rope_axis, 2, dtype=F32) / rope_axis)
    ang = jnp.concatenate([row[:, None] * freqs, col[:, None] * freqs], axis=-1)
    cos, sin = jnp.cos(ang), jnp.sin(ang)
    cos_t = jnp.tile(jnp.concatenate([cos, cos], axis=-1), (1, LANES // C_HD))
    sin_t = jnp.tile(jnp.concatenate([-sin, sin], axis=-1), (1, LANES // C_HD))

    n2 = FFT_N2
    n1 = n // n2

    def dft_angles(rows, cols, period):
        prod = (jnp.arange(rows, dtype=jnp.int32)[:, None] * jnp.arange(cols, dtype=jnp.int32)[None, :]) % period
        return prod.astype(F32) * (2.0 * math.pi / period)

    a1_ang = dft_angles(n1, n1, n1)
    a1 = jnp.concatenate([jnp.cos(a1_ang), -jnp.sin(a1_ang)], axis=0)
    tw_ang = dft_angles(n2, n1, n)
    scale = 1.0 / math.sqrt(n * B_GC)
    tw_cos = jnp.broadcast_to((jnp.cos(tw_ang) * scale)[:, :, None], (n2, n1, LANES))
    tw_sin = jnp.broadcast_to((jnp.sin(tw_ang) * scale)[:, :, None], (n2, n1, LANES))
    ang2 = dft_angles(n2, n2, n2)
    c2, s2 = jnp.cos(ang2), jnp.sin(ang2)
    eye = jnp.eye(FFT_K1, dtype=F32)
    blk = lambda m: (m[:, None, :, None] * eye[None, :, None, :]).reshape(n2 * FFT_K1, n2 * FFT_K1)
    l2 = jnp.block([[blk(c2), blk(s2)], [blk(-s2), blk(c2)]])
    angc = dft_angles(B_GC, B_GC, B_GC)
    grp = jnp.eye(B_GROUPS, dtype=F32)
    kron = lambda m: (grp[:, None, :, None] * m[None, :, None, :]).reshape(B_W, B_W)
    chan_cos = kron(jnp.cos(angc))
    chan_sin = kron(jnp.sin(angc))
    return {
        "cos": cos_t, "sin": sin_t,
        "a1": a1.astype(BF16),
        "tw_cos": tw_cos, "tw_sin": tw_sin,
        "l2": l2.astype(BF16),
        "chan_cos": chan_cos.astype(BF16), "chan_sin": chan_sin.astype(BF16),
    }


def _layer(x, mod, row0, l, pp, tabs):
    gq, gk, gv, gf, gb, fu, aq, kt4, va, du, dvn, sz = _inproj(x, mod, row0, l, pp, tabs)
    of, ob = _gla_rec(gq, gk, gv, gf, gb)
    oa = _gla_band(gq, gk, gv, gf, gb, of, ob)
    yb = _fnet(fu, l, pp, tabs)
    ao = _attention(aq, kt4, va)
    return _outproj(x, mod, row0, l, oa, yb, ao, du, dvn, sz, pp)


def kernel(x_prompt, x_sample, c_prompt, c_sample, ada_w, ada_b, norm_pre_g, norm_post_g, w_in,
           gla_wg2_f, gla_bg_f, gla_wg2_b, gla_bg_b, gla_onorm_g, fnet_w, q_norm_g, k_norm_g,
           sgu_norm_g, sgu_w, sgu_b, w_out):
    bp, bs = x_prompt.shape[0], x_sample.shape[0]
    pad_rows = (-(bp + bs)) % 8
    c_all = jnp.concatenate([c_prompt, c_sample, jnp.zeros((pad_rows, D_MODEL), F32)], axis=0)
    mod = _adaln(c_all, ada_w, ada_b)
    mod = mod.reshape(DEPTH, mod.shape[1], 1, 3 * D_MODEL)
    pp = _stacked_params(norm_pre_g, norm_post_g, w_in, gla_wg2_f, gla_bg_f, gla_wg2_b, gla_bg_b,
                         gla_onorm_g, fnet_w, q_norm_g, k_norm_g, sgu_norm_g, sgu_w, sgu_b, w_out)
    tabs_p = _seq_tables(x_prompt.shape[1])
    tabs_s = _seq_tables(x_sample.shape[1])
    y_prompt, y_sample = x_prompt, x_sample
    for l in range(DEPTH):
        y_prompt = _layer(y_prompt, mod, 0, l, pp, tabs_p)
        y_sample = _layer(y_sample, mod, bp, l, pp, tabs_s)
    return (y_prompt, y_sample)
```

```python
import math

import numpy as np
import jax
import jax.numpy as jnp
from jax import lax
from jax.experimental import pallas as pl
from jax.experimental.pallas import tpu as pltpu

F32 = jnp.float32
BF16 = jnp.bfloat16
HIGHEST = lax.Precision.HIGHEST

D_MODEL = 1024
DEPTH = 2
GRID_W = 64
A_HEADS, A_DK, A_DV = 4, 32, 64
A_W, A_KW = A_HEADS * A_DV, A_HEADS * A_DK
GLA_RANK = 16
GLA_GATE_NORM = 16.0
B_GROUPS, B_GC = 4, 64
B_W = B_GROUPS * B_GC
C_HEADS, C_KV, C_HD = 8, 2, 64
C_GRP = C_HEADS // C_KV
C_W, C_KVW = C_HEADS * C_HD, C_KV * C_HD
ROPE_THETA = 10000.0
D_GROUPS, D_GC = 4, 64
D_W = D_GROUPS * D_GC
SGU_CHUNK = 128
D_MIX = A_W + B_W + C_W + D_W
EPS = 1e-6
LOG2E = 1.4426950408889634

LANES = 128
VMEM_LIMIT_BYTES = 56 * 1024 * 1024

GQ, GK, GV, FU, AQ, AK, LR, AV, DU, DV, ZZ = 0, 128, 256, 512, 768, 1280, 1408, 1536, 1792, 2048, 2304
W_PACKED = 3584
AV_W = 2 * LANES
ONES_LANE = C_HD

TM_IN = 1024
IN_ROW_PARTS = 4
TM_OUT = 1024
OUT_X_AHEAD = 2
OUT_X_SLOTS = OUT_X_AHEAD + 1
GLA_SUB = 16
GLA_BAND_TILE = 64
GLA_PITCH = GLA_SUB + 4
GLA_GRP = 128
GLA_SEG = 1024
GLA_LOOKAHEAD = 3
FFT_N2 = 64
FFT_NB = 16
FFT_K1 = 8
FFT_K1_GROUPS = 2
ATT_TQ = 512
ATT_TK = 2048
ATT_UNROLL = 2


def _params(*sem):
    return pltpu.CompilerParams(dimension_semantics=sem, vmem_limit_bytes=VMEM_LIMIT_BYTES)


def _full(shape):
    n = len(shape)
    return pl.BlockSpec(shape, lambda *_: (0,) * n)


def _of_layer(l, shape):
    n = len(shape)
    return pl.BlockSpec((1,) + tuple(shape), lambda *_: (l,) + (0,) * n)


def _adaln_kernel(c_ref, w_ref, b_ref, o_ref):
    c = c_ref[...]
    sc = c / (1.0 + jnp.exp(-c))
    o_ref[0] = jnp.dot(sc, w_ref[0], precision=HIGHEST, preferred_element_type=F32) + b_ref[0]


def _adaln(c_all, ada_w, ada_b):
    rows = c_all.shape[0]
    ncol = 3 * D_MODEL // D_MODEL
    return pl.pallas_call(
        _adaln_kernel,
        out_shape=jax.ShapeDtypeStruct((DEPTH, rows, 3 * D_MODEL), F32),
        grid=(DEPTH, ncol),
        in_specs=[
            pl.BlockSpec((rows, D_MODEL), lambda l, j: (0, 0)),
            pl.BlockSpec((1, D_MODEL, D_MODEL), lambda l, j: (l, 0, j)),
            pl.BlockSpec((1, 1, D_MODEL), lambda l, j: (l, 0, j)),
        ],
        out_specs=pl.BlockSpec((1, rows, D_MODEL), lambda l, j: (l, 0, j)),
        compiler_params=_params("arbitrary", "arbitrary"),
        name="adaln",
    )(c_all, ada_w, ada_b.reshape(DEPTH, 1, 3 * D_MODEL))


def _rope(x, cos, sin_signed):
    width = x.shape[-1]
    lane = lax.broadcasted_iota(jnp.int32, x.shape, 1)
    first_half = (lane % C_HD) < (C_HD // 2)
    swapped = jnp.where(first_half, pltpu.roll(x, width - C_HD // 2, 1), pltpu.roll(x, C_HD // 2, 1))
    reps = width // LANES
    cos_w = jnp.concatenate([cos] * reps, axis=-1) if reps > 1 else cos
    sin_w = jnp.concatenate([sin_signed] * reps, axis=-1) if reps > 1 else sin_signed
    return x * cos_w + swapped * sin_w


def _inproj_kernel(x_ref, mod_ref, preg_ref, w_ref, wg2_ref, bg_ref, qg_ref, kg_ref, sg_ref,
                   cos_ref, sin_ref, bd_ref, vone_ref,
                   gq_ref, gk_ref, gv_ref, gf_ref, gb_ref, fu_ref, aq_ref, kt_ref, va_ref,
                   du_ref, dvn_ref, sz_ref):
    tm = x_ref.shape[1]
    shift = mod_ref[0, 0, :, 0:D_MODEL]
    gain = preg_ref[0] * (1.0 + mod_ref[0, 0, :, D_MODEL:2 * D_MODEL])
    for part in range(IN_ROW_PARTS):
        rows = slice(part * tm // IN_ROW_PARTS, (part + 1) * tm // IN_ROW_PARTS)
        x = x_ref[0, rows, :]
        ms = jnp.mean(x * x, axis=-1, keepdims=True)
        hb = ((x * lax.rsqrt(ms + EPS)) * gain + shift).astype(BF16)

        def seg(off, width):
            return jnp.dot(hb, w_ref[0, :, off:off + width], preferred_element_type=F32)

        klr = seg(AK, C_KVW + LANES)
        q = seg(AQ, C_W)
        z = seg(ZZ, D_MIX)
        gqk = seg(GQ, 2 * A_KW)
        gv = seg(GV, A_W)
        fu = seg(FU, B_W)
        va = seg(AV, AV_W)
        du = seg(DU, D_W)
        dv = seg(DV, D_W)
        k = klr[:, :C_KVW]
        lr = klr[:, C_KVW:].astype(BF16)
        logits = jnp.dot(lr, wg2_ref[0], preferred_element_type=F32) + bg_ref[0]
        qms = jnp.dot((q * q).astype(BF16), bd_ref[...], preferred_element_type=F32)
        kms = jnp.dot((k * k).astype(BF16), bd_ref[0:C_KVW, 0:C_KVW], preferred_element_type=F32)

        gq_ref[0, rows, :] = gqk[:, :A_KW] * (A_DK ** -0.5)
        gk_ref[0, rows, :] = gqk[:, A_KW:]
        gv_ref[0, rows, :] = gv
        logg = (jnp.minimum(logits, 0.0) - jnp.log1p(jnp.exp(-jnp.abs(logits)))) * (1.0 / GLA_GATE_NORM)
        gf_ref[0, rows, :] = logg[:, :A_KW]
        gb_ref[0, rows, :] = logg[:, A_KW:]

        fu_ref[0, rows, :] = fu.astype(BF16)

        cos = cos_ref[rows, :]
        sin = sin_ref[rows, :]
        qn = q * lax.rsqrt(qms + EPS) * qg_ref[0]
        aq_ref[0, rows, :] = (_rope(qn, cos, sin) * (C_HD ** -0.5 * LOG2E)).astype(BF16)
        kn = k * lax.rsqrt(kms + EPS) * kg_ref[0]
        kt = _rope(kn, cos, sin).T.astype(BF16)
        for g in range(C_KV):
            for r in range(C_GRP):
                kt_ref[0, g, r * C_HD:(r + 1) * C_HD, rows] = kt[g * C_HD:(g + 1) * C_HD, :]
        vab = (va + vone_ref[...]).astype(BF16)
        va_ref[0, 0, rows, :] = vab[:, :LANES]
        va_ref[0, 1, rows, :] = vab[:, LANES:]

        du_ref[0, rows, :] = du.astype(BF16)
        dms = jnp.mean(dv * dv, axis=-1, keepdims=True)
        dvn_ref[0, rows, :] = (dv * lax.rsqrt(dms + EPS) * sg_ref[0]).astype(BF16)

        sz_ref[0, rows, :] = (z / (1.0 + jnp.exp(-z))).astype(BF16)


def _inproj(x, mod, row0, l, pp, tabs):
    bsz, n, _ = x.shape
    tm = min(TM_IN, n)
    grid = (bsz, n // tm)
    tok = lambda w: pl.BlockSpec((1, tm, w), lambda b, i: (b, i, 0))
    out_shapes = (
        jax.ShapeDtypeStruct((bsz, n, A_KW), F32),
        jax.ShapeDtypeStruct((bsz, n, A_KW), F32),
        jax.ShapeDtypeStruct((bsz, n, A_W), F32),
        jax.ShapeDtypeStruct((bsz, n, A_KW), F32),
        jax.ShapeDtypeStruct((bsz, n, A_KW), F32),
        jax.ShapeDtypeStruct((bsz, n, B_W), BF16),
        jax.ShapeDtypeStruct((bsz, n, C_W), BF16),
        jax.ShapeDtypeStruct((bsz, C_KV, C_GRP * C_HD, n), BF16),
        jax.ShapeDtypeStruct((bsz, C_KV, n, LANES), BF16),
        jax.ShapeDtypeStruct((bsz, n, D_W), BF16),
        jax.ShapeDtypeStruct((bsz, n, D_W), BF16),
        jax.ShapeDtypeStruct((bsz, n, D_MIX), BF16),
    )
    out_specs = (
        tok(A_KW), tok(A_KW), tok(A_W), tok(A_KW), tok(A_KW), tok(B_W), tok(C_W),
        pl.BlockSpec((1, C_KV, C_GRP * C_HD, tm), lambda b, i: (b, 0, 0, i)),
        pl.BlockSpec((1, C_KV, tm, LANES), lambda b, i: (b, 0, i, 0)),
        tok(D_W), tok(D_W), tok(D_MIX),
    )
    in_specs = [
        tok(D_MODEL),
        pl.BlockSpec((1, 1, 1, 3 * D_MODEL), lambda b, i: (l, row0 + b, 0, 0)),
        _of_layer(l, (1, D_MODEL)),
        _of_layer(l, (D_MODEL, W_PACKED)),
        _of_layer(l, (LANES, 2 * A_KW)),
        _of_layer(l, (1, 2 * A_KW)),
        _of_layer(l, (1, C_W)),
        _of_layer(l, (1, C_KVW)),
        _of_layer(l, (1, D_W)),
        pl.BlockSpec((tm, LANES), lambda b, i: (i, 0)),
        pl.BlockSpec((tm, LANES), lambda b, i: (i, 0)),
        _full((C_W, C_W)),
        _full((1, 2 * LANES)),
    ]
    return pl.pallas_call(
        _inproj_kernel, out_shape=out_shapes, grid=grid, in_specs=in_specs, out_specs=out_specs,
        compiler_params=_params("parallel", "parallel"), name="inproj",
    )(x, mod, pp["pre_g"], pp["w_in"], pp["wg2"], pp["bg"], pp["qg"], pp["kg"], pp["sg"],
      tabs["cos"], tabs["sin"], pp["bd_q"], pp["vone"])


def _head_match(shape, rows_per_head, cols_per_head):
    return (lax.broadcasted_iota(jnp.int32, shape, 0) // rows_per_head
            == lax.broadcasted_iota(jnp.int32, shape, 1) // cols_per_head)


def _gla_band_kernel(q_ref, k_ref, v_ref, gf_ref, gb_ref, of_ref, ob_ref, o_ref,
                     qf_ref, kf_ref, vf_ref, bf_ref, cb_ref, pit_ref, os_ref):
    rows = q_ref.shape[1]
    ts = rows // GLA_SUB
    n_slab = A_W // LANES
    every = lambda p: pl.ds(p, ts, stride=GLA_PITCH)
    hsum = jnp.where(_head_match((A_KW, A_W), A_DK, A_DV), 1.0, 0.0).astype(BF16)
    slab_q, slab_k, slab_gf, slab_gb, slab_v = 0, 1, 2, 3, 4
    for i in range(ts):
        src = slice(i * GLA_SUB, (i + 1) * GLA_SUB)
        dst = slice(i * GLA_PITCH, i * GLA_PITCH + GLA_SUB)
        pit_ref[slab_q, dst, :] = q_ref[0, src, :]
        pit_ref[slab_k, dst, :] = k_ref[0, src, :]
        pit_ref[slab_gf, dst, :] = gf_ref[0, src, :]
        pit_ref[slab_gb, dst, :] = gb_ref[0, src, :]
        for h in range(n_slab):
            pit_ref[slab_v + h, dst, :] = v_ref[0, src, h * LANES:(h + 1) * LANES]
    run = jnp.zeros((ts, A_KW), F32)
    for p in range(GLA_SUB):
        run = run + pit_ref[slab_gf, every(p), :] * LOG2E
        bf_ref[p] = run
        qf_ref[p] = pit_ref[slab_q, every(p), :]
        kf_ref[p] = pit_ref[slab_k, every(p), :]
        vf_ref[p] = jnp.concatenate([pit_ref[slab_v + h, every(p), :] for h in range(n_slab)], axis=-1)
    run = jnp.zeros((ts, A_KW), F32)
    for p in reversed(range(GLA_SUB)):
        run = run + pit_ref[slab_gb, every(p), :] * LOG2E
        cb_ref[p] = run
    for p in range(GLA_SUB):
        q = qf_ref[p]
        prods = []
        for s in range(GLA_SUB):
            qk = q * kf_ref[s]
            if s < p:
                qk = qk * jnp.exp2(bf_ref[p] - bf_ref[s])
            elif s > p:
                qk = qk * jnp.exp2(cb_ref[p] - cb_ref[s])
            prods.append(qk.astype(BF16))
        a = jnp.dot(jnp.concatenate(prods, axis=0), hsum, preferred_element_type=F32)
        acc = a[0:ts] * vf_ref[0]
        for s in range(1, GLA_SUB):
            acc = acc + a[s * ts:(s + 1) * ts] * vf_ref[s]
        for h in range(n_slab):
            os_ref[h, every(p), :] = acc[:, h * LANES:(h + 1) * LANES]
    for i in range(ts):
        src = slice(i * GLA_PITCH, i * GLA_PITCH + GLA_SUB)
        dst = slice(i * GLA_SUB, (i + 1) * GLA_SUB)
        band = jnp.concatenate([os_ref[h, src, :] for h in range(n_slab)], axis=-1)
        o_ref[0, dst, :] = (band + of_ref[0, dst, :].astype(F32) + ob_ref[0, dst, :].astype(F32)).astype(BF16)


def _gla_band(gq, gk, gv, gf, gb, of, ob):
    bsz, n, _ = gq.shape
    rows = min(GLA_BAND_TILE * GLA_SUB, n)
    ts = rows // GLA_SUB
    blk = lambda w: pl.BlockSpec((1, rows, w), lambda b, i: (b, i, 0))
    return pl.pallas_call(
        _gla_band_kernel,
        out_shape=jax.ShapeDtypeStruct((bsz, n, A_W), BF16),
        grid=(bsz, n // rows),
        in_specs=[blk(A_KW), blk(A_KW), blk(A_W), blk(A_KW), blk(A_KW), blk(A_W), blk(A_W)],
        out_specs=blk(A_W),
        scratch_shapes=[
            pltpu.VMEM((GLA_SUB, ts, A_KW), F32), pltpu.VMEM((GLA_SUB, ts, A_KW), F32),
            pltpu.VMEM((GLA_SUB, ts, A_W), F32),
            pltpu.VMEM((GLA_SUB, ts, A_KW), F32), pltpu.VMEM((GLA_SUB, ts, A_KW), F32),
            pltpu.VMEM((4 + A_W // LANES, ts * GLA_PITCH, LANES), F32),
            pltpu.VMEM((A_W // LANES, ts * GLA_PITCH, LANES), F32),
        ],
        compiler_params=_params("parallel", "parallel"), name="gla_band",
    )(gq, gk, gv, gf, gb, of, ob)


def _split3(x):
    hi = x.astype(BF16)
    rest = x - hi.astype(F32)
    mid = rest.astype(BF16)
    lo = (rest - mid.astype(F32)).astype(BF16)
    return hi, mid, lo


def _gla_prep(q_ref, k_ref, g_ref, qt_ref, kh_ref, dect_ref, tot_ref, pad_ref, reverse):
    seg = q_ref.shape[1]
    n_sub = seg // GLA_SUB
    r_i = lax.broadcasted_iota(jnp.int32, (GLA_GRP, GLA_GRP), 0)
    c_i = lax.broadcasted_iota(jnp.int32, (GLA_GRP, GLA_GRP), 1)
    same = (r_i // GLA_SUB) == (c_i // GLA_SUB)
    tri = (c_i >= r_i) if reverse else (c_i <= r_i)
    sums_mat = jnp.concatenate([jnp.where(same & tri, 1.0, 0.0), jnp.where(same, 1.0, 0.0)], axis=0).astype(BF16)
    for r in range(seg // GLA_GRP):
        rows = slice(r * GLA_GRP, (r + 1) * GLA_GRP)
        g3 = jnp.concatenate(_split3(g_ref[0, rows, :]), axis=-1)
        sums = jnp.dot(sums_mat, g3, preferred_element_type=F32)
        sums = sums[:, 0:A_KW] + sums[:, A_KW:2 * A_KW] + sums[:, 2 * A_KW:]
        cum, tot = sums[:GLA_GRP], sums[GLA_GRP:]
        qt_ref[rows, :] = (q_ref[0, rows, :] * jnp.exp(cum)).astype(BF16)
        kh_ref[rows, :] = (k_ref[0, rows, :] * jnp.exp(tot - cum)).astype(BF16)
        tot_ref[rows, :] = tot
    pad_ref[...] = jnp.zeros_like(pad_ref)
    pad_ref[0:n_sub, :] = tot_ref[pl.ds(0, n_sub, stride=GLA_SUB), :]
    dect_ref[...] = jnp.exp(pad_ref[...].T)


def _gla_rec_kernel(qf_ref, kf_ref, vf_ref, gf_ref, qb_ref, kb_ref, vb_ref, gb_ref, of_ref, ob_ref,
                    sf_ref, sb_ref, qtf_ref, khf_ref, dtf_ref, qtb_ref, khb_ref, dtb_ref, tot_ref, pad_ref):
    @pl.when(pl.program_id(1) == 0)
    def _():
        sf_ref[...] = jnp.zeros_like(sf_ref)
        sb_ref[...] = jnp.zeros_like(sb_ref)

    seg = qf_ref.shape[1]
    n_sub = seg // GLA_SUB
    _gla_prep(qf_ref, kf_ref, gf_ref, qtf_ref, khf_ref, dtf_ref, tot_ref, pad_ref, reverse=False)
    _gla_prep(qb_ref, kb_ref, gb_ref, qtb_ref, khb_ref, dtb_ref, tot_ref, pad_ref, reverse=True)
    smask = _head_match((A_KW, A_W), A_DK, A_DV)

    def update(i, kh_ref, v_ref):
        rows = slice(i * GLA_SUB, (i + 1) * GLA_SUB)
        upd = lax.dot_general(kh_ref[rows, :], v_ref[0, rows, :].astype(BF16), (((0,), (0,)), ((), ())),
                              preferred_element_type=F32)
        return jnp.where(smask, upd, 0.0)

    def step(i, s, upd, qt_ref, dect_ref, o_ref):
        rows = slice(i * GLA_SUB, (i + 1) * GLA_SUB)
        o_ref[0, rows, :] = jnp.dot(qt_ref[rows, :], s.astype(BF16), preferred_element_type=F32).astype(BF16)
        return s * dect_ref[:, i:i + 1] + upd

    order_f = list(range(n_sub))
    order_b = list(reversed(range(n_sub)))
    upd_f = [update(i, khf_ref, vf_ref) for i in order_f[:GLA_LOOKAHEAD]]
    upd_b = [update(i, khb_ref, vb_ref) for i in order_b[:GLA_LOOKAHEAD]]
    s_f = sf_ref[...]
    s_b = sb_ref[...]
    for j in range(n_sub):
        if j + GLA_LOOKAHEAD < n_sub:
            upd_f.append(update(order_f[j + GLA_LOOKAHEAD], khf_ref, vf_ref))
            upd_b.append(update(order_b[j + GLA_LOOKAHEAD], khb_ref, vb_ref))
        s_f = step(order_f[j], s_f, upd_f[j], qtf_ref, dtf_ref, of_ref)
        s_b = step(order_b[j], s_b, upd_b[j], qtb_ref, dtb_ref, ob_ref)
    sf_ref[...] = s_f
    sb_ref[...] = s_b


def _gla_rec(gq, gk, gv, gf, gb):
    bsz, n, _ = gq.shape
    seg = min(GLA_SEG, n)
    nseg = n // seg
    assert seg // GLA_SUB <= LANES
    fwd = lambda w: pl.BlockSpec((1, seg, w), lambda b, s: (b, s, 0))
    bwd = lambda w: pl.BlockSpec((1, seg, w), lambda b, s: (b, nseg - 1 - s, 0))
    return pl.pallas_call(
        _gla_rec_kernel,
        out_shape=(jax.ShapeDtypeStruct((bsz, n, A_W), BF16), jax.ShapeDtypeStruct((bsz, n, A_W), BF16)),
        grid=(bsz, nseg),
        in_specs=[fwd(A_KW), fwd(A_KW), fwd(A_W), fwd(A_KW), bwd(A_KW), bwd(A_KW), bwd(A_W), bwd(A_KW)],
        out_specs=(fwd(A_W), bwd(A_W)),
        scratch_shapes=[
            pltpu.VMEM((A_KW, A_W), F32), pltpu.VMEM((A_KW, A_W), F32),
            pltpu.VMEM((seg, A_KW), BF16), pltpu.VMEM((seg, A_KW), BF16), pltpu.VMEM((A_KW, LANES), F32),
            pltpu.VMEM((seg, A_KW), BF16), pltpu.VMEM((seg, A_KW), BF16), pltpu.VMEM((A_KW, LANES), F32),
            pltpu.VMEM((seg, A_KW), F32), pltpu.VMEM((LANES, A_KW), F32),
        ],
        compiler_params=_params("parallel", "arbitrary"), name="gla_rec",
    )(gq, gk, gv, gf, gq, gk, gv, gb)


def _fft1_kernel(u_ref, a1_ref, cos_ref, sin_ref, t_ref):
    n1 = u_ref.shape[1]
    nb = cos_ref.shape[0]
    t = jnp.dot(a1_ref[...], u_ref[0], preferred_element_type=F32)
    for j in range(nb):
        cols = slice(j * B_W, (j + 1) * B_W)
        tr = t[:n1, cols]
        ti = t[n1:, cols]
        c = jnp.concatenate([cos_ref[j]] * (B_W // LANES), axis=-1)
        s = jnp.concatenate([sin_ref[j]] * (B_W // LANES), axis=-1)
        t_ref[0, 0, :, cols] = (tr * c + ti * s).astype(BF16)
        t_ref[0, 1, :, cols] = (ti * c - tr * s).astype(BF16)


def _fft2_kernel(t_ref, l_ref, cc_ref, cs_ref, w_ref, o_ref):
    n2 = t_ref.shape[3]
    half = FFT_K1 * n2
    groups = range(t_ref.shape[2] // FFT_K1)
    xs = []
    for grp in groups:
        k1s = slice(grp * FFT_K1, (grp + 1) * FFT_K1)
        xs.append(jnp.concatenate([t_ref[0, 0, k1s].reshape(half, B_W), t_ref[0, 1, k1s].reshape(half, B_W)],
                                  axis=0))
    gs = [jnp.dot(l_ref[...], x, preferred_element_type=F32) for x in xs]
    mixed = [jnp.dot(g[:half].astype(BF16), cc_ref[...], preferred_element_type=F32)
             + jnp.dot(g[half:].astype(BF16), cs_ref[...], preferred_element_type=F32) for g in gs]
    ys = [jnp.dot(m.astype(BF16), w_ref[0], preferred_element_type=F32) for m in mixed]
    o_ref[0] = jnp.concatenate([y.reshape(n2, FFT_K1, B_W) for y in ys], axis=1).astype(BF16)


def _fnet(fu, l, pp, tabs):
    bsz, n, _ = fu.shape
    n2 = FFT_N2
    n1 = n // n2
    nb = FFT_NB
    k1b = FFT_K1 * FFT_K1_GROUPS
    u2 = fu.reshape(bsz, n1, n2 * B_W)
    tt = pl.pallas_call(
        _fft1_kernel,
        out_shape=jax.ShapeDtypeStruct((bsz, 2, n1, n2 * B_W), BF16),
        grid=(n2 // nb, bsz),
        in_specs=[
            pl.BlockSpec((1, n1, nb * B_W), lambda j, b: (b, 0, j)),
            _full((2 * n1, n1)),
            pl.BlockSpec((nb, n1, LANES), lambda j, b: (j, 0, 0)),
            pl.BlockSpec((nb, n1, LANES), lambda j, b: (j, 0, 0)),
        ],
        out_specs=pl.BlockSpec((1, 2, n1, nb * B_W), lambda j, b: (b, 0, 0, j)),
        compiler_params=_params("parallel", "parallel"), name="fft_pass1",
    )(u2, tabs["a1"], tabs["tw_cos"], tabs["tw_sin"])
    t5 = tt.reshape(bsz, 2, n1, n2, B_W)
    y = pl.pallas_call(
        _fft2_kernel,
        out_shape=jax.ShapeDtypeStruct((bsz, n2, n1, B_W), BF16),
        grid=(bsz, n1 // k1b),
        in_specs=[
            pl.BlockSpec((1, 2, k1b, n2, B_W), lambda b, i: (b, 0, i, 0, 0)),
            _full((2 * FFT_K1 * n2, 2 * FFT_K1 * n2)),
            _full((B_W, B_W)), _full((B_W, B_W)), _of_layer(l, (B_W, B_W)),
        ],
        out_specs=pl.BlockSpec((1, n2, k1b, B_W), lambda b, i: (b, 0, i, 0)),
        compiler_params=_params("parallel", "parallel"), name="fft_pass2",
    )(t5, tabs["l2"], tabs["chan_cos"], tabs["chan_sin"], pp["fnet_w"])
    return y.reshape(bsz, n, B_W)


def _attn_kernel(q_ref, kt_ref, va_ref, o_ref, qm_ref, m_ref, acc_ref):
    tq = q_ref.shape[1]
    n = kt_ref.shape[3]
    tk = min(ATT_TK, n)
    q = q_ref[0]
    lane = lax.broadcasted_iota(jnp.int32, q.shape, 1)
    for h in range(C_GRP):
        qm_ref[h * tq:(h + 1) * tq, :] = jnp.where(lane // C_HD == h, q, jnp.zeros_like(q))

    def scores(c):
        c0 = pl.multiple_of(c * tk, tk)
        return jnp.dot(qm_ref[...], kt_ref[0, 0, :, pl.ds(c0, tk)], preferred_element_type=F32)

    def softmax_accumulate(c, s, first):
        c0 = pl.multiple_of(c * tk, tk)
        v = va_ref[0, 0, pl.ds(c0, tk), :]
        s_max = s[:, 0:LANES]
        for t in range(1, tk // LANES):
            s_max = jnp.maximum(s_max, s[:, t * LANES:(t + 1) * LANES])
        row_max = jnp.max(s_max, axis=-1, keepdims=True)
        if first:
            m_new = jnp.broadcast_to(row_max, m_ref.shape)
        else:
            m_old = m_ref[...]
            m_new = jnp.maximum(m_old, row_max)
        p = jnp.exp2(s - jnp.concatenate([m_new] * (tk // LANES), axis=-1))
        pv = jnp.dot(p.astype(BF16), v, preferred_element_type=F32)
        acc_ref[...] = pv if first else jnp.exp2(m_old - m_new) * acc_ref[...] + pv
        m_ref[...] = m_new

    def step(c, carry):
        softmax_accumulate(c, scores(c), first=False)
        return carry

    softmax_accumulate(0, scores(0), first=True)
    lax.fori_loop(1, n // tk, step, 0, unroll=ATT_UNROLL)
    acc = acc_ref[...]
    inv = 1.0 / acc[:, ONES_LANE:ONES_LANE + 1]
    outs = [(acc[h * tq:(h + 1) * tq, 0:C_HD] * inv[h * tq:(h + 1) * tq]) for h in range(C_GRP)]
    o_ref[0] = jnp.concatenate(outs, axis=-1).astype(BF16)


def _attention(aq, kt4, va):
    bsz, n, _ = aq.shape
    tq = min(ATT_TQ, n)
    return pl.pallas_call(
        _attn_kernel,
        out_shape=jax.ShapeDtypeStruct((bsz, n, C_W), BF16),
        grid=(bsz, C_KV, n // tq),
        in_specs=[
            pl.BlockSpec((1, tq, C_GRP * C_HD), lambda b, g, i: (b, i, g)),
            pl.BlockSpec((1, 1, C_GRP * C_HD, n), lambda b, g, i: (b, g, 0, 0)),
            pl.BlockSpec((1, 1, n, LANES), lambda b, g, i: (b, g, 0, 0)),
        ],
        out_specs=pl.BlockSpec((1, tq, C_GRP * C_HD), lambda b, g, i: (b, i, g)),
        scratch_shapes=[
            pltpu.VMEM((C_GRP * tq, C_GRP * C_HD), BF16),
            pltpu.VMEM((C_GRP * tq, LANES), F32),
            pltpu.VMEM((C_GRP * tq, LANES), F32),
        ],
        compiler_params=_params("parallel", "parallel", "arbitrary"), name="attention",
    )(aq, kt4, va)


def _outproj_kernel(x_hbm, mod_ref, oa_ref, yb_ref, ao_ref, du_ref, dvn_ref, sz_ref,
                    w_ref, postg_ref, og_ref, bd_ref, sw_ref, sb_ref, o_ref, xbuf_ref, xsem_ref):
    tm = o_ref.shape[1]
    tiles = x_hbm.shape[1] // tm
    step = pl.program_id(0)
    n_steps = pl.num_programs(0)

    def x_copy(s, slot):
        return pltpu.make_async_copy(x_hbm.at[s // tiles, pl.ds((s % tiles) * tm, tm), :],
                                     xbuf_ref.at[slot], xsem_ref.at[slot])

    @pl.when(step == 0)
    def _():
        x_copy(0, 0).start()

    @pl.when((step == 0) & (n_steps > 1))
    def _():
        x_copy(1, 1).start()

    @pl.when(step + OUT_X_AHEAD < n_steps)
    def _():
        x_copy(step + OUT_X_AHEAD, (step + OUT_X_AHEAD) % OUT_X_SLOTS).start()

    slot = step % OUT_X_SLOTS
    sz = sz_ref[0].astype(F32)
    off_b, off_c, off_d = A_W, A_W + B_W, A_W + B_W + C_W

    sgu = [jnp.dot(sw_ref[0], dvn_ref[0, c * SGU_CHUNK:(c + 1) * SGU_CHUNK, :], preferred_element_type=F32)
           for c in range(tm // SGU_CHUNK)]
    m_c = (ao_ref[0].astype(F32) * sz[:, off_c:off_d]).astype(BF16)
    y = jnp.dot(m_c, w_ref[0, off_c:off_d, :], preferred_element_type=F32)
    m_b = (yb_ref[0].astype(F32) * sz[:, off_b:off_c]).astype(BF16)
    y = y + jnp.dot(m_b, w_ref[0, off_b:off_c, :], preferred_element_type=F32)

    o = oa_ref[0].astype(F32)
    oms = jnp.dot((o * o).astype(BF16), bd_ref[...], preferred_element_type=F32)
    out_a = o * lax.rsqrt(oms + EPS) * og_ref[0]
    y = y + jnp.dot((out_a * sz[:, 0:off_b]).astype(BF16), w_ref[0, 0:off_b, :], preferred_element_type=F32)

    lane = lax.broadcasted_iota(jnp.int32, (SGU_CHUNK, D_W), 1)
    parts = []
    for c, r in enumerate(sgu):
        rows = slice(c * SGU_CHUNK, (c + 1) * SGU_CHUNK)
        mix = sb_ref[0]
        for g in range(D_GROUPS):
            mix = mix + jnp.where(lane // D_GC == g, r[g * SGU_CHUNK:(g + 1) * SGU_CHUNK, :], 0.0)
        parts.append(du_ref[0, rows, :].astype(F32) * mix)
    out_d = jnp.concatenate(parts, axis=0)
    m_d = (out_d * sz[:, off_d:]).astype(BF16)
    y = y + jnp.dot(m_d, w_ref[0, off_d:, :], preferred_element_type=F32)

    yms = jnp.mean(y * y, axis=-1, keepdims=True)
    yn = y * lax.rsqrt(yms + EPS) * postg_ref[0]
    gate = mod_ref[0, 0, :, 2 * D_MODEL:3 * D_MODEL]
    x_copy(step, slot).wait()
    o_ref[0] = xbuf_ref[slot] + gate * yn


def _outproj(x, mod, row0, l, oa, yb, ao, du, dvn, sz, pp):
    bsz, n, _ = x.shape
    tm = min(TM_OUT, n)
    tiles = n // tm
    tok = lambda w: pl.BlockSpec((1, tm, w), lambda s: (s // tiles, s % tiles, 0))
    return pl.pallas_call(
        _outproj_kernel,
        out_shape=jax.ShapeDtypeStruct((bsz, n, D_MODEL), F32),
        grid=(bsz * tiles,),
        in_specs=[
            pl.BlockSpec(memory_space=pl.ANY),
            pl.BlockSpec((1, 1, 1, 3 * D_MODEL), lambda s: (l, row0 + s // tiles, 0, 0)),
            tok(A_W), tok(B_W), tok(C_W), tok(D_W), tok(D_W), tok(D_MIX),
            _of_layer(l, (D_MIX, D_MODEL)), _of_layer(l, (1, D_MODEL)), _of_layer(l, (1, A_W)), _full((A_W, A_W)),
            _of_layer(l, (D_GROUPS * SGU_CHUNK, SGU_CHUNK)), _of_layer(l, (SGU_CHUNK, D_W)),
        ],
        out_specs=tok(D_MODEL),
        scratch_shapes=[pltpu.VMEM((OUT_X_SLOTS, tm, D_MODEL), F32), pltpu.SemaphoreType.DMA((OUT_X_SLOTS,))],
        compiler_params=_params("arbitrary"), name="outproj",
    )(x, mod, oa, yb, ao, du, dvn, sz,
      pp["w_out"], pp["post_g"], pp["og"], pp["bd_a"], pp["sgu_w"], pp["sgu_b"])


def _pack_w_in(w):
    sizes = [A_KW, A_KW, A_W, 2 * GLA_RANK, B_W, C_W, C_KVW, C_KVW, D_W, D_W, D_MIX]
    off = [0] + [int(o) for o in np.cumsum(sizes)]
    a_q, a_k, a_v, a_lr, b_u, c_q, c_k, c_v, d_u, d_v, z = [w[:, off[i]:off[i + 1]] for i in range(len(sizes))]

    def deinterleave(x, heads):
        r = x.reshape(D_MODEL, heads, C_HD // 2, 2)
        return jnp.concatenate([r[..., 0], r[..., 1]], axis=-1).reshape(D_MODEL, heads * C_HD)

    zeros = lambda width: jnp.zeros((D_MODEL, width), w.dtype)
    pieces = [a_q, a_k, a_v, b_u, deinterleave(c_q, C_HEADS), deinterleave(c_k, C_KV),
              a_lr, zeros(LANES - 2 * GLA_RANK)]
    for h in range(C_KV):
        pieces += [c_v[:, h * C_HD:(h + 1) * C_HD], zeros(LANES - C_HD)]
    pieces += [d_u, d_v, z]
    packed = jnp.concatenate(pieces, axis=1)
    assert packed.shape[1] == W_PACKED
    return packed.astype(BF16)


def _block_diag_mean(width, group):
    idx = np.arange(width) // group
    return jnp.asarray((idx[:, None] == idx[None, :]).astype(np.float32) / group, BF16)


def _stacked_params(norm_pre_g, norm_post_g, w_in, gla_wg2_f, gla_bg_f, gla_wg2_b, gla_bg_b,
                    gla_onorm_g, fnet_w, q_norm_g, k_norm_g, sgu_norm_g, sgu_w, sgu_b, w_out):
    depth = w_in.shape[0]
    deint = np.concatenate([np.arange(0, C_HD, 2), np.arange(1, C_HD, 2)])
    wg2 = jnp.zeros((depth, LANES, 2 * A_KW), F32)
    wg2 = wg2.at[:, 0:GLA_RANK, 0:A_KW].set(gla_wg2_f)
    wg2 = wg2.at[:, GLA_RANK:2 * GLA_RANK, A_KW:].set(gla_wg2_b)
    vone = np.zeros((1, 2 * LANES), np.float32)
    vone[0, ONES_LANE] = 1.0
    vone[0, LANES + ONES_LANE] = 1.0
    return {
        "pre_g": norm_pre_g.reshape(depth, 1, D_MODEL),
        "post_g": norm_post_g.reshape(depth, 1, D_MODEL),
        "w_in": jax.vmap(_pack_w_in)(w_in),
        "wg2": wg2.astype(BF16),
        "bg": jnp.concatenate([gla_bg_f, gla_bg_b], axis=-1).reshape(depth, 1, 2 * A_KW),
        "qg": jnp.tile(q_norm_g[:, deint], (1, C_HEADS)).reshape(depth, 1, C_W),
        "kg": jnp.tile(k_norm_g[:, deint], (1, C_KV)).reshape(depth, 1, C_KVW),
        "sg": sgu_norm_g.reshape(depth, 1, D_W),
        "bd_q": _block_diag_mean(C_W, C_HD),
        "bd_a": _block_diag_mean(A_W, A_DV),
        "vone": jnp.asarray(vone),
        "og": jnp.tile(gla_onorm_g, (1, A_HEADS)).reshape(depth, 1, A_W),
        "fnet_w": fnet_w.astype(BF16),
        "sgu_w": sgu_w.reshape(depth, D_GROUPS * SGU_CHUNK, SGU_CHUNK).astype(BF16),
        "sgu_b": jnp.repeat(jnp.swapaxes(sgu_b, 1, 2), D_GC, axis=2),
        "w_out": w_out.astype(BF16),
    }


def _seq_tables(n):
    rows = n // GRID_W
    row = jnp.repeat(jnp.arange(rows, dtype=F32), GRID_W)
    col = jnp.tile(jnp.arange(GRID_W, dtype=F32), rows)
    rope_axis = C_HD // 2
    freqs = ROPE_THETA ** (-jnp.arange(0, rope_axis, 2, dtype=F32) / rope_axis)
    ang = jnp.concatenate([row[:, None] * freqs, col[:, None] * freqs], axis=-1)
    cos, sin = jnp.cos(ang), jnp.sin(ang)
    cos_t = jnp.tile(jnp.concatenate([cos, cos], axis=-1), (1, LANES // C_HD))
    sin_t = jnp.tile(jnp.concatenate([-sin, sin], axis=-1), (1, LANES // C_HD))

    n2 = FFT_N2
    n1 = n // n2

    def dft_angles(rows, cols, period):
        prod = (jnp.arange(rows, dtype=jnp.int32)[:, None] * jnp.arange(cols, dtype=jnp.int32)[None, :]) % period
        return prod.astype(F32) * (2.0 * math.pi / period)

    a1_ang = dft_angles(n1, n1, n1)
    a1 = jnp.concatenate([jnp.cos(a1_ang), -jnp.sin(a1_ang)], axis=0)
    tw_ang = dft_angles(n2, n1, n)
    scale = 1.0 / math.sqrt(n * B_GC)
    tw_cos = jnp.broadcast_to((jnp.cos(tw_ang) * scale)[:, :, None], (n2, n1, LANES))
    tw_sin = jnp.broadcast_to((jnp.sin(tw_ang) * scale)[:, :, None], (n2, n1, LANES))
    ang2 = dft_angles(n2, n2, n2)
    c2, s2 = jnp.cos(ang2), jnp.sin(ang2)
    eye = jnp.eye(FFT_K1, dtype=F32)
    blk = lambda m: (m[:, None, None, :] * eye[None, :, :, None]).reshape(n2 * FFT_K1, FFT_K1 * n2)
    l2 = jnp.block([[blk(c2), blk(s2)], [blk(-s2), blk(c2)]])
    angc = dft_angles(B_GC, B_GC, B_GC)
    grp = jnp.eye(B_GROUPS, dtype=F32)
    kron = lambda m: (grp[:, None, :, None] * m[None, :, None, :]).reshape(B_W, B_W)
    chan_cos = kron(jnp.cos(angc))
    chan_sin = kron(jnp.sin(angc))
    return {
        "cos": cos_t, "sin": sin_t,
        "a1": a1.astype(BF16),
        "tw_cos": tw_cos, "tw_sin": tw_sin,
        "l2": l2.astype(BF16),
        "chan_cos": chan_cos.astype(BF16), "chan_sin": chan_sin.astype(BF16),
    }


def _layer(x, mod, row0, l, pp, tabs):
    gq, gk, gv, gf, gb, fu, aq, kt4, va, du, dvn, sz = _inproj(x, mod, row0, l, pp, tabs)
    of, ob = _gla_rec(gq, gk, gv, gf, gb)
    oa = _gla_band(gq, gk, gv, gf, gb, of, ob)
    yb = _fnet(fu, l, pp, tabs)
    ao = _attention(aq, kt4, va)
    return _outproj(x, mod, row0, l, oa, yb, ao, du, dvn, sz, pp)


def kernel(x_prompt, x_sample, c_prompt, c_sample, ada_w, ada_b, norm_pre_g, norm_post_g, w_in,
           gla_wg2_f, gla_bg_f, gla_wg2_b, gla_bg_b, gla_onorm_g, fnet_w, q_norm_g, k_norm_g,
           sgu_norm_g, sgu_w, sgu_b, w_out):
    bp, bs = x_prompt.shape[0], x_sample.shape[0]
    pad_rows = (-(bp + bs)) % 8
    c_all = jnp.concatenate([c_prompt, c_sample, jnp.zeros((pad_rows, D_MODEL), F32)], axis=0)
    mod = _adaln(c_all, ada_w, ada_b)
    mod = mod.reshape(DEPTH, mod.shape[1], 1, 3 * D_MODEL)
    pp = _stacked_params(norm_pre_g, norm_post_g, w_in, gla_wg2_f, gla_bg_f, gla_wg2_b, gla_bg_b,
                         gla_onorm_g, fnet_w, q_norm_g, k_norm_g, sgu_norm_g, sgu_w, sgu_b, w_out)
    tabs_p = _seq_tables(x_prompt.shape[1])
    tabs_s = _seq_tables(x_sample.shape[1])
    y_prompt, y_sample = x_prompt, x_sample
    for l in range(DEPTH):
        y_prompt = _layer(y_prompt, mod, 0, l, pp, tabs_p)
        y_sample = _layer(y_sample, mod, bp, l, pp, tabs_s)
    return (y_prompt, y_sample)
```

```python
import math

import numpy as np
import jax
import jax.numpy as jnp
from jax import lax
from jax.experimental import pallas as pl
from jax.experimental.pallas import tpu as pltpu

F32 = jnp.float32
BF16 = jnp.bfloat16
HIGHEST = lax.Precision.HIGHEST

D_MODEL = 1024
DEPTH = 2
GRID_W = 64
A_HEADS, A_DK, A_DV = 4, 32, 64
A_W, A_KW = A_HEADS * A_DV, A_HEADS * A_DK
GLA_RANK = 16
GLA_GATE_NORM = 16.0
B_GROUPS, B_GC = 4, 64
B_W = B_GROUPS * B_GC
C_HEADS, C_KV, C_HD = 8, 2, 64
C_GRP = C_HEADS // C_KV
C_W, C_KVW = C_HEADS * C_HD, C_KV * C_HD
ROPE_THETA = 10000.0
D_GROUPS, D_GC = 4, 64
D_W = D_GROUPS * D_GC
SGU_CHUNK = 128
D_MIX = A_W + B_W + C_W + D_W
EPS = 1e-6
LOG2E = 1.4426950408889634

LANES = 128
VMEM_LIMIT_BYTES = 56 * 1024 * 1024

GQ, GK, GV, FU, AQ, AK, LR, AV, DU, DV, ZZ = 0, 128, 256, 512, 768, 1280, 1408, 1536, 1792, 2048, 2304
W_PACKED = 3584
AV_W = 2 * LANES
ONES_LANE = C_HD

TM_IN = 1024
IN_ROW_PARTS = 4
TM_OUT = 1024
OUT_X_AHEAD = 2
OUT_X_SLOTS = OUT_X_AHEAD + 1
GLA_SUB = 16
GLA_BAND_TILE = 64
GLA_PITCH = GLA_SUB + 4
GLA_GRP = 128
GLA_SEG = 1024
GLA_LOOKAHEAD = 3
FFT_N2 = 64
FFT_NB = 16
FFT_K1 = 8
FFT_K1_GROUPS = 2
ATT_TQ = 512
ATT_TK = 2048
ATT_UNROLL = 2


def _params(*sem):
    return pltpu.CompilerParams(dimension_semantics=sem, vmem_limit_bytes=VMEM_LIMIT_BYTES)


def _full(shape):
    n = len(shape)
    return pl.BlockSpec(shape, lambda *_: (0,) * n)


def _of_layer(l, shape):
    n = len(shape)
    return pl.BlockSpec((1,) + tuple(shape), lambda *_: (l,) + (0,) * n)


def _adaln_kernel(c_ref, w_ref, b_ref, o_ref):
    c = c_ref[...]
    sc = c / (1.0 + jnp.exp(-c))
    o_ref[0] = jnp.dot(sc, w_ref[0], precision=HIGHEST, preferred_element_type=F32) + b_ref[0]


def _adaln(c_all, ada_w, ada_b):
    rows = c_all.shape[0]
    ncol = 3 * D_MODEL // D_MODEL
    return pl.pallas_call(
        _adaln_kernel,
        out_shape=jax.ShapeDtypeStruct((DEPTH, rows, 3 * D_MODEL), F32),
        grid=(DEPTH, ncol),
        in_specs=[
            pl.BlockSpec((rows, D_MODEL), lambda l, j: (0, 0)),
            pl.BlockSpec((1, D_MODEL, D_MODEL), lambda l, j: (l, 0, j)),
            pl.BlockSpec((1, 1, D_MODEL), lambda l, j: (l, 0, j)),
        ],
        out_specs=pl.BlockSpec((1, rows, D_MODEL), lambda l, j: (l, 0, j)),
        compiler_params=_params("arbitrary", "arbitrary"),
        name="adaln",
    )(c_all, ada_w, ada_b.reshape(DEPTH, 1, 3 * D_MODEL))


def _rope(x, cos, sin_signed):
    width = x.shape[-1]
    lane = lax.broadcasted_iota(jnp.int32, x.shape, 1)
    first_half = (lane % C_HD) < (C_HD // 2)
    swapped = jnp.where(first_half, pltpu.roll(x, width - C_HD // 2, 1), pltpu.roll(x, C_HD // 2, 1))
    reps = width // LANES
    cos_w = jnp.concatenate([cos] * reps, axis=-1) if reps > 1 else cos
    sin_w = jnp.concatenate([sin_signed] * reps, axis=-1) if reps > 1 else sin_signed
    return x * cos_w + swapped * sin_w


def _inproj_kernel(x_ref, mod_ref, preg_ref, w_ref, wg2_ref, bg_ref, qg_ref, kg_ref, sg_ref,
                   cos_ref, sin_ref, bd_ref, vone_ref,
                   gq_ref, gk_ref, gv_ref, gf_ref, gb_ref, fu_ref, aq_ref, kt_ref, va_ref,
                   du_ref, dvn_ref, sz_ref):
    tm = x_ref.shape[1]
    shift = mod_ref[0, 0, :, 0:D_MODEL]
    gain = preg_ref[0] * (1.0 + mod_ref[0, 0, :, D_MODEL:2 * D_MODEL])
    for part in range(IN_ROW_PARTS):
        rows = slice(part * tm // IN_ROW_PARTS, (part + 1) * tm // IN_ROW_PARTS)
        x = x_ref[0, rows, :]
        ms = jnp.mean(x * x, axis=-1, keepdims=True)
        hb = ((x * lax.rsqrt(ms + EPS)) * gain + shift).astype(BF16)

        def seg(off, width):
            return jnp.dot(hb, w_ref[0, :, off:off + width], preferred_element_type=F32)

        klr = seg(AK, C_KVW + LANES)
        q = seg(AQ, C_W)
        z = seg(ZZ, D_MIX)
        gqk = seg(GQ, 2 * A_KW)
        gv = seg(GV, A_W)
        fu = seg(FU, B_W)
        va = seg(AV, AV_W)
        du = seg(DU, D_W)
        dv = seg(DV, D_W)
        k = klr[:, :C_KVW]
        lr = klr[:, C_KVW:].astype(BF16)
        logits = jnp.dot(lr, wg2_ref[0], preferred_element_type=F32) + bg_ref[0]
        qms = jnp.dot((q * q).astype(BF16), bd_ref[...], preferred_element_type=F32)
        kms = jnp.dot((k * k).astype(BF16), bd_ref[0:C_KVW, 0:C_KVW], preferred_element_type=F32)

        gq_ref[0, rows, :] = gqk[:, :A_KW] * (A_DK ** -0.5)
        gk_ref[0, rows, :] = gqk[:, A_KW:]
        gv_ref[0, rows, :] = gv
        logg = (jnp.minimum(logits, 0.0) - jnp.log1p(jnp.exp(-jnp.abs(logits)))) * (1.0 / GLA_GATE_NORM)
        gf_ref[0, rows, :] = logg[:, :A_KW]
        gb_ref[0, rows, :] = logg[:, A_KW:]

        fu_ref[0, rows, :] = fu.astype(BF16)

        cos = cos_ref[rows, :]
        sin = sin_ref[rows, :]
        qn = q * lax.rsqrt(qms + EPS) * qg_ref[0]
        aq_ref[0, rows, :] = (_rope(qn, cos, sin) * (C_HD ** -0.5 * LOG2E)).astype(BF16)
        kn = k * lax.rsqrt(kms + EPS) * kg_ref[0]
        kt = _rope(kn, cos, sin).T.astype(BF16)
        for g in range(C_KV):
            for r in range(C_GRP):
                kt_ref[0, g, r * C_HD:(r + 1) * C_HD, rows] = kt[g * C_HD:(g + 1) * C_HD, :]
        vab = (va + vone_ref[...]).astype(BF16)
        va_ref[0, 0, rows, :] = vab[:, :LANES]
        va_ref[0, 1, rows, :] = vab[:, LANES:]

        du_ref[0, rows, :] = du.astype(BF16)
        dms = jnp.mean(dv * dv, axis=-1, keepdims=True)
        dvn_ref[0, rows, :] = (dv * lax.rsqrt(dms + EPS) * sg_ref[0]).astype(BF16)

        sz_ref[0, rows, :] = (z / (1.0 + jnp.exp(-z))).astype(BF16)


def _inproj(x, mod, row0, l, pp, tabs):
    bsz, n, _ = x.shape
    tm = min(TM_IN, n)
    grid = (bsz, n // tm)
    tok = lambda w: pl.BlockSpec((1, tm, w), lambda b, i: (b, i, 0))
    out_shapes = (
        jax.ShapeDtypeStruct((bsz, n, A_KW), F32),
        jax.ShapeDtypeStruct((bsz, n, A_KW), F32),
        jax.ShapeDtypeStruct((bsz, n, A_W), F32),
        jax.ShapeDtypeStruct((bsz, n, A_KW), F32),
        jax.ShapeDtypeStruct((bsz, n, A_KW), F32),
        jax.ShapeDtypeStruct((bsz, n, B_W), BF16),
        jax.ShapeDtypeStruct((bsz, n, C_W), BF16),
        jax.ShapeDtypeStruct((bsz, C_KV, C_GRP * C_HD, n), BF16),
        jax.ShapeDtypeStruct((bsz, C_KV, n, LANES), BF16),
        jax.ShapeDtypeStruct((bsz, n, D_W), BF16),
        jax.ShapeDtypeStruct((bsz, n, D_W), BF16),
        jax.ShapeDtypeStruct((bsz, n, D_MIX), BF16),
    )
    out_specs = (
        tok(A_KW), tok(A_KW), tok(A_W), tok(A_KW), tok(A_KW), tok(B_W), tok(C_W),
        pl.BlockSpec((1, C_KV, C_GRP * C_HD, tm), lambda b, i: (b, 0, 0, i)),
        pl.BlockSpec((1, C_KV, tm, LANES), lambda b, i: (b, 0, i, 0)),
        tok(D_W), tok(D_W), tok(D_MIX),
    )
    in_specs = [
        tok(D_MODEL),
        pl.BlockSpec((1, 1, 1, 3 * D_MODEL), lambda b, i: (l, row0 + b, 0, 0)),
        _of_layer(l, (1, D_MODEL)),
        _of_layer(l, (D_MODEL, W_PACKED)),
        _of_layer(l, (LANES, 2 * A_KW)),
        _of_layer(l, (1, 2 * A_KW)),
        _of_layer(l, (1, C_W)),
        _of_layer(l, (1, C_KVW)),
        _of_layer(l, (1, D_W)),
        pl.BlockSpec((tm, LANES), lambda b, i: (i, 0)),
        pl.BlockSpec((tm, LANES), lambda b, i: (i, 0)),
        _full((C_W, C_W)),
        _full((1, 2 * LANES)),
    ]
    return pl.pallas_call(
        _inproj_kernel, out_shape=out_shapes, grid=grid, in_specs=in_specs, out_specs=out_specs,
        compiler_params=_params("parallel", "parallel"), name="inproj",
    )(x, mod, pp["pre_g"], pp["w_in"], pp["wg2"], pp["bg"], pp["qg"], pp["kg"], pp["sg"],
      tabs["cos"], tabs["sin"], pp["bd_q"], pp["vone"])


def _head_match(shape, rows_per_head, cols_per_head):
    return (lax.broadcasted_iota(jnp.int32, shape, 0) // rows_per_head
            == lax.broadcasted_iota(jnp.int32, shape, 1) // cols_per_head)


def _gla_band_kernel(q_ref, k_ref, v_ref, gf_ref, gb_ref, of_ref, ob_ref, o_ref,
                     qf_ref, kf_ref, vf_ref, bf_ref, cb_ref, pit_ref, os_ref):
    rows = q_ref.shape[1]
    ts = rows // GLA_SUB
    n_slab = A_W // LANES
    every = lambda p: pl.ds(p, ts, stride=GLA_PITCH)
    hsum = jnp.where(_head_match((A_KW, A_W), A_DK, A_DV), 1.0, 0.0).astype(BF16)
    slab_q, slab_k, slab_gf, slab_gb, slab_v = 0, 1, 2, 3, 4
    for i in range(ts):
        src = slice(i * GLA_SUB, (i + 1) * GLA_SUB)
        dst = slice(i * GLA_PITCH, i * GLA_PITCH + GLA_SUB)
        pit_ref[slab_q, dst, :] = q_ref[0, src, :]
        pit_ref[slab_k, dst, :] = k_ref[0, src, :]
        pit_ref[slab_gf, dst, :] = gf_ref[0, src, :]
        pit_ref[slab_gb, dst, :] = gb_ref[0, src, :]
        for h in range(n_slab):
            pit_ref[slab_v + h, dst, :] = v_ref[0, src, h * LANES:(h + 1) * LANES]
    run = jnp.zeros((ts, A_KW), F32)
    for p in range(GLA_SUB):
        run = run + pit_ref[slab_gf, every(p), :] * LOG2E
        bf_ref[p] = run
        qf_ref[p] = pit_ref[slab_q, every(p), :]
        kf_ref[p] = pit_ref[slab_k, every(p), :]
        vf_ref[p] = jnp.concatenate([pit_ref[slab_v + h, every(p), :] for h in range(n_slab)], axis=-1)
    run = jnp.zeros((ts, A_KW), F32)
    for p in reversed(range(GLA_SUB)):
        run = run + pit_ref[slab_gb, every(p), :] * LOG2E
        cb_ref[p] = run
    for p in range(GLA_SUB):
        q = qf_ref[p]
        prods = []
        for s in range(GLA_SUB):
            qk = q * kf_ref[s]
            if s < p:
                qk = qk * jnp.exp2(bf_ref[p] - bf_ref[s])
            elif s > p:
                qk = qk * jnp.exp2(cb_ref[p] - cb_ref[s])
            prods.append(qk.astype(BF16))
        a = jnp.dot(jnp.concatenate(prods, axis=0), hsum, preferred_element_type=F32)
        acc = a[0:ts] * vf_ref[0]
        for s in range(1, GLA_SUB):
            acc = acc + a[s * ts:(s + 1) * ts] * vf_ref[s]
        for h in range(n_slab):
            os_ref[h, every(p), :] = acc[:, h * LANES:(h + 1) * LANES]
    for i in range(ts):
        src = slice(i * GLA_PITCH, i * GLA_PITCH + GLA_SUB)
        dst = slice(i * GLA_SUB, (i + 1) * GLA_SUB)
        band = jnp.concatenate([os_ref[h, src, :] for h in range(n_slab)], axis=-1)
        o_ref[0, dst, :] = (band + of_ref[0, dst, :].astype(F32) + ob_ref[0, dst, :].astype(F32)).astype(BF16)


def _gla_band(gq, gk, gv, gf, gb, of, ob):
    bsz, n, _ = gq.shape
    rows = min(GLA_BAND_TILE * GLA_SUB, n)
    ts = rows // GLA_SUB
    blk = lambda w: pl.BlockSpec((1, rows, w), lambda b, i: (b, i, 0))
    return pl.pallas_call(
        _gla_band_kernel,
        out_shape=jax.ShapeDtypeStruct((bsz, n, A_W), BF16),
        grid=(bsz, n // rows),
        in_specs=[blk(A_KW), blk(A_KW), blk(A_W), blk(A_KW), blk(A_KW), blk(A_W), blk(A_W)],
        out_specs=blk(A_W),
        scratch_shapes=[
            pltpu.VMEM((GLA_SUB, ts, A_KW), F32), pltpu.VMEM((GLA_SUB, ts, A_KW), F32),
            pltpu.VMEM((GLA_SUB, ts, A_W), F32),
            pltpu.VMEM((GLA_SUB, ts, A_KW), F32), pltpu.VMEM((GLA_SUB, ts, A_KW), F32),
            pltpu.VMEM((4 + A_W // LANES, ts * GLA_PITCH, LANES), F32),
            pltpu.VMEM((A_W // LANES, ts * GLA_PITCH, LANES), F32),
        ],
        compiler_params=_params("parallel", "parallel"), name="gla_band",
    )(gq, gk, gv, gf, gb, of, ob)


def _split3(x):
    hi = x.astype(BF16)
    rest = x - hi.astype(F32)
    mid = rest.astype(BF16)
    lo = (rest - mid.astype(F32)).astype(BF16)
    return hi, mid, lo


def _gla_prep(q_ref, k_ref, g_ref, qt_ref, kh_ref, dect_ref, tot_ref, pad_ref, reverse):
    seg = q_ref.shape[1]
    n_sub = seg // GLA_SUB
    r_i = lax.broadcasted_iota(jnp.int32, (GLA_GRP, GLA_GRP), 0)
    c_i = lax.broadcasted_iota(jnp.int32, (GLA_GRP, GLA_GRP), 1)
    same = (r_i // GLA_SUB) == (c_i // GLA_SUB)
    tri = (c_i >= r_i) if reverse else (c_i <= r_i)
    sums_mat = jnp.concatenate([jnp.where(same & tri, 1.0, 0.0), jnp.where(same, 1.0, 0.0)], axis=0).astype(BF16)
    for r in range(seg // GLA_GRP):
        rows = slice(r * GLA_GRP, (r + 1) * GLA_GRP)
        g3 = jnp.concatenate(_split3(g_ref[0, rows, :]), axis=-1)
        sums = jnp.dot(sums_mat, g3, preferred_element_type=F32)
        sums = sums[:, 0:A_KW] + sums[:, A_KW:2 * A_KW] + sums[:, 2 * A_KW:]
        cum, tot = sums[:GLA_GRP], sums[GLA_GRP:]
        qt_ref[rows, :] = (q_ref[0, rows, :] * jnp.exp(cum)).astype(BF16)
        kh_ref[rows, :] = (k_ref[0, rows, :] * jnp.exp(tot - cum)).astype(BF16)
        tot_ref[rows, :] = tot
    pad_ref[...] = jnp.zeros_like(pad_ref)
    pad_ref[0:n_sub, :] = tot_ref[pl.ds(0, n_sub, stride=GLA_SUB), :]
    dect_ref[...] = jnp.exp(pad_ref[...].T)


def _gla_rec_kernel(qf_ref, kf_ref, vf_ref, gf_ref, qb_ref, kb_ref, vb_ref, gb_ref, of_ref, ob_ref,
                    sf_ref, sb_ref, qtf_ref, khf_ref, dtf_ref, qtb_ref, khb_ref, dtb_ref, tot_ref, pad_ref):
    @pl.when(pl.program_id(1) == 0)
    def _():
        sf_ref[...] = jnp.zeros_like(sf_ref)
        sb_ref[...] = jnp.zeros_like(sb_ref)

    seg = qf_ref.shape[1]
    n_sub = seg // GLA_SUB
    _gla_prep(qf_ref, kf_ref, gf_ref, qtf_ref, khf_ref, dtf_ref, tot_ref, pad_ref, reverse=False)
    _gla_prep(qb_ref, kb_ref, gb_ref, qtb_ref, khb_ref, dtb_ref, tot_ref, pad_ref, reverse=True)
    smask = _head_match((A_KW, A_W), A_DK, A_DV)

    def update(i, kh_ref, v_ref):
        rows = slice(i * GLA_SUB, (i + 1) * GLA_SUB)
        upd = lax.dot_general(kh_ref[rows, :], v_ref[0, rows, :].astype(BF16), (((0,), (0,)), ((), ())),
                              preferred_element_type=F32)
        return jnp.where(smask, upd, 0.0)

    def step(i, s, upd, qt_ref, dect_ref, o_ref):
        rows = slice(i * GLA_SUB, (i + 1) * GLA_SUB)
        o_ref[0, rows, :] = jnp.dot(qt_ref[rows, :], s.astype(BF16), preferred_element_type=F32).astype(BF16)
        return s * dect_ref[:, i:i + 1] + upd

    order_f = list(range(n_sub))
    order_b = list(reversed(range(n_sub)))
    upd_f = [update(i, khf_ref, vf_ref) for i in order_f[:GLA_LOOKAHEAD]]
    upd_b = [update(i, khb_ref, vb_ref) for i in order_b[:GLA_LOOKAHEAD]]
    s_f = sf_ref[...]
    s_b = sb_ref[...]
    for j in range(n_sub):
        if j + GLA_LOOKAHEAD < n_sub:
            upd_f.append(update(order_f[j + GLA_LOOKAHEAD], khf_ref, vf_ref))
            upd_b.append(update(order_b[j + GLA_LOOKAHEAD], khb_ref, vb_ref))
        s_f = step(order_f[j], s_f, upd_f[j], qtf_ref, dtf_ref, of_ref)
        s_b = step(order_b[j], s_b, upd_b[j], qtb_ref, dtb_ref, ob_ref)
    sf_ref[...] = s_f
    sb_ref[...] = s_b


def _gla_rec(gq, gk, gv, gf, gb):
    bsz, n, _ = gq.shape
    seg = min(GLA_SEG, n)
    nseg = n // seg
    assert seg // GLA_SUB <= LANES
    fwd = lambda w: pl.BlockSpec((1, seg, w), lambda b, s: (b, s, 0))
    bwd = lambda w: pl.BlockSpec((1, seg, w), lambda b, s: (b, nseg - 1 - s, 0))
    return pl.pallas_call(
        _gla_rec_kernel,
        out_shape=(jax.ShapeDtypeStruct((bsz, n, A_W), BF16), jax.ShapeDtypeStruct((bsz, n, A_W), BF16)),
        grid=(bsz, nseg),
        in_specs=[fwd(A_KW), fwd(A_KW), fwd(A_W), fwd(A_KW), bwd(A_KW), bwd(A_KW), bwd(A_W), bwd(A_KW)],
        out_specs=(fwd(A_W), bwd(A_W)),
        scratch_shapes=[
            pltpu.VMEM((A_KW, A_W), F32), pltpu.VMEM((A_KW, A_W), F32),
            pltpu.VMEM((seg, A_KW), BF16), pltpu.VMEM((seg, A_KW), BF16), pltpu.VMEM((A_KW, LANES), F32),
            pltpu.VMEM((seg, A_KW), BF16), pltpu.VMEM((seg, A_KW), BF16), pltpu.VMEM((A_KW, LANES), F32),
            pltpu.VMEM((seg, A_KW), F32), pltpu.VMEM((LANES, A_KW), F32),
        ],
        compiler_params=_params("parallel", "arbitrary"), name="gla_rec",
    )(gq, gk, gv, gf, gq, gk, gv, gb)


def _fft1_kernel(u_ref, a1_ref, cos_ref, sin_ref, t_ref):
    n1 = u_ref.shape[1]
    nb = cos_ref.shape[0]
    t = jnp.dot(a1_ref[...], u_ref[0], preferred_element_type=F32)
    for j in range(nb):
        cols = slice(j * B_W, (j + 1) * B_W)
        tr = t[:n1, cols]
        ti = t[n1:, cols]
        c = jnp.concatenate([cos_ref[j]] * (B_W // LANES), axis=-1)
        s = jnp.concatenate([sin_ref[j]] * (B_W // LANES), axis=-1)
        t_ref[0, 0, :, cols] = (tr * c + ti * s).astype(BF16)
        t_ref[0, 1, :, cols] = (ti * c - tr * s).astype(BF16)


def _fft2_kernel(t_ref, l_ref, cc_ref, cs_ref, w_ref, o_ref):
    n2 = t_ref.shape[3]
    half = FFT_K1 * n2
    groups = range(t_ref.shape[2] // FFT_K1)
    xs = []
    for grp in groups:
        k1s = slice(grp * FFT_K1, (grp + 1) * FFT_K1)
        xs.append(jnp.concatenate([t_ref[0, 0, k1s].reshape(half, B_W), t_ref[0, 1, k1s].reshape(half, B_W)],
                                  axis=0))
    gs = [jnp.dot(l_ref[...], x, preferred_element_type=F32) for x in xs]
    mixed = [jnp.dot(g[:half].astype(BF16), cc_ref[...], preferred_element_type=F32)
             + jnp.dot(g[half:].astype(BF16), cs_ref[...], preferred_element_type=F32) for g in gs]
    ys = [jnp.dot(m.astype(BF16), w_ref[0], preferred_element_type=F32) for m in mixed]
    o_ref[0] = jnp.concatenate([y.reshape(n2, FFT_K1, B_W) for y in ys], axis=1).astype(BF16)


def _fnet(fu, l, pp, tabs):
    bsz, n, _ = fu.shape
    n2 = FFT_N2
    n1 = n // n2
    nb = FFT_NB
    k1b = FFT_K1 * FFT_K1_GROUPS
    u2 = fu.reshape(bsz, n1, n2 * B_W)
    tt = pl.pallas_call(
        _fft1_kernel,
        out_shape=jax.ShapeDtypeStruct((bsz, 2, n1, n2 * B_W), BF16),
        grid=(n2 // nb, bsz),
        in_specs=[
            pl.BlockSpec((1, n1, nb * B_W), lambda j, b: (b, 0, j)),
            _full((2 * n1, n1)),
            pl.BlockSpec((nb, n1, LANES), lambda j, b: (j, 0, 0)),
            pl.BlockSpec((nb, n1, LANES), lambda j, b: (j, 0, 0)),
        ],
        out_specs=pl.BlockSpec((1, 2, n1, nb * B_W), lambda j, b: (b, 0, 0, j)),
        compiler_params=_params("parallel", "parallel"), name="fft_pass1",
    )(u2, tabs["a1"], tabs["tw_cos"], tabs["tw_sin"])
    t5 = tt.reshape(bsz, 2, n1, n2, B_W)
    y = pl.pallas_call(
        _fft2_kernel,
        out_shape=jax.ShapeDtypeStruct((bsz, n2, n1, B_W), BF16),
        grid=(bsz, n1 // k1b),
        in_specs=[
            pl.BlockSpec((1, 2, k1b, n2, B_W), lambda b, i: (b, 0, i, 0, 0)),
            _full((2 * FFT_K1 * n2, 2 * FFT_K1 * n2)),
            _full((B_W, B_W)), _full((B_W, B_W)), _of_layer(l, (B_W, B_W)),
        ],
        out_specs=pl.BlockSpec((1, n2, k1b, B_W), lambda b, i: (b, 0, i, 0)),
        compiler_params=_params("parallel", "parallel"), name="fft_pass2",
    )(t5, tabs["l2"], tabs["chan_cos"], tabs["chan_sin"], pp["fnet_w"])
    return y.reshape(bsz, n, B_W)


def _attn_kernel(q_ref, kt_ref, va_ref, o_ref, qm_ref, m_ref, acc_ref):
    tq = q_ref.shape[1]
    n = kt_ref.shape[3]
    tk = min(ATT_TK, n)
    q = q_ref[0]
    lane = lax.broadcasted_iota(jnp.int32, q.shape, 1)
    for h in range(C_GRP):
        qm_ref[h * tq:(h + 1) * tq, :] = jnp.where(lane // C_HD == h, q, jnp.zeros_like(q))

    def scores(c):
        c0 = pl.multiple_of(c * tk, tk)
        return jnp.dot(qm_ref[...], kt_ref[0, 0, :, pl.ds(c0, tk)], preferred_element_type=F32)

    def softmax_accumulate(c, s, first):
        c0 = pl.multiple_of(c * tk, tk)
        v = va_ref[0, 0, pl.ds(c0, tk), :]
        s_max = s[:, 0:LANES]
        for t in range(1, tk // LANES):
            s_max = jnp.maximum(s_max, s[:, t * LANES:(t + 1) * LANES])
        row_max = jnp.max(s_max, axis=-1, keepdims=True)
        if first:
            m_new = jnp.broadcast_to(row_max, m_ref.shape)
        else:
            m_old = m_ref[...]
            m_new = jnp.maximum(m_old, row_max)
        p = jnp.exp2(s - jnp.concatenate([m_new] * (tk // LANES), axis=-1))
        pv = jnp.dot(p.astype(BF16), v, preferred_element_type=F32)
        acc_ref[...] = pv if first else jnp.exp2(m_old - m_new) * acc_ref[...] + pv
        m_ref[...] = m_new

    def step(c, carry):
        softmax_accumulate(c, scores(c), first=False)
        return carry

    softmax_accumulate(0, scores(0), first=True)
    lax.fori_loop(1, n // tk, step, 0, unroll=ATT_UNROLL)
    acc = acc_ref[...]
    inv = 1.0 / acc[:, ONES_LANE:ONES_LANE + 1]
    outs = [(acc[h * tq:(h + 1) * tq, 0:C_HD] * inv[h * tq:(h + 1) * tq]) for h in range(C_GRP)]
    o_ref[0] = jnp.concatenate(outs, axis=-1).astype(BF16)


def _attention(aq, kt4, va):
    bsz, n, _ = aq.shape
    tq = min(ATT_TQ, n)
    return pl.pallas_call(
        _attn_kernel,
        out_shape=jax.ShapeDtypeStruct((bsz, n, C_W), BF16),
        grid=(bsz, C_KV, n // tq),
        in_specs=[
            pl.BlockSpec((1, tq, C_GRP * C_HD), lambda b, g, i: (b, i, g)),
            pl.BlockSpec((1, 1, C_GRP * C_HD, n), lambda b, g, i: (b, g, 0, 0)),
            pl.BlockSpec((1, 1, n, LANES), lambda b, g, i: (b, g, 0, 0)),
        ],
        out_specs=pl.BlockSpec((1, tq, C_GRP * C_HD), lambda b, g, i: (b, i, g)),
        scratch_shapes=[
            pltpu.VMEM((C_GRP * tq, C_GRP * C_HD), BF16),
            pltpu.VMEM((C_GRP * tq, LANES), F32),
            pltpu.VMEM((C_GRP * tq, LANES), F32),
        ],
        compiler_params=_params("parallel", "parallel", "arbitrary"), name="attention",
    )(aq, kt4, va)


def _outproj_kernel(x_hbm, mod_ref, oa_ref, yb_ref, ao_ref, du_ref, dvn_ref, sz_hbm,
                    w_ref, postg_ref, og_ref, bd_ref, sw_ref, sb_ref, o_ref,
                    xbuf_ref, xsem_ref, zbuf_ref, zsem_ref):
    tm = o_ref.shape[1]
    tiles = x_hbm.shape[1] // tm
    step = pl.program_id(0)
    n_steps = pl.num_programs(0)

    def ring_copies(s, slot):
        rows = pl.ds((s % tiles) * tm, tm)
        return (pltpu.make_async_copy(x_hbm.at[s // tiles, rows, :], xbuf_ref.at[slot], xsem_ref.at[slot]),
                pltpu.make_async_copy(sz_hbm.at[s // tiles, rows, :], zbuf_ref.at[slot], zsem_ref.at[slot]))

    def start(s, slot):
        for cp in ring_copies(s, slot):
            cp.start()

    @pl.when(step == 0)
    def _():
        start(0, 0)

    @pl.when((step == 0) & (n_steps > 1))
    def _():
        start(1, 1)

    @pl.when(step + OUT_X_AHEAD < n_steps)
    def _():
        start(step + OUT_X_AHEAD, (step + OUT_X_AHEAD) % OUT_X_SLOTS)

    slot = step % OUT_X_SLOTS
    x_copy, sz_copy = ring_copies(step, slot)
    sz_copy.wait()
    sz = zbuf_ref[slot].astype(F32)
    off_b, off_c, off_d = A_W, A_W + B_W, A_W + B_W + C_W

    sgu = [jnp.dot(sw_ref[0], dvn_ref[0, c * SGU_CHUNK:(c + 1) * SGU_CHUNK, :], preferred_element_type=F32)
           for c in range(tm // SGU_CHUNK)]
    m_c = (ao_ref[0].astype(F32) * sz[:, off_c:off_d]).astype(BF16)
    y = jnp.dot(m_c, w_ref[0, off_c:off_d, :], preferred_element_type=F32)
    m_b = (yb_ref[0].astype(F32) * sz[:, off_b:off_c]).astype(BF16)
    y = y + jnp.dot(m_b, w_ref[0, off_b:off_c, :], preferred_element_type=F32)

    o = oa_ref[0].astype(F32)
    oms = jnp.dot((o * o).astype(BF16), bd_ref[...], preferred_element_type=F32)
    out_a = o * lax.rsqrt(oms + EPS) * og_ref[0]
    y = y + jnp.dot((out_a * sz[:, 0:off_b]).astype(BF16), w_ref[0, 0:off_b, :], preferred_element_type=F32)

    lane = lax.broadcasted_iota(jnp.int32, (SGU_CHUNK, D_W), 1)
    parts = []
    for c, r in enumerate(sgu):
        rows = slice(c * SGU_CHUNK, (c + 1) * SGU_CHUNK)
        mix = sb_ref[0]
        for g in range(D_GROUPS):
            mix = mix + jnp.where(lane // D_GC == g, r[g * SGU_CHUNK:(g + 1) * SGU_CHUNK, :], 0.0)
        parts.append(du_ref[0, rows, :].astype(F32) * mix)
    out_d = jnp.concatenate(parts, axis=0)
    m_d = (out_d * sz[:, off_d:]).astype(BF16)
    y = y + jnp.dot(m_d, w_ref[0, off_d:, :], preferred_element_type=F32)

    yms = jnp.mean(y * y, axis=-1, keepdims=True)
    yn = y * lax.rsqrt(yms + EPS) * postg_ref[0]
    gate = mod_ref[0, 0, :, 2 * D_MODEL:3 * D_MODEL]
    x_copy.wait()
    o_ref[0] = xbuf_ref[slot] + gate * yn


def _outproj(x, mod, row0, l, oa, yb, ao, du, dvn, sz, pp):
    bsz, n, _ = x.shape
    tm = min(TM_OUT, n)
    tiles = n // tm
    tok = lambda w: pl.BlockSpec((1, tm, w), lambda s: (s // tiles, s % tiles, 0))
    return pl.pallas_call(
        _outproj_kernel,
        out_shape=jax.ShapeDtypeStruct((bsz, n, D_MODEL), F32),
        grid=(bsz * tiles,),
        in_specs=[
            pl.BlockSpec(memory_space=pl.ANY),
            pl.BlockSpec((1, 1, 1, 3 * D_MODEL), lambda s: (l, row0 + s // tiles, 0, 0)),
            tok(A_W), tok(B_W), tok(C_W), tok(D_W), tok(D_W), pl.BlockSpec(memory_space=pl.ANY),
            _of_layer(l, (D_MIX, D_MODEL)), _of_layer(l, (1, D_MODEL)), _of_layer(l, (1, A_W)), _full((A_W, A_W)),
            _of_layer(l, (D_GROUPS * SGU_CHUNK, SGU_CHUNK)), _of_layer(l, (SGU_CHUNK, D_W)),
        ],
        out_specs=tok(D_MODEL),
        scratch_shapes=[pltpu.VMEM((OUT_X_SLOTS, tm, D_MODEL), F32), pltpu.SemaphoreType.DMA((OUT_X_SLOTS,)),
                        pltpu.VMEM((OUT_X_SLOTS, tm, D_MIX), BF16), pltpu.SemaphoreType.DMA((OUT_X_SLOTS,))],
        compiler_params=_params("arbitrary"), name="outproj",
    )(x, mod, oa, yb, ao, du, dvn, sz,
      pp["w_out"], pp["post_g"], pp["og"], pp["bd_a"], pp["sgu_w"], pp["sgu_b"])


def _pack_w_in(w):
    sizes = [A_KW, A_KW, A_W, 2 * GLA_RANK, B_W, C_W, C_KVW, C_KVW, D_W, D_W, D_MIX]
    off = [0] + [int(o) for o in np.cumsum(sizes)]
    a_q, a_k, a_v, a_lr, b_u, c_q, c_k, c_v, d_u, d_v, z = [w[:, off[i]:off[i + 1]] for i in range(len(sizes))]

    def deinterleave(x, heads):
        r = x.reshape(D_MODEL, heads, C_HD // 2, 2)
        return jnp.concatenate([r[..., 0], r[..., 1]], axis=-1).reshape(D_MODEL, heads * C_HD)

    zeros = lambda width: jnp.zeros((D_MODEL, width), w.dtype)
    pieces = [a_q, a_k, a_v, b_u, deinterleave(c_q, C_HEADS), deinterleave(c_k, C_KV),
              a_lr, zeros(LANES - 2 * GLA_RANK)]
    for h in range(C_KV):
        pieces += [c_v[:, h * C_HD:(h + 1) * C_HD], zeros(LANES - C_HD)]
    pieces += [d_u, d_v, z]
    packed = jnp.concatenate(pieces, axis=1)
    assert packed.shape[1] == W_PACKED
    return packed.astype(BF16)


def _block_diag_mean(width, group):
    idx = np.arange(width) // group
    return jnp.asarray((idx[:, None] == idx[None, :]).astype(np.float32) / group, BF16)


def _stacked_params(norm_pre_g, norm_post_g, w_in, gla_wg2_f, gla_bg_f, gla_wg2_b, gla_bg_b,
                    gla_onorm_g, fnet_w, q_norm_g, k_norm_g, sgu_norm_g, sgu_w, sgu_b, w_out):
    depth = w_in.shape[0]
    deint = np.concatenate([np.arange(0, C_HD, 2), np.arange(1, C_HD, 2)])
    wg2 = jnp.zeros((depth, LANES, 2 * A_KW), F32)
    wg2 = wg2.at[:, 0:GLA_RANK, 0:A_KW].set(gla_wg2_f)
    wg2 = wg2.at[:, GLA_RANK:2 * GLA_RANK, A_KW:].set(gla_wg2_b)
    vone = np.zeros((1, 2 * LANES), np.float32)
    vone[0, ONES_LANE] = 1.0
    vone[0, LANES + ONES_LANE] = 1.0
    return {
        "pre_g": norm_pre_g.reshape(depth, 1, D_MODEL),
        "post_g": norm_post_g.reshape(depth, 1, D_MODEL),
        "w_in": jax.vmap(_pack_w_in)(w_in),
        "wg2": wg2.astype(BF16),
        "bg": jnp.concatenate([gla_bg_f, gla_bg_b], axis=-1).reshape(depth, 1, 2 * A_KW),
        "qg": jnp.tile(q_norm_g[:, deint], (1, C_HEADS)).reshape(depth, 1, C_W),
        "kg": jnp.tile(k_norm_g[:, deint], (1, C_KV)).reshape(depth, 1, C_KVW),
        "sg": sgu_norm_g.reshape(depth, 1, D_W),
        "bd_q": _block_diag_mean(C_W, C_HD),
        "bd_a": _block_diag_mean(A_W, A_DV),
        "vone": jnp.asarray(vone),
        "og": jnp.tile(gla_onorm_g, (1, A_HEADS)).reshape(depth, 1, A_W),
        "fnet_w": fnet_w.astype(BF16),
        "sgu_w": sgu_w.reshape(depth, D_GROUPS * SGU_CHUNK, SGU_CHUNK).astype(BF16),
        "sgu_b": jnp.repeat(jnp.swapaxes(sgu_b, 1, 2), D_GC, axis=2),
        "w_out": w_out.astype(BF16),
    }


def _seq_tables(n):
    rows = n // GRID_W
    row = jnp.repeat(jnp.arange(rows, dtype=F32), GRID_W)
    col = jnp.tile(jnp.arange(GRID_W, dtype=F32), rows)
    rope_axis = C_HD // 2
    freqs = ROPE_THETA ** (-jnp.arange(0, rope_axis, 2, dtype=F32) / rope_axis)
    ang = jnp.concatenate([row[:, None] * freqs, col[:, None] * freqs], axis=-1)
    cos, sin = jnp.cos(ang), jnp.sin(ang)
    cos_t = jnp.tile(jnp.concatenate([cos, cos], axis=-1), (1, LANES // C_HD))
    sin_t = jnp.tile(jnp.concatenate([-sin, sin], axis=-1), (1, LANES // C_HD))

    n2 = FFT_N2
    n1 = n // n2

    def dft_angles(rows, cols, period):
        prod = (jnp.arange(rows, dtype=jnp.int32)[:, None] * jnp.arange(cols, dtype=jnp.int32)[None, :]) % period
        return prod.astype(F32) * (2.0 * math.pi / period)

    a1_ang = dft_angles(n1, n1, n1)
    a1 = jnp.concatenate([jnp.cos(a1_ang), -jnp.sin(a1_ang)], axis=0)
    tw_ang = dft_angles(n2, n1, n)
    scale = 1.0 / math.sqrt(n * B_GC)
    tw_cos = jnp.broadcast_to((jnp.cos(tw_ang) * scale)[:, :, None], (n2, n1, LANES))
    tw_sin = jnp.broadcast_to((jnp.sin(tw_ang) * scale)[:, :, None], (n2, n1, LANES))
    ang2 = dft_angles(n2, n2, n2)
    c2, s2 = jnp.cos(ang2), jnp.sin(ang2)
    eye = jnp.eye(FFT_K1, dtype=F32)
    blk = lambda m: (m[:, None, None, :] * eye[None, :, :, None]).reshape(n2 * FFT_K1, FFT_K1 * n2)
    l2 = jnp.block([[blk(c2), blk(s2)], [blk(-s2), blk(c2)]])
    angc = dft_angles(B_GC, B_GC, B_GC)
    grp = jnp.eye(B_GROUPS, dtype=F32)
    kron = lambda m: (grp[:, None, :, None] * m[None, :, None, :]).reshape(B_W, B_W)
    chan_cos = kron(jnp.cos(angc))
    chan_sin = kron(jnp.sin(angc))
    return {
        "cos": cos_t, "sin": sin_t,
        "a1": a1.astype(BF16),
        "tw_cos": tw_cos, "tw_sin": tw_sin,
        "l2": l2.astype(BF16),
        "chan_cos": chan_cos.astype(BF16), "chan_sin": chan_sin.astype(BF16),
    }


def _layer(x, mod, row0, l, pp, tabs):
    gq, gk, gv, gf, gb, fu, aq, kt4, va, du, dvn, sz = _inproj(x, mod, row0, l, pp, tabs)
    of, ob = _gla_rec(gq, gk, gv, gf, gb)
    oa = _gla_band(gq, gk, gv, gf, gb, of, ob)
    yb = _fnet(fu, l, pp, tabs)
    ao = _attention(aq, kt4, va)
    return _outproj(x, mod, row0, l, oa, yb, ao, du, dvn, sz, pp)


def kernel(x_prompt, x_sample, c_prompt, c_sample, ada_w, ada_b, norm_pre_g, norm_post_g, w_in,
           gla_wg2_f, gla_bg_f, gla_wg2_b, gla_bg_b, gla_onorm_g, fnet_w, q_norm_g, k_norm_g,
           sgu_norm_g, sgu_w, sgu_b, w_out):
    bp, bs = x_prompt.shape[0], x_sample.shape[0]
    pad_rows = (-(bp + bs)) % 8
    c_all = jnp.concatenate([c_prompt, c_sample, jnp.zeros((pad_rows, D_MODEL), F32)], axis=0)
    mod = _adaln(c_all, ada_w, ada_b)
    mod = mod.reshape(DEPTH, mod.shape[1], 1, 3 * D_MODEL)
    pp = _stacked_params(norm_pre_g, norm_post_g, w_in, gla_wg2_f, gla_bg_f, gla_wg2_b, gla_bg_b,
                         gla_onorm_g, fnet_w, q_norm_g, k_norm_g, sgu_norm_g, sgu_w, sgu_b, w_out)
    tabs_p = _seq_tables(x_prompt.shape[1])
    tabs_s = _seq_tables(x_sample.shape[1])
    y_prompt, y_sample = x_prompt, x_sample
    for l in range(DEPTH):
        y_prompt = _layer(y_prompt, mod, 0, l, pp, tabs_p)
        y_sample = _layer(y_sample, mod, bp, l, pp, tabs_s)
    return (y_prompt, y_sample)
```

```python
import math

import numpy as np
import jax
import jax.numpy as jnp
from jax import lax
from jax.experimental import pallas as pl
from jax.experimental.pallas import tpu as pltpu

F32 = jnp.float32
BF16 = jnp.bfloat16
HIGHEST = lax.Precision.HIGHEST

D_MODEL = 1024
DEPTH = 2
GRID_W = 64
A_HEADS, A_DK, A_DV = 4, 32, 64
A_W, A_KW = A_HEADS * A_DV, A_HEADS * A_DK
GLA_RANK = 16
GLA_GATE_NORM = 16.0
B_GROUPS, B_GC = 4, 64
B_W = B_GROUPS * B_GC
C_HEADS, C_KV, C_HD = 8, 2, 64
C_GRP = C_HEADS // C_KV
C_W, C_KVW = C_HEADS * C_HD, C_KV * C_HD
ROPE_THETA = 10000.0
D_GROUPS, D_GC = 4, 64
D_W = D_GROUPS * D_GC
SGU_CHUNK = 128
D_MIX = A_W + B_W + C_W + D_W
EPS = 1e-6
LOG2E = 1.4426950408889634

LANES = 128
VMEM_LIMIT_BYTES = 56 * 1024 * 1024

GQ, GK, GV, FU, AQ, AK, LR, AV, DU, DV, ZZ = 0, 128, 256, 512, 768, 1280, 1408, 1536, 1792, 2048, 2304
W_PACKED = 3584
AV_W = 2 * LANES
ONES_LANE = C_HD

TM_IN = 1024
IN_ROW_PARTS = 4
TM_OUT = 1024
OUT_X_AHEAD = 2
OUT_X_SLOTS = OUT_X_AHEAD + 1
GLA_SUB = 16
GLA_BAND_TILE = 64
GLA_PITCH = GLA_SUB + 4
GLA_GRP = 128
GLA_SEG = 1024
GLA_LOOKAHEAD = 3
FFT_N2 = 64
FFT_NB = 32
FFT_K1 = 8
FFT_K1_GROUPS = 2
ATT_TQ = 512
ATT_TK = 2048
ATT_UNROLL = 2


def _params(*sem):
    return pltpu.CompilerParams(dimension_semantics=sem, vmem_limit_bytes=VMEM_LIMIT_BYTES)


def _full(shape):
    n = len(shape)
    return pl.BlockSpec(shape, lambda *_: (0,) * n)


def _of_layer(l, shape):
    n = len(shape)
    return pl.BlockSpec((1,) + tuple(shape), lambda *_: (l,) + (0,) * n)


def _adaln_kernel(c_ref, w_ref, b_ref, o_ref):
    c = c_ref[...]
    sc = c / (1.0 + jnp.exp(-c))
    o_ref[0] = jnp.dot(sc, w_ref[0], precision=HIGHEST, preferred_element_type=F32) + b_ref[0]


def _adaln(c_all, ada_w, ada_b):
    rows = c_all.shape[0]
    ncol = 3 * D_MODEL // D_MODEL
    return pl.pallas_call(
        _adaln_kernel,
        out_shape=jax.ShapeDtypeStruct((DEPTH, rows, 3 * D_MODEL), F32),
        grid=(DEPTH, ncol),
        in_specs=[
            pl.BlockSpec((rows, D_MODEL), lambda l, j: (0, 0)),
            pl.BlockSpec((1, D_MODEL, D_MODEL), lambda l, j: (l, 0, j)),
            pl.BlockSpec((1, 1, D_MODEL), lambda l, j: (l, 0, j)),
        ],
        out_specs=pl.BlockSpec((1, rows, D_MODEL), lambda l, j: (l, 0, j)),
        compiler_params=_params("arbitrary", "arbitrary"),
        name="adaln",
    )(c_all, ada_w, ada_b.reshape(DEPTH, 1, 3 * D_MODEL))


def _rope(x, cos, sin_signed):
    width = x.shape[-1]
    lane = lax.broadcasted_iota(jnp.int32, x.shape, 1)
    first_half = (lane % C_HD) < (C_HD // 2)
    swapped = jnp.where(first_half, pltpu.roll(x, width - C_HD // 2, 1), pltpu.roll(x, C_HD // 2, 1))
    reps = width // LANES
    cos_w = jnp.concatenate([cos] * reps, axis=-1) if reps > 1 else cos
    sin_w = jnp.concatenate([sin_signed] * reps, axis=-1) if reps > 1 else sin_signed
    return x * cos_w + swapped * sin_w


def _inproj_kernel(x_ref, mod_ref, preg_ref, w_ref, wg2_ref, bg_ref, qg_ref, kg_ref, sg_ref,
                   cos_ref, sin_ref, bd_ref, vone_ref,
                   gq_ref, gk_ref, gv_ref, gf_ref, gb_ref, fu_ref, aq_ref, kt_ref, va_ref,
                   du_ref, dvn_ref, sz_ref):
    tm = x_ref.shape[1]
    shift = mod_ref[0, 0, :, 0:D_MODEL]
    gain = preg_ref[0] * (1.0 + mod_ref[0, 0, :, D_MODEL:2 * D_MODEL])
    for part in range(IN_ROW_PARTS):
        rows = slice(part * tm // IN_ROW_PARTS, (part + 1) * tm // IN_ROW_PARTS)
        x = x_ref[0, rows, :]
        ms = jnp.mean(x * x, axis=-1, keepdims=True)
        hb = ((x * lax.rsqrt(ms + EPS)) * gain + shift).astype(BF16)

        def seg(off, width):
            return jnp.dot(hb, w_ref[0, :, off:off + width], preferred_element_type=F32)

        klr = seg(AK, C_KVW + LANES)
        q = seg(AQ, C_W)
        z = seg(ZZ, D_MIX)
        gqk = seg(GQ, 2 * A_KW)
        gv = seg(GV, A_W)
        fu = seg(FU, B_W)
        va = seg(AV, AV_W)
        du = seg(DU, D_W)
        dv = seg(DV, D_W)
        k = klr[:, :C_KVW]
        lr = klr[:, C_KVW:].astype(BF16)
        logits = jnp.dot(lr, wg2_ref[0], preferred_element_type=F32) + bg_ref[0]
        qms = jnp.dot((q * q).astype(BF16), bd_ref[...], preferred_element_type=F32)
        kms = jnp.dot((k * k).astype(BF16), bd_ref[0:C_KVW, 0:C_KVW], preferred_element_type=F32)

        gq_ref[0, rows, :] = gqk[:, :A_KW] * (A_DK ** -0.5)
        gk_ref[0, rows, :] = gqk[:, A_KW:]
        gv_ref[0, rows, :] = gv
        logg = (jnp.minimum(logits, 0.0) - jnp.log1p(jnp.exp(-jnp.abs(logits)))) * (1.0 / GLA_GATE_NORM)
        gf_ref[0, rows, :] = logg[:, :A_KW]
        gb_ref[0, rows, :] = logg[:, A_KW:]

        fu_ref[0, rows, :] = fu.astype(BF16)

        cos = cos_ref[rows, :]
        sin = sin_ref[rows, :]
        qn = q * lax.rsqrt(qms + EPS) * qg_ref[0]
        aq_ref[0, rows, :] = (_rope(qn, cos, sin) * (C_HD ** -0.5 * LOG2E)).astype(BF16)
        kn = k * lax.rsqrt(kms + EPS) * kg_ref[0]
        kt = _rope(kn, cos, sin).T.astype(BF16)
        for g in range(C_KV):
            for r in range(C_GRP):
                kt_ref[0, g, r * C_HD:(r + 1) * C_HD, rows] = kt[g * C_HD:(g + 1) * C_HD, :]
        vab = (va + vone_ref[...]).astype(BF16)
        va_ref[0, 0, rows, :] = vab[:, :LANES]
        va_ref[0, 1, rows, :] = vab[:, LANES:]

        du_ref[0, rows, :] = du.astype(BF16)
        dms = jnp.mean(dv * dv, axis=-1, keepdims=True)
        dvn_ref[0, rows, :] = (dv * lax.rsqrt(dms + EPS) * sg_ref[0]).astype(BF16)

        sz_ref[0, rows, :] = (z / (1.0 + jnp.exp(-z))).astype(BF16)


def _inproj(x, mod, row0, l, pp, tabs):
    bsz, n, _ = x.shape
    tm = min(TM_IN, n)
    grid = (bsz, n // tm)
    tok = lambda w: pl.BlockSpec((1, tm, w), lambda b, i: (b, i, 0))
    out_shapes = (
        jax.ShapeDtypeStruct((bsz, n, A_KW), F32),
        jax.ShapeDtypeStruct((bsz, n, A_KW), F32),
        jax.ShapeDtypeStruct((bsz, n, A_W), F32),
        jax.ShapeDtypeStruct((bsz, n, A_KW), F32),
        jax.ShapeDtypeStruct((bsz, n, A_KW), F32),
        jax.ShapeDtypeStruct((bsz, n, B_W), BF16),
        jax.ShapeDtypeStruct((bsz, n, C_W), BF16),
        jax.ShapeDtypeStruct((bsz, C_KV, C_GRP * C_HD, n), BF16),
        jax.ShapeDtypeStruct((bsz, C_KV, n, LANES), BF16),
        jax.ShapeDtypeStruct((bsz, n, D_W), BF16),
        jax.ShapeDtypeStruct((bsz, n, D_W), BF16),
        jax.ShapeDtypeStruct((bsz, n, D_MIX), BF16),
    )
    out_specs = (
        tok(A_KW), tok(A_KW), tok(A_W), tok(A_KW), tok(A_KW), tok(B_W), tok(C_W),
        pl.BlockSpec((1, C_KV, C_GRP * C_HD, tm), lambda b, i: (b, 0, 0, i)),
        pl.BlockSpec((1, C_KV, tm, LANES), lambda b, i: (b, 0, i, 0)),
        tok(D_W), tok(D_W), tok(D_MIX),
    )
    in_specs = [
        tok(D_MODEL),
        pl.BlockSpec((1, 1, 1, 3 * D_MODEL), lambda b, i: (l, row0 + b, 0, 0)),
        _of_layer(l, (1, D_MODEL)),
        _of_layer(l, (D_MODEL, W_PACKED)),
        _of_layer(l, (LANES, 2 * A_KW)),
        _of_layer(l, (1, 2 * A_KW)),
        _of_layer(l, (1, C_W)),
        _of_layer(l, (1, C_KVW)),
        _of_layer(l, (1, D_W)),
        pl.BlockSpec((tm, LANES), lambda b, i: (i, 0)),
        pl.BlockSpec((tm, LANES), lambda b, i: (i, 0)),
        _full((C_W, C_W)),
        _full((1, 2 * LANES)),
    ]
    return pl.pallas_call(
        _inproj_kernel, out_shape=out_shapes, grid=grid, in_specs=in_specs, out_specs=out_specs,
        compiler_params=_params("parallel", "parallel"), name="inproj",
    )(x, mod, pp["pre_g"], pp["w_in"], pp["wg2"], pp["bg"], pp["qg"], pp["kg"], pp["sg"],
      tabs["cos"], tabs["sin"], pp["bd_q"], pp["vone"])


def _head_match(shape, rows_per_head, cols_per_head):
    return (lax.broadcasted_iota(jnp.int32, shape, 0) // rows_per_head
            == lax.broadcasted_iota(jnp.int32, shape, 1) // cols_per_head)


def _gla_band_kernel(q_ref, k_ref, v_ref, gf_ref, gb_ref, of_ref, ob_ref, o_ref,
                     qf_ref, kf_ref, vf_ref, bf_ref, cb_ref, pit_ref, os_ref):
    rows = q_ref.shape[1]
    ts = rows // GLA_SUB
    n_slab = A_W // LANES
    every = lambda p: pl.ds(p, ts, stride=GLA_PITCH)
    hsum = jnp.where(_head_match((A_KW, A_W), A_DK, A_DV), 1.0, 0.0).astype(BF16)
    slab_q, slab_k, slab_gf, slab_gb, slab_v = 0, 1, 2, 3, 4
    for i in range(ts):
        src = slice(i * GLA_SUB, (i + 1) * GLA_SUB)
        dst = slice(i * GLA_PITCH, i * GLA_PITCH + GLA_SUB)
        pit_ref[slab_q, dst, :] = q_ref[0, src, :]
        pit_ref[slab_k, dst, :] = k_ref[0, src, :]
        pit_ref[slab_gf, dst, :] = gf_ref[0, src, :]
        pit_ref[slab_gb, dst, :] = gb_ref[0, src, :]
        for h in range(n_slab):
            pit_ref[slab_v + h, dst, :] = v_ref[0, src, h * LANES:(h + 1) * LANES]
    run = jnp.zeros((ts, A_KW), F32)
    for p in range(GLA_SUB):
        run = run + pit_ref[slab_gf, every(p), :] * LOG2E
        bf_ref[p] = run
        qf_ref[p] = pit_ref[slab_q, every(p), :]
        kf_ref[p] = pit_ref[slab_k, every(p), :]
        vf_ref[p] = jnp.concatenate([pit_ref[slab_v + h, every(p), :] for h in range(n_slab)], axis=-1)
    run = jnp.zeros((ts, A_KW), F32)
    for p in reversed(range(GLA_SUB)):
        run = run + pit_ref[slab_gb, every(p), :] * LOG2E
        cb_ref[p] = run
    for p in range(GLA_SUB):
        q = qf_ref[p]
        prods = []
        for s in range(GLA_SUB):
            qk = q * kf_ref[s]
            if s < p:
                qk = qk * jnp.exp2(bf_ref[p] - bf_ref[s])
            elif s > p:
                qk = qk * jnp.exp2(cb_ref[p] - cb_ref[s])
            prods.append(qk.astype(BF16))
        a = jnp.dot(jnp.concatenate(prods, axis=0), hsum, preferred_element_type=F32)
        acc = a[0:ts] * vf_ref[0]
        for s in range(1, GLA_SUB):
            acc = acc + a[s * ts:(s + 1) * ts] * vf_ref[s]
        for h in range(n_slab):
            os_ref[h, every(p), :] = acc[:, h * LANES:(h + 1) * LANES]
    for i in range(ts):
        src = slice(i * GLA_PITCH, i * GLA_PITCH + GLA_SUB)
        dst = slice(i * GLA_SUB, (i + 1) * GLA_SUB)
        band = jnp.concatenate([os_ref[h, src, :] for h in range(n_slab)], axis=-1)
        o_ref[0, dst, :] = (band + of_ref[0, dst, :].astype(F32) + ob_ref[0, dst, :].astype(F32)).astype(BF16)


def _gla_band(gq, gk, gv, gf, gb, of, ob):
    bsz, n, _ = gq.shape
    rows = min(GLA_BAND_TILE * GLA_SUB, n)
    ts = rows // GLA_SUB
    blk = lambda w: pl.BlockSpec((1, rows, w), lambda b, i: (b, i, 0))
    return pl.pallas_call(
        _gla_band_kernel,
        out_shape=jax.ShapeDtypeStruct((bsz, n, A_W), BF16),
        grid=(bsz, n // rows),
        in_specs=[blk(A_KW), blk(A_KW), blk(A_W), blk(A_KW), blk(A_KW), blk(A_W), blk(A_W)],
        out_specs=blk(A_W),
        scratch_shapes=[
            pltpu.VMEM((GLA_SUB, ts, A_KW), F32), pltpu.VMEM((GLA_SUB, ts, A_KW), F32),
            pltpu.VMEM((GLA_SUB, ts, A_W), F32),
            pltpu.VMEM((GLA_SUB, ts, A_KW), F32), pltpu.VMEM((GLA_SUB, ts, A_KW), F32),
            pltpu.VMEM((4 + A_W // LANES, ts * GLA_PITCH, LANES), F32),
            pltpu.VMEM((A_W // LANES, ts * GLA_PITCH, LANES), F32),
        ],
        compiler_params=_params("parallel", "parallel"), name="gla_band",
    )(gq, gk, gv, gf, gb, of, ob)


def _split3(x):
    hi = x.astype(BF16)
    rest = x - hi.astype(F32)
    mid = rest.astype(BF16)
    lo = (rest - mid.astype(F32)).astype(BF16)
    return hi, mid, lo


def _gla_prep(q_ref, k_ref, g_ref, qt_ref, kh_ref, dect_ref, tot_ref, pad_ref, reverse):
    seg = q_ref.shape[1]
    n_sub = seg // GLA_SUB
    r_i = lax.broadcasted_iota(jnp.int32, (GLA_GRP, GLA_GRP), 0)
    c_i = lax.broadcasted_iota(jnp.int32, (GLA_GRP, GLA_GRP), 1)
    same = (r_i // GLA_SUB) == (c_i // GLA_SUB)
    tri = (c_i >= r_i) if reverse else (c_i <= r_i)
    sums_mat = jnp.concatenate([jnp.where(same & tri, 1.0, 0.0), jnp.where(same, 1.0, 0.0)], axis=0).astype(BF16)
    for r in range(seg // GLA_GRP):
        rows = slice(r * GLA_GRP, (r + 1) * GLA_GRP)
        g3 = jnp.concatenate(_split3(g_ref[0, rows, :]), axis=-1)
        sums = jnp.dot(sums_mat, g3, preferred_element_type=F32)
        sums = sums[:, 0:A_KW] + sums[:, A_KW:2 * A_KW] + sums[:, 2 * A_KW:]
        cum, tot = sums[:GLA_GRP], sums[GLA_GRP:]
        qt_ref[rows, :] = (q_ref[0, rows, :] * jnp.exp(cum)).astype(BF16)
        kh_ref[rows, :] = (k_ref[0, rows, :] * jnp.exp(tot - cum)).astype(BF16)
        tot_ref[rows, :] = tot
    pad_ref[...] = jnp.zeros_like(pad_ref)
    pad_ref[0:n_sub, :] = tot_ref[pl.ds(0, n_sub, stride=GLA_SUB), :]
    dect_ref[...] = jnp.exp(pad_ref[...].T)


def _gla_rec_kernel(qf_ref, kf_ref, vf_ref, gf_ref, qb_ref, kb_ref, vb_ref, gb_ref, of_ref, ob_ref,
                    sf_ref, sb_ref, qtf_ref, khf_ref, dtf_ref, qtb_ref, khb_ref, dtb_ref, tot_ref, pad_ref):
    @pl.when(pl.program_id(1) == 0)
    def _():
        sf_ref[...] = jnp.zeros_like(sf_ref)
        sb_ref[...] = jnp.zeros_like(sb_ref)

    seg = qf_ref.shape[1]
    n_sub = seg // GLA_SUB
    _gla_prep(qf_ref, kf_ref, gf_ref, qtf_ref, khf_ref, dtf_ref, tot_ref, pad_ref, reverse=False)
    _gla_prep(qb_ref, kb_ref, gb_ref, qtb_ref, khb_ref, dtb_ref, tot_ref, pad_ref, reverse=True)
    smask = _head_match((A_KW, A_W), A_DK, A_DV)

    def update(i, kh_ref, v_ref):
        rows = slice(i * GLA_SUB, (i + 1) * GLA_SUB)
        upd = lax.dot_general(kh_ref[rows, :], v_ref[0, rows, :].astype(BF16), (((0,), (0,)), ((), ())),
                              preferred_element_type=F32)
        return jnp.where(smask, upd, 0.0)

    def step(i, s, upd, qt_ref, dect_ref, o_ref):
        rows = slice(i * GLA_SUB, (i + 1) * GLA_SUB)
        o_ref[0, rows, :] = jnp.dot(qt_ref[rows, :], s.astype(BF16), preferred_element_type=F32).astype(BF16)
        return s * dect_ref[:, i:i + 1] + upd

    order_f = list(range(n_sub))
    order_b = list(reversed(range(n_sub)))
    upd_f = [update(i, khf_ref, vf_ref) for i in order_f[:GLA_LOOKAHEAD]]
    upd_b = [update(i, khb_ref, vb_ref) for i in order_b[:GLA_LOOKAHEAD]]
    s_f = sf_ref[...]
    s_b = sb_ref[...]
    for j in range(n_sub):
        if j + GLA_LOOKAHEAD < n_sub:
            upd_f.append(update(order_f[j + GLA_LOOKAHEAD], khf_ref, vf_ref))
            upd_b.append(update(order_b[j + GLA_LOOKAHEAD], khb_ref, vb_ref))
        s_f = step(order_f[j], s_f, upd_f[j], qtf_ref, dtf_ref, of_ref)
        s_b = step(order_b[j], s_b, upd_b[j], qtb_ref, dtb_ref, ob_ref)
    sf_ref[...] = s_f
    sb_ref[...] = s_b


def _gla_rec(gq, gk, gv, gf, gb):
    bsz, n, _ = gq.shape
    seg = min(GLA_SEG, n)
    nseg = n // seg
    assert seg // GLA_SUB <= LANES
    fwd = lambda w: pl.BlockSpec((1, seg, w), lambda b, s: (b, s, 0))
    bwd = lambda w: pl.BlockSpec((1, seg, w), lambda b, s: (b, nseg - 1 - s, 0))
    return pl.pallas_call(
        _gla_rec_kernel,
        out_shape=(jax.ShapeDtypeStruct((bsz, n, A_W), BF16), jax.ShapeDtypeStruct((bsz, n, A_W), BF16)),
        grid=(bsz, nseg),
        in_specs=[fwd(A_KW), fwd(A_KW), fwd(A_W), fwd(A_KW), bwd(A_KW), bwd(A_KW), bwd(A_W), bwd(A_KW)],
        out_specs=(fwd(A_W), bwd(A_W)),
        scratch_shapes=[
            pltpu.VMEM((A_KW, A_W), F32), pltpu.VMEM((A_KW, A_W), F32),
            pltpu.VMEM((seg, A_KW), BF16), pltpu.VMEM((seg, A_KW), BF16), pltpu.VMEM((A_KW, LANES), F32),
            pltpu.VMEM((seg, A_KW), BF16), pltpu.VMEM((seg, A_KW), BF16), pltpu.VMEM((A_KW, LANES), F32),
            pltpu.VMEM((seg, A_KW), F32), pltpu.VMEM((LANES, A_KW), F32),
        ],
        compiler_params=_params("parallel", "arbitrary"), name="gla_rec",
    )(gq, gk, gv, gf, gq, gk, gv, gb)


def _fft1_kernel(u_ref, a1_ref, cos_ref, sin_ref, t_ref):
    n1 = u_ref.shape[1]
    nb = cos_ref.shape[0]
    t = jnp.dot(a1_ref[...], u_ref[0], preferred_element_type=F32)
    for j in range(nb):
        cols = slice(j * B_W, (j + 1) * B_W)
        tr = t[:n1, cols]
        ti = t[n1:, cols]
        c = jnp.concatenate([cos_ref[j]] * (B_W // LANES), axis=-1)
        s = jnp.concatenate([sin_ref[j]] * (B_W // LANES), axis=-1)
        t_ref[0, 0, :, cols] = (tr * c + ti * s).astype(BF16)
        t_ref[0, 1, :, cols] = (ti * c - tr * s).astype(BF16)


def _fft2_kernel(t_ref, l_ref, cc_ref, cs_ref, w_ref, o_ref):
    n2 = t_ref.shape[3]
    half = FFT_K1 * n2
    groups = range(t_ref.shape[2] // FFT_K1)
    xs = []
    for grp in groups:
        k1s = slice(grp * FFT_K1, (grp + 1) * FFT_K1)
        xs.append(jnp.concatenate([t_ref[0, 0, k1s].reshape(half, B_W), t_ref[0, 1, k1s].reshape(half, B_W)],
                                  axis=0))
    gs = [jnp.dot(l_ref[...], x, preferred_element_type=F32) for x in xs]
    mixed = [jnp.dot(g[:half].astype(BF16), cc_ref[...], preferred_element_type=F32)
             + jnp.dot(g[half:].astype(BF16), cs_ref[...], preferred_element_type=F32) for g in gs]
    ys = [jnp.dot(m.astype(BF16), w_ref[0], preferred_element_type=F32) for m in mixed]
    o_ref[0] = jnp.concatenate([y.reshape(n2, FFT_K1, B_W) for y in ys], axis=1).astype(BF16)


def _fnet(fu, l, pp, tabs):
    bsz, n, _ = fu.shape
    n2 = FFT_N2
    n1 = n // n2
    nb = FFT_NB
    k1b = FFT_K1 * FFT_K1_GROUPS
    u2 = fu.reshape(bsz, n1, n2 * B_W)
    tt = pl.pallas_call(
        _fft1_kernel,
        out_shape=jax.ShapeDtypeStruct((bsz, 2, n1, n2 * B_W), BF16),
        grid=(n2 // nb, bsz),
        in_specs=[
            pl.BlockSpec((1, n1, nb * B_W), lambda j, b: (b, 0, j)),
            _full((2 * n1, n1)),
            pl.BlockSpec((nb, n1, LANES), lambda j, b: (j, 0, 0)),
            pl.BlockSpec((nb, n1, LANES), lambda j, b: (j, 0, 0)),
        ],
        out_specs=pl.BlockSpec((1, 2, n1, nb * B_W), lambda j, b: (b, 0, 0, j)),
        compiler_params=_params("parallel", "parallel"), name="fft_pass1",
    )(u2, tabs["a1"], tabs["tw_cos"], tabs["tw_sin"])
    t5 = tt.reshape(bsz, 2, n1, n2, B_W)
    y = pl.pallas_call(
        _fft2_kernel,
        out_shape=jax.ShapeDtypeStruct((bsz, n2, n1, B_W), BF16),
        grid=(bsz, n1 // k1b),
        in_specs=[
            pl.BlockSpec((1, 2, k1b, n2, B_W), lambda b, i: (b, 0, i, 0, 0)),
            _full((2 * FFT_K1 * n2, 2 * FFT_K1 * n2)),
            _full((B_W, B_W)), _full((B_W, B_W)), _of_layer(l, (B_W, B_W)),
        ],
        out_specs=pl.BlockSpec((1, n2, k1b, B_W), lambda b, i: (b, 0, i, 0)),
        compiler_params=_params("parallel", "parallel"), name="fft_pass2",
    )(t5, tabs["l2"], tabs["chan_cos"], tabs["chan_sin"], pp["fnet_w"])
    return y.reshape(bsz, n, B_W)


def _attn_kernel(q_ref, kt_ref, va_ref, o_ref, qm_ref, m_ref, acc_ref):
    tq = q_ref.shape[1]
    n = kt_ref.shape[3]
    tk = min(ATT_TK, n)
    q = q_ref[0]
    lane = lax.broadcasted_iota(jnp.int32, q.shape, 1)
    for h in range(C_GRP):
        qm_ref[h * tq:(h + 1) * tq, :] = jnp.where(lane // C_HD == h, q, jnp.zeros_like(q))

    def scores(c):
        c0 = pl.multiple_of(c * tk, tk)
        return jnp.dot(qm_ref[...], kt_ref[0, 0, :, pl.ds(c0, tk)], preferred_element_type=F32)

    def softmax_accumulate(c, s, first):
        c0 = pl.multiple_of(c * tk, tk)
        v = va_ref[0, 0, pl.ds(c0, tk), :]
        s_max = s[:, 0:LANES]
        for t in range(1, tk // LANES):
            s_max = jnp.maximum(s_max, s[:, t * LANES:(t + 1) * LANES])
        row_max = jnp.max(s_max, axis=-1, keepdims=True)
        if first:
            m_new = jnp.broadcast_to(row_max, m_ref.shape)
        else:
            m_old = m_ref[...]
            m_new = jnp.maximum(m_old, row_max)
        p = jnp.exp2(s - jnp.concatenate([m_new] * (tk // LANES), axis=-1))
        pv = jnp.dot(p.astype(BF16), v, preferred_element_type=F32)
        acc_ref[...] = pv if first else jnp.exp2(m_old - m_new) * acc_ref[...] + pv
        m_ref[...] = m_new

    def step(c, carry):
        softmax_accumulate(c, scores(c), first=False)
        return carry

    softmax_accumulate(0, scores(0), first=True)
    lax.fori_loop(1, n // tk, step, 0, unroll=ATT_UNROLL)
    acc = acc_ref[...]
    inv = 1.0 / acc[:, ONES_LANE:ONES_LANE + 1]
    outs = [(acc[h * tq:(h + 1) * tq, 0:C_HD] * inv[h * tq:(h + 1) * tq]) for h in range(C_GRP)]
    o_ref[0] = jnp.concatenate(outs, axis=-1).astype(BF16)


def _attention(aq, kt4, va):
    bsz, n, _ = aq.shape
    tq = min(ATT_TQ, n)
    return pl.pallas_call(
        _attn_kernel,
        out_shape=jax.ShapeDtypeStruct((bsz, n, C_W), BF16),
        grid=(bsz, C_KV, n // tq),
        in_specs=[
            pl.BlockSpec((1, tq, C_GRP * C_HD), lambda b, g, i: (b, i, g)),
            pl.BlockSpec((1, 1, C_GRP * C_HD, n), lambda b, g, i: (b, g, 0, 0)),
            pl.BlockSpec((1, 1, n, LANES), lambda b, g, i: (b, g, 0, 0)),
        ],
        out_specs=pl.BlockSpec((1, tq, C_GRP * C_HD), lambda b, g, i: (b, i, g)),
        scratch_shapes=[
            pltpu.VMEM((C_GRP * tq, C_GRP * C_HD), BF16),
            pltpu.VMEM((C_GRP * tq, LANES), F32),
            pltpu.VMEM((C_GRP * tq, LANES), F32),
        ],
        compiler_params=_params("parallel", "parallel", "arbitrary"), name="attention",
    )(aq, kt4, va)


def _outproj_kernel(x_hbm, mod_ref, oa_ref, yb_ref, ao_ref, du_ref, dvn_ref, sz_ref,
                    w_ref, postg_ref, og_ref, bd_ref, sw_ref, sb_ref, o_ref, xbuf_ref, xsem_ref):
    tm = o_ref.shape[1]
    tiles = x_hbm.shape[1] // tm
    step = pl.program_id(0)
    n_steps = pl.num_programs(0)

    def x_copy(s, slot):
        return pltpu.make_async_copy(x_hbm.at[s // tiles, pl.ds((s % tiles) * tm, tm), :],
                                     xbuf_ref.at[slot], xsem_ref.at[slot])

    @pl.when(step == 0)
    def _():
        x_copy(0, 0).start()

    @pl.when((step == 0) & (n_steps > 1))
    def _():
        x_copy(1, 1).start()

    @pl.when(step + OUT_X_AHEAD < n_steps)
    def _():
        x_copy(step + OUT_X_AHEAD, (step + OUT_X_AHEAD) % OUT_X_SLOTS).start()

    slot = step % OUT_X_SLOTS
    sz = sz_ref[0].astype(F32)
    off_b, off_c, off_d = A_W, A_W + B_W, A_W + B_W + C_W

    sgu = [jnp.dot(sw_ref[0], dvn_ref[0, c * SGU_CHUNK:(c + 1) * SGU_CHUNK, :], preferred_element_type=F32)
           for c in range(tm // SGU_CHUNK)]
    m_c = (ao_ref[0].astype(F32) * sz[:, off_c:off_d]).astype(BF16)
    y = jnp.dot(m_c, w_ref[0, off_c:off_d, :], preferred_element_type=F32)
    m_b = (yb_ref[0].astype(F32) * sz[:, off_b:off_c]).astype(BF16)
    y = y + jnp.dot(m_b, w_ref[0, off_b:off_c, :], preferred_element_type=F32)

    o = oa_ref[0].astype(F32)
    oms = jnp.dot((o * o).astype(BF16), bd_ref[...], preferred_element_type=F32)
    out_a = o * lax.rsqrt(oms + EPS) * og_ref[0]
    y = y + jnp.dot((out_a * sz[:, 0:off_b]).astype(BF16), w_ref[0, 0:off_b, :], preferred_element_type=F32)

    lane = lax.broadcasted_iota(jnp.int32, (SGU_CHUNK, D_W), 1)
    parts = []
    for c, r in enumerate(sgu):
        rows = slice(c * SGU_CHUNK, (c + 1) * SGU_CHUNK)
        mix = sb_ref[0]
        for g in range(D_GROUPS):
            mix = mix + jnp.where(lane // D_GC == g, r[g * SGU_CHUNK:(g + 1) * SGU_CHUNK, :], 0.0)
        parts.append(du_ref[0, rows, :].astype(F32) * mix)
    out_d = jnp.concatenate(parts, axis=0)
    m_d = (out_d * sz[:, off_d:]).astype(BF16)
    y = y + jnp.dot(m_d, w_ref[0, off_d:, :], preferred_element_type=F32)

    yms = jnp.mean(y * y, axis=-1, keepdims=True)
    yn = y * lax.rsqrt(yms + EPS) * postg_ref[0]
    gate = mod_ref[0, 0, :, 2 * D_MODEL:3 * D_MODEL]
    x_copy(step, slot).wait()
    o_ref[0] = xbuf_ref[slot] + gate * yn


def _outproj(x, mod, row0, l, oa, yb, ao, du, dvn, sz, pp):
    bsz, n, _ = x.shape
    tm = min(TM_OUT, n)
    tiles = n // tm
    tok = lambda w: pl.BlockSpec((1, tm, w), lambda s: (s // tiles, s % tiles, 0))
    return pl.pallas_call(
        _outproj_kernel,
        out_shape=jax.ShapeDtypeStruct((bsz, n, D_MODEL), F32),
        grid=(bsz * tiles,),
        in_specs=[
            pl.BlockSpec(memory_space=pl.ANY),
            pl.BlockSpec((1, 1, 1, 3 * D_MODEL), lambda s: (l, row0 + s // tiles, 0, 0)),
            tok(A_W), tok(B_W), tok(C_W), tok(D_W), tok(D_W), tok(D_MIX),
            _of_layer(l, (D_MIX, D_MODEL)), _of_layer(l, (1, D_MODEL)), _of_layer(l, (1, A_W)), _full((A_W, A_W)),
            _of_layer(l, (D_GROUPS * SGU_CHUNK, SGU_CHUNK)), _of_layer(l, (SGU_CHUNK, D_W)),
        ],
        out_specs=tok(D_MODEL),
        scratch_shapes=[pltpu.VMEM((OUT_X_SLOTS, tm, D_MODEL), F32), pltpu.SemaphoreType.DMA((OUT_X_SLOTS,))],
        compiler_params=_params("arbitrary"), name="outproj",
    )(x, mod, oa, yb, ao, du, dvn, sz,
      pp["w_out"], pp["post_g"], pp["og"], pp["bd_a"], pp["sgu_w"], pp["sgu_b"])


def _pack_w_in(w):
    sizes = [A_KW, A_KW, A_W, 2 * GLA_RANK, B_W, C_W, C_KVW, C_KVW, D_W, D_W, D_MIX]
    off = [0] + [int(o) for o in np.cumsum(sizes)]
    a_q, a_k, a_v, a_lr, b_u, c_q, c_k, c_v, d_u, d_v, z = [w[:, off[i]:off[i + 1]] for i in range(len(sizes))]

    def deinterleave(x, heads):
        r = x.reshape(D_MODEL, heads, C_HD // 2, 2)
        return jnp.concatenate([r[..., 0], r[..., 1]], axis=-1).reshape(D_MODEL, heads * C_HD)

    zeros = lambda width: jnp.zeros((D_MODEL, width), w.dtype)
    pieces = [a_q, a_k, a_v, b_u, deinterleave(c_q, C_HEADS), deinterleave(c_k, C_KV),
              a_lr, zeros(LANES - 2 * GLA_RANK)]
    for h in range(C_KV):
        pieces += [c_v[:, h * C_HD:(h + 1) * C_HD], zeros(LANES - C_HD)]
    pieces += [d_u, d_v, z]
    packed = jnp.concatenate(pieces, axis=1)
    assert packed.shape[1] == W_PACKED
    return packed.astype(BF16)


def _block_diag_mean(width, group):
    idx = np.arange(width) // group
    return jnp.asarray((idx[:, None] == idx[None, :]).astype(np.float32) / group, BF16)


def _stacked_params(norm_pre_g, norm_post_g, w_in, gla_wg2_f, gla_bg_f, gla_wg2_b, gla_bg_b,
                    gla_onorm_g, fnet_w, q_norm_g, k_norm_g, sgu_norm_g, sgu_w, sgu_b, w_out):
    depth = w_in.shape[0]
    deint = np.concatenate([np.arange(0, C_HD, 2), np.arange(1, C_HD, 2)])
    wg2 = jnp.zeros((depth, LANES, 2 * A_KW), F32)
    wg2 = wg2.at[:, 0:GLA_RANK, 0:A_KW].set(gla_wg2_f)
    wg2 = wg2.at[:, GLA_RANK:2 * GLA_RANK, A_KW:].set(gla_wg2_b)
    vone = np.zeros((1, 2 * LANES), np.float32)
    vone[0, ONES_LANE] = 1.0
    vone[0, LANES + ONES_LANE] = 1.0
    return {
        "pre_g": norm_pre_g.reshape(depth, 1, D_MODEL),
        "post_g": norm_post_g.reshape(depth, 1, D_MODEL),
        "w_in": jax.vmap(_pack_w_in)(w_in),
        "wg2": wg2.astype(BF16),
        "bg": jnp.concatenate([gla_bg_f, gla_bg_b], axis=-1).reshape(depth, 1, 2 * A_KW),
        "qg": jnp.tile(q_norm_g[:, deint], (1, C_HEADS)).reshape(depth, 1, C_W),
        "kg": jnp.tile(k_norm_g[:, deint], (1, C_KV)).reshape(depth, 1, C_KVW),
        "sg": sgu_norm_g.reshape(depth, 1, D_W),
        "bd_q": _block_diag_mean(C_W, C_HD),
        "bd_a": _block_diag_mean(A_W, A_DV),
        "vone": jnp.asarray(vone),
        "og": jnp.tile(gla_onorm_g, (1, A_HEADS)).reshape(depth, 1, A_W),
        "fnet_w": fnet_w.astype(BF16),
        "sgu_w": sgu_w.reshape(depth, D_GROUPS * SGU_CHUNK, SGU_CHUNK).astype(BF16),
        "sgu_b": jnp.repeat(jnp.swapaxes(sgu_b, 1, 2), D_GC, axis=2),
        "w_out": w_out.astype(BF16),
    }


def _seq_tables(n):
    rows = n // GRID_W
    row = jnp.repeat(jnp.arange(rows, dtype=F32), GRID_W)
    col = jnp.tile(jnp.arange(GRID_W, dtype=F32), rows)
    rope_axis = C_HD // 2
    freqs = ROPE_THETA ** (-jnp.arange(0, rope_axis, 2, dtype=F32) / rope_axis)
    ang = jnp.concatenate([row[:, None] * freqs, col[:, None] * freqs], axis=-1)
    cos, sin = jnp.cos(ang), jnp.sin(ang)
    cos_t = jnp.tile(jnp.concatenate([cos, cos], axis=-1), (1, LANES // C_HD))
    sin_t = jnp.tile(jnp.concatenate([-sin, sin], axis=-1), (1, LANES // C_HD))

    n2 = FFT_N2
    n1 = n // n2

    def dft_angles(rows, cols, period):
        prod = (jnp.arange(rows, dtype=jnp.int32)[:, None] * jnp.arange(cols, dtype=jnp.int32)[None, :]) % period
        return prod.astype(F32) * (2.0 * math.pi / period)

    a1_ang = dft_angles(n1, n1, n1)
    a1 = jnp.concatenate([jnp.cos(a1_ang), -jnp.sin(a1_ang)], axis=0)
    tw_ang = dft_angles(n2, n1, n)
    scale = 1.0 / math.sqrt(n * B_GC)
    tw_cos = jnp.broadcast_to((jnp.cos(tw_ang) * scale)[:, :, None], (n2, n1, LANES))
    tw_sin = jnp.broadcast_to((jnp.sin(tw_ang) * scale)[:, :, None], (n2, n1, LANES))
    ang2 = dft_angles(n2, n2, n2)
    c2, s2 = jnp.cos(ang2), jnp.sin(ang2)
    eye = jnp.eye(FFT_K1, dtype=F32)
    blk = lambda m: (m[:, None, None, :] * eye[None, :, :, None]).reshape(n2 * FFT_K1, FFT_K1 * n2)
    l2 = jnp.block([[blk(c2), blk(s2)], [blk(-s2), blk(c2)]])
    angc = dft_angles(B_GC, B_GC, B_GC)
    grp = jnp.eye(B_GROUPS, dtype=F32)
    kron = lambda m: (grp[:, None, :, None] * m[None, :, None, :]).reshape(B_W, B_W)
    chan_cos = kron(jnp.cos(angc))
    chan_sin = kron(jnp.sin(angc))
    return {
        "cos": cos_t, "sin": sin_t,
        "a1": a1.astype(BF16),
        "tw_cos": tw_cos, "tw_sin": tw_sin,
        "l2": l2.astype(BF16),
        "chan_cos": chan_cos.astype(BF16), "chan_sin": chan_sin.astype(BF16),
    }


def _layer(x, mod, row0, l, pp, tabs):
    gq, gk, gv, gf, gb, fu, aq, kt4, va, du, dvn, sz = _inproj(x, mod, row0, l, pp, tabs)
    of, ob = _gla_rec(gq, gk, gv, gf, gb)
    oa = _gla_band(gq, gk, gv, gf, gb, of, ob)
    yb = _fnet(fu, l, pp, tabs)
    ao = _attention(aq, kt4, va)
    return _outproj(x, mod, row0, l, oa, yb, ao, du, dvn, sz, pp)


def kernel(x_prompt, x_sample, c_prompt, c_sample, ada_w, ada_b, norm_pre_g, norm_post_g, w_in,
           gla_wg2_f, gla_bg_f, gla_wg2_b, gla_bg_b, gla_onorm_g, fnet_w, q_norm_g, k_norm_g,
           sgu_norm_g, sgu_w, sgu_b, w_out):
    bp, bs = x_prompt.shape[0], x_sample.shape[0]
    pad_rows = (-(bp + bs)) % 8
    c_all = jnp.concatenate([c_prompt, c_sample, jnp.zeros((pad_rows, D_MODEL), F32)], axis=0)
    mod = _adaln(c_all, ada_w, ada_b)
    mod = mod.reshape(DEPTH, mod.shape[1], 1, 3 * D_MODEL)
    pp = _stacked_params(norm_pre_g, norm_post_g, w_in, gla_wg2_f, gla_bg_f, gla_wg2_b, gla_bg_b,
                         gla_onorm_g, fnet_w, q_norm_g, k_norm_g, sgu_norm_g, sgu_w, sgu_b, w_out)
    tabs_p = _seq_tables(x_prompt.shape[1])
    tabs_s = _seq_tables(x_sample.shape[1])
    y_prompt, y_sample = x_prompt, x_sample
    for l in range(DEPTH):
        y_prompt = _layer(y_prompt, mod, 0, l, pp, tabs_p)
        y_sample = _layer(y_sample, mod, bp, l, pp, tabs_s)
    return (y_prompt, y_sample)
```
